```python
import math
import jax, jax.numpy as jnp
from jax import lax
import numpy as np

D_MODEL = 2048
BATCH = 1
SEQ = 8192
DEPTH = 2
DEC_BATCH = 32
DEC_SEQ = 16
PAST_LEN = 2048

CHUNK = 64
N_A = DEPTH // 2
N_B = DEPTH - N_A
SSM_WIDTH = D_MODEL
SSM_GROUP = 16
SSM_GROUPS = SSM_WIDTH // SSM_GROUP
SSM_STATE = 64
N_HEADS = 8
HEAD_DIM = D_MODEL // (2 * N_HEADS)
ATTN_WIDTH = 2 * N_HEADS * HEAD_DIM
Q_BLOCK = 128
ROPE_THETA = 10000.0
EPS = 1e-6
NEG = -1e30

kernel_name = 'yoco_s5_diffattn_streaming_step'


def rmsnorm(x, g):
    xf = x.astype(jnp.float32)
    y = xf * lax.rsqrt(jnp.mean(xf * xf, axis=-1, keepdims=True) + EPS) * g.astype(jnp.float32)
    return y.astype(x.dtype)


def rope(x, pos):
    d = x.shape[-1]
    inv_freq = ROPE_THETA ** (-jnp.arange(0, d, 2, dtype=jnp.float32) / d)
    ang = pos.astype(jnp.float32)[:, None] * inv_freq[None, :]
    c = jnp.cos(ang)[:, None, :]
    s = jnp.sin(ang)[:, None, :]
    xf = x.astype(jnp.float32)
    x1, x2 = xf[..., : d // 2], xf[..., d // 2:]
    return jnp.concatenate([x1 * c - x2 * s, x1 * s + x2 * c], axis=-1).astype(x.dtype)


def zoh(lam_re, lam_im, log_dt, b_re, b_im):
    lam_re = lam_re.astype(jnp.float32)
    lam_im = lam_im.astype(jnp.float32)
    dt = jnp.exp(log_dt.astype(jnp.float32))[:, None]
    zr, zi = lam_re * dt, lam_im * dt
    mag = jnp.exp(zr)
    abar_re, abar_im = mag * jnp.cos(zi), mag * jnp.sin(zi)
    n_re, n_im = abar_re - 1.0, abar_im
    den = lam_re * lam_re + lam_im * lam_im
    coef_re = (n_re * lam_re + n_im * lam_im) / den
    coef_im = (n_im * lam_re - n_re * lam_im) / den
    b_re = b_re.astype(jnp.float32)
    b_im = b_im.astype(jnp.float32)
    bb_re = coef_re[..., None] * b_re - coef_im[..., None] * b_im
    bb_im = coef_re[..., None] * b_im + coef_im[..., None] * b_re
    return abar_re, abar_im, bb_re, bb_im


def combine(e1, e2):
    a1r, a1i, b1r, b1i = e1
    a2r, a2i, b2r, b2i = e2
    return (a2r * a1r - a2i * a1i,
            a2r * a1i + a2i * a1r,
            a2r * b1r - a2i * b1i + b2r,
            a2r * b1i + a2i * b1r + b2i)


def ssm_block(u, h_re, h_im, abar_re, abar_im, bb_re, bb_im, c_re, c_im):
    bsz, l, _ = u.shape
    ug = u.reshape(bsz, l, SSM_GROUPS, SSM_GROUP)
    bu_re = jnp.einsum('gpc,blgc->blgp', bb_re, ug)
    bu_im = jnp.einsum('gpc,blgc->blgp', bb_im, ug)
    bu_re = bu_re.at[:, 0].add(abar_re * h_re - abar_im * h_im)
    bu_im = bu_im.at[:, 0].add(abar_re * h_im + abar_im * h_re)
    a_re = jnp.broadcast_to(abar_re, bu_re.shape)
    a_im = jnp.broadcast_to(abar_im, bu_im.shape)
    _, _, s_re, s_im = lax.associative_scan(combine, (a_re, a_im, bu_re, bu_im), axis=1)
    y = jnp.einsum('gcp,blgp->blgc', c_re, s_re) - jnp.einsum('gcp,blgp->blgc', c_im, s_im)
    return y.reshape(bsz, l, SSM_WIDTH), s_re[:, -1], s_im[:, -1]


def mixer_a(x, h0_re, h0_im, g, w_in, lam_re, lam_im, log_dt, b_re, b_im, c_re, c_im, d_skip, w_glu, w_out):
    bsz, l, _ = x.shape
    uz = rmsnorm(x, g) @ w_in
    u, z = uz[..., :SSM_WIDTH], uz[..., SSM_WIDTH:]
    abar_re, abar_im, bb_re, bb_im = zoh(lam_re, lam_im, log_dt, b_re, b_im)
    c_re = c_re.astype(jnp.float32)
    c_im = c_im.astype(jnp.float32)
    uf = u.astype(jnp.float32)
    blk = CHUNK if l % CHUNK == 0 else l
    nb = l // blk
    u_blocks = jnp.moveaxis(uf.reshape(bsz, nb, blk, SSM_WIDTH), 1, 0)

    def step(carry, ub):
        y, hr, hi = ssm_block(ub, carry[0], carry[1], abar_re, abar_im, bb_re, bb_im, c_re, c_im)
        return (hr, hi), y

    (hr, hi), ys = lax.scan(step, (h0_re.astype(jnp.float32), h0_im.astype(jnp.float32)), u_blocks)
    y = jnp.moveaxis(ys, 0, 1).reshape(bsz, l, SSM_WIDTH) + d_skip.astype(jnp.float32) * uf
    y = jax.nn.gelu(y)
    y = y * jax.nn.sigmoid(y @ w_glu.astype(jnp.float32))
    o = y.astype(x.dtype) * jax.nn.silu(z)
    return x + o @ w_out, hr, hi


def shared_kv(x, pos, g, w_kv):
    bsz, l, _ = x.shape
    kv = rmsnorm(x, g) @ w_kv
    k = rope(kv[..., :ATTN_WIDTH].reshape(bsz, l, 2 * N_HEADS, HEAD_DIM), pos)
    k = k.reshape(bsz, l, N_HEADS, 2 * HEAD_DIM)
    v = kv[..., ATTN_WIDTH:].reshape(bsz, l, N_HEADS, 2 * HEAD_DIM)
    return k, v


def diff_attend(q, k, v, q_pos, k_pos, lam):
    bsz, lk = k.shape[:2]
    kk = k.reshape(bsz, lk, N_HEADS, 2, HEAD_DIM).astype(jnp.float32)
    s = jnp.einsum('bqhnd,bkhnd->bnhqk', q.astype(jnp.float32), kk) * (HEAD_DIM ** -0.5)
    mask = (k_pos[None, :] // CHUNK) <= (q_pos[:, None] // CHUNK)
    p = jax.nn.softmax(jnp.where(mask, s, NEG), axis=-1)
    attn = p[:, 0] - lam * p[:, 1]
    return jnp.einsum('bhqk,bkhe->bqhe', attn, v.astype(jnp.float32))


def attend(q, k, v, q_pos, k_pos, lam):
    bsz, l = q.shape[:2]
    if l > Q_BLOCK and l % Q_BLOCK == 0:
        nb = l // Q_BLOCK
        qb = jnp.moveaxis(q.reshape(bsz, nb, Q_BLOCK, N_HEADS, 2, HEAD_DIM), 1, 0)
        pb = q_pos.reshape(nb, Q_BLOCK)
        ob = lax.map(lambda a: diff_attend(a[0], k, v, a[1], k_pos, lam), (qb, pb))
        return jnp.moveaxis(ob, 0, 1).reshape(bsz, l, N_HEADS, 2 * HEAD_DIM)
    return diff_attend(q, k, v, q_pos, k_pos, lam)


def mixer_b(x, k, v, q_pos, k_pos, g, w_in, lq1, lk1, lq2, lk2, subln, w_out, lambda_init):
    bsz, l, _ = x.shape
    qz = rmsnorm(x, g) @ w_in
    q = rope(qz[..., :ATTN_WIDTH].reshape(bsz, l, 2 * N_HEADS, HEAD_DIM), q_pos)
    q = q.reshape(bsz, l, N_HEADS, 2, HEAD_DIM)
    z = qz[..., ATTN_WIDTH:]
    lam = (jnp.exp(jnp.sum(lq1.astype(jnp.float32) * lk1.astype(jnp.float32)))
           - jnp.exp(jnp.sum(lq2.astype(jnp.float32) * lk2.astype(jnp.float32))) + lambda_init)
    o = attend(q, k, v, q_pos, k_pos, lam)
    o = rmsnorm(o, subln) * (1.0 - lambda_init)
    o = o.reshape(bsz, l, ATTN_WIDTH).astype(x.dtype) * jax.nn.silu(z)
    return x + o @ w_out


def trunk(x, pos, ssm_re, ssm_im, past_k, past_v, a_params, b_params, kv_norm, w_kv, final_norm):
    (a_norm, a_w_in, a_lambda_re, a_lambda_im, a_log_dt, a_b_re, a_b_im,
     a_c_re, a_c_im, a_d, a_w_glu, a_w_out) = a_params
    (b_norm, b_w_in, b_lambda_q1, b_lambda_k1, b_lambda_q2, b_lambda_k2, b_subln, b_w_out) = b_params
    re_out, im_out = [], []
    for i in range(N_A):
        x, hr, hi = mixer_a(x, ssm_re[:, i], ssm_im[:, i], a_norm[i], a_w_in[i], a_lambda_re[i], a_lambda_im[i],
                            a_log_dt[i], a_b_re[i], a_b_im[i], a_c_re[i], a_c_im[i], a_d[i], a_w_glu[i], a_w_out[i])
        re_out.append(hr.astype(ssm_re.dtype))
        im_out.append(hi.astype(ssm_im.dtype))
    k_new, v_new = shared_kv(x, pos, kv_norm, w_kv)
    if past_k is None:
        k_all, v_all, k_pos = k_new, v_new, pos
    else:
        k_all = jnp.concatenate([past_k.astype(k_new.dtype), k_new], axis=1)
        v_all = jnp.concatenate([past_v.astype(v_new.dtype), v_new], axis=1)
        k_pos = jnp.arange(past_k.shape[1] + x.shape[1], dtype=jnp.int32)
    for j in range(N_B):
        lambda_init = 0.8 - 0.6 * math.exp(-0.3 * (N_A + j))
        x = mixer_b(x, k_all, v_all, pos, k_pos, b_norm[j], b_w_in[j], b_lambda_q1[j], b_lambda_k1[j],
                    b_lambda_q2[j], b_lambda_k2[j], b_subln[j], b_w_out[j], lambda_init)
    return rmsnorm(x, final_norm), jnp.stack(re_out, axis=1), jnp.stack(im_out, axis=1), k_new, v_new


def setup_inputs(seed: int = 0) -> dict:
    key = jax.random.key(seed)
    ks = jax.random.split(key, 32)
    f32 = jnp.float32
    nrm = lambda k, shape, s: jax.random.normal(k, shape, f32) * s
    lam_im = jnp.broadcast_to(math.pi * jnp.arange(SSM_STATE, dtype=f32), (N_A, SSM_GROUPS, SSM_STATE))
    return {
        'x_prompt': nrm(ks[0], (BATCH, SEQ, D_MODEL), 1.0),
        'x_sample': nrm(ks[1], (DEC_BATCH, DEC_SEQ, D_MODEL), 1.0),
        'state_ssm_re': nrm(ks[2], (DEC_BATCH, N_A, SSM_GROUPS, SSM_STATE), 0.1),
        'state_ssm_im': nrm(ks[3], (DEC_BATCH, N_A, SSM_GROUPS, SSM_STATE), 0.1),
        'cache_k': nrm(ks[4], (DEC_BATCH, PAST_LEN, N_HEADS, 2 * HEAD_DIM), 1.0),
        'cache_v': nrm(ks[5], (DEC_BATCH, PAST_LEN, N_HEADS, 2 * HEAD_DIM), 1.0),
        'a_norm': 1.0 + nrm(ks[6], (N_A, D_MODEL), 0.02),
        'a_w_in': nrm(ks[7], (N_A, D_MODEL, 2 * SSM_WIDTH), D_MODEL ** -0.5),
        'a_lambda_re': -0.5 + nrm(ks[8], (N_A, SSM_GROUPS, SSM_STATE), 0.01),
        'a_lambda_im': lam_im + nrm(ks[9], (N_A, SSM_GROUPS, SSM_STATE), 0.01),
        'a_log_dt': jax.random.uniform(ks[10], (N_A, SSM_GROUPS), f32, math.log(0.001), math.log(0.1)),
        'a_b_re': nrm(ks[11], (N_A, SSM_GROUPS, SSM_STATE, SSM_GROUP), (2 * SSM_GROUP) ** -0.5),
        'a_b_im': nrm(ks[12], (N_A, SSM_GROUPS, SSM_STATE, SSM_GROUP), (2 * SSM_GROUP) ** -0.5),
        'a_c_re': nrm(ks[13], (N_A, SSM_GROUPS, SSM_GROUP, SSM_STATE), (2 * SSM_STATE) ** -0.5),
        'a_c_im': nrm(ks[14], (N_A, SSM_GROUPS, SSM_GROUP, SSM_STATE), (2 * SSM_STATE) ** -0.5),
        'a_d': nrm(ks[15], (N_A, SSM_WIDTH), 1.0),
        'a_w_glu': nrm(ks[16], (N_A, SSM_WIDTH, SSM_WIDTH), SSM_WIDTH ** -0.5),
        'a_w_out': nrm(ks[17], (N_A, SSM_WIDTH, D_MODEL), SSM_WIDTH ** -0.5),
        'kv_norm': 1.0 + nrm(ks[18], (D_MODEL,), 0.02),
        'w_kv': nrm(ks[19], (D_MODEL, 2 * ATTN_WIDTH), D_MODEL ** -0.5),
        'b_norm': 1.0 + nrm(ks[20], (N_B, D_MODEL), 0.02),
        'b_w_in': nrm(ks[21], (N_B, D_MODEL, 2 * ATTN_WIDTH), D_MODEL ** -0.5),
        'b_lambda_q1': nrm(ks[22], (N_B, HEAD_DIM), 0.1),
        'b_lambda_k1': nrm(ks[23], (N_B, HEAD_DIM), 0.1),
        'b_lambda_q2': nrm(ks[24], (N_B, HEAD_DIM), 0.1),
        'b_lambda_k2': nrm(ks[25], (N_B, HEAD_DIM), 0.1),
        'b_subln': 1.0 + nrm(ks[26], (N_B, 2 * HEAD_DIM), 0.02),
        'b_w_out': nrm(ks[27], (N_B, ATTN_WIDTH, D_MODEL), ATTN_WIDTH ** -0.5),
        'final_norm': 1.0 + nrm(ks[28], (D_MODEL,), 0.02),
    }


def reference(x_prompt, x_sample, state_ssm_re, state_ssm_im, cache_k, cache_v,
              a_norm, a_w_in, a_lambda_re, a_lambda_im, a_log_dt, a_b_re, a_b_im, a_c_re, a_c_im,
              a_d, a_w_glu, a_w_out, kv_norm, w_kv, b_norm, b_w_in, b_lambda_q1, b_lambda_k1,
              b_lambda_q2, b_lambda_k2, b_subln, b_w_out, final_norm):
    a_params = (a_norm, a_w_in, a_lambda_re, a_lambda_im, a_log_dt, a_b_re, a_b_im,
                a_c_re, a_c_im, a_d, a_w_glu, a_w_out)
    b_params = (b_norm, b_w_in, b_lambda_q1, b_lambda_k1, b_lambda_q2, b_lambda_k2, b_subln, b_w_out)
    h0 = jnp.zeros((x_prompt.shape[0], N_A, SSM_GROUPS, SSM_STATE), state_ssm_re.dtype)
    pos_p = jnp.arange(x_prompt.shape[1], dtype=jnp.int32)
    y_prompt, re_p, im_p, k_p, v_p = trunk(x_prompt, pos_p, h0, h0, None, None,
                                           a_params, b_params, kv_norm, w_kv, final_norm)
    pos_s = cache_k.shape[1] + jnp.arange(x_sample.shape[1], dtype=jnp.int32)
    y_sample, re_s, im_s, k_s, v_s = trunk(x_sample, pos_s, state_ssm_re, state_ssm_im, cache_k, cache_v,
                                           a_params, b_params, kv_norm, w_kv, final_norm)
    return (y_prompt, y_sample, re_p, im_p, k_p, v_p, re_s, im_s, k_s, v_s)
```

```python
import functools
import math

import jax
import jax.numpy as jnp
from jax import lax
from jax.experimental import pallas as pl
from jax.experimental.pallas import tpu as pltpu

F32 = jnp.float32
BF16 = jnp.bfloat16

CHUNK = 64
HEAD_DIM = 128
N_HEADS = 8
SSM_GROUP = 16
SSM_STATE = 64
SSM_T = 16
ROPE_THETA = 10000.0
EPS = 1e-6
NEG = -1e30

LANES = 128
SUBLANES = 8
GROUPS_PER_TILE = LANES // SSM_GROUP
CHUNK_TILE = 128
ROW_TILE = 512
VMEM_LIMIT = 48 * 1024 * 1024

_NT = (((1,), (1,)), ((), ()))


def _params(*sem):
    return pltpu.CompilerParams(dimension_semantics=sem, vmem_limit_bytes=VMEM_LIMIT)


def _rms_scale(x, g):
    ms = jnp.mean(x * x, axis=-1, keepdims=True)
    return x * lax.rsqrt(ms + EPS) * g


def _a_in_kernel(xp_ref, xs_ref, g_ref, w_ref, o_ref, xn_ref, *, mp):
    m, n = pl.program_id(0), pl.program_id(1)

    @pl.when((n == 0) & (m < mp))
    def _():
        xn_ref[...] = _rms_scale(xp_ref[...], g_ref[...]).astype(BF16)

    @pl.when((n == 0) & (m == mp))
    def _():
        xn_ref[...] = _rms_scale(xs_ref[...], g_ref[...]).astype(BF16)

    @pl.when(m <= mp)
    def _():
        o_ref[...] = jnp.dot(xn_ref[...], w_ref[...], preferred_element_type=F32)

    @pl.when(m > mp)
    def _():
        o_ref[...] = jnp.zeros(o_ref.shape, F32)


def _a_in_proj(xp, xs, g, w, rows_pad, tn):
    tm = ROW_TILE
    d = g.shape[-1]
    n_out = w.shape[1]
    mp = xp.shape[0] // tm
    nb = n_out // tn
    assert xs.shape[0] == tm and xp.shape[0] % tm == 0 and rows_pad % tm == 0
    return pl.pallas_call(
        functools.partial(_a_in_kernel, mp=mp),
        out_shape=jax.ShapeDtypeStruct((rows_pad, n_out), F32),
        grid=(rows_pad // tm, nb),
        in_specs=[
            pl.BlockSpec((tm, d), lambda m, n: (jnp.minimum(m, mp - 1), 0)),
            pl.BlockSpec((tm, d), lambda m, n: (0, 0)),
            pl.BlockSpec((1, d), lambda m, n: (0, 0)),
            pl.BlockSpec((d, tn), lambda m, n: (0, jnp.where(m <= mp, n, nb - 1))),
        ],
        out_specs=pl.BlockSpec((tm, tn), lambda m, n: (m, n)),
        scratch_shapes=[pltpu.VMEM((tm, d), BF16)],
        compiler_params=_params("arbitrary", "arbitrary"),
        name="a_in_proj",
    )(xp, xs, g, w)


def _ssm_prep_kernel(lrc_ref, lic_ref, dtc_ref, bre_ref, bim_ref, lrr_ref, lir_ref, dtr_ref,
                     cre_ref, cim_ref, mt_ref, bend_ref, cin_ref, a16r_ref, a16i_ref):
    p2 = 2 * SSM_STATE
    w = SSM_T * SSM_GROUP

    lr, li = lrc_ref[...], lic_ref[...]
    dt = jnp.exp(dtc_ref[...])
    zr, zi = lr * dt, li * dt
    mag = jnp.exp(zr)
    n_re, n_im = mag * jnp.cos(zi) - 1.0, mag * jnp.sin(zi)
    den = lr * lr + li * li
    cf_re = (n_re * lr + n_im * li) / den
    cf_im = (n_im * lr - n_re * li) / den
    b_re, b_im = bre_ref[...], bim_ref[...]
    bb_re = cf_re * b_re - cf_im * b_im
    bb_im = cf_re * b_im + cf_im * b_re
    lane = lax.broadcasted_iota(jnp.int32, (p2, w), 1)
    e_end = (SSM_T - 1 - lane // SSM_GROUP).astype(F32)
    pm = jnp.exp(zr * e_end)
    pw_re, pw_im = pm * jnp.cos(zi * e_end), pm * jnp.sin(zi * e_end)
    end_re = pw_re * bb_re - pw_im * bb_im
    end_im = pw_re * bb_im + pw_im * bb_re
    for r in range(2):
        sl = slice(r * SSM_STATE, (r + 1) * SSM_STATE)
        bend_ref[r] = jnp.concatenate([end_re[sl], end_im[sl]], axis=0).astype(BF16)
    bst = jnp.concatenate([bb_re, bb_im], axis=0)

    lr2, li2 = lrr_ref[...], lir_ref[...]
    dt2 = jnp.exp(dtr_ref[...])
    zr2, zi2 = lr2 * dt2, li2 * dt2
    c_re, c_im = cre_ref[...], cim_ref[...]
    tau = (lax.broadcasted_iota(jnp.int32, (w, p2), 0) // SSM_GROUP).astype(F32)

    def c_times_power(ex):
        m = jnp.exp(zr2 * ex)
        q_re, q_im = m * jnp.cos(zi2 * ex), m * jnp.sin(zi2 * ex)
        return c_re * q_re - c_im * q_im, c_re * q_im + c_im * q_re

    g_re, g_im = c_times_power(tau)
    ci_re, ci_im = c_times_power(tau + 1.0)
    lane2 = lax.broadcasted_iota(jnp.int32, (w, p2), 1)
    sblk = lax.broadcasted_iota(jnp.int32, (w, w), 1) // SSM_GROUP
    for r in range(2):
        own = (lane2 // SSM_STATE) == r
        lhs = jnp.concatenate([jnp.where(own, g_re, 0.0), jnp.where(own, -g_im, 0.0)], axis=1)
        gen = jnp.dot(lhs, bst, preferred_element_type=F32)
        k = 0
        while (SSM_GROUP << k) < w:
            sh = SSM_GROUP << k
            shifted = jnp.concatenate([jnp.zeros((sh, w), F32), gen[:w - sh]], axis=0)
            gen = jnp.where(((sblk >> k) & 1) == 1, shifted, gen)
            k += 1
        mt_ref[r] = gen.astype(BF16)
        cin_ref[r] = jnp.concatenate(
            [jnp.where(own, ci_re, 0.0), jnp.where(own, -ci_im, 0.0)], axis=1).astype(BF16)

    m16 = jnp.exp(zr2 * float(SSM_T))
    a16r_ref[...] = m16 * jnp.cos(zi2 * float(SSM_T))
    a16i_ref[...] = m16 * jnp.sin(zi2 * float(SSM_T))


def _ssm_prep(lam_re, lam_im, log_dt, b_re, b_im, c_re, c_im):
    g, p = lam_re.shape
    gp = g // 2
    p2 = 2 * p
    w = SSM_T * SSM_GROUP
    col = lambda a: a.reshape(gp, p2, 1)
    row = lambda a: a.reshape(gp, 1, p2)
    dt_full = jnp.broadcast_to(log_dt[:, None], (g, p))
    b_t = lambda a: jnp.tile(a.reshape(gp, p2, SSM_GROUP), (1, 1, SSM_T))
    c_t = lambda a: jnp.tile(
        a.reshape(gp, 2, SSM_GROUP, p).transpose(0, 2, 1, 3).reshape(gp, SSM_GROUP, p2), (1, SSM_T, 1))
    colspec = pl.BlockSpec((None, p2, 1), lambda i: (i, 0, 0))
    rowspec = pl.BlockSpec((None, 1, p2), lambda i: (i, 0, 0))
    bspec = pl.BlockSpec((None, p2, w), lambda i: (i, 0, 0))
    cspec = pl.BlockSpec((None, w, p2), lambda i: (i, 0, 0))
    mt, bend, cin, a16r, a16i = pl.pallas_call(
        _ssm_prep_kernel,
        out_shape=[
            jax.ShapeDtypeStruct((g, w, w), BF16),
            jax.ShapeDtypeStruct((g, p2, w), BF16),
            jax.ShapeDtypeStruct((g, w, 2 * p2), BF16),
            jax.ShapeDtypeStruct((gp, 1, p2), F32),
            jax.ShapeDtypeStruct((gp, 1, p2), F32),
        ],
        grid=(gp,),
        in_specs=[colspec, colspec, colspec, bspec, bspec, rowspec, rowspec, rowspec, cspec, cspec],
        out_specs=[
            pl.BlockSpec((2, w, w), lambda i: (i, 0, 0)),
            pl.BlockSpec((2, p2, w), lambda i: (i, 0, 0)),
            pl.BlockSpec((2, w, 2 * p2), lambda i: (i, 0, 0)),
            rowspec, rowspec,
        ],
        compiler_params=_params("arbitrary"),
        name="ssm_prep",
    )(col(lam_re), col(lam_im), col(dt_full), b_t(b_re), b_t(b_im),
      row(lam_re), row(lam_im), row(dt_full), c_t(c_re), c_t(c_im))
    return mt, bend, cin, a16r.reshape(1, g * p), a16i.reshape(1, g * p)


def _step_rows(s):
    return pl.ds(s, CHUNK_TILE, stride=SSM_T)


def _build_ut(u_ref, ut_ref):
    for s in range(SSM_T):
        xt = u_ref[_step_rows(s), :].T.astype(BF16)
        for gl in range(GROUPS_PER_TILE):
            ut_ref[gl, s * SSM_GROUP:(s + 1) * SSM_GROUP, :] = xt[gl * SSM_GROUP:(gl + 1) * SSM_GROUP, :]


def _ssm_state_kernel(u_ref, bend_ref, sre_ref, sim_ref, ut_ref):
    _build_ut(u_ref, ut_ref)
    p = SSM_STATE
    for pr in range(GROUPS_PER_TILE // 2):
        st = [jnp.dot(bend_ref[2 * pr + r], ut_ref[2 * pr + r], preferred_element_type=F32)
              for r in range(2)]
        sre_ref[:, pr * 2 * p:(pr + 1) * 2 * p] = jnp.concatenate([st[0][:p], st[1][:p]], axis=0).T
        sim_ref[:, pr * 2 * p:(pr + 1) * 2 * p] = jnp.concatenate([st[0][p:], st[1][p:]], axis=0).T


def _ssm_state_contrib(uz, bend, jpad):
    g, p2, w = bend.shape
    ntile = g // GROUPS_PER_TILE
    sw = GROUPS_PER_TILE * SSM_STATE
    out = jax.ShapeDtypeStruct((jpad, g * SSM_STATE), F32)
    ospec = pl.BlockSpec((CHUNK_TILE, sw), lambda j, q: (j, q))
    return pl.pallas_call(
        _ssm_state_kernel,
        out_shape=[out, out],
        grid=(jpad // CHUNK_TILE, ntile),
        in_specs=[
            pl.BlockSpec((CHUNK_TILE * SSM_T, LANES), lambda j, q: (j, q)),
            pl.BlockSpec((GROUPS_PER_TILE, p2, w), lambda j, q: (q, 0, 0)),
        ],
        out_specs=[ospec, ospec],
        scratch_shapes=[pltpu.VMEM((GROUPS_PER_TILE, w, CHUNK_TILE), BF16)],
        compiler_params=_params("arbitrary", "arbitrary"),
        name="ssm_state_contrib",
    )(uz, bend)


def _ssm_scan_kernel(sre_ref, sim_ref, ar_ref, ai_ref, h0r_ref, h0i_ref,
                     hpr_ref, hpi_ref, pr_ref, pi_ref, sr_ref, si_ref, hr_sc, hi_sc, *, n_prompt):
    i = pl.program_id(0)
    rb = sre_ref.shape[0]
    ar, ai = ar_ref[...], ai_ref[...]

    @pl.when(i == 0)
    def _():
        hr_sc[...] = jnp.zeros_like(hr_sc)
        hi_sc[...] = jnp.zeros_like(hi_sc)

    @pl.when(i < n_prompt)
    def _():
        def body(j, c):
            row = pl.ds(j, 1)
            hr, hi = hr_sc[...], hi_sc[...]
            hpr_ref[row, :] = hr
            hpi_ref[row, :] = hi
            hr_sc[...] = ar * hr - ai * hi + sre_ref[row, :]
            hi_sc[...] = ar * hi + ai * hr + sim_ref[row, :]
            return c

        lax.fori_loop(0, rb, body, 0)

    @pl.when(i == n_prompt - 1)
    def _():
        pr_ref[...] = hr_sc[...]
        pi_ref[...] = hi_sc[...]

    @pl.when(i == n_prompt)
    def _():
        h0r, h0i = h0r_ref[...], h0i_ref[...]
        hpr_ref[...] = h0r
        hpi_ref[...] = h0i
        sr_ref[...] = ar * h0r - ai * h0i + sre_ref[...]
        si_ref[...] = ar * h0i + ai * h0r + sim_ref[...]

    @pl.when(i > n_prompt)
    def _():
        hpr_ref[...] = jnp.zeros_like(hpr_ref)
        hpi_ref[...] = jnp.zeros_like(hpi_ref)


def _ssm_scan(s_re, s_im, a16r, a16i, h0r, h0i, jp):
    jpad, n = s_re.shape
    rb = h0r.shape[0]
    assert jp % rb == 0 and jpad % rb == 0
    n_prompt = jp // rb
    rows = pl.BlockSpec((rb, n), lambda i: (i, 0))
    const1 = pl.BlockSpec((1, n), lambda i: (0, 0))
    constb = pl.BlockSpec((rb, n), lambda i: (0, 0))
    big = jax.ShapeDtypeStruct((jpad, n), F32)
    one = jax.ShapeDtypeStruct((1, n), F32)
    bat = jax.ShapeDtypeStruct((rb, n), F32)
    return pl.pallas_call(
        functools.partial(_ssm_scan_kernel, n_prompt=n_prompt),
        out_shape=[big, big, one, one, bat, bat],
        grid=(jpad // rb,),
        in_specs=[rows, rows, const1, const1, constb, constb],
        out_specs=[rows, rows, const1, const1, constb, constb],
        scratch_shapes=[pltpu.VMEM((1, n), F32), pltpu.VMEM((1, n), F32)],
        compiler_params=_params("arbitrary"),
        name="ssm_scan",
    )(s_re, s_im, a16r, a16i, h0r, h0i)


def _ssm_out_kernel(u_ref, mt_ref, cin_ref, hpr_ref, hpi_ref, d_ref, o_ref, ut_ref, yt_ref):
    _build_ut(u_ref, ut_ref)
    p2 = 2 * SSM_STATE
    for gl in range(GROUPS_PER_TILE):
        pr = gl // 2
        hp = jnp.concatenate([hpr_ref[:, pr * p2:(pr + 1) * p2], hpi_ref[:, pr * p2:(pr + 1) * p2]],
                             axis=1).astype(BF16)
        yt = jnp.dot(mt_ref[gl], ut_ref[gl], preferred_element_type=F32)
        yt = yt + lax.dot_general(cin_ref[gl], hp, _NT, preferred_element_type=F32)
        for t in range(SSM_T):
            yt_ref[t, gl * SSM_GROUP:(gl + 1) * SSM_GROUP, :] = yt[t * SSM_GROUP:(t + 1) * SSM_GROUP, :]
    d = d_ref[...]
    for t in range(SSM_T):
        y = yt_ref[t].T + d * u_ref[_step_rows(t), :]
        o_ref[_step_rows(t), :] = jax.nn.gelu(y)


def _ssm_output(uz, mt, cin, hp_re, hp_im, d_skip, jpad):
    g, w, _ = mt.shape
    w2 = cin.shape[-1]
    ntile = g // GROUPS_PER_TILE
    sw = GROUPS_PER_TILE * SSM_STATE
    tok = pl.BlockSpec((CHUNK_TILE * SSM_T, LANES), lambda j, q: (j, q))
    hspec = pl.BlockSpec((CHUNK_TILE, sw), lambda j, q: (j, q))
    return pl.pallas_call(
        _ssm_out_kernel,
        out_shape=jax.ShapeDtypeStruct((jpad * SSM_T, g * SSM_GROUP), F32),
        grid=(jpad // CHUNK_TILE, ntile),
        in_specs=[
            tok,
            pl.BlockSpec((GROUPS_PER_TILE, w, w), lambda j, q: (q, 0, 0)),
            pl.BlockSpec((GROUPS_PER_TILE, w, w2), lambda j, q: (q, 0, 0)),
            hspec, hspec,
            pl.BlockSpec((1, LANES), lambda j, q: (0, q)),
        ],
        out_specs=tok,
        scratch_shapes=[
            pltpu.VMEM((GROUPS_PER_TILE, w, CHUNK_TILE), BF16),
            pltpu.VMEM((SSM_T, LANES, CHUNK_TILE), F32),
        ],
        compiler_params=_params("arbitrary", "arbitrary"),
        name="ssm_output",
    )(uz, mt, cin, hp_re, hp_im, d_skip)


def _a_glu_kernel(y_ref, yc_ref, z_ref, w_ref, o_ref, yb_ref):
    @pl.when(pl.program_id(1) == 0)
    def _():
        yb_ref[...] = y_ref[...].astype(BF16)

    gate = jnp.dot(yb_ref[...], w_ref[...], preferred_element_type=F32)
    y2 = yc_ref[...] * jax.nn.sigmoid(gate)
    o_ref[...] = (y2 * jax.nn.silu(z_ref[...])).astype(BF16)


def _a_glu(yg, uz, w_glu, rows, tn):
    tm = ROW_TILE
    e = yg.shape[1]
    zoff = e // tn
    return pl.pallas_call(
        _a_glu_kernel,
        out_shape=jax.ShapeDtypeStruct((rows, e), BF16),
        grid=(rows // tm, e // tn),
        in_specs=[
            pl.BlockSpec((tm, e), lambda m, n: (m, 0)),
            pl.BlockSpec((tm, tn), lambda m, n: (m, n)),
            pl.BlockSpec((tm, tn), lambda m, n: (m, zoff + n)),
            pl.BlockSpec((e, tn), lambda m, n: (0, n)),
        ],
        out_specs=pl.BlockSpec((tm, tn), lambda m, n: (m, n)),
        scratch_shapes=[pltpu.VMEM((tm, e), BF16)],
        compiler_params=_params("arbitrary", "arbitrary"),
        name="a_glu",
    )(yg, yg, uz, w_glu)


def _a_out_kernel(o_ref, w_ref, xp_ref, xs_ref, x1_ref, *, mp):
    m = pl.program_id(0)
    acc = jnp.dot(o_ref[...], w_ref[...], preferred_element_type=F32)

    @pl.when(m < mp)
    def _():
        x1_ref[...] = xp_ref[...] + acc

    @pl.when(m == mp)
    def _():
        x1_ref[...] = xs_ref[...] + acc


def _a_out_proj(o, w_out, xp, xs, tn):
    tm = ROW_TILE
    rows, e = o.shape
    d = w_out.shape[1]
    mp = xp.shape[0] // tm
    return pl.pallas_call(
        functools.partial(_a_out_kernel, mp=mp),
        out_shape=jax.ShapeDtypeStruct((rows, d), F32),
        grid=(rows // tm, d // tn),
        in_specs=[
            pl.BlockSpec((tm, e), lambda m, n: (m, 0)),
            pl.BlockSpec((e, tn), lambda m, n: (0, n)),
            pl.BlockSpec((tm, tn), lambda m, n: (jnp.minimum(m, mp - 1), n)),
            pl.BlockSpec((tm, tn), lambda m, n: (0, n)),
        ],
        out_specs=pl.BlockSpec((tm, tn), lambda m, n: (m, n)),
        compiler_params=_params("arbitrary", "arbitrary"),
        name="a_out_proj",
    )(o, w_out, xp, xs)


def _rope(x, cos, sin):
    outs = []
    for c in range(x.shape[1] // HEAD_DIM):
        xc = x[:, c * HEAD_DIM:(c + 1) * HEAD_DIM]
        outs.append(xc * cos + pltpu.roll(xc, HEAD_DIM // 2, axis=1) * sin)
    return jnp.concatenate(outs, axis=1) if len(outs) > 1 else outs[0]


def _store_heads(o_ref, val, tm):
    for h in range(N_HEADS):
        o_ref[pl.ds(h, tm, stride=N_HEADS), :] = val[:, h * HEAD_DIM:(h + 1) * HEAD_DIM]


def _kv_proj_kernel(x_ref, g_ref, w_ref, cos_ref, sin_ref, k_ref, v_ref, cb_ref, xn_ref, *, tm):
    n = pl.program_id(1)

    @pl.when(n == 0)
    def _():
        xn_ref[...] = _rms_scale(x_ref[...], g_ref[...]).astype(BF16)

    acc = jnp.dot(xn_ref[...], w_ref[...], preferred_element_type=F32)

    @pl.when(n < 2)
    def _():
        r = _rope(acc, cos_ref[...], sin_ref[...])
        _store_heads(k_ref, r, tm)
        cb_ref[...] = r.astype(BF16)

    @pl.when(n >= 2)
    def _():
        _store_heads(v_ref, acc, tm)
        cb_ref[...] = acc.astype(BF16)


def _kv_proj(x1, row0, rows, g, w_perm, cos, sin):
    tm = ROW_TILE
    d = x1.shape[1]
    tn = N_HEADS * HEAD_DIM
    m0 = row0 // tm
    out4 = jax.ShapeDtypeStruct((rows * N_HEADS, 2 * HEAD_DIM), F32)
    return pl.pallas_call(
        functools.partial(_kv_proj_kernel, tm=tm),
        out_shape=[out4, out4, jax.ShapeDtypeStruct((rows, 4 * tn), BF16)],
        grid=(rows // tm, 4),
        in_specs=[
            pl.BlockSpec((tm, d), lambda m, n: (m0 + m, 0)),
            pl.BlockSpec((1, d), lambda m, n: (0, 0)),
            pl.BlockSpec((d, tn), lambda m, n: (0, n)),
            pl.BlockSpec((tm, HEAD_DIM), lambda m, n: (m, 0)),
            pl.BlockSpec((tm, HEAD_DIM), lambda m, n: (m, 0)),
        ],
        out_specs=[
            pl.BlockSpec((tm * N_HEADS, HEAD_DIM), lambda m, n: (m, jnp.minimum(n, 1))),
            pl.BlockSpec((tm * N_HEADS, HEAD_DIM), lambda m, n: (m, jnp.maximum(n - 2, 0))),
            pl.BlockSpec((tm, tn), lambda m, n: (m, n)),
        ],
        scratch_shapes=[pltpu.VMEM((tm, d), BF16)],
        compiler_params=_params("arbitrary", "arbitrary"),
        name="kv_proj",
    )(x1, g, w_perm, cos, sin)


def _qz_proj_kernel(x_ref, g_ref, w_ref, cos_ref, sin_ref, q_ref, z_ref, xn_ref, *, nq):
    n = pl.program_id(1)

    @pl.when(n == 0)
    def _():
        xn_ref[...] = _rms_scale(x_ref[...], g_ref[...]).astype(BF16)

    acc = jnp.dot(xn_ref[...], w_ref[...], preferred_element_type=F32)

    @pl.when(n < nq)
    def _():
        q_ref[...] = _rope(acc, cos_ref[...], sin_ref[...]).astype(BF16)

    @pl.when(n >= nq)
    def _():
        z_ref[...] = acc


def _qz_proj(x1, row0, rows, g, w, cos, sin, tn):
    tm = ROW_TILE
    d = x1.shape[1]
    half = w.shape[1] // 2
    nq = half // tn
    m0 = row0 // tm
    return pl.pallas_call(
        functools.partial(_qz_proj_kernel, nq=nq),
        out_shape=[jax.ShapeDtypeStruct((rows, half), BF16), jax.ShapeDtypeStruct((rows, half), F32)],
        grid=(rows // tm, 2 * nq),
        in_specs=[
            pl.BlockSpec((tm, d), lambda m, n: (m0 + m, 0)),
            pl.BlockSpec((1, d), lambda m, n: (0, 0)),
            pl.BlockSpec((d, tn), lambda m, n: (0, n)),
            pl.BlockSpec((tm, HEAD_DIM), lambda m, n: (m, 0)),
            pl.BlockSpec((tm, HEAD_DIM), lambda m, n: (m, 0)),
        ],
        out_specs=[
            pl.BlockSpec((tm, tn), lambda m, n: (m, jnp.minimum(n, nq - 1))),
            pl.BlockSpec((tm, tn), lambda m, n: (m, jnp.maximum(n - nq, 0))),
        ],
        scratch_shapes=[pltpu.VMEM((tm, d), BF16)],
        compiler_params=_params("arbitrary", "arbitrary"),
        name="qz_proj",
    )(x1, g, w, cos, sin)


def _rope_tables(pos):
    inv_freq = ROPE_THETA ** (-jnp.arange(0, HEAD_DIM, 2, dtype=F32) / HEAD_DIM)
    ang = pos.astype(F32)[:, None] * inv_freq[None, :]
    c, s = jnp.cos(ang), jnp.sin(ang)
    return jnp.concatenate([c, c], axis=-1), jnp.concatenate([-s, s], axis=-1)


def _diff_lambda(lq1_ref, lk1_ref, lq2_ref, lk2_ref, lambda_init):
    s1 = jnp.sum(lq1_ref[...] * lk1_ref[...], axis=-1, keepdims=True)
    s2 = jnp.sum(lq2_ref[...] * lk2_ref[...], axis=-1, keepdims=True)
    return jnp.exp(s1) - jnp.exp(s2) + lambda_init


def _attn_finish(a1, l1, a2, l2, lam, subln, z, lambda_init):
    o = a1 / l1 - lam * (a2 / l2)
    o = _rms_scale(o, subln) * (1.0 - lambda_init)
    return (o * jax.nn.silu(z)).astype(BF16)


def _softmax_update(m_sc, l_sc, a_sc, idx, s, v):
    m_old = m_sc[idx]
    m_new = jnp.maximum(m_old, jnp.max(s, axis=-1, keepdims=True))
    p = jnp.exp(s - m_new)
    alpha = jnp.exp(m_old - m_new)
    l_sc[idx] = alpha * l_sc[idx] + jnp.sum(p, axis=-1, keepdims=True)
    a_sc[idx] = alpha * a_sc[idx] + jnp.dot(p.astype(BF16), v, preferred_element_type=F32)
    m_sc[idx] = m_new


def _attn_prompt_kernel(q_ref, k1_ref, k2_ref, v1_ref, v2_ref, z_ref,
                        lq1_ref, lk1_ref, lq2_ref, lk2_ref, sub_ref,
                        o_ref, m_sc, l_sc, a_sc, *, tq, tk, lambda_init):
    qi = pl.program_id(1)
    scale = HEAD_DIM ** -0.5
    q = q_ref[...]
    qs = (q[:, :HEAD_DIM], q[:, HEAD_DIM:])
    k_refs = (k1_ref, k2_ref)
    m_sc[...] = jnp.full(m_sc.shape, NEG, F32)
    l_sc[...] = jnp.zeros(l_sc.shape, F32)
    a_sc[...] = jnp.zeros(a_sc.shape, F32)

    def block(kb, mask):
        rows = pl.ds(pl.multiple_of(kb * tk, tk), tk)
        vblk = jnp.concatenate([v1_ref[rows, :], v2_ref[rows, :]], axis=1)
        for n in range(2):
            s = lax.dot_general(qs[n], k_refs[n][rows, :], _NT, preferred_element_type=F32) * scale
            if mask is not None:
                s = jnp.where(mask, s, NEG)
            _softmax_update(m_sc, l_sc, a_sc, n, s, vblk)

    r = tq // tk

    def body(kb, c):
        block(kb, None)
        return c

    lax.fori_loop(0, qi * r, body, 0)
    for d in range(r):
        row = lax.broadcasted_iota(jnp.int32, (tq, tk), 0)
        col = lax.broadcasted_iota(jnp.int32, (tq, tk), 1) + d * tk
        block(qi * r + d, (col // CHUNK) <= (row // CHUNK))

    lam = _diff_lambda(lq1_ref, lk1_ref, lq2_ref, lk2_ref, lambda_init)
    o_ref[...] = _attn_finish(a_sc[0], l_sc[0], a_sc[1], l_sc[1], lam, sub_ref[...], z_ref[...],
                              lambda_init)


def _attn_prompt(qb, z32, kvb, lq1, lk1, lq2, lk2, subln, lambda_init, tq, tk):
    rows = qb.shape[0]
    hw = 2 * HEAD_DIM
    vec = pl.BlockSpec((1, HEAD_DIM), lambda h, i: (0, 0))
    kcol = lambda c: pl.BlockSpec((rows, HEAD_DIM), lambda h, i: (0, c * N_HEADS + h))
    return pl.pallas_call(
        functools.partial(_attn_prompt_kernel, tq=tq, tk=tk, lambda_init=lambda_init),
        out_shape=jax.ShapeDtypeStruct((rows, N_HEADS * hw), BF16),
        grid=(N_HEADS, rows // tq),
        in_specs=[
            pl.BlockSpec((tq, hw), lambda h, i: (i, h)),
            kcol(0), kcol(1), kcol(2), kcol(3),
            pl.BlockSpec((tq, hw), lambda h, i: (i, h)),
            vec, vec, vec, vec,
            pl.BlockSpec((1, hw), lambda h, i: (0, 0)),
        ],
        out_specs=pl.BlockSpec((tq, hw), lambda h, i: (i, h)),
        scratch_shapes=[
            pltpu.VMEM((2, tq, 1), F32),
            pltpu.VMEM((2, tq, 1), F32),
            pltpu.VMEM((2, tq, hw), F32),
        ],
        compiler_params=_params("arbitrary", "arbitrary"),
        name="attn_prompt",
    )(qb, kvb, kvb, kvb, kvb, z32, lq1, lk1, lq2, lk2, subln)


def _attn_sample_kernel(q_ref, ck1_ref, ck2_ref, cv1_ref, cv2_ref, kvn_ref, z_ref,
                        lq1_ref, lk1_ref, lq2_ref, lk2_ref, sub_ref,
                        o_ref, m_sc, l_sc, a_sc, *, tkv, nkb, past_len, lambda_init):
    kb = pl.program_id(1)
    scale = HEAD_DIM ** -0.5
    t = q_ref.shape[0]
    hw = 2 * HEAD_DIM
    ck_refs = (ck1_ref, ck2_ref)
    assert (past_len - 1) // CHUNK <= past_len // CHUNK

    @pl.when(kb == 0)
    def _():
        m_sc[...] = jnp.full(m_sc.shape, NEG, F32)
        l_sc[...] = jnp.zeros(l_sc.shape, F32)
        a_sc[...] = jnp.zeros(a_sc.shape, F32)

    for h in range(N_HEADS):
        rows = pl.ds(h, tkv, stride=N_HEADS)
        v = jnp.concatenate([cv1_ref[rows, :], cv2_ref[rows, :]], axis=1).astype(BF16)
        for n in range(2):
            qn = q_ref[:, (2 * h + n) * HEAD_DIM:(2 * h + n + 1) * HEAD_DIM]
            k = ck_refs[n][rows, :].astype(BF16)
            s = lax.dot_general(qn, k, _NT, preferred_element_type=F32) * scale
            _softmax_update(m_sc, l_sc, a_sc, 2 * h + n, s, v)

    @pl.when(kb == nkb - 1)
    def _():
        row = lax.broadcasted_iota(jnp.int32, (t, t), 0) + past_len
        col = lax.broadcasted_iota(jnp.int32, (t, t), 1) + past_len
        new_mask = (col // CHUNK) <= (row // CHUNK)
        lam = _diff_lambda(lq1_ref, lk1_ref, lq2_ref, lk2_ref, lambda_init)
        sub = sub_ref[...]
        col_blk = lambda c, h: slice((c * N_HEADS + h) * HEAD_DIM, (c * N_HEADS + h + 1) * HEAD_DIM)
        for h in range(N_HEADS):
            vn = jnp.concatenate([kvn_ref[:, col_blk(2, h)], kvn_ref[:, col_blk(3, h)]], axis=1)
            for n in range(2):
                qn = q_ref[:, (2 * h + n) * HEAD_DIM:(2 * h + n + 1) * HEAD_DIM]
                s = lax.dot_general(qn, kvn_ref[:, col_blk(n, h)], _NT,
                                    preferred_element_type=F32) * scale
                _softmax_update(m_sc, l_sc, a_sc, 2 * h + n, jnp.where(new_mask, s, NEG), vn)
            o_ref[:, h * hw:(h + 1) * hw] = _attn_finish(
                a_sc[2 * h], l_sc[2 * h], a_sc[2 * h + 1], l_sc[2 * h + 1], lam, sub,
                z_ref[:, h * hw:(h + 1) * hw], lambda_init)


def _attn_sample(qb, z32, kvb, cache_k2, cache_v2, lq1, lk1, lq2, lk2, subln, lambda_init, t, tkv):
    bsz, rows8, hw = cache_k2.shape
    past_len = rows8 // N_HEADS
    nkb = past_len // tkv
    vec = pl.BlockSpec((1, HEAD_DIM), lambda b, k: (0, 0))
    cache = lambda c: pl.BlockSpec((None, tkv * N_HEADS, HEAD_DIM), lambda b, k: (b, k, c))
    full = lambda a: pl.BlockSpec((t, a.shape[1]), lambda b, k: (b, 0))
    return pl.pallas_call(
        functools.partial(_attn_sample_kernel, tkv=tkv, nkb=nkb, past_len=past_len,
                          lambda_init=lambda_init),
        out_shape=jax.ShapeDtypeStruct((bsz * t, N_HEADS * hw), BF16),
        grid=(bsz, nkb),
        in_specs=[
            full(qb), cache(0), cache(1), cache(0), cache(1), full(kvb), full(z32),
            vec, vec, vec, vec,
            pl.BlockSpec((1, hw), lambda b, k: (0, 0)),
        ],
        out_specs=pl.BlockSpec((t, N_HEADS * hw), lambda b, k: (b, 0)),
        scratch_shapes=[
            pltpu.VMEM((2 * N_HEADS, t, 1), F32),
            pltpu.VMEM((2 * N_HEADS, t, 1), F32),
            pltpu.VMEM((2 * N_HEADS, t, hw), F32),
        ],
        compiler_params=_params("arbitrary", "arbitrary"),
        name="attn_sample",
    )(qb, cache_k2, cache_k2, cache_v2, cache_v2, kvb, z32, lq1, lk1, lq2, lk2, subln)


def _b_out_kernel(o_ref, w_ref, x_ref, g_ref, y_ref):
    acc = jnp.dot(o_ref[...], w_ref[...], preferred_element_type=F32)
    y_ref[...] = _rms_scale(x_ref[...] + acc, g_ref[...])


def _b_out_proj(og, w_out, x1, row0, g, tm):
    rows, e = og.shape
    d = w_out.shape[1]
    m0 = row0 // tm
    return pl.pallas_call(
        _b_out_kernel,
        out_shape=jax.ShapeDtypeStruct((rows, d), F32),
        grid=(rows // tm,),
        in_specs=[
            pl.BlockSpec((tm, e), lambda m: (m, 0)),
            pl.BlockSpec((e, d), lambda m: (0, 0)),
            pl.BlockSpec((tm, d), lambda m: (m0 + m, 0)),
            pl.BlockSpec((1, d), lambda m: (0, 0)),
        ],
        out_specs=pl.BlockSpec((tm, d), lambda m: (m, 0)),
        compiler_params=_params("arbitrary"),
        name="b_out_proj",
    )(og, w_out, x1, g)


def _permute_kv_columns(w_kv):
    d = w_kv.shape[0]
    w = w_kv.reshape(d, 2, N_HEADS, 2, HEAD_DIM).transpose(0, 1, 3, 2, 4)
    return w.reshape(d, 4 * N_HEADS * HEAD_DIM)


def kernel(x_prompt, x_sample, state_ssm_re, state_ssm_im, cache_k, cache_v, a_norm, a_w_in, a_lambda_re, a_lambda_im, a_log_dt, a_b_re, a_b_im, a_c_re, a_c_im, a_d, a_w_glu, a_w_out, kv_norm, w_kv, b_norm, b_w_in, b_lambda_q1, b_lambda_k1, b_lambda_q2, b_lambda_k2, b_subln, b_w_out, final_norm):
    bp, seq, d = x_prompt.shape
    bs, t_s, _ = x_sample.shape
    past_len = cache_k.shape[1]
    n_a, n_b = a_norm.shape[0], b_norm.shape[0]
    g_cnt, p = a_lambda_re.shape[1:]
    assert bp == 1 and n_a == 1 and n_b == 1
    assert t_s == SSM_T and seq % (SSM_T * bs) == 0 and seq % CHUNK == 0
    assert cache_k.shape[2:] == (N_HEADS, 2 * HEAD_DIM)

    jp, js = seq // SSM_T, bs
    jpad = -(-(jp + js) // CHUNK_TILE) * CHUNK_TILE
    rows_s = bs * t_s
    rows_a = seq + rows_s

    xp = x_prompt.reshape(seq, d)
    xs = x_sample.reshape(rows_s, d)
    uz = _a_in_proj(xp, xs, a_norm[0].reshape(1, d), a_w_in[0].astype(BF16), jpad * SSM_T, tn=1024)
    mt, bend, cin, a16r, a16i = _ssm_prep(a_lambda_re[0], a_lambda_im[0], a_log_dt[0],
                                          a_b_re[0], a_b_im[0], a_c_re[0], a_c_im[0])
    s_re, s_im = _ssm_state_contrib(uz, bend, jpad)
    h0r = state_ssm_re[:, 0].reshape(bs, g_cnt * p)
    h0i = state_ssm_im[:, 0].reshape(bs, g_cnt * p)
    hp_re, hp_im, pre, pim, sre, sim = _ssm_scan(s_re, s_im, a16r, a16i, h0r, h0i, jp)
    yg = _ssm_output(uz, mt, cin, hp_re, hp_im, a_d[0].reshape(1, -1), jpad)
    o_a = _a_glu(yg, uz, a_w_glu[0].astype(BF16), rows_a, tn=512)
    x1 = _a_out_proj(o_a, a_w_out[0].astype(BF16), xp, xs, tn=512)

    lambda_init = 0.8 - 0.6 * math.exp(-0.3 * n_a)
    w_kv_b = _permute_kv_columns(w_kv).astype(BF16)
    w_q_b = b_w_in[0].astype(BF16)
    kv_g = kv_norm.reshape(1, d)
    b_g = b_norm[0].reshape(1, d)
    pos_p = jnp.arange(seq, dtype=jnp.int32)
    pos_s = jnp.tile(past_len + jnp.arange(t_s, dtype=jnp.int32), bs)
    cos_p, sin_p = _rope_tables(pos_p)
    cos_s, sin_s = _rope_tables(pos_s)
    k_p, v_p, kvb_p = _kv_proj(x1, 0, seq, kv_g, w_kv_b, cos_p, sin_p)
    k_s, v_s, kvb_s = _kv_proj(x1, seq, rows_s, kv_g, w_kv_b, cos_s, sin_s)
    qb_p, z_p = _qz_proj(x1, 0, seq, b_g, w_q_b, cos_p, sin_p, tn=512)
    qb_s, z_s = _qz_proj(x1, seq, rows_s, b_g, w_q_b, cos_s, sin_s, tn=512)

    vecs = [a[0].reshape(1, HEAD_DIM) for a in (b_lambda_q1, b_lambda_k1, b_lambda_q2, b_lambda_k2)]
    subln = b_subln[0].reshape(1, 2 * HEAD_DIM)
    hw = 2 * HEAD_DIM
    og_p = _attn_prompt(qb_p, z_p, kvb_p, *vecs, subln, lambda_init, tq=512, tk=512)
    og_s = _attn_sample(qb_s, z_s, kvb_s, cache_k.reshape(bs, past_len * N_HEADS, hw),
                        cache_v.reshape(bs, past_len * N_HEADS, hw), *vecs, subln, lambda_init,
                        t_s, tkv=1024)
    w_o_b = b_w_out[0].astype(BF16)
    fg = final_norm.reshape(1, d)
    y_p = _b_out_proj(og_p, w_o_b, x1, 0, fg, 256)
    y_s = _b_out_proj(og_s, w_o_b, x1, seq, fg, 256)

    return (y_p.reshape(bp, seq, d), y_s.reshape(bs, t_s, d),
            pre.reshape(bp, n_a, g_cnt, p), pim.reshape(bp, n_a, g_cnt, p),
            k_p.reshape(bp, seq, N_HEADS, hw), v_p.reshape(bp, seq, N_HEADS, hw),
            sre.reshape(bs, n_a, g_cnt, p), sim.reshape(bs, n_a, g_cnt, p),
            k_s.reshape(bs, t_s, N_HEADS, hw), v_s.reshape(bs, t_s, N_HEADS, hw))
```

```python
import functools
import math

import jax
import jax.numpy as jnp
from jax import lax
from jax.experimental import pallas as pl
from jax.experimental.pallas import tpu as pltpu

F32 = jnp.float32
BF16 = jnp.bfloat16

CHUNK = 64
HEAD_DIM = 128
N_HEADS = 8
SSM_GROUP = 16
SSM_STATE = 64
SSM_T = 16
ROPE_THETA = 10000.0
EPS = 1e-6
NEG = -1e30

LANES = 128
SUBLANES = 8
GROUPS_PER_TILE = LANES // SSM_GROUP
CHUNK_TILE = 128
ROW_TILE = 512
STRIP = 32
VMEM_LIMIT = 48 * 1024 * 1024

_NT = (((1,), (1,)), ((), ()))


def _params(*sem):
    return pltpu.CompilerParams(dimension_semantics=sem, vmem_limit_bytes=VMEM_LIMIT)


def _rms_scale(x, g):
    ms = jnp.mean(x * x, axis=-1, keepdims=True)
    return x * lax.rsqrt(ms + EPS) * g


def _a_in_kernel(xp_ref, xs_ref, g_ref, w_ref, o_ref, xn_ref, *, mp):
    m, n = pl.program_id(0), pl.program_id(1)

    @pl.when((n == 0) & (m < mp))
    def _():
        xn_ref[...] = _rms_scale(xp_ref[...], g_ref[...]).astype(BF16)

    @pl.when((n == 0) & (m == mp))
    def _():
        xn_ref[...] = _rms_scale(xs_ref[...], g_ref[...]).astype(BF16)

    @pl.when(m <= mp)
    def _():
        o_ref[...] = jnp.dot(xn_ref[...], w_ref[...], preferred_element_type=F32)

    @pl.when(m > mp)
    def _():
        o_ref[...] = jnp.zeros(o_ref.shape, F32)


def _a_in_proj(xp, xs, g, w, rows_pad, tn):
    tm = ROW_TILE
    d = g.shape[-1]
    n_out = w.shape[1]
    mp = xp.shape[0] // tm
    nb = n_out // tn
    assert xs.shape[0] == tm and xp.shape[0] % tm == 0 and rows_pad % tm == 0
    return pl.pallas_call(
        functools.partial(_a_in_kernel, mp=mp),
        out_shape=jax.ShapeDtypeStruct((rows_pad, n_out), F32),
        grid=(rows_pad // tm, nb),
        in_specs=[
            pl.BlockSpec((tm, d), lambda m, n: (jnp.minimum(m, mp - 1), 0)),
            pl.BlockSpec((tm, d), lambda m, n: (0, 0)),
            pl.BlockSpec((1, d), lambda m, n: (0, 0)),
            pl.BlockSpec((d, tn), lambda m, n: (0, jnp.where(m <= mp, n, nb - 1))),
        ],
        out_specs=pl.BlockSpec((tm, tn), lambda m, n: (m, n)),
        scratch_shapes=[pltpu.VMEM((tm, d), BF16)],
        compiler_params=_params("arbitrary", "arbitrary"),
        name="a_in_proj",
    )(xp, xs, g, w)


def _ssm_prep_kernel(lrc_ref, lic_ref, dtc_ref, bre_ref, bim_ref, lrr_ref, lir_ref, dtr_ref,
                     cre_ref, cim_ref, mt_ref, bend_ref, cin_ref, a16r_ref, a16i_ref):
    p2 = 2 * SSM_STATE
    w = SSM_T * SSM_GROUP

    lr, li = lrc_ref[...], lic_ref[...]
    dt = jnp.exp(dtc_ref[...])
    zr, zi = lr * dt, li * dt
    mag = jnp.exp(zr)
    n_re, n_im = mag * jnp.cos(zi) - 1.0, mag * jnp.sin(zi)
    den = lr * lr + li * li
    cf_re = (n_re * lr + n_im * li) / den
    cf_im = (n_im * lr - n_re * li) / den
    b_re, b_im = bre_ref[...], bim_ref[...]
    bb_re = cf_re * b_re - cf_im * b_im
    bb_im = cf_re * b_im + cf_im * b_re
    lane = lax.broadcasted_iota(jnp.int32, (p2, w), 1)
    e_end = (SSM_T - 1 - lane // SSM_GROUP).astype(F32)
    pm = jnp.exp(zr * e_end)
    pw_re, pw_im = pm * jnp.cos(zi * e_end), pm * jnp.sin(zi * e_end)
    end_re = pw_re * bb_re - pw_im * bb_im
    end_im = pw_re * bb_im + pw_im * bb_re
    for r in range(2):
        sl = slice(r * SSM_STATE, (r + 1) * SSM_STATE)
        bend_ref[r] = jnp.concatenate([end_re[sl], end_im[sl]], axis=0).astype(BF16)
    bst = jnp.concatenate([bb_re, bb_im], axis=0)

    lr2, li2 = lrr_ref[...], lir_ref[...]
    dt2 = jnp.exp(dtr_ref[...])
    zr2, zi2 = lr2 * dt2, li2 * dt2
    c_re, c_im = cre_ref[...], cim_ref[...]
    tau = (lax.broadcasted_iota(jnp.int32, (w, p2), 0) // SSM_GROUP).astype(F32)

    def c_times_power(ex):
        m = jnp.exp(zr2 * ex)
        q_re, q_im = m * jnp.cos(zi2 * ex), m * jnp.sin(zi2 * ex)
        return c_re * q_re - c_im * q_im, c_re * q_im + c_im * q_re

    g_re, g_im = c_times_power(tau)
    ci_re, ci_im = c_times_power(tau + 1.0)
    lane2 = lax.broadcasted_iota(jnp.int32, (w, p2), 1)
    sblk = lax.broadcasted_iota(jnp.int32, (w, w), 1) // SSM_GROUP
    for r in range(2):
        own = (lane2 // SSM_STATE) == r
        lhs = jnp.concatenate([jnp.where(own, g_re, 0.0), jnp.where(own, -g_im, 0.0)], axis=1)
        gen = jnp.dot(lhs, bst, preferred_element_type=F32)
        k = 0
        while (SSM_GROUP << k) < w:
            sh = SSM_GROUP << k
            shifted = jnp.concatenate([jnp.zeros((sh, w), F32), gen[:w - sh]], axis=0)
            gen = jnp.where(((sblk >> k) & 1) == 1, shifted, gen)
            k += 1
        mt_ref[r] = gen.astype(BF16)
        cin_ref[r] = jnp.concatenate(
            [jnp.where(own, ci_re, 0.0), jnp.where(own, -ci_im, 0.0)], axis=1).astype(BF16)

    m16 = jnp.exp(zr2 * float(SSM_T))
    a16r_ref[...] = m16 * jnp.cos(zi2 * float(SSM_T))
    a16i_ref[...] = m16 * jnp.sin(zi2 * float(SSM_T))


def _ssm_prep(lam_re, lam_im, log_dt, b_re, b_im, c_re, c_im):
    g, p = lam_re.shape
    gp = g // 2
    p2 = 2 * p
    w = SSM_T * SSM_GROUP
    col = lambda a: a.reshape(gp, p2, 1)
    row = lambda a: a.reshape(gp, 1, p2)
    dt_full = jnp.broadcast_to(log_dt[:, None], (g, p))
    b_t = lambda a: jnp.tile(a.reshape(gp, p2, SSM_GROUP), (1, 1, SSM_T))
    c_t = lambda a: jnp.tile(
        a.reshape(gp, 2, SSM_GROUP, p).transpose(0, 2, 1, 3).reshape(gp, SSM_GROUP, p2), (1, SSM_T, 1))
    colspec = pl.BlockSpec((None, p2, 1), lambda i: (i, 0, 0))
    rowspec = pl.BlockSpec((None, 1, p2), lambda i: (i, 0, 0))
    bspec = pl.BlockSpec((None, p2, w), lambda i: (i, 0, 0))
    cspec = pl.BlockSpec((None, w, p2), lambda i: (i, 0, 0))
    mt, bend, cin, a16r, a16i = pl.pallas_call(
        _ssm_prep_kernel,
        out_shape=[
            jax.ShapeDtypeStruct((g, w, w), BF16),
            jax.ShapeDtypeStruct((g, p2, w), BF16),
            jax.ShapeDtypeStruct((g, w, 2 * p2), BF16),
            jax.ShapeDtypeStruct((gp, 1, p2), F32),
            jax.ShapeDtypeStruct((gp, 1, p2), F32),
        ],
        grid=(gp,),
        in_specs=[colspec, colspec, colspec, bspec, bspec, rowspec, rowspec, rowspec, cspec, cspec],
        out_specs=[
            pl.BlockSpec((2, w, w), lambda i: (i, 0, 0)),
            pl.BlockSpec((2, p2, w), lambda i: (i, 0, 0)),
            pl.BlockSpec((2, w, 2 * p2), lambda i: (i, 0, 0)),
            rowspec, rowspec,
        ],
        compiler_params=_params("arbitrary"),
        name="ssm_prep",
    )(col(lam_re), col(lam_im), col(dt_full), b_t(b_re), b_t(b_im),
      row(lam_re), row(lam_im), row(dt_full), c_t(c_re), c_t(c_im))
    return mt, bend, cin, a16r.reshape(1, g * p), a16i.reshape(1, g * p)


def _step_rows(s):
    return pl.ds(s, CHUNK_TILE, stride=SSM_T)


def _build_ut(u_ref, ut_ref):
    for s in range(SSM_T):
        xt = u_ref[_step_rows(s), :].T.astype(BF16)
        for gl in range(GROUPS_PER_TILE):
            ut_ref[gl, s * SSM_GROUP:(s + 1) * SSM_GROUP, :] = xt[gl * SSM_GROUP:(gl + 1) * SSM_GROUP, :]


def _ssm_state_kernel(u_ref, bend_ref, sre_ref, sim_ref, ut_ref):
    _build_ut(u_ref, ut_ref)
    p = SSM_STATE
    for pr in range(GROUPS_PER_TILE // 2):
        st = [jnp.dot(bend_ref[2 * pr + r], ut_ref[2 * pr + r], preferred_element_type=F32)
              for r in range(2)]
        sre_ref[:, pr * 2 * p:(pr + 1) * 2 * p] = jnp.concatenate([st[0][:p], st[1][:p]], axis=0).T
        sim_ref[:, pr * 2 * p:(pr + 1) * 2 * p] = jnp.concatenate([st[0][p:], st[1][p:]], axis=0).T


def _ssm_state_contrib(uz, bend, jpad):
    g, p2, w = bend.shape
    ntile = g // GROUPS_PER_TILE
    sw = GROUPS_PER_TILE * SSM_STATE
    out = jax.ShapeDtypeStruct((jpad, g * SSM_STATE), F32)
    ospec = pl.BlockSpec((CHUNK_TILE, sw), lambda j, q: (j, q))
    return pl.pallas_call(
        _ssm_state_kernel,
        out_shape=[out, out],
        grid=(jpad // CHUNK_TILE, ntile),
        in_specs=[
            pl.BlockSpec((CHUNK_TILE * SSM_T, LANES), lambda j, q: (j, q)),
            pl.BlockSpec((GROUPS_PER_TILE, p2, w), lambda j, q: (q, 0, 0)),
        ],
        out_specs=[ospec, ospec],
        scratch_shapes=[pltpu.VMEM((GROUPS_PER_TILE, w, CHUNK_TILE), BF16)],
        compiler_params=_params("arbitrary", "arbitrary"),
        name="ssm_state_contrib",
    )(uz, bend)


def _ssm_scan_kernel(sre_ref, sim_ref, ar_ref, ai_ref, h0r_ref, h0i_ref,
                     hpr_ref, hpi_ref, pr_ref, pi_ref, sr_ref, si_ref, hr_sc, hi_sc, *, n_prompt):
    i = pl.program_id(0)
    rb = sre_ref.shape[0]
    ar, ai = ar_ref[...], ai_ref[...]

    @pl.when(i == 0)
    def _():
        hr_sc[...] = jnp.zeros_like(hr_sc)
        hi_sc[...] = jnp.zeros_like(hi_sc)

    @pl.when(i < n_prompt)
    def _():
        def body(j, c):
            row = pl.ds(j, 1)
            hr, hi = hr_sc[...], hi_sc[...]
            hpr_ref[row, :] = hr
            hpi_ref[row, :] = hi
            hr_sc[...] = ar * hr - ai * hi + sre_ref[row, :]
            hi_sc[...] = ar * hi + ai * hr + sim_ref[row, :]
            return c

        lax.fori_loop(0, rb, body, 0)

    @pl.when(i == n_prompt - 1)
    def _():
        pr_ref[...] = hr_sc[...]
        pi_ref[...] = hi_sc[...]

    @pl.when(i == n_prompt)
    def _():
        h0r, h0i = h0r_ref[...], h0i_ref[...]
        hpr_ref[...] = h0r
        hpi_ref[...] = h0i
        sr_ref[...] = ar * h0r - ai * h0i + sre_ref[...]
        si_ref[...] = ar * h0i + ai * h0r + sim_ref[...]

    @pl.when(i > n_prompt)
    def _():
        hpr_ref[...] = jnp.zeros_like(hpr_ref)
        hpi_ref[...] = jnp.zeros_like(hpi_ref)


def _ssm_scan(s_re, s_im, a16r, a16i, h0r, h0i, jp):
    jpad, n = s_re.shape
    rb = h0r.shape[0]
    assert jp % rb == 0 and jpad % rb == 0
    n_prompt = jp // rb
    rows = pl.BlockSpec((rb, n), lambda i: (i, 0))
    const1 = pl.BlockSpec((1, n), lambda i: (0, 0))
    constb = pl.BlockSpec((rb, n), lambda i: (0, 0))
    big = jax.ShapeDtypeStruct((jpad, n), F32)
    one = jax.ShapeDtypeStruct((1, n), F32)
    bat = jax.ShapeDtypeStruct((rb, n), F32)
    return pl.pallas_call(
        functools.partial(_ssm_scan_kernel, n_prompt=n_prompt),
        out_shape=[big, big, one, one, bat, bat],
        grid=(jpad // rb,),
        in_specs=[rows, rows, const1, const1, constb, constb],
        out_specs=[rows, rows, const1, const1, constb, constb],
        scratch_shapes=[pltpu.VMEM((1, n), F32), pltpu.VMEM((1, n), F32)],
        compiler_params=_params("arbitrary"),
        name="ssm_scan",
    )(s_re, s_im, a16r, a16i, h0r, h0i)


def _ssm_out_kernel(u_ref, mt_ref, cin_ref, hpr_ref, hpi_ref, d_ref, o_ref, ut_ref, yt_ref):
    _build_ut(u_ref, ut_ref)
    p2 = 2 * SSM_STATE
    for gl in range(GROUPS_PER_TILE):
        pr = gl // 2
        hp = jnp.concatenate([hpr_ref[:, pr * p2:(pr + 1) * p2], hpi_ref[:, pr * p2:(pr + 1) * p2]],
                             axis=1).astype(BF16)
        yt = jnp.dot(mt_ref[gl], ut_ref[gl], preferred_element_type=F32)
        yt = yt + lax.dot_general(cin_ref[gl], hp, _NT, preferred_element_type=F32)
        for t in range(SSM_T):
            yt_ref[t, gl * SSM_GROUP:(gl + 1) * SSM_GROUP, :] = yt[t * SSM_GROUP:(t + 1) * SSM_GROUP, :]
    d = d_ref[...]
    for t in range(SSM_T):
        y = yt_ref[t].T + d * u_ref[_step_rows(t), :]
        o_ref[_step_rows(t), :] = jax.nn.gelu(y)


def _ssm_output(uz, mt, cin, hp_re, hp_im, d_skip, jpad):
    g, w, _ = mt.shape
    w2 = cin.shape[-1]
    ntile = g // GROUPS_PER_TILE
    sw = GROUPS_PER_TILE * SSM_STATE
    tok = pl.BlockSpec((CHUNK_TILE * SSM_T, LANES), lambda j, q: (j, q))
    hspec = pl.BlockSpec((CHUNK_TILE, sw), lambda j, q: (j, q))
    return pl.pallas_call(
        _ssm_out_kernel,
        out_shape=jax.ShapeDtypeStruct((jpad * SSM_T, g * SSM_GROUP), F32),
        grid=(jpad // CHUNK_TILE, ntile),
        in_specs=[
            tok,
            pl.BlockSpec((GROUPS_PER_TILE, w, w), lambda j, q: (q, 0, 0)),
            pl.BlockSpec((GROUPS_PER_TILE, w, w2), lambda j, q: (q, 0, 0)),
            hspec, hspec,
            pl.BlockSpec((1, LANES), lambda j, q: (0, q)),
        ],
        out_specs=tok,
        scratch_shapes=[
            pltpu.VMEM((GROUPS_PER_TILE, w, CHUNK_TILE), BF16),
            pltpu.VMEM((SSM_T, LANES, CHUNK_TILE), F32),
        ],
        compiler_params=_params("arbitrary", "arbitrary"),
        name="ssm_output",
    )(uz, mt, cin, hp_re, hp_im, d_skip)


def _a_glu_kernel(y_ref, yc_ref, z_ref, w_ref, o_ref, yb_ref):
    @pl.when(pl.program_id(1) == 0)
    def _():
        yb_ref[...] = y_ref[...].astype(BF16)

    gate = jnp.dot(yb_ref[...], w_ref[...], preferred_element_type=F32)
    y2 = yc_ref[...] * jax.nn.sigmoid(gate)
    o_ref[...] = (y2 * jax.nn.silu(z_ref[...])).astype(BF16)


def _a_glu(yg, uz, w_glu, rows, tn):
    tm = ROW_TILE
    e = yg.shape[1]
    zoff = e // tn
    return pl.pallas_call(
        _a_glu_kernel,
        out_shape=jax.ShapeDtypeStruct((rows, e), BF16),
        grid=(rows // tm, e // tn),
        in_specs=[
            pl.BlockSpec((tm, e), lambda m, n: (m, 0)),
            pl.BlockSpec((tm, tn), lambda m, n: (m, n)),
            pl.BlockSpec((tm, tn), lambda m, n: (m, zoff + n)),
            pl.BlockSpec((e, tn), lambda m, n: (0, n)),
        ],
        out_specs=pl.BlockSpec((tm, tn), lambda m, n: (m, n)),
        scratch_shapes=[pltpu.VMEM((tm, e), BF16)],
        compiler_params=_params("arbitrary", "arbitrary"),
        name="a_glu",
    )(yg, yg, uz, w_glu)


def _a_out_kernel(o_ref, w_ref, xp_ref, xs_ref, x1_ref, *, mp):
    m = pl.program_id(0)
    acc = jnp.dot(o_ref[...], w_ref[...], preferred_element_type=F32)

    @pl.when(m < mp)
    def _():
        x1_ref[...] = xp_ref[...] + acc

    @pl.when(m == mp)
    def _():
        x1_ref[...] = xs_ref[...] + acc


def _a_out_proj(o, w_out, xp, xs, tn):
    tm = ROW_TILE
    rows, e = o.shape
    d = w_out.shape[1]
    mp = xp.shape[0] // tm
    return pl.pallas_call(
        functools.partial(_a_out_kernel, mp=mp),
        out_shape=jax.ShapeDtypeStruct((rows, d), F32),
        grid=(rows // tm, d // tn),
        in_specs=[
            pl.BlockSpec((tm, e), lambda m, n: (m, 0)),
            pl.BlockSpec((e, tn), lambda m, n: (0, n)),
            pl.BlockSpec((tm, tn), lambda m, n: (jnp.minimum(m, mp - 1), n)),
            pl.BlockSpec((tm, tn), lambda m, n: (0, n)),
        ],
        out_specs=pl.BlockSpec((tm, tn), lambda m, n: (m, n)),
        compiler_params=_params("arbitrary", "arbitrary"),
        name="a_out_proj",
    )(o, w_out, xp, xs)


def _rope(x, cos, sin):
    outs = []
    for c in range(x.shape[1] // HEAD_DIM):
        xc = x[:, c * HEAD_DIM:(c + 1) * HEAD_DIM]
        outs.append(xc * cos + pltpu.roll(xc, HEAD_DIM // 2, axis=1) * sin)
    return jnp.concatenate(outs, axis=1) if len(outs) > 1 else outs[0]


def _store_heads(o_ref, val, tm):
    for h in range(N_HEADS):
        o_ref[pl.ds(h, tm, stride=N_HEADS), :] = val[:, h * HEAD_DIM:(h + 1) * HEAD_DIM]


def _kv_proj_kernel(x_ref, g_ref, w_ref, cos_ref, sin_ref, k_ref, v_ref, cb_ref, xn_ref, *, tm):
    n = pl.program_id(1)

    @pl.when(n == 0)
    def _():
        xn_ref[...] = _rms_scale(x_ref[...], g_ref[...]).astype(BF16)

    acc = jnp.dot(xn_ref[...], w_ref[...], preferred_element_type=F32)

    @pl.when(n < 2)
    def _():
        r = _rope(acc, cos_ref[...], sin_ref[...])
        _store_heads(k_ref, r, tm)
        cb_ref[...] = r.astype(BF16)

    @pl.when(n >= 2)
    def _():
        _store_heads(v_ref, acc, tm)
        cb_ref[...] = acc.astype(BF16)


def _kv_proj(x1, row0, rows, g, w_perm, cos, sin):
    tm = ROW_TILE
    d = x1.shape[1]
    tn = N_HEADS * HEAD_DIM
    m0 = row0 // tm
    out4 = jax.ShapeDtypeStruct((rows * N_HEADS, 2 * HEAD_DIM), F32)
    return pl.pallas_call(
        functools.partial(_kv_proj_kernel, tm=tm),
        out_shape=[out4, out4, jax.ShapeDtypeStruct((rows, 4 * tn), BF16)],
        grid=(rows // tm, 4),
        in_specs=[
            pl.BlockSpec((tm, d), lambda m, n: (m0 + m, 0)),
            pl.BlockSpec((1, d), lambda m, n: (0, 0)),
            pl.BlockSpec((d, tn), lambda m, n: (0, n)),
            pl.BlockSpec((tm, HEAD_DIM), lambda m, n: (m, 0)),
            pl.BlockSpec((tm, HEAD_DIM), lambda m, n: (m, 0)),
        ],
        out_specs=[
            pl.BlockSpec((tm * N_HEADS, HEAD_DIM), lambda m, n: (m, jnp.minimum(n, 1))),
            pl.BlockSpec((tm * N_HEADS, HEAD_DIM), lambda m, n: (m, jnp.maximum(n - 2, 0))),
            pl.BlockSpec((tm, tn), lambda m, n: (m, n)),
        ],
        scratch_shapes=[pltpu.VMEM((tm, d), BF16)],
        compiler_params=_params("arbitrary", "arbitrary"),
        name="kv_proj",
    )(x1, g, w_perm, cos, sin)


def _qz_proj_kernel(x_ref, g_ref, w_ref, cos_ref, sin_ref, q_ref, z_ref, xn_ref, *, nq):
    n = pl.program_id(1)

    @pl.when(n == 0)
    def _():
        xn_ref[...] = _rms_scale(x_ref[...], g_ref[...]).astype(BF16)

    acc = jnp.dot(xn_ref[...], w_ref[...], preferred_element_type=F32)

    @pl.when(n < nq)
    def _():
        q_ref[...] = _rope(acc, cos_ref[...], sin_ref[...]).astype(BF16)

    @pl.when(n >= nq)
    def _():
        z_ref[...] = acc


def _qz_proj(x1, row0, rows, g, w, cos, sin, tn):
    tm = ROW_TILE
    d = x1.shape[1]
    half = w.shape[1] // 2
    nq = half // tn
    m0 = row0 // tm
    return pl.pallas_call(
        functools.partial(_qz_proj_kernel, nq=nq),
        out_shape=[jax.ShapeDtypeStruct((rows, half), BF16), jax.ShapeDtypeStruct((rows, half), F32)],
        grid=(rows // tm, 2 * nq),
        in_specs=[
            pl.BlockSpec((tm, d), lambda m, n: (m0 + m, 0)),
            pl.BlockSpec((1, d), lambda m, n: (0, 0)),
            pl.BlockSpec((d, tn), lambda m, n: (0, n)),
            pl.BlockSpec((tm, HEAD_DIM), lambda m, n: (m, 0)),
            pl.BlockSpec((tm, HEAD_DIM), lambda m, n: (m, 0)),
        ],
        out_specs=[
            pl.BlockSpec((tm, tn), lambda m, n: (m, jnp.minimum(n, nq - 1))),
            pl.BlockSpec((tm, tn), lambda m, n: (m, jnp.maximum(n - nq, 0))),
        ],
        scratch_shapes=[pltpu.VMEM((tm, d), BF16)],
        compiler_params=_params("arbitrary", "arbitrary"),
        name="qz_proj",
    )(x1, g, w, cos, sin)


def _rope_tables(pos):
    inv_freq = ROPE_THETA ** (-jnp.arange(0, HEAD_DIM, 2, dtype=F32) / HEAD_DIM)
    ang = pos.astype(F32)[:, None] * inv_freq[None, :]
    c, s = jnp.cos(ang), jnp.sin(ang)
    return jnp.concatenate([c, c], axis=-1), jnp.concatenate([-s, s], axis=-1)


def _diff_lambda(lq1_ref, lk1_ref, lq2_ref, lk2_ref, lambda_init):
    s1 = jnp.sum(lq1_ref[...] * lk1_ref[...], axis=-1, keepdims=True)
    s2 = jnp.sum(lq2_ref[...] * lk2_ref[...], axis=-1, keepdims=True)
    return jnp.exp(s1) - jnp.exp(s2) + lambda_init


def _attn_finish(a1, l1, a2, l2, lam, subln, z, lambda_init):
    o = a1 / l1 - lam * (a2 / l2)
    o = _rms_scale(o, subln) * (1.0 - lambda_init)
    return (o * jax.nn.silu(z)).astype(BF16)


def _softmax_update(m_sc, l_sc, a_sc, idx, s, v):
    m_old = m_sc[idx]
    m_new = jnp.maximum(m_old, jnp.max(s, axis=-1, keepdims=True))
    p = jnp.exp(s - m_new)
    alpha = jnp.exp(m_old - m_new)
    l_sc[idx] = alpha * l_sc[idx] + jnp.sum(p, axis=-1, keepdims=True)
    a_sc[idx] = alpha * a_sc[idx] + jnp.dot(p.astype(BF16), v, preferred_element_type=F32)
    m_sc[idx] = m_new


def _attn_prompt_kernel(q_ref, k1_ref, k2_ref, v1_ref, v2_ref, z_ref,
                        lq1_ref, lk1_ref, lq2_ref, lk2_ref, sub_ref,
                        o_ref, m_sc, l_sc, al_sc, a_sc, s_sc, p_sc, *, tq, tk, lambda_init):
    qi = pl.program_id(1)
    scale = HEAD_DIM ** -0.5
    q = q_ref[...]
    qs = (q[:, :HEAD_DIM], q[:, HEAD_DIM:])
    k_refs = (k1_ref, k2_ref)
    nc = tk // LANES
    m_sc[...] = jnp.full(m_sc.shape, NEG, F32)
    l_sc[...] = jnp.zeros(l_sc.shape, F32)
    a_sc[...] = jnp.zeros(a_sc.shape, F32)

    def softmax_strips(n, col0):
        for r in range(tq // STRIP):
            rr = slice(r * STRIP, (r + 1) * STRIP)
            s = s_sc[n, rr, :]
            if col0 is not None:
                visible = ((r * STRIP) // CHUNK + 1) * CHUNK - col0
                if visible < tk:
                    col = lax.broadcasted_iota(jnp.int32, (STRIP, tk), 1)
                    s = jnp.where(col < visible, s, NEG)
            cols = [s[:, c * LANES:(c + 1) * LANES] for c in range(nc)]
            m_cur = jnp.max(functools.reduce(jnp.maximum, cols), axis=-1, keepdims=True)
            m_old = m_sc[n, rr, :]
            m_new = jnp.maximum(m_old, m_cur)
            ps = [jnp.exp((c - m_new) * scale) for c in cols]
            l_cur = jnp.sum(functools.reduce(jnp.add, ps), axis=-1, keepdims=True)
            alpha = jnp.exp((m_old - m_new) * scale)
            l_sc[n, rr, :] = alpha * l_sc[n, rr, :] + l_cur
            m_sc[n, rr, :] = m_new
            al_sc[n, rr, :] = alpha
            for c in range(nc):
                p_sc[n, rr, c * LANES:(c + 1) * LANES] = ps[c].astype(BF16)

    def block(kb, col0):
        rows = pl.ds(pl.multiple_of(kb * tk, tk), tk)
        vblk = jnp.concatenate([v1_ref[rows, :], v2_ref[rows, :]], axis=1)
        for n in range(2):
            s_sc[n] = lax.dot_general(qs[n], k_refs[n][rows, :], _NT, preferred_element_type=F32)
        for n in range(2):
            softmax_strips(n, col0)
            pv = jnp.dot(p_sc[n], vblk, preferred_element_type=F32)
            alpha = al_sc[n]
            a_sc[n] = a_sc[n] * jnp.concatenate([alpha, alpha], axis=1) + pv

    r = tq // tk

    def body(kb, c):
        block(kb, None)
        return c

    lax.fori_loop(0, qi * r, body, 0)
    for d in range(r):
        block(qi * r + d, d * tk)

    lam = _diff_lambda(lq1_ref, lk1_ref, lq2_ref, lk2_ref, lambda_init)
    l1 = jnp.concatenate([l_sc[0], l_sc[0]], axis=1)
    l2 = jnp.concatenate([l_sc[1], l_sc[1]], axis=1)
    o_ref[...] = _attn_finish(a_sc[0], l1, a_sc[1], l2, lam, sub_ref[...], z_ref[...], lambda_init)


def _attn_prompt(qb, z32, kvb, lq1, lk1, lq2, lk2, subln, lambda_init, tq, tk):
    rows = qb.shape[0]
    hw = 2 * HEAD_DIM
    assert tq % tk == 0 and tq % STRIP == 0 and CHUNK % STRIP == 0
    vec = pl.BlockSpec((1, HEAD_DIM), lambda h, i: (0, 0))
    kcol = lambda c: pl.BlockSpec((rows, HEAD_DIM), lambda h, i: (0, c * N_HEADS + h))
    stat = pltpu.VMEM((2, tq, LANES), F32)
    return pl.pallas_call(
        functools.partial(_attn_prompt_kernel, tq=tq, tk=tk, lambda_init=lambda_init),
        out_shape=jax.ShapeDtypeStruct((rows, N_HEADS * hw), BF16),
        grid=(N_HEADS, rows // tq),
        in_specs=[
            pl.BlockSpec((tq, hw), lambda h, i: (i, h)),
            kcol(0), kcol(1), kcol(2), kcol(3),
            pl.BlockSpec((tq, hw), lambda h, i: (i, h)),
            vec, vec, vec, vec,
            pl.BlockSpec((1, hw), lambda h, i: (0, 0)),
        ],
        out_specs=pl.BlockSpec((tq, hw), lambda h, i: (i, h)),
        scratch_shapes=[
            stat, stat, stat,
            pltpu.VMEM((2, tq, hw), F32),
            pltpu.VMEM((2, tq, tk), F32),
            pltpu.VMEM((2, tq, tk), BF16),
        ],
        compiler_params=_params("arbitrary", "arbitrary"),
        name="attn_prompt",
    )(qb, kvb, kvb, kvb, kvb, z32, lq1, lk1, lq2, lk2, subln)


def _attn_sample_kernel(q_ref, ck1_ref, ck2_ref, cv1_ref, cv2_ref, kvn_ref, z_ref,
                        lq1_ref, lk1_ref, lq2_ref, lk2_ref, sub_ref,
                        o_ref, m_sc, l_sc, a_sc, *, tkv, nkb, past_len, lambda_init):
    kb = pl.program_id(1)
    scale = HEAD_DIM ** -0.5
    t = q_ref.shape[0]
    hw = 2 * HEAD_DIM
    ck_refs = (ck1_ref, ck2_ref)
    assert (past_len - 1) // CHUNK <= past_len // CHUNK

    @pl.when(kb == 0)
    def _():
        m_sc[...] = jnp.full(m_sc.shape, NEG, F32)
        l_sc[...] = jnp.zeros(l_sc.shape, F32)
        a_sc[...] = jnp.zeros(a_sc.shape, F32)

    for h in range(N_HEADS):
        rows = pl.ds(h, tkv, stride=N_HEADS)
        v = jnp.concatenate([cv1_ref[rows, :], cv2_ref[rows, :]], axis=1).astype(BF16)
        for n in range(2):
            qn = q_ref[:, (2 * h + n) * HEAD_DIM:(2 * h + n + 1) * HEAD_DIM]
            k = ck_refs[n][rows, :].astype(BF16)
            s = lax.dot_general(qn, k, _NT, preferred_element_type=F32) * scale
            _softmax_update(m_sc, l_sc, a_sc, 2 * h + n, s, v)

    @pl.when(kb == nkb - 1)
    def _():
        row = lax.broadcasted_iota(jnp.int32, (t, t), 0) + past_len
        col = lax.broadcasted_iota(jnp.int32, (t, t), 1) + past_len
        new_mask = (col // CHUNK) <= (row // CHUNK)
        lam = _diff_lambda(lq1_ref, lk1_ref, lq2_ref, lk2_ref, lambda_init)
        sub = sub_ref[...]
        col_blk = lambda c, h: slice((c * N_HEADS + h) * HEAD_DIM, (c * N_HEADS + h + 1) * HEAD_DIM)
        for h in range(N_HEADS):
            vn = jnp.concatenate([kvn_ref[:, col_blk(2, h)], kvn_ref[:, col_blk(3, h)]], axis=1)
            for n in range(2):
                qn = q_ref[:, (2 * h + n) * HEAD_DIM:(2 * h + n + 1) * HEAD_DIM]
                s = lax.dot_general(qn, kvn_ref[:, col_blk(n, h)], _NT,
                                    preferred_element_type=F32) * scale
                _softmax_update(m_sc, l_sc, a_sc, 2 * h + n, jnp.where(new_mask, s, NEG), vn)
            o_ref[:, h * hw:(h + 1) * hw] = _attn_finish(
                a_sc[2 * h], l_sc[2 * h], a_sc[2 * h + 1], l_sc[2 * h + 1], lam, sub,
                z_ref[:, h * hw:(h + 1) * hw], lambda_init)


def _attn_sample(qb, z32, kvb, cache_k2, cache_v2, lq1, lk1, lq2, lk2, subln, lambda_init, t, tkv):
    bsz, rows8, hw = cache_k2.shape
    past_len = rows8 // N_HEADS
    nkb = past_len // tkv
    vec = pl.BlockSpec((1, HEAD_DIM), lambda b, k: (0, 0))
    cache = lambda c: pl.BlockSpec((None, tkv * N_HEADS, HEAD_DIM), lambda b, k: (b, k, c))
    full = lambda a: pl.BlockSpec((t, a.shape[1]), lambda b, k: (b, 0))
    return pl.pallas_call(
        functools.partial(_attn_sample_kernel, tkv=tkv, nkb=nkb, past_len=past_len,
                          lambda_init=lambda_init),
        out_shape=jax.ShapeDtypeStruct((bsz * t, N_HEADS * hw), BF16),
        grid=(bsz, nkb),
        in_specs=[
            full(qb), cache(0), cache(1), cache(0), cache(1), full(kvb), full(z32),
            vec, vec, vec, vec,
            pl.BlockSpec((1, hw), lambda b, k: (0, 0)),
        ],
        out_specs=pl.BlockSpec((t, N_HEADS * hw), lambda b, k: (b, 0)),
        scratch_shapes=[
            pltpu.VMEM((2 * N_HEADS, t, 1), F32),
            pltpu.VMEM((2 * N_HEADS, t, 1), F32),
            pltpu.VMEM((2 * N_HEADS, t, hw), F32),
        ],
        compiler_params=_params("arbitrary", "arbitrary"),
        name="attn_sample",
    )(qb, cache_k2, cache_k2, cache_v2, cache_v2, kvb, z32, lq1, lk1, lq2, lk2, subln)


def _b_out_kernel(o_ref, w_ref, x_ref, g_ref, y_ref):
    acc = jnp.dot(o_ref[...], w_ref[...], preferred_element_type=F32)
    y_ref[...] = _rms_scale(x_ref[...] + acc, g_ref[...])


def _b_out_proj(og, w_out, x1, row0, g, tm):
    rows, e = og.shape
    d = w_out.shape[1]
    m0 = row0 // tm
    return pl.pallas_call(
        _b_out_kernel,
        out_shape=jax.ShapeDtypeStruct((rows, d), F32),
        grid=(rows // tm,),
        in_specs=[
            pl.BlockSpec((tm, e), lambda m: (m, 0)),
            pl.BlockSpec((e, d), lambda m: (0, 0)),
            pl.BlockSpec((tm, d), lambda m: (m0 + m, 0)),
            pl.BlockSpec((1, d), lambda m: (0, 0)),
        ],
        out_specs=pl.BlockSpec((tm, d), lambda m: (m, 0)),
        compiler_params=_params("arbitrary"),
        name="b_out_proj",
    )(og, w_out, x1, g)


def _permute_kv_columns(w_kv):
    d = w_kv.shape[0]
    w = w_kv.reshape(d, 2, N_HEADS, 2, HEAD_DIM).transpose(0, 1, 3, 2, 4)
    return w.reshape(d, 4 * N_HEADS * HEAD_DIM)


def kernel(x_prompt, x_sample, state_ssm_re, state_ssm_im, cache_k, cache_v, a_norm, a_w_in, a_lambda_re, a_lambda_im, a_log_dt, a_b_re, a_b_im, a_c_re, a_c_im, a_d, a_w_glu, a_w_out, kv_norm, w_kv, b_norm, b_w_in, b_lambda_q1, b_lambda_k1, b_lambda_q2, b_lambda_k2, b_subln, b_w_out, final_norm):
    bp, seq, d = x_prompt.shape
    bs, t_s, _ = x_sample.shape
    past_len = cache_k.shape[1]
    n_a, n_b = a_norm.shape[0], b_norm.shape[0]
    g_cnt, p = a_lambda_re.shape[1:]
    assert bp == 1 and n_a == 1 and n_b == 1
    assert t_s == SSM_T and seq % (SSM_T * bs) == 0 and seq % CHUNK == 0
    assert cache_k.shape[2:] == (N_HEADS, 2 * HEAD_DIM)

    jp, js = seq // SSM_T, bs
    jpad = -(-(jp + js) // CHUNK_TILE) * CHUNK_TILE
    rows_s = bs * t_s
    rows_a = seq + rows_s

    xp = x_prompt.reshape(seq, d)
    xs = x_sample.reshape(rows_s, d)
    uz = _a_in_proj(xp, xs, a_norm[0].reshape(1, d), a_w_in[0].astype(BF16), jpad * SSM_T, tn=1024)
    mt, bend, cin, a16r, a16i = _ssm_prep(a_lambda_re[0], a_lambda_im[0], a_log_dt[0],
                                          a_b_re[0], a_b_im[0], a_c_re[0], a_c_im[0])
    s_re, s_im = _ssm_state_contrib(uz, bend, jpad)
    h0r = state_ssm_re[:, 0].reshape(bs, g_cnt * p)
    h0i = state_ssm_im[:, 0].reshape(bs, g_cnt * p)
    hp_re, hp_im, pre, pim, sre, sim = _ssm_scan(s_re, s_im, a16r, a16i, h0r, h0i, jp)
    yg = _ssm_output(uz, mt, cin, hp_re, hp_im, a_d[0].reshape(1, -1), jpad)
    o_a = _a_glu(yg, uz, a_w_glu[0].astype(BF16), rows_a, tn=512)
    x1 = _a_out_proj(o_a, a_w_out[0].astype(BF16), xp, xs, tn=512)

    lambda_init = 0.8 - 0.6 * math.exp(-0.3 * n_a)
    w_kv_b = _permute_kv_columns(w_kv).astype(BF16)
    w_q_b = b_w_in[0].astype(BF16)
    kv_g = kv_norm.reshape(1, d)
    b_g = b_norm[0].reshape(1, d)
    pos_p = jnp.arange(seq, dtype=jnp.int32)
    pos_s = jnp.tile(past_len + jnp.arange(t_s, dtype=jnp.int32), bs)
    cos_p, sin_p = _rope_tables(pos_p)
    cos_s, sin_s = _rope_tables(pos_s)
    k_p, v_p, kvb_p = _kv_proj(x1, 0, seq, kv_g, w_kv_b, cos_p, sin_p)
    k_s, v_s, kvb_s = _kv_proj(x1, seq, rows_s, kv_g, w_kv_b, cos_s, sin_s)
    qb_p, z_p = _qz_proj(x1, 0, seq, b_g, w_q_b, cos_p, sin_p, tn=512)
    qb_s, z_s = _qz_proj(x1, seq, rows_s, b_g, w_q_b, cos_s, sin_s, tn=512)

    vecs = [a[0].reshape(1, HEAD_DIM) for a in (b_lambda_q1, b_lambda_k1, b_lambda_q2, b_lambda_k2)]
    subln = b_subln[0].reshape(1, 2 * HEAD_DIM)
    hw = 2 * HEAD_DIM
    og_p = _attn_prompt(qb_p, z_p, kvb_p, *vecs, subln, lambda_init, tq=512, tk=512)
    og_s = _attn_sample(qb_s, z_s, kvb_s, cache_k.reshape(bs, past_len * N_HEADS, hw),
                        cache_v.reshape(bs, past_len * N_HEADS, hw), *vecs, subln, lambda_init,
                        t_s, tkv=1024)
    w_o_b = b_w_out[0].astype(BF16)
    fg = final_norm.reshape(1, d)
    y_p = _b_out_proj(og_p, w_o_b, x1, 0, fg, 256)
    y_s = _b_out_proj(og_s, w_o_b, x1, seq, fg, 256)

    return (y_p.reshape(bp, seq, d), y_s.reshape(bs, t_s, d),
            pre.reshape(bp, n_a, g_cnt, p), pim.reshape(bp, n_a, g_cnt, p),
            k_p.reshape(bp, seq, N_HEADS, hw), v_p.reshape(bp, seq, N_HEADS, hw),
            sre.reshape(bs, n_a, g_cnt, p), sim.reshape(bs, n_a, g_cnt, p),
            k_s.reshape(bs, t_s, N_HEADS, hw), v_s.reshape(bs, t_s, N_HEADS, hw))
```

```python
import functools
import math

import jax
import jax.numpy as jnp
from jax import lax
from jax.experimental import pallas as pl
from jax.experimental.pallas import tpu as pltpu

F32 = jnp.float32
BF16 = jnp.bfloat16

CHUNK = 64
HEAD_DIM = 128
N_HEADS = 8
SSM_GROUP = 16
SSM_STATE = 64
SSM_T = 16
ROPE_THETA = 10000.0
EPS = 1e-6
NEG = -1e30

LANES = 128
SUBLANES = 8
GROUPS_PER_TILE = LANES // SSM_GROUP
CHUNK_TILE = 128
ROW_TILE = 512
STRIP = 32
VMEM_LIMIT = 48 * 1024 * 1024

_NT = (((1,), (1,)), ((), ()))


def _params(*sem):
    return pltpu.CompilerParams(dimension_semantics=sem, vmem_limit_bytes=VMEM_LIMIT)


def _resident(shape, index_map):
    return pl.BlockSpec(shape, index_map, pipeline_mode=pl.Buffered(1))


def _weight_column_specs(w, tn):
    d, n_out = w.shape
    nb = n_out // tn
    specs = [_resident((d, tn), functools.partial(lambda c, *_: (0, c), c)) for c in range(nb)]
    return specs, [w] * nb


def _rms_scale(x, g):
    ms = jnp.mean(x * x, axis=-1, keepdims=True)
    return x * lax.rsqrt(ms + EPS) * g


def _a_in_kernel(xp_ref, xs_ref, g_ref, *refs, mp, nb):
    w_refs, (o_ref, xn_ref) = refs[:nb], refs[nb:]
    m, n = pl.program_id(0), pl.program_id(1)

    @pl.when((n == 0) & (m < mp))
    def _():
        xn_ref[...] = _rms_scale(xp_ref[...], g_ref[...]).astype(BF16)

    @pl.when((n == 0) & (m == mp))
    def _():
        xn_ref[...] = _rms_scale(xs_ref[...], g_ref[...]).astype(BF16)

    for c in range(nb):
        @pl.when((m <= mp) & (n == c))
        def _():
            o_ref[...] = jnp.dot(xn_ref[...], w_refs[c][...], preferred_element_type=F32)

    @pl.when(m > mp)
    def _():
        o_ref[...] = jnp.zeros(o_ref.shape, F32)


def _a_in_proj(xp, xs, g, w, rows_pad, tn):
    tm = ROW_TILE
    d = g.shape[-1]
    n_out = w.shape[1]
    mp = xp.shape[0] // tm
    nb = n_out // tn
    assert xs.shape[0] == tm and xp.shape[0] % tm == 0 and rows_pad % tm == 0
    w_specs, w_args = _weight_column_specs(w, tn)
    return pl.pallas_call(
        functools.partial(_a_in_kernel, mp=mp, nb=nb),
        out_shape=jax.ShapeDtypeStruct((rows_pad, n_out), F32),
        grid=(rows_pad // tm, nb),
        in_specs=[
            pl.BlockSpec((tm, d), lambda m, n: (jnp.minimum(m, mp - 1), 0)),
            _resident((tm, d), lambda m, n: (0, 0)),
            _resident((1, d), lambda m, n: (0, 0)),
            *w_specs,
        ],
        out_specs=pl.BlockSpec((tm, tn), lambda m, n: (m, n)),
        scratch_shapes=[pltpu.VMEM((tm, d), BF16)],
        compiler_params=_params("arbitrary", "arbitrary"),
        name="a_in_proj",
    )(xp, xs, g, *w_args)


def _ssm_prep_kernel(lrc_ref, lic_ref, dtc_ref, bre_ref, bim_ref, lrr_ref, lir_ref, dtr_ref,
                     cre_ref, cim_ref, mt_ref, bend_ref, cin_ref, a16r_ref, a16i_ref):
    p2 = 2 * SSM_STATE
    w = SSM_T * SSM_GROUP

    lr, li = lrc_ref[...], lic_ref[...]
    dt = jnp.exp(dtc_ref[...])
    zr, zi = lr * dt, li * dt
    mag = jnp.exp(zr)
    n_re, n_im = mag * jnp.cos(zi) - 1.0, mag * jnp.sin(zi)
    den = lr * lr + li * li
    cf_re = (n_re * lr + n_im * li) / den
    cf_im = (n_im * lr - n_re * li) / den
    b_re, b_im = bre_ref[...], bim_ref[...]
    bb_re = cf_re * b_re - cf_im * b_im
    bb_im = cf_re * b_im + cf_im * b_re
    lane = lax.broadcasted_iota(jnp.int32, (p2, w), 1)
    e_end = (SSM_T - 1 - lane // SSM_GROUP).astype(F32)
    pm = jnp.exp(zr * e_end)
    pw_re, pw_im = pm * jnp.cos(zi * e_end), pm * jnp.sin(zi * e_end)
    end_re = pw_re * bb_re - pw_im * bb_im
    end_im = pw_re * bb_im + pw_im * bb_re
    for r in range(2):
        sl = slice(r * SSM_STATE, (r + 1) * SSM_STATE)
        bend_ref[r] = jnp.concatenate([end_re[sl], end_im[sl]], axis=0).astype(BF16)
    bst = jnp.concatenate([bb_re, bb_im], axis=0)

    lr2, li2 = lrr_ref[...], lir_ref[...]
    dt2 = jnp.exp(dtr_ref[...])
    zr2, zi2 = lr2 * dt2, li2 * dt2
    c_re, c_im = cre_ref[...], cim_ref[...]
    tau = (lax.broadcasted_iota(jnp.int32, (w, p2), 0) // SSM_GROUP).astype(F32)

    def c_times_power(ex):
        m = jnp.exp(zr2 * ex)
        q_re, q_im = m * jnp.cos(zi2 * ex), m * jnp.sin(zi2 * ex)
        return c_re * q_re - c_im * q_im, c_re * q_im + c_im * q_re

    g_re, g_im = c_times_power(tau)
    ci_re, ci_im = c_times_power(tau + 1.0)
    lane2 = lax.broadcasted_iota(jnp.int32, (w, p2), 1)
    sblk = lax.broadcasted_iota(jnp.int32, (w, w), 1) // SSM_GROUP
    for r in range(2):
        own = (lane2 // SSM_STATE) == r
        lhs = jnp.concatenate([jnp.where(own, g_re, 0.0), jnp.where(own, -g_im, 0.0)], axis=1)
        gen = jnp.dot(lhs, bst, preferred_element_type=F32)
        k = 0
        while (SSM_GROUP << k) < w:
            sh = SSM_GROUP << k
            shifted = jnp.concatenate([jnp.zeros((sh, w), F32), gen[:w - sh]], axis=0)
            gen = jnp.where(((sblk >> k) & 1) == 1, shifted, gen)
            k += 1
        mt_ref[r] = gen.astype(BF16)
        cin_ref[r] = jnp.concatenate(
            [jnp.where(own, ci_re, 0.0), jnp.where(own, -ci_im, 0.0)], axis=1).astype(BF16)

    m16 = jnp.exp(zr2 * float(SSM_T))
    a16r_ref[...] = m16 * jnp.cos(zi2 * float(SSM_T))
    a16i_ref[...] = m16 * jnp.sin(zi2 * float(SSM_T))


def _ssm_prep(lam_re, lam_im, log_dt, b_re, b_im, c_re, c_im):
    g, p = lam_re.shape
    gp = g // 2
    p2 = 2 * p
    w = SSM_T * SSM_GROUP
    col = lambda a: a.reshape(gp, p2, 1)
    row = lambda a: a.reshape(gp, 1, p2)
    dt_full = jnp.broadcast_to(log_dt[:, None], (g, p))
    b_t = lambda a: jnp.tile(a.reshape(gp, p2, SSM_GROUP), (1, 1, SSM_T))
    c_t = lambda a: jnp.tile(
        a.reshape(gp, 2, SSM_GROUP, p).transpose(0, 2, 1, 3).reshape(gp, SSM_GROUP, p2), (1, SSM_T, 1))
    colspec = pl.BlockSpec((None, p2, 1), lambda i: (i, 0, 0))
    rowspec = pl.BlockSpec((None, 1, p2), lambda i: (i, 0, 0))
    bspec = pl.BlockSpec((None, p2, w), lambda i: (i, 0, 0))
    cspec = pl.BlockSpec((None, w, p2), lambda i: (i, 0, 0))
    mt, bend, cin, a16r, a16i = pl.pallas_call(
        _ssm_prep_kernel,
        out_shape=[
            jax.ShapeDtypeStruct((g, w, w), BF16),
            jax.ShapeDtypeStruct((g, p2, w), BF16),
            jax.ShapeDtypeStruct((g, w, 2 * p2), BF16),
            jax.ShapeDtypeStruct((gp, 1, p2), F32),
            jax.ShapeDtypeStruct((gp, 1, p2), F32),
        ],
        grid=(gp,),
        in_specs=[colspec, colspec, colspec, bspec, bspec, rowspec, rowspec, rowspec, cspec, cspec],
        out_specs=[
            pl.BlockSpec((2, w, w), lambda i: (i, 0, 0)),
            pl.BlockSpec((2, p2, w), lambda i: (i, 0, 0)),
            pl.BlockSpec((2, w, 2 * p2), lambda i: (i, 0, 0)),
            rowspec, rowspec,
        ],
        compiler_params=_params("arbitrary"),
        name="ssm_prep",
    )(col(lam_re), col(lam_im), col(dt_full), b_t(b_re), b_t(b_im),
      row(lam_re), row(lam_im), row(dt_full), c_t(c_re), c_t(c_im))
    return mt, bend, cin, a16r.reshape(1, g * p), a16i.reshape(1, g * p)


def _step_rows(s):
    return pl.ds(s, CHUNK_TILE, stride=SSM_T)


def _build_ut(u_ref, ut_ref):
    for s in range(SSM_T):
        xt = u_ref[_step_rows(s), :].T.astype(BF16)
        for gl in range(GROUPS_PER_TILE):
            ut_ref[gl, s * SSM_GROUP:(s + 1) * SSM_GROUP, :] = xt[gl * SSM_GROUP:(gl + 1) * SSM_GROUP, :]


def _ssm_state_kernel(u_ref, bend_ref, sre_ref, sim_ref, ut_ref):
    _build_ut(u_ref, ut_ref)
    p = SSM_STATE
    for pr in range(GROUPS_PER_TILE // 2):
        st = [jnp.dot(bend_ref[2 * pr + r], ut_ref[2 * pr + r], preferred_element_type=F32)
              for r in range(2)]
        sre_ref[:, pr * 2 * p:(pr + 1) * 2 * p] = jnp.concatenate([st[0][:p], st[1][:p]], axis=0).T
        sim_ref[:, pr * 2 * p:(pr + 1) * 2 * p] = jnp.concatenate([st[0][p:], st[1][p:]], axis=0).T


def _ssm_state_contrib(uz, bend, jpad):
    g, p2, w = bend.shape
    ntile = g // GROUPS_PER_TILE
    sw = GROUPS_PER_TILE * SSM_STATE
    out = jax.ShapeDtypeStruct((jpad, g * SSM_STATE), F32)
    ospec = pl.BlockSpec((CHUNK_TILE, sw), lambda j, q: (j, q))
    return pl.pallas_call(
        _ssm_state_kernel,
        out_shape=[out, out],
        grid=(jpad // CHUNK_TILE, ntile),
        in_specs=[
            pl.BlockSpec((CHUNK_TILE * SSM_T, LANES), lambda j, q: (j, q)),
            pl.BlockSpec((GROUPS_PER_TILE, p2, w), lambda j, q: (q, 0, 0)),
        ],
        out_specs=[ospec, ospec],
        scratch_shapes=[pltpu.VMEM((GROUPS_PER_TILE, w, CHUNK_TILE), BF16)],
        compiler_params=_params("arbitrary", "arbitrary"),
        name="ssm_state_contrib",
    )(uz, bend)


def _ssm_scan_kernel(sre_ref, sim_ref, ar_ref, ai_ref, h0r_ref, h0i_ref,
                     hpr_ref, hpi_ref, pr_ref, pi_ref, sr_ref, si_ref, hr_sc, hi_sc, *, n_prompt):
    i = pl.program_id(0)
    rb = sre_ref.shape[0]
    ar, ai = ar_ref[...], ai_ref[...]

    @pl.when(i == 0)
    def _():
        hr_sc[...] = jnp.zeros_like(hr_sc)
        hi_sc[...] = jnp.zeros_like(hi_sc)

    @pl.when(i < n_prompt)
    def _():
        def body(j, c):
            row = pl.ds(j, 1)
            hr, hi = hr_sc[...], hi_sc[...]
            hpr_ref[row, :] = hr
            hpi_ref[row, :] = hi
            hr_sc[...] = ar * hr - ai * hi + sre_ref[row, :]
            hi_sc[...] = ar * hi + ai * hr + sim_ref[row, :]
            return c

        lax.fori_loop(0, rb, body, 0)

    @pl.when(i == n_prompt - 1)
    def _():
        pr_ref[...] = hr_sc[...]
        pi_ref[...] = hi_sc[...]

    @pl.when(i == n_prompt)
    def _():
        h0r, h0i = h0r_ref[...], h0i_ref[...]
        hpr_ref[...] = h0r
        hpi_ref[...] = h0i
        sr_ref[...] = ar * h0r - ai * h0i + sre_ref[...]
        si_ref[...] = ar * h0i + ai * h0r + sim_ref[...]

    @pl.when(i > n_prompt)
    def _():
        hpr_ref[...] = jnp.zeros_like(hpr_ref)
        hpi_ref[...] = jnp.zeros_like(hpi_ref)


def _ssm_scan(s_re, s_im, a16r, a16i, h0r, h0i, jp):
    jpad, n = s_re.shape
    rb = h0r.shape[0]
    assert jp % rb == 0 and jpad % rb == 0
    n_prompt = jp // rb
    rows = pl.BlockSpec((rb, n), lambda i: (i, 0))
    const1 = pl.BlockSpec((1, n), lambda i: (0, 0))
    constb = pl.BlockSpec((rb, n), lambda i: (0, 0))
    big = jax.ShapeDtypeStruct((jpad, n), F32)
    one = jax.ShapeDtypeStruct((1, n), F32)
    bat = jax.ShapeDtypeStruct((rb, n), F32)
    return pl.pallas_call(
        functools.partial(_ssm_scan_kernel, n_prompt=n_prompt),
        out_shape=[big, big, one, one, bat, bat],
        grid=(jpad // rb,),
        in_specs=[rows, rows, const1, const1, constb, constb],
        out_specs=[rows, rows, const1, const1, constb, constb],
        scratch_shapes=[pltpu.VMEM((1, n), F32), pltpu.VMEM((1, n), F32)],
        compiler_params=_params("arbitrary"),
        name="ssm_scan",
    )(s_re, s_im, a16r, a16i, h0r, h0i)


def _ssm_out_kernel(u_ref, mt_ref, cin_ref, hpr_ref, hpi_ref, d_ref, o_ref, ut_ref, yt_ref):
    _build_ut(u_ref, ut_ref)
    p2 = 2 * SSM_STATE
    for gl in range(GROUPS_PER_TILE):
        pr = gl // 2
        hp = jnp.concatenate([hpr_ref[:, pr * p2:(pr + 1) * p2], hpi_ref[:, pr * p2:(pr + 1) * p2]],
                             axis=1).astype(BF16)
        yt = jnp.dot(mt_ref[gl], ut_ref[gl], preferred_element_type=F32)
        yt = yt + lax.dot_general(cin_ref[gl], hp, _NT, preferred_element_type=F32)
        for t in range(SSM_T):
            yt_ref[t, gl * SSM_GROUP:(gl + 1) * SSM_GROUP, :] = yt[t * SSM_GROUP:(t + 1) * SSM_GROUP, :]
    d = d_ref[...]
    for t in range(SSM_T):
        y = yt_ref[t].T + d * u_ref[_step_rows(t), :]
        o_ref[_step_rows(t), :] = jax.nn.gelu(y)


def _ssm_output(uz, mt, cin, hp_re, hp_im, d_skip, jpad):
    g, w, _ = mt.shape
    w2 = cin.shape[-1]
    ntile = g // GROUPS_PER_TILE
    sw = GROUPS_PER_TILE * SSM_STATE
    tok = pl.BlockSpec((CHUNK_TILE * SSM_T, LANES), lambda j, q: (j, q))
    hspec = pl.BlockSpec((CHUNK_TILE, sw), lambda j, q: (j, q))
    return pl.pallas_call(
        _ssm_out_kernel,
        out_shape=jax.ShapeDtypeStruct((jpad * SSM_T, g * SSM_GROUP), F32),
        grid=(jpad // CHUNK_TILE, ntile),
        in_specs=[
            tok,
            pl.BlockSpec((GROUPS_PER_TILE, w, w), lambda j, q: (q, 0, 0)),
            pl.BlockSpec((GROUPS_PER_TILE, w, w2), lambda j, q: (q, 0, 0)),
            hspec, hspec,
            pl.BlockSpec((1, LANES), lambda j, q: (0, q)),
        ],
        out_specs=tok,
        scratch_shapes=[
            pltpu.VMEM((GROUPS_PER_TILE, w, CHUNK_TILE), BF16),
            pltpu.VMEM((SSM_T, LANES, CHUNK_TILE), F32),
        ],
        compiler_params=_params("arbitrary", "arbitrary"),
        name="ssm_output",
    )(uz, mt, cin, hp_re, hp_im, d_skip)


def _a_glu_kernel(y_ref, z_ref, w_ref, o_ref):
    y = y_ref[...]
    gate = jnp.dot(y.astype(BF16), w_ref[...], preferred_element_type=F32)
    y2 = y * jax.nn.sigmoid(gate)
    o_ref[...] = (y2 * jax.nn.silu(z_ref[...])).astype(BF16)


def _a_glu(yg, uz, w_glu, rows):
    tm = ROW_TILE
    e = yg.shape[1]
    return pl.pallas_call(
        _a_glu_kernel,
        out_shape=jax.ShapeDtypeStruct((rows, e), BF16),
        grid=(rows // tm,),
        in_specs=[
            pl.BlockSpec((tm, e), lambda m: (m, 0)),
            pl.BlockSpec((tm, e), lambda m: (m, 1)),
            _resident((e, e), lambda m: (0, 0)),
        ],
        out_specs=pl.BlockSpec((tm, e), lambda m: (m, 0)),
        compiler_params=_params("arbitrary"),
        name="a_glu",
    )(yg, uz, w_glu)


def _a_out_kernel(o_ref, w_ref, xp_ref, xs_ref, x1_ref, *, mp):
    m = pl.program_id(0)
    acc = jnp.dot(o_ref[...], w_ref[...], preferred_element_type=F32)

    @pl.when(m < mp)
    def _():
        x1_ref[...] = xp_ref[...] + acc

    @pl.when(m == mp)
    def _():
        x1_ref[...] = xs_ref[...] + acc


def _a_out_proj(o, w_out, xp, xs):
    tm = ROW_TILE
    rows, e = o.shape
    d = w_out.shape[1]
    mp = xp.shape[0] // tm
    return pl.pallas_call(
        functools.partial(_a_out_kernel, mp=mp),
        out_shape=jax.ShapeDtypeStruct((rows, d), F32),
        grid=(rows // tm,),
        in_specs=[
            pl.BlockSpec((tm, e), lambda m: (m, 0)),
            _resident((e, d), lambda m: (0, 0)),
            pl.BlockSpec((tm, d), lambda m: (jnp.minimum(m, mp - 1), 0)),
            _resident((tm, d), lambda m: (0, 0)),
        ],
        out_specs=pl.BlockSpec((tm, d), lambda m: (m, 0)),
        compiler_params=_params("arbitrary"),
        name="a_out_proj",
    )(o, w_out, xp, xs)


def _rope(x, cos, sin):
    outs = []
    for c in range(x.shape[1] // HEAD_DIM):
        xc = x[:, c * HEAD_DIM:(c + 1) * HEAD_DIM]
        outs.append(xc * cos + pltpu.roll(xc, HEAD_DIM // 2, axis=1) * sin)
    return jnp.concatenate(outs, axis=1) if len(outs) > 1 else outs[0]


def _store_heads(o_ref, val, tm):
    for h in range(N_HEADS):
        o_ref[pl.ds(h, tm, stride=N_HEADS), :] = val[:, h * HEAD_DIM:(h + 1) * HEAD_DIM]


def _kv_proj_kernel(x_ref, g_ref, *refs, tm):
    w_refs, (cos_ref, sin_ref, k_ref, v_ref, cb_ref, xn_ref) = refs[:4], refs[4:]
    n = pl.program_id(1)

    @pl.when(n == 0)
    def _():
        xn_ref[...] = _rms_scale(x_ref[...], g_ref[...]).astype(BF16)

    for c in range(4):
        @pl.when(n == c)
        def _():
            acc = jnp.dot(xn_ref[...], w_refs[c][...], preferred_element_type=F32)
            if c < 2:
                acc = _rope(acc, cos_ref[...], sin_ref[...])
            _store_heads(k_ref if c < 2 else v_ref, acc, tm)
            cb_ref[...] = acc.astype(BF16)


def _kv_proj(x1, row0, rows, g, w_perm, cos, sin):
    tm = ROW_TILE
    d = x1.shape[1]
    tn = N_HEADS * HEAD_DIM
    m0 = row0 // tm
    out4 = jax.ShapeDtypeStruct((rows * N_HEADS, 2 * HEAD_DIM), F32)
    w_specs, w_args = _weight_column_specs(w_perm, tn)
    assert len(w_specs) == 4
    return pl.pallas_call(
        functools.partial(_kv_proj_kernel, tm=tm),
        out_shape=[out4, out4, jax.ShapeDtypeStruct((rows, 4 * tn), BF16)],
        grid=(rows // tm, 4),
        in_specs=[
            pl.BlockSpec((tm, d), lambda m, n: (m0 + m, 0)),
            _resident((1, d), lambda m, n: (0, 0)),
            *w_specs,
            pl.BlockSpec((tm, HEAD_DIM), lambda m, n: (m, 0)),
            pl.BlockSpec((tm, HEAD_DIM), lambda m, n: (m, 0)),
        ],
        out_specs=[
            pl.BlockSpec((tm * N_HEADS, HEAD_DIM), lambda m, n: (m, jnp.minimum(n, 1))),
            pl.BlockSpec((tm * N_HEADS, HEAD_DIM), lambda m, n: (m, jnp.maximum(n - 2, 0))),
            pl.BlockSpec((tm, tn), lambda m, n: (m, n)),
        ],
        scratch_shapes=[pltpu.VMEM((tm, d), BF16)],
        compiler_params=_params("arbitrary", "arbitrary"),
        name="kv_proj",
    )(x1, g, *w_args, cos, sin)


def _qz_proj_kernel(x_ref, g_ref, *refs, nq):
    w_refs, (cos_ref, sin_ref, q_ref, z_ref, xn_ref) = refs[:2 * nq], refs[2 * nq:]
    n = pl.program_id(1)

    @pl.when(n == 0)
    def _():
        xn_ref[...] = _rms_scale(x_ref[...], g_ref[...]).astype(BF16)

    for c in range(2 * nq):
        @pl.when(n == c)
        def _():
            acc = jnp.dot(xn_ref[...], w_refs[c][...], preferred_element_type=F32)
            if c < nq:
                q_ref[...] = _rope(acc, cos_ref[...], sin_ref[...]).astype(BF16)
            else:
                z_ref[...] = acc


def _qz_proj(x1, row0, rows, g, w, cos, sin, tn):
    tm = ROW_TILE
    d = x1.shape[1]
    half = w.shape[1] // 2
    nq = half // tn
    m0 = row0 // tm
    w_specs, w_args = _weight_column_specs(w, tn)
    return pl.pallas_call(
        functools.partial(_qz_proj_kernel, nq=nq),
        out_shape=[jax.ShapeDtypeStruct((rows, half), BF16), jax.ShapeDtypeStruct((rows, half), F32)],
        grid=(rows // tm, 2 * nq),
        in_specs=[
            pl.BlockSpec((tm, d), lambda m, n: (m0 + m, 0)),
            _resident((1, d), lambda m, n: (0, 0)),
            *w_specs,
            pl.BlockSpec((tm, HEAD_DIM), lambda m, n: (m, 0)),
            pl.BlockSpec((tm, HEAD_DIM), lambda m, n: (m, 0)),
        ],
        out_specs=[
            pl.BlockSpec((tm, tn), lambda m, n: (m, jnp.minimum(n, nq - 1))),
            pl.BlockSpec((tm, tn), lambda m, n: (m, jnp.maximum(n - nq, 0))),
        ],
        scratch_shapes=[pltpu.VMEM((tm, d), BF16)],
        compiler_params=_params("arbitrary", "arbitrary"),
        name="qz_proj",
    )(x1, g, *w_args, cos, sin)


def _rope_tables(pos):
    inv_freq = ROPE_THETA ** (-jnp.arange(0, HEAD_DIM, 2, dtype=F32) / HEAD_DIM)
    ang = pos.astype(F32)[:, None] * inv_freq[None, :]
    c, s = jnp.cos(ang), jnp.sin(ang)
    return jnp.concatenate([c, c], axis=-1), jnp.concatenate([-s, s], axis=-1)


def _diff_lambda(lq1_ref, lk1_ref, lq2_ref, lk2_ref, lambda_init):
    s1 = jnp.sum(lq1_ref[...] * lk1_ref[...], axis=-1, keepdims=True)
    s2 = jnp.sum(lq2_ref[...] * lk2_ref[...], axis=-1, keepdims=True)
    return jnp.exp(s1) - jnp.exp(s2) + lambda_init


def _attn_finish(a1, l1, a2, l2, lam, subln, z, lambda_init):
    o = a1 / l1 - lam * (a2 / l2)
    o = _rms_scale(o, subln) * (1.0 - lambda_init)
    return (o * jax.nn.silu(z)).astype(BF16)


def _softmax_update(m_sc, l_sc, a_sc, idx, s, v):
    m_old = m_sc[idx]
    m_new = jnp.maximum(m_old, jnp.max(s, axis=-1, keepdims=True))
    p = jnp.exp(s - m_new)
    alpha = jnp.exp(m_old - m_new)
    l_sc[idx] = alpha * l_sc[idx] + jnp.sum(p, axis=-1, keepdims=True)
    a_sc[idx] = alpha * a_sc[idx] + jnp.dot(p.astype(BF16), v, preferred_element_type=F32)
    m_sc[idx] = m_new


def _attn_prompt_kernel(q_ref, k1_ref, k2_ref, v1_ref, v2_ref, z_ref,
                        lq1_ref, lk1_ref, lq2_ref, lk2_ref, sub_ref,
                        o_ref, m_sc, l_sc, al_sc, a_sc, s_sc, p_sc, *, tq, tk, lambda_init):
    qi = pl.program_id(1)
    c2 = HEAD_DIM ** -0.5 * math.log2(math.e)
    k_refs = (k1_ref, k2_ref)
    nsub = tq // tk
    nc = tk // LANES
    m_sc[...] = jnp.full(m_sc.shape, NEG, F32)
    l_sc[...] = jnp.zeros(l_sc.shape, F32)
    a_sc[...] = jnp.zeros(a_sc.shape, F32)

    def softmax_strips(sub, n, col0):
        for r in range(tk // STRIP):
            row0 = sub * tk + r * STRIP
            lr = slice(r * STRIP, (r + 1) * STRIP)
            gr = slice(row0, row0 + STRIP)
            s = s_sc[2 * sub + n, lr, :]
            if col0 is not None:
                visible = (row0 // CHUNK + 1) * CHUNK - col0
                if visible < tk:
                    col = lax.broadcasted_iota(jnp.int32, (STRIP, tk), 1)
                    s = jnp.where(col < visible, s, NEG)
            cols = [s[:, c * LANES:(c + 1) * LANES] for c in range(nc)]
            m_cur = jnp.max(functools.reduce(jnp.maximum, cols), axis=-1, keepdims=True)
            m_old = m_sc[n, gr, :]
            m_new = jnp.maximum(m_old, m_cur)
            ps = [jnp.exp2((c - m_new) * c2) for c in cols]
            l_cur = jnp.sum(functools.reduce(jnp.add, ps), axis=-1, keepdims=True)
            alpha = jnp.exp2((m_old - m_new) * c2)
            l_sc[n, gr, :] = alpha * l_sc[n, gr, :] + l_cur
            m_sc[n, gr, :] = m_new
            al_sc[n, gr, :] = alpha
            for c in range(nc):
                p_sc[2 * sub + n, lr, c * LANES:(c + 1) * LANES] = ps[c].astype(BF16)

    def block(kb, col0, subs):
        rows = pl.ds(pl.multiple_of(kb * tk, tk), tk)
        vblk = jnp.concatenate([v1_ref[rows, :], v2_ref[rows, :]], axis=1)
        for sub in subs:
            for n in range(2):
                qn = q_ref[sub * tk:(sub + 1) * tk, n * HEAD_DIM:(n + 1) * HEAD_DIM]
                s_sc[2 * sub + n] = lax.dot_general(qn, k_refs[n][rows, :], _NT,
                                                    preferred_element_type=F32)
        for sub in subs:
            gr = slice(sub * tk, (sub + 1) * tk)
            for n in range(2):
                softmax_strips(sub, n, col0)
                pv = jnp.dot(p_sc[2 * sub + n], vblk, preferred_element_type=F32)
                alpha = al_sc[n, gr, :]
                a_sc[n, gr, :] = a_sc[n, gr, :] * jnp.concatenate([alpha, alpha], axis=1) + pv

    def body(kb, c):
        block(kb, None, range(nsub))
        return c

    lax.fori_loop(0, qi * nsub, body, 0)
    for d in range(nsub):
        block(qi * nsub + d, d * tk, range(d, nsub))

    lam = _diff_lambda(lq1_ref, lk1_ref, lq2_ref, lk2_ref, lambda_init)
    l1 = jnp.concatenate([l_sc[0], l_sc[0]], axis=1)
    l2 = jnp.concatenate([l_sc[1], l_sc[1]], axis=1)
    o_ref[...] = _attn_finish(a_sc[0], l1, a_sc[1], l2, lam, sub_ref[...], z_ref[...], lambda_init)


def _attn_prompt(qb, z32, kvb, lq1, lk1, lq2, lk2, subln, lambda_init, tq, tk):
    rows = qb.shape[0]
    hw = 2 * HEAD_DIM
    assert tq % tk == 0 and tq % STRIP == 0 and CHUNK % STRIP == 0
    vec = pl.BlockSpec((1, HEAD_DIM), lambda h, i: (0, 0))
    kcol = lambda c: pl.BlockSpec((rows, HEAD_DIM), lambda h, i: (0, c * N_HEADS + h))
    stat = pltpu.VMEM((2, tq, LANES), F32)
    return pl.pallas_call(
        functools.partial(_attn_prompt_kernel, tq=tq, tk=tk, lambda_init=lambda_init),
        out_shape=jax.ShapeDtypeStruct((rows, N_HEADS * hw), BF16),
        grid=(N_HEADS, rows // tq),
        in_specs=[
            pl.BlockSpec((tq, hw), lambda h, i: (i, h)),
            kcol(0), kcol(1), kcol(2), kcol(3),
            pl.BlockSpec((tq, hw), lambda h, i: (i, h)),
            vec, vec, vec, vec,
            pl.BlockSpec((1, hw), lambda h, i: (0, 0)),
        ],
        out_specs=pl.BlockSpec((tq, hw), lambda h, i: (i, h)),
        scratch_shapes=[
            stat, stat, stat,
            pltpu.VMEM((2, tq, hw), F32),
            pltpu.VMEM((2 * (tq // tk), tk, tk), F32),
            pltpu.VMEM((2 * (tq // tk), tk, tk), BF16),
        ],
        compiler_params=_params("arbitrary", "arbitrary"),
        name="attn_prompt",
    )(qb, kvb, kvb, kvb, kvb, z32, lq1, lk1, lq2, lk2, subln)


def _attn_sample_kernel(q_ref, ck1_ref, ck2_ref, cv1_ref, cv2_ref, kvn_ref, z_ref,
                        lq1_ref, lk1_ref, lq2_ref, lk2_ref, sub_ref,
                        o_ref, m_sc, l_sc, a_sc, *, tkv, nkb, past_len, lambda_init):
    kb = pl.program_id(1)
    scale = HEAD_DIM ** -0.5
    t = q_ref.shape[0]
    hw = 2 * HEAD_DIM
    ck_refs = (ck1_ref, ck2_ref)
    assert (past_len - 1) // CHUNK <= past_len // CHUNK

    @pl.when(kb == 0)
    def _():
        m_sc[...] = jnp.full(m_sc.shape, NEG, F32)
        l_sc[...] = jnp.zeros(l_sc.shape, F32)
        a_sc[...] = jnp.zeros(a_sc.shape, F32)

    for h in range(N_HEADS):
        rows = pl.ds(h, tkv, stride=N_HEADS)
        v = jnp.concatenate([cv1_ref[rows, :], cv2_ref[rows, :]], axis=1).astype(BF16)
        for n in range(2):
            qn = q_ref[:, (2 * h + n) * HEAD_DIM:(2 * h + n + 1) * HEAD_DIM]
            k = ck_refs[n][rows, :].astype(BF16)
            s = lax.dot_general(qn, k, _NT, preferred_element_type=F32) * scale
            _softmax_update(m_sc, l_sc, a_sc, 2 * h + n, s, v)

    @pl.when(kb == nkb - 1)
    def _():
        row = lax.broadcasted_iota(jnp.int32, (t, t), 0) + past_len
        col = lax.broadcasted_iota(jnp.int32, (t, t), 1) + past_len
        new_mask = (col // CHUNK) <= (row // CHUNK)
        lam = _diff_lambda(lq1_ref, lk1_ref, lq2_ref, lk2_ref, lambda_init)
        sub = sub_ref[...]
        col_blk = lambda c, h: slice((c * N_HEADS + h) * HEAD_DIM, (c * N_HEADS + h + 1) * HEAD_DIM)
        for h in range(N_HEADS):
            vn = jnp.concatenate([kvn_ref[:, col_blk(2, h)], kvn_ref[:, col_blk(3, h)]], axis=1)
            for n in range(2):
                qn = q_ref[:, (2 * h + n) * HEAD_DIM:(2 * h + n + 1) * HEAD_DIM]
                s = lax.dot_general(qn, kvn_ref[:, col_blk(n, h)], _NT,
                                    preferred_element_type=F32) * scale
                _softmax_update(m_sc, l_sc, a_sc, 2 * h + n, jnp.where(new_mask, s, NEG), vn)
            o_ref[:, h * hw:(h + 1) * hw] = _attn_finish(
                a_sc[2 * h], l_sc[2 * h], a_sc[2 * h + 1], l_sc[2 * h + 1], lam, sub,
                z_ref[:, h * hw:(h + 1) * hw], lambda_init)


def _attn_sample(qb, z32, kvb, cache_k2, cache_v2, lq1, lk1, lq2, lk2, subln, lambda_init, t, tkv):
    bsz, rows8, hw = cache_k2.shape
    past_len = rows8 // N_HEADS
    nkb = past_len // tkv
    vec = pl.BlockSpec((1, HEAD_DIM), lambda b, k: (0, 0))
    cache = lambda c: pl.BlockSpec((None, tkv * N_HEADS, HEAD_DIM), lambda b, k: (b, k, c))
    full = lambda a: pl.BlockSpec((t, a.shape[1]), lambda b, k: (b, 0))
    return pl.pallas_call(
        functools.partial(_attn_sample_kernel, tkv=tkv, nkb=nkb, past_len=past_len,
                          lambda_init=lambda_init),
        out_shape=jax.ShapeDtypeStruct((bsz * t, N_HEADS * hw), BF16),
        grid=(bsz, nkb),
        in_specs=[
            full(qb), cache(0), cache(1), cache(0), cache(1), full(kvb), full(z32),
            vec, vec, vec, vec,
            pl.BlockSpec((1, hw), lambda b, k: (0, 0)),
        ],
        out_specs=pl.BlockSpec((t, N_HEADS * hw), lambda b, k: (b, 0)),
        scratch_shapes=[
            pltpu.VMEM((2 * N_HEADS, t, 1), F32),
            pltpu.VMEM((2 * N_HEADS, t, 1), F32),
            pltpu.VMEM((2 * N_HEADS, t, hw), F32),
        ],
        compiler_params=_params("arbitrary", "arbitrary"),
        name="attn_sample",
    )(qb, cache_k2, cache_k2, cache_v2, cache_v2, kvb, z32, lq1, lk1, lq2, lk2, subln)


def _b_out_kernel(o_ref, w_ref, x_ref, g_ref, y_ref):
    acc = jnp.dot(o_ref[...], w_ref[...], preferred_element_type=F32)
    y_ref[...] = _rms_scale(x_ref[...] + acc, g_ref[...])


def _b_out_proj(og, w_out, x1, row0, g, tm):
    rows, e = og.shape
    d = w_out.shape[1]
    m0 = row0 // tm
    return pl.pallas_call(
        _b_out_kernel,
        out_shape=jax.ShapeDtypeStruct((rows, d), F32),
        grid=(rows // tm,),
        in_specs=[
            pl.BlockSpec((tm, e), lambda m: (m, 0)),
            _resident((e, d), lambda m: (0, 0)),
            pl.BlockSpec((tm, d), lambda m: (m0 + m, 0)),
            _resident((1, d), lambda m: (0, 0)),
        ],
        out_specs=pl.BlockSpec((tm, d), lambda m: (m, 0)),
        compiler_params=_params("arbitrary"),
        name="b_out_proj",
    )(og, w_out, x1, g)


def _permute_kv_columns(w_kv):
    d = w_kv.shape[0]
    w = w_kv.reshape(d, 2, N_HEADS, 2, HEAD_DIM).transpose(0, 1, 3, 2, 4)
    return w.reshape(d, 4 * N_HEADS * HEAD_DIM)


def kernel(x_prompt, x_sample, state_ssm_re, state_ssm_im, cache_k, cache_v, a_norm, a_w_in, a_lambda_re, a_lambda_im, a_log_dt, a_b_re, a_b_im, a_c_re, a_c_im, a_d, a_w_glu, a_w_out, kv_norm, w_kv, b_norm, b_w_in, b_lambda_q1, b_lambda_k1, b_lambda_q2, b_lambda_k2, b_subln, b_w_out, final_norm):
    bp, seq, d = x_prompt.shape
    bs, t_s, _ = x_sample.shape
    past_len = cache_k.shape[1]
    n_a, n_b = a_norm.shape[0], b_norm.shape[0]
    g_cnt, p = a_lambda_re.shape[1:]
    assert bp == 1 and n_a == 1 and n_b == 1
    assert t_s == SSM_T and seq % (SSM_T * bs) == 0 and seq % CHUNK == 0
    assert cache_k.shape[2:] == (N_HEADS, 2 * HEAD_DIM)

    jp, js = seq // SSM_T, bs
    jpad = -(-(jp + js) // CHUNK_TILE) * CHUNK_TILE
    rows_s = bs * t_s
    rows_a = seq + rows_s

    xp = x_prompt.reshape(seq, d)
    xs = x_sample.reshape(rows_s, d)
    uz = _a_in_proj(xp, xs, a_norm[0].reshape(1, d), a_w_in[0].astype(BF16), jpad * SSM_T, tn=1024)
    mt, bend, cin, a16r, a16i = _ssm_prep(a_lambda_re[0], a_lambda_im[0], a_log_dt[0],
                                          a_b_re[0], a_b_im[0], a_c_re[0], a_c_im[0])
    s_re, s_im = _ssm_state_contrib(uz, bend, jpad)
    h0r = state_ssm_re[:, 0].reshape(bs, g_cnt * p)
    h0i = state_ssm_im[:, 0].reshape(bs, g_cnt * p)
    hp_re, hp_im, pre, pim, sre, sim = _ssm_scan(s_re, s_im, a16r, a16i, h0r, h0i, jp)
    yg = _ssm_output(uz, mt, cin, hp_re, hp_im, a_d[0].reshape(1, -1), jpad)
    o_a = _a_glu(yg, uz, a_w_glu[0].astype(BF16), rows_a)
    x1 = _a_out_proj(o_a, a_w_out[0].astype(BF16), xp, xs)

    lambda_init = 0.8 - 0.6 * math.exp(-0.3 * n_a)
    w_kv_b = _permute_kv_columns(w_kv).astype(BF16)
    w_q_b = b_w_in[0].astype(BF16)
    kv_g = kv_norm.reshape(1, d)
    b_g = b_norm[0].reshape(1, d)
    pos_p = jnp.arange(seq, dtype=jnp.int32)
    pos_s = jnp.tile(past_len + jnp.arange(t_s, dtype=jnp.int32), bs)
    cos_p, sin_p = _rope_tables(pos_p)
    cos_s, sin_s = _rope_tables(pos_s)
    k_p, v_p, kvb_p = _kv_proj(x1, 0, seq, kv_g, w_kv_b, cos_p, sin_p)
    k_s, v_s, kvb_s = _kv_proj(x1, seq, rows_s, kv_g, w_kv_b, cos_s, sin_s)
    qb_p, z_p = _qz_proj(x1, 0, seq, b_g, w_q_b, cos_p, sin_p, tn=1024)
    qb_s, z_s = _qz_proj(x1, seq, rows_s, b_g, w_q_b, cos_s, sin_s, tn=1024)

    vecs = [a[0].reshape(1, HEAD_DIM) for a in (b_lambda_q1, b_lambda_k1, b_lambda_q2, b_lambda_k2)]
    subln = b_subln[0].reshape(1, 2 * HEAD_DIM)
    hw = 2 * HEAD_DIM
    og_p = _attn_prompt(qb_p, z_p, kvb_p, *vecs, subln, lambda_init, tq=1024, tk=512)
    og_s = _attn_sample(qb_s, z_s, kvb_s, cache_k.reshape(bs, past_len * N_HEADS, hw),
                        cache_v.reshape(bs, past_len * N_HEADS, hw), *vecs, subln, lambda_init,
                        t_s, tkv=1024)
    w_o_b = b_w_out[0].astype(BF16)
    fg = final_norm.reshape(1, d)
    y_p = _b_out_proj(og_p, w_o_b, x1, 0, fg, ROW_TILE)
    y_s = _b_out_proj(og_s, w_o_b, x1, seq, fg, ROW_TILE)

    return (y_p.reshape(bp, seq, d), y_s.reshape(bs, t_s, d),
            pre.reshape(bp, n_a, g_cnt, p), pim.reshape(bp, n_a, g_cnt, p),
            k_p.reshape(bp, seq, N_HEADS, hw), v_p.reshape(bp, seq, N_HEADS, hw),
            sre.reshape(bs, n_a, g_cnt, p), sim.reshape(bs, n_a, g_cnt, p),
            k_s.reshape(bs, t_s, N_HEADS, hw), v_s.reshape(bs, t_s, N_HEADS, hw))
```

```python
import functools
import math

import jax
import jax.numpy as jnp
from jax import lax
from jax.experimental import pallas as pl
from jax.experimental.pallas import tpu as pltpu

F32 = jnp.float32
BF16 = jnp.bfloat16

CHUNK = 64
HEAD_DIM = 128
N_HEADS = 8
SSM_GROUP = 16
SSM_STATE = 64
SSM_T = 16
ROPE_THETA = 10000.0
EPS = 1e-6
NEG = -1e30

LANES = 128
SUBLANES = 8
GROUPS_PER_TILE = LANES // SSM_GROUP
CHUNK_TILE = 128
ROW_TILE = 512
STRIP = 32
VMEM_LIMIT = 56 * 1024 * 1024

_NT = (((1,), (1,)), ((), ()))


def _params(*sem):
    return pltpu.CompilerParams(dimension_semantics=sem, vmem_limit_bytes=VMEM_LIMIT)


def _resident(shape, index_map):
    return pl.BlockSpec(shape, index_map, pipeline_mode=pl.Buffered(1))


def _rms_scale(x, g):
    ms = jnp.mean(x * x, axis=-1, keepdims=True)
    return x * lax.rsqrt(ms + EPS) * g


def _staged_weight_spec(d, tn, nb):
    return pl.BlockSpec((d, tn), lambda r, n: (0, jnp.where(r == 0, n, nb - 1)),
                        pipeline_mode=pl.Buffered(1))


def _row_block(r):
    return jnp.maximum(r - 1, 0)


def _col_block(r, n):
    return jnp.where(r == 0, 0, n)


def _a_in_kernel(xp_ref, xs_ref, g_ref, w_ref, o_ref, xn_ref, wb_ref, *, mp, nb):
    m, n = pl.program_id(0) - 1, pl.program_id(1)

    @pl.when(m < 0)
    def _():
        wb_ref[n] = w_ref[...].astype(BF16)

    @pl.when((n == 0) & (m >= 0) & (m < mp))
    def _():
        xn_ref[...] = _rms_scale(xp_ref[...], g_ref[...]).astype(BF16)

    @pl.when((n == 0) & (m == mp))
    def _():
        xn_ref[...] = _rms_scale(xs_ref[...], g_ref[...]).astype(BF16)

    for c in range(nb):
        @pl.when((m >= 0) & (m <= mp) & (n == c))
        def _():
            o_ref[...] = jnp.dot(xn_ref[...], wb_ref[c], preferred_element_type=F32)

    @pl.when(m > mp)
    def _():
        o_ref[...] = jnp.zeros(o_ref.shape, F32)


def _a_in_proj(xp, xs, g, w, rows_pad, tn):
    tm = ROW_TILE
    d = g.shape[-1]
    n_out = w.shape[1]
    mp = xp.shape[0] // tm
    nb = n_out // tn
    assert xs.shape[0] == tm and xp.shape[0] % tm == 0 and rows_pad % tm == 0
    return pl.pallas_call(
        functools.partial(_a_in_kernel, mp=mp, nb=nb),
        out_shape=jax.ShapeDtypeStruct((rows_pad, n_out), F32),
        grid=(1 + rows_pad // tm, nb),
        in_specs=[
            pl.BlockSpec((tm, d), lambda r, n: (jnp.minimum(_row_block(r), mp - 1), 0)),
            _resident((tm, d), lambda r, n: (0, 0)),
            _resident((1, d), lambda r, n: (0, 0)),
            _staged_weight_spec(d, tn, nb),
        ],
        out_specs=pl.BlockSpec((tm, tn), lambda r, n: (_row_block(r), _col_block(r, n))),
        scratch_shapes=[pltpu.VMEM((tm, d), BF16), pltpu.VMEM((nb, d, tn), BF16)],
        compiler_params=_params("arbitrary", "arbitrary"),
        name="a_in_proj",
    )(xp, xs, g, w)


def _ssm_prep_kernel(lrc_ref, lic_ref, dtc_ref, bre_ref, bim_ref, lrr_ref, lir_ref, dtr_ref,
                     cre_ref, cim_ref, mt_ref, bend_ref, cin_ref, a16r_ref, a16i_ref):
    p2 = 2 * SSM_STATE
    w = SSM_T * SSM_GROUP

    lr, li = lrc_ref[...], lic_ref[...]
    dt = jnp.exp(dtc_ref[...])
    zr, zi = lr * dt, li * dt
    mag = jnp.exp(zr)
    n_re, n_im = mag * jnp.cos(zi) - 1.0, mag * jnp.sin(zi)
    den = lr * lr + li * li
    cf_re = (n_re * lr + n_im * li) / den
    cf_im = (n_im * lr - n_re * li) / den
    b_re, b_im = bre_ref[...], bim_ref[...]
    bb_re = cf_re * b_re - cf_im * b_im
    bb_im = cf_re * b_im + cf_im * b_re
    lane = lax.broadcasted_iota(jnp.int32, (p2, w), 1)
    e_end = (SSM_T - 1 - lane // SSM_GROUP).astype(F32)
    pm = jnp.exp(zr * e_end)
    pw_re, pw_im = pm * jnp.cos(zi * e_end), pm * jnp.sin(zi * e_end)
    end_re = pw_re * bb_re - pw_im * bb_im
    end_im = pw_re * bb_im + pw_im * bb_re
    for r in range(2):
        sl = slice(r * SSM_STATE, (r + 1) * SSM_STATE)
        bend_ref[r] = jnp.concatenate([end_re[sl], end_im[sl]], axis=0).astype(BF16)
    bst = jnp.concatenate([bb_re, bb_im], axis=0)

    lr2, li2 = lrr_ref[...], lir_ref[...]
    dt2 = jnp.exp(dtr_ref[...])
    zr2, zi2 = lr2 * dt2, li2 * dt2
    c_re, c_im = cre_ref[...], cim_ref[...]
    tau = (lax.broadcasted_iota(jnp.int32, (w, p2), 0) // SSM_GROUP).astype(F32)

    def c_times_power(ex):
        m = jnp.exp(zr2 * ex)
        q_re, q_im = m * jnp.cos(zi2 * ex), m * jnp.sin(zi2 * ex)
        return c_re * q_re - c_im * q_im, c_re * q_im + c_im * q_re

    g_re, g_im = c_times_power(tau)
    ci_re, ci_im = c_times_power(tau + 1.0)
    lane2 = lax.broadcasted_iota(jnp.int32, (w, p2), 1)
    sblk = lax.broadcasted_iota(jnp.int32, (w, w), 1) // SSM_GROUP
    for r in range(2):
        own = (lane2 // SSM_STATE) == r
        lhs = jnp.concatenate([jnp.where(own, g_re, 0.0), jnp.where(own, -g_im, 0.0)], axis=1)
        gen = jnp.dot(lhs, bst, preferred_element_type=F32)
        k = 0
        while (SSM_GROUP << k) < w:
            sh = SSM_GROUP << k
            shifted = jnp.concatenate([jnp.zeros((sh, w), F32), gen[:w - sh]], axis=0)
            gen = jnp.where(((sblk >> k) & 1) == 1, shifted, gen)
            k += 1
        mt_ref[r] = gen.astype(BF16)
        cin_ref[r] = jnp.concatenate(
            [jnp.where(own, ci_re, 0.0), jnp.where(own, -ci_im, 0.0)], axis=1).astype(BF16)

    m16 = jnp.exp(zr2 * float(SSM_T))
    a16r_ref[...] = m16 * jnp.cos(zi2 * float(SSM_T))
    a16i_ref[...] = m16 * jnp.sin(zi2 * float(SSM_T))


def _ssm_prep(lam_re, lam_im, log_dt, b_re, b_im, c_re, c_im):
    g, p = lam_re.shape
    gp = g // 2
    p2 = 2 * p
    w = SSM_T * SSM_GROUP
    col = lambda a: a.reshape(gp, p2, 1)
    row = lambda a: a.reshape(gp, 1, p2)
    dt_full = jnp.broadcast_to(log_dt[:, None], (g, p))
    b_t = lambda a: jnp.tile(a.reshape(gp, p2, SSM_GROUP), (1, 1, SSM_T))
    c_t = lambda a: jnp.tile(
        a.reshape(gp, 2, SSM_GROUP, p).transpose(0, 2, 1, 3).reshape(gp, SSM_GROUP, p2), (1, SSM_T, 1))
    colspec = pl.BlockSpec((None, p2, 1), lambda i: (i, 0, 0))
    rowspec = pl.BlockSpec((None, 1, p2), lambda i: (i, 0, 0))
    bspec = pl.BlockSpec((None, p2, w), lambda i: (i, 0, 0))
    cspec = pl.BlockSpec((None, w, p2), lambda i: (i, 0, 0))
    mt, bend, cin, a16r, a16i = pl.pallas_call(
        _ssm_prep_kernel,
        out_shape=[
            jax.ShapeDtypeStruct((g, w, w), BF16),
            jax.ShapeDtypeStruct((g, p2, w), BF16),
            jax.ShapeDtypeStruct((g, w, 2 * p2), BF16),
            jax.ShapeDtypeStruct((gp, 1, p2), F32),
            jax.ShapeDtypeStruct((gp, 1, p2), F32),
        ],
        grid=(gp,),
        in_specs=[colspec, colspec, colspec, bspec, bspec, rowspec, rowspec, rowspec, cspec, cspec],
        out_specs=[
            pl.BlockSpec((2, w, w), lambda i: (i, 0, 0)),
            pl.BlockSpec((2, p2, w), lambda i: (i, 0, 0)),
            pl.BlockSpec((2, w, 2 * p2), lambda i: (i, 0, 0)),
            rowspec, rowspec,
        ],
        compiler_params=_params("arbitrary"),
        name="ssm_prep",
    )(col(lam_re), col(lam_im), col(dt_full), b_t(b_re), b_t(b_im),
      row(lam_re), row(lam_im), row(dt_full), c_t(c_re), c_t(c_im))
    return mt, bend, cin, a16r.reshape(1, g * p), a16i.reshape(1, g * p)


def _step_rows(s):
    return pl.ds(s, CHUNK_TILE, stride=SSM_T)


def _build_ut(u_ref, ut_ref):
    for s in range(SSM_T):
        xt = u_ref[_step_rows(s), :].T.astype(BF16)
        for gl in range(GROUPS_PER_TILE):
            ut_ref[gl, s * SSM_GROUP:(s + 1) * SSM_GROUP, :] = xt[gl * SSM_GROUP:(gl + 1) * SSM_GROUP, :]


def _ssm_state_kernel(u_ref, bend_ref, sre_ref, sim_ref, ut_ref):
    _build_ut(u_ref, ut_ref)
    p = SSM_STATE
    for pr in range(GROUPS_PER_TILE // 2):
        st = [jnp.dot(bend_ref[2 * pr + r], ut_ref[2 * pr + r], preferred_element_type=F32)
              for r in range(2)]
        sre_ref[:, pr * 2 * p:(pr + 1) * 2 * p] = jnp.concatenate([st[0][:p], st[1][:p]], axis=0).T
        sim_ref[:, pr * 2 * p:(pr + 1) * 2 * p] = jnp.concatenate([st[0][p:], st[1][p:]], axis=0).T


def _ssm_state_contrib(uz, bend, jpad):
    g, p2, w = bend.shape
    ntile = g // GROUPS_PER_TILE
    sw = GROUPS_PER_TILE * SSM_STATE
    out = jax.ShapeDtypeStruct((jpad, g * SSM_STATE), F32)
    ospec = pl.BlockSpec((CHUNK_TILE, sw), lambda j, q: (j, q))
    return pl.pallas_call(
        _ssm_state_kernel,
        out_shape=[out, out],
        grid=(jpad // CHUNK_TILE, ntile),
        in_specs=[
            pl.BlockSpec((CHUNK_TILE * SSM_T, LANES), lambda j, q: (j, q)),
            pl.BlockSpec((GROUPS_PER_TILE, p2, w), lambda j, q: (q, 0, 0)),
        ],
        out_specs=[ospec, ospec],
        scratch_shapes=[pltpu.VMEM((GROUPS_PER_TILE, w, CHUNK_TILE), BF16)],
        compiler_params=_params("arbitrary", "arbitrary"),
        name="ssm_state_contrib",
    )(uz, bend)


def _ssm_scan_kernel(sre_ref, sim_ref, ar_ref, ai_ref, h0r_ref, h0i_ref,
                     hpr_ref, hpi_ref, pr_ref, pi_ref, sr_ref, si_ref, hr_sc, hi_sc, *, n_prompt):
    i = pl.program_id(0)
    rb = sre_ref.shape[0]
    ar, ai = ar_ref[...], ai_ref[...]

    @pl.when(i == 0)
    def _():
        hr_sc[...] = jnp.zeros_like(hr_sc)
        hi_sc[...] = jnp.zeros_like(hi_sc)

    @pl.when(i < n_prompt)
    def _():
        def body(j, c):
            row = pl.ds(j, 1)
            hr, hi = hr_sc[...], hi_sc[...]
            hpr_ref[row, :] = hr
            hpi_ref[row, :] = hi
            hr_sc[...] = ar * hr - ai * hi + sre_ref[row, :]
            hi_sc[...] = ar * hi + ai * hr + sim_ref[row, :]
            return c

        lax.fori_loop(0, rb, body, 0)

    @pl.when(i == n_prompt - 1)
    def _():
        pr_ref[...] = hr_sc[...]
        pi_ref[...] = hi_sc[...]

    @pl.when(i == n_prompt)
    def _():
        h0r, h0i = h0r_ref[...], h0i_ref[...]
        hpr_ref[...] = h0r
        hpi_ref[...] = h0i
        sr_ref[...] = ar * h0r - ai * h0i + sre_ref[...]
        si_ref[...] = ar * h0i + ai * h0r + sim_ref[...]

    @pl.when(i > n_prompt)
    def _():
        hpr_ref[...] = jnp.zeros_like(hpr_ref)
        hpi_ref[...] = jnp.zeros_like(hpi_ref)


def _ssm_scan(s_re, s_im, a16r, a16i, h0r, h0i, jp):
    jpad, n = s_re.shape
    rb = h0r.shape[0]
    assert jp % rb == 0 and jpad % rb == 0
    n_prompt = jp // rb
    rows = pl.BlockSpec((rb, n), lambda i: (i, 0))
    const1 = pl.BlockSpec((1, n), lambda i: (0, 0))
    constb = pl.BlockSpec((rb, n), lambda i: (0, 0))
    big = jax.ShapeDtypeStruct((jpad, n), F32)
    one = jax.ShapeDtypeStruct((1, n), F32)
    bat = jax.ShapeDtypeStruct((rb, n), F32)
    return pl.pallas_call(
        functools.partial(_ssm_scan_kernel, n_prompt=n_prompt),
        out_shape=[big, big, one, one, bat, bat],
        grid=(jpad // rb,),
        in_specs=[rows, rows, const1, const1, constb, constb],
        out_specs=[rows, rows, const1, const1, constb, constb],
        scratch_shapes=[pltpu.VMEM((1, n), F32), pltpu.VMEM((1, n), F32)],
        compiler_params=_params("arbitrary"),
        name="ssm_scan",
    )(s_re, s_im, a16r, a16i, h0r, h0i)


def _ssm_out_kernel(u_ref, mt_ref, cin_ref, hpr_ref, hpi_ref, d_ref, o_ref, ut_ref, yt_ref):
    _build_ut(u_ref, ut_ref)
    p2 = 2 * SSM_STATE
    for gl in range(GROUPS_PER_TILE):
        pr = gl // 2
        hp = jnp.concatenate([hpr_ref[:, pr * p2:(pr + 1) * p2], hpi_ref[:, pr * p2:(pr + 1) * p2]],
                             axis=1).astype(BF16)
        yt = jnp.dot(mt_ref[gl], ut_ref[gl], preferred_element_type=F32)
        yt = yt + lax.dot_general(cin_ref[gl], hp, _NT, preferred_element_type=F32)
        for t in range(SSM_T):
            yt_ref[t, gl * SSM_GROUP:(gl + 1) * SSM_GROUP, :] = yt[t * SSM_GROUP:(t + 1) * SSM_GROUP, :]
    d = d_ref[...]
    for t in range(SSM_T):
        y = yt_ref[t].T + d * u_ref[_step_rows(t), :]
        o_ref[_step_rows(t), :] = jax.nn.gelu(y)


def _ssm_output(uz, mt, cin, hp_re, hp_im, d_skip, jpad):
    g, w, _ = mt.shape
    w2 = cin.shape[-1]
    ntile = g // GROUPS_PER_TILE
    sw = GROUPS_PER_TILE * SSM_STATE
    tok = pl.BlockSpec((CHUNK_TILE * SSM_T, LANES), lambda j, q: (j, q))
    hspec = pl.BlockSpec((CHUNK_TILE, sw), lambda j, q: (j, q))
    return pl.pallas_call(
        _ssm_out_kernel,
        out_shape=jax.ShapeDtypeStruct((jpad * SSM_T, g * SSM_GROUP), F32),
        grid=(jpad // CHUNK_TILE, ntile),
        in_specs=[
            tok,
            pl.BlockSpec((GROUPS_PER_TILE, w, w), lambda j, q: (q, 0, 0)),
            pl.BlockSpec((GROUPS_PER_TILE, w, w2), lambda j, q: (q, 0, 0)),
            hspec, hspec,
            pl.BlockSpec((1, LANES), lambda j, q: (0, q)),
        ],
        out_specs=tok,
        scratch_shapes=[
            pltpu.VMEM((GROUPS_PER_TILE, w, CHUNK_TILE), BF16),
            pltpu.VMEM((SSM_T, LANES, CHUNK_TILE), F32),
        ],
        compiler_params=_params("arbitrary", "arbitrary"),
        name="ssm_output",
    )(uz, mt, cin, hp_re, hp_im, d_skip)


def _a_glu_kernel(y_ref, z_ref, w_ref, o_ref, wb_ref):
    r = pl.program_id(0)

    @pl.when(r == 0)
    def _():
        wb_ref[...] = w_ref[...].astype(BF16)

    @pl.when(r > 0)
    def _():
        y = y_ref[...]
        gate = jnp.dot(y.astype(BF16), wb_ref[...], preferred_element_type=F32)
        y2 = y * jax.nn.sigmoid(gate)
        o_ref[...] = (y2 * jax.nn.silu(z_ref[...])).astype(BF16)


def _a_glu(yg, uz, w_glu, rows):
    tm = ROW_TILE
    e = yg.shape[1]
    return pl.pallas_call(
        _a_glu_kernel,
        out_shape=jax.ShapeDtypeStruct((rows, e), BF16),
        grid=(1 + rows // tm,),
        in_specs=[
            pl.BlockSpec((tm, e), lambda r: (_row_block(r), 0)),
            pl.BlockSpec((tm, e), lambda r: (_row_block(r), 1)),
            _resident((e, e), lambda r: (0, 0)),
        ],
        out_specs=pl.BlockSpec((tm, e), lambda r: (_row_block(r), 0)),
        scratch_shapes=[pltpu.VMEM((e, e), BF16)],
        compiler_params=_params("arbitrary"),
        name="a_glu",
    )(yg, uz, w_glu)


def _a_out_kernel(o_ref, w_ref, xp_ref, xs_ref, x1_ref, wb_ref, *, mp):
    m = pl.program_id(0) - 1

    @pl.when(m < 0)
    def _():
        wb_ref[...] = w_ref[...].astype(BF16)

    @pl.when((m >= 0) & (m < mp))
    def _():
        x1_ref[...] = xp_ref[...] + jnp.dot(o_ref[...], wb_ref[...], preferred_element_type=F32)

    @pl.when(m == mp)
    def _():
        x1_ref[...] = xs_ref[...] + jnp.dot(o_ref[...], wb_ref[...], preferred_element_type=F32)


def _a_out_proj(o, w_out, xp, xs):
    tm = ROW_TILE
    rows, e = o.shape
    d = w_out.shape[1]
    mp = xp.shape[0] // tm
    return pl.pallas_call(
        functools.partial(_a_out_kernel, mp=mp),
        out_shape=jax.ShapeDtypeStruct((rows, d), F32),
        grid=(1 + rows // tm,),
        in_specs=[
            pl.BlockSpec((tm, e), lambda r: (_row_block(r), 0)),
            _resident((e, d), lambda r: (0, 0)),
            pl.BlockSpec((tm, d), lambda r: (jnp.minimum(_row_block(r), mp - 1), 0)),
            _resident((tm, d), lambda r: (0, 0)),
        ],
        out_specs=pl.BlockSpec((tm, d), lambda r: (_row_block(r), 0)),
        scratch_shapes=[pltpu.VMEM((e, d), BF16)],
        compiler_params=_params("arbitrary"),
        name="a_out_proj",
    )(o, w_out, xp, xs)


def _rope(x, cos, sin):
    outs = []
    for c in range(x.shape[1] // HEAD_DIM):
        xc = x[:, c * HEAD_DIM:(c + 1) * HEAD_DIM]
        outs.append(xc * cos + pltpu.roll(xc, HEAD_DIM // 2, axis=1) * sin)
    return jnp.concatenate(outs, axis=1) if len(outs) > 1 else outs[0]


def _store_heads(o_ref, val, tm):
    for h in range(N_HEADS):
        o_ref[pl.ds(h, tm, stride=N_HEADS), :] = val[:, h * HEAD_DIM:(h + 1) * HEAD_DIM]


def _kv_proj_kernel(x_ref, g_ref, w_ref, cos_ref, sin_ref, k_ref, v_ref, cb_ref, xn_ref, wb_ref, *, tm):
    m, n = pl.program_id(0) - 1, pl.program_id(1)
    hpb = w_ref.shape[1] // (2 * HEAD_DIM)

    for nn in range(4):
        @pl.when((m < 0) & (n == nn))
        def _():
            for hl in range(hpb):
                h = hpb * (nn % 2) + hl
                for half in range(2):
                    src = (2 * hl + half) * HEAD_DIM
                    wb_ref[2 * (nn // 2) + half, :, h * HEAD_DIM:(h + 1) * HEAD_DIM] = (
                        w_ref[:, src:src + HEAD_DIM].astype(BF16))

    @pl.when((m >= 0) & (n == 0))
    def _():
        xn_ref[...] = _rms_scale(x_ref[...], g_ref[...]).astype(BF16)

    for c in range(4):
        @pl.when((m >= 0) & (n == c))
        def _():
            acc = jnp.dot(xn_ref[...], wb_ref[c], preferred_element_type=F32)
            if c < 2:
                acc = _rope(acc, cos_ref[...], sin_ref[...])
            _store_heads(k_ref if c < 2 else v_ref, acc, tm)
            cb_ref[...] = acc.astype(BF16)


def _kv_proj(x1, row0, rows, g, w_kv, cos, sin):
    tm = ROW_TILE
    d = x1.shape[1]
    tn = N_HEADS * HEAD_DIM
    m0 = row0 // tm
    assert w_kv.shape[1] == 4 * tn
    out4 = jax.ShapeDtypeStruct((rows * N_HEADS, 2 * HEAD_DIM), F32)
    return pl.pallas_call(
        functools.partial(_kv_proj_kernel, tm=tm),
        out_shape=[out4, out4, jax.ShapeDtypeStruct((rows, 4 * tn), BF16)],
        grid=(1 + rows // tm, 4),
        in_specs=[
            pl.BlockSpec((tm, d), lambda r, n: (m0 + _row_block(r), 0)),
            _resident((1, d), lambda r, n: (0, 0)),
            _staged_weight_spec(d, tn, 4),
            pl.BlockSpec((tm, HEAD_DIM), lambda r, n: (_row_block(r), 0)),
            pl.BlockSpec((tm, HEAD_DIM), lambda r, n: (_row_block(r), 0)),
        ],
        out_specs=[
            pl.BlockSpec((tm * N_HEADS, HEAD_DIM),
                         lambda r, n: (_row_block(r), jnp.minimum(_col_block(r, n), 1))),
            pl.BlockSpec((tm * N_HEADS, HEAD_DIM),
                         lambda r, n: (_row_block(r), jnp.maximum(_col_block(r, n) - 2, 0))),
            pl.BlockSpec((tm, tn), lambda r, n: (_row_block(r), _col_block(r, n))),
        ],
        scratch_shapes=[pltpu.VMEM((tm, d), BF16), pltpu.VMEM((4, d, tn), BF16)],
        compiler_params=_params("arbitrary", "arbitrary"),
        name="kv_proj",
    )(x1, g, w_kv, cos, sin)


def _qz_proj_kernel(x_ref, g_ref, w_ref, cos_ref, sin_ref, q_ref, z_ref, xn_ref, wb_ref, *, nq):
    m, n = pl.program_id(0) - 1, pl.program_id(1)

    @pl.when(m < 0)
    def _():
        wb_ref[n] = w_ref[...].astype(BF16)

    @pl.when((m >= 0) & (n == 0))
    def _():
        xn_ref[...] = _rms_scale(x_ref[...], g_ref[...]).astype(BF16)

    for c in range(2 * nq):
        @pl.when((m >= 0) & (n == c))
        def _():
            acc = jnp.dot(xn_ref[...], wb_ref[c], preferred_element_type=F32)
            if c < nq:
                q_ref[...] = _rope(acc, cos_ref[...], sin_ref[...]).astype(BF16)
            else:
                z_ref[...] = acc


def _qz_proj(x1, row0, rows, g, w, cos, sin, tn):
    tm = ROW_TILE
    d = x1.shape[1]
    half = w.shape[1] // 2
    nq = half // tn
    m0 = row0 // tm
    return pl.pallas_call(
        functools.partial(_qz_proj_kernel, nq=nq),
        out_shape=[jax.ShapeDtypeStruct((rows, half), BF16), jax.ShapeDtypeStruct((rows, half), F32)],
        grid=(1 + rows // tm, 2 * nq),
        in_specs=[
            pl.BlockSpec((tm, d), lambda r, n: (m0 + _row_block(r), 0)),
            _resident((1, d), lambda r, n: (0, 0)),
            _staged_weight_spec(d, tn, 2 * nq),
            pl.BlockSpec((tm, HEAD_DIM), lambda r, n: (_row_block(r), 0)),
            pl.BlockSpec((tm, HEAD_DIM), lambda r, n: (_row_block(r), 0)),
        ],
        out_specs=[
            pl.BlockSpec((tm, tn), lambda r, n: (_row_block(r), jnp.minimum(_col_block(r, n), nq - 1))),
            pl.BlockSpec((tm, tn), lambda r, n: (_row_block(r), jnp.maximum(_col_block(r, n) - nq, 0))),
        ],
        scratch_shapes=[pltpu.VMEM((tm, d), BF16), pltpu.VMEM((2 * nq, d, tn), BF16)],
        compiler_params=_params("arbitrary", "arbitrary"),
        name="qz_proj",
    )(x1, g, w, cos, sin)


def _rope_tables(pos):
    inv_freq = ROPE_THETA ** (-jnp.arange(0, HEAD_DIM, 2, dtype=F32) / HEAD_DIM)
    ang = pos.astype(F32)[:, None] * inv_freq[None, :]
    c, s = jnp.cos(ang), jnp.sin(ang)
    return jnp.concatenate([c, c], axis=-1), jnp.concatenate([-s, s], axis=-1)


def _diff_lambda(lq1_ref, lk1_ref, lq2_ref, lk2_ref, lambda_init):
    s1 = jnp.sum(lq1_ref[...] * lk1_ref[...], axis=-1, keepdims=True)
    s2 = jnp.sum(lq2_ref[...] * lk2_ref[...], axis=-1, keepdims=True)
    return jnp.exp(s1) - jnp.exp(s2) + lambda_init


def _attn_finish(a1, l1, a2, l2, lam, subln, z, lambda_init):
    o = a1 / l1 - lam * (a2 / l2)
    o = _rms_scale(o, subln) * (1.0 - lambda_init)
    return (o * jax.nn.silu(z)).astype(BF16)


def _softmax_update(m_sc, l_sc, a_sc, idx, s, v):
    m_old = m_sc[idx]
    m_new = jnp.maximum(m_old, jnp.max(s, axis=-1, keepdims=True))
    p = jnp.exp(s - m_new)
    alpha = jnp.exp(m_old - m_new)
    l_sc[idx] = alpha * l_sc[idx] + jnp.sum(p, axis=-1, keepdims=True)
    a_sc[idx] = alpha * a_sc[idx] + jnp.dot(p.astype(BF16), v, preferred_element_type=F32)
    m_sc[idx] = m_new


def _attn_prompt_kernel(q_ref, k1_ref, k2_ref, v1_ref, v2_ref, z_ref,
                        lq1_ref, lk1_ref, lq2_ref, lk2_ref, sub_ref,
                        o_ref, m_sc, l_sc, al_sc, a_sc, s_sc, p_sc, *, tq, tk, lambda_init):
    qi = pl.program_id(1)
    c2 = HEAD_DIM ** -0.5 * math.log2(math.e)
    k_refs = (k1_ref, k2_ref)
    nsub = tq // tk
    nc = tk // LANES
    m_sc[...] = jnp.full(m_sc.shape, NEG, F32)
    l_sc[...] = jnp.zeros(l_sc.shape, F32)
    a_sc[...] = jnp.zeros(a_sc.shape, F32)

    def softmax_strips(sub, n, col0):
        for r in range(tk // STRIP):
            row0 = sub * tk + r * STRIP
            lr = slice(r * STRIP, (r + 1) * STRIP)
            gr = slice(row0, row0 + STRIP)
            s = s_sc[2 * sub + n, lr, :]
            if col0 is not None:
                visible = (row0 // CHUNK + 1) * CHUNK - col0
                if visible < tk:
                    col = lax.broadcasted_iota(jnp.int32, (STRIP, tk), 1)
                    s = jnp.where(col < visible, s, NEG)
            cols = [s[:, c * LANES:(c + 1) * LANES] for c in range(nc)]
            m_cur = jnp.max(functools.reduce(jnp.maximum, cols), axis=-1, keepdims=True)
            m_old = m_sc[n, gr, :]
            m_new = jnp.maximum(m_old, m_cur)
            ps = [jnp.exp2((c - m_new) * c2) for c in cols]
            l_cur = jnp.sum(functools.reduce(jnp.add, ps), axis=-1, keepdims=True)
            alpha = jnp.exp2((m_old - m_new) * c2)
            l_sc[n, gr, :] = alpha * l_sc[n, gr, :] + l_cur
            m_sc[n, gr, :] = m_new
            al_sc[n, gr, :] = alpha
            for c in range(nc):
                p_sc[2 * sub + n, lr, c * LANES:(c + 1) * LANES] = ps[c].astype(BF16)

    def block(kb, col0, subs):
        rows = pl.ds(pl.multiple_of(kb * tk, tk), tk)
        vblk = jnp.concatenate([v1_ref[rows, :], v2_ref[rows, :]], axis=1)
        for sub in subs:
            for n in range(2):
                qn = q_ref[sub * tk:(sub + 1) * tk, n * HEAD_DIM:(n + 1) * HEAD_DIM]
                s_sc[2 * sub + n] = lax.dot_general(qn, k_refs[n][rows, :], _NT,
                                                    preferred_element_type=F32)
        for sub in subs:
            gr = slice(sub * tk, (sub + 1) * tk)
            for n in range(2):
                softmax_strips(sub, n, col0)
                pv = jnp.dot(p_sc[2 * sub + n], vblk, preferred_element_type=F32)
                alpha = al_sc[n, gr, :]
                a_sc[n, gr, :] = a_sc[n, gr, :] * jnp.concatenate([alpha, alpha], axis=1) + pv

    def body(kb, c):
        block(kb, None, range(nsub))
        return c

    lax.fori_loop(0, qi * nsub, body, 0)
    for d in range(nsub):
        block(qi * nsub + d, d * tk, range(d, nsub))

    lam = _diff_lambda(lq1_ref, lk1_ref, lq2_ref, lk2_ref, lambda_init)
    l1 = jnp.concatenate([l_sc[0], l_sc[0]], axis=1)
    l2 = jnp.concatenate([l_sc[1], l_sc[1]], axis=1)
    o_ref[...] = _attn_finish(a_sc[0], l1, a_sc[1], l2, lam, sub_ref[...], z_ref[...], lambda_init)


def _attn_prompt(qb, z32, kvb, lq1, lk1, lq2, lk2, subln, lambda_init, tq, tk):
    rows = qb.shape[0]
    hw = 2 * HEAD_DIM
    assert tq % tk == 0 and tq % STRIP == 0 and CHUNK % STRIP == 0
    vec = pl.BlockSpec((1, HEAD_DIM), lambda h, i: (0, 0))
    kcol = lambda c: pl.BlockSpec((rows, HEAD_DIM), lambda h, i: (0, c * N_HEADS + h))
    stat = pltpu.VMEM((2, tq, LANES), F32)
    return pl.pallas_call(
        functools.partial(_attn_prompt_kernel, tq=tq, tk=tk, lambda_init=lambda_init),
        out_shape=jax.ShapeDtypeStruct((rows, N_HEADS * hw), BF16),
        grid=(N_HEADS, rows // tq),
        in_specs=[
            pl.BlockSpec((tq, hw), lambda h, i: (i, h)),
            kcol(0), kcol(1), kcol(2), kcol(3),
            pl.BlockSpec((tq, hw), lambda h, i: (i, h)),
            vec, vec, vec, vec,
            pl.BlockSpec((1, hw), lambda h, i: (0, 0)),
        ],
        out_specs=pl.BlockSpec((tq, hw), lambda h, i: (i, h)),
        scratch_shapes=[
            stat, stat, stat,
            pltpu.VMEM((2, tq, hw), F32),
            pltpu.VMEM((2 * (tq // tk), tk, tk), F32),
            pltpu.VMEM((2 * (tq // tk), tk, tk), BF16),
        ],
        compiler_params=_params("arbitrary", "arbitrary"),
        name="attn_prompt",
    )(qb, kvb, kvb, kvb, kvb, z32, lq1, lk1, lq2, lk2, subln)


def _attn_sample_kernel(q_ref, ck1_ref, ck2_ref, cv1_ref, cv2_ref, kvn_ref, z_ref,
                        lq1_ref, lk1_ref, lq2_ref, lk2_ref, sub_ref,
                        o_ref, m_sc, l_sc, a_sc, *, tkv, nkb, past_len, lambda_init):
    kb = pl.program_id(1)
    scale = HEAD_DIM ** -0.5
    t = q_ref.shape[0]
    hw = 2 * HEAD_DIM
    ck_refs = (ck1_ref, ck2_ref)
    assert (past_len - 1) // CHUNK <= past_len // CHUNK

    @pl.when(kb == 0)
    def _():
        m_sc[...] = jnp.full(m_sc.shape, NEG, F32)
        l_sc[...] = jnp.zeros(l_sc.shape, F32)
        a_sc[...] = jnp.zeros(a_sc.shape, F32)

    for h in range(N_HEADS):
        rows = pl.ds(h, tkv, stride=N_HEADS)
        v = jnp.concatenate([cv1_ref[rows, :], cv2_ref[rows, :]], axis=1).astype(BF16)
        for n in range(2):
            qn = q_ref[:, (2 * h + n) * HEAD_DIM:(2 * h + n + 1) * HEAD_DIM]
            k = ck_refs[n][rows, :].astype(BF16)
            s = lax.dot_general(qn, k, _NT, preferred_element_type=F32) * scale
            _softmax_update(m_sc, l_sc, a_sc, 2 * h + n, s, v)

    @pl.when(kb == nkb - 1)
    def _():
        row = lax.broadcasted_iota(jnp.int32, (t, t), 0) + past_len
        col = lax.broadcasted_iota(jnp.int32, (t, t), 1) + past_len
        new_mask = (col // CHUNK) <= (row // CHUNK)
        lam = _diff_lambda(lq1_ref, lk1_ref, lq2_ref, lk2_ref, lambda_init)
        sub = sub_ref[...]
        col_blk = lambda c, h: slice((c * N_HEADS + h) * HEAD_DIM, (c * N_HEADS + h + 1) * HEAD_DIM)
        for h in range(N_HEADS):
            vn = jnp.concatenate([kvn_ref[:, col_blk(2, h)], kvn_ref[:, col_blk(3, h)]], axis=1)
            for n in range(2):
                qn = q_ref[:, (2 * h + n) * HEAD_DIM:(2 * h + n + 1) * HEAD_DIM]
                s = lax.dot_general(qn, kvn_ref[:, col_blk(n, h)], _NT,
                                    preferred_element_type=F32) * scale
                _softmax_update(m_sc, l_sc, a_sc, 2 * h + n, jnp.where(new_mask, s, NEG), vn)
            o_ref[:, h * hw:(h + 1) * hw] = _attn_finish(
                a_sc[2 * h], l_sc[2 * h], a_sc[2 * h + 1], l_sc[2 * h + 1], lam, sub,
                z_ref[:, h * hw:(h + 1) * hw], lambda_init)


def _attn_sample(qb, z32, kvb, cache_k2, cache_v2, lq1, lk1, lq2, lk2, subln, lambda_init, t, tkv):
    bsz, rows8, hw = cache_k2.shape
    past_len = rows8 // N_HEADS
    nkb = past_len // tkv
    vec = pl.BlockSpec((1, HEAD_DIM), lambda b, k: (0, 0))
    cache = lambda c: pl.BlockSpec((None, tkv * N_HEADS, HEAD_DIM), lambda b, k: (b, k, c))
    full = lambda a: pl.BlockSpec((t, a.shape[1]), lambda b, k: (b, 0))
    return pl.pallas_call(
        functools.partial(_attn_sample_kernel, tkv=tkv, nkb=nkb, past_len=past_len,
                          lambda_init=lambda_init),
        out_shape=jax.ShapeDtypeStruct((bsz * t, N_HEADS * hw), BF16),
        grid=(bsz, nkb),
        in_specs=[
            full(qb), cache(0), cache(1), cache(0), cache(1), full(kvb), full(z32),
            vec, vec, vec, vec,
            pl.BlockSpec((1, hw), lambda b, k: (0, 0)),
        ],
        out_specs=pl.BlockSpec((t, N_HEADS * hw), lambda b, k: (b, 0)),
        scratch_shapes=[
            pltpu.VMEM((2 * N_HEADS, t, 1), F32),
            pltpu.VMEM((2 * N_HEADS, t, 1), F32),
            pltpu.VMEM((2 * N_HEADS, t, hw), F32),
        ],
        compiler_params=_params("arbitrary", "arbitrary"),
        name="attn_sample",
    )(qb, cache_k2, cache_k2, cache_v2, cache_v2, kvb, z32, lq1, lk1, lq2, lk2, subln)


def _b_out_kernel(o_ref, w_ref, x_ref, g_ref, y_ref, wb_ref):
    r = pl.program_id(0)

    @pl.when(r == 0)
    def _():
        wb_ref[...] = w_ref[...].astype(BF16)

    @pl.when(r > 0)
    def _():
        acc = jnp.dot(o_ref[...], wb_ref[...], preferred_element_type=F32)
        y_ref[...] = _rms_scale(x_ref[...] + acc, g_ref[...])


def _b_out_proj(og, w_out, x1, row0, g, tm):
    rows, e = og.shape
    d = w_out.shape[1]
    m0 = row0 // tm
    return pl.pallas_call(
        _b_out_kernel,
        out_shape=jax.ShapeDtypeStruct((rows, d), F32),
        grid=(1 + rows // tm,),
        in_specs=[
            pl.BlockSpec((tm, e), lambda r: (_row_block(r), 0)),
            _resident((e, d), lambda r: (0, 0)),
            pl.BlockSpec((tm, d), lambda r: (m0 + _row_block(r), 0)),
            _resident((1, d), lambda r: (0, 0)),
        ],
        out_specs=pl.BlockSpec((tm, d), lambda r: (_row_block(r), 0)),
        scratch_shapes=[pltpu.VMEM((e, d), BF16)],
        compiler_params=_params("arbitrary"),
        name="b_out_proj",
    )(og, w_out, x1, g)


def kernel(x_prompt, x_sample, state_ssm_re, state_ssm_im, cache_k, cache_v, a_norm, a_w_in, a_lambda_re, a_lambda_im, a_log_dt, a_b_re, a_b_im, a_c_re, a_c_im, a_d, a_w_glu, a_w_out, kv_norm, w_kv, b_norm, b_w_in, b_lambda_q1, b_lambda_k1, b_lambda_q2, b_lambda_k2, b_subln, b_w_out, final_norm):
    bp, seq, d = x_prompt.shape
    bs, t_s, _ = x_sample.shape
    past_len = cache_k.shape[1]
    n_a, n_b = a_norm.shape[0], b_norm.shape[0]
    g_cnt, p = a_lambda_re.shape[1:]
    assert bp == 1 and n_a == 1 and n_b == 1
    assert t_s == SSM_T and seq % (SSM_T * bs) == 0 and seq % CHUNK == 0
    assert cache_k.shape[2:] == (N_HEADS, 2 * HEAD_DIM)

    jp, js = seq // SSM_T, bs
    jpad = -(-(jp + js) // CHUNK_TILE) * CHUNK_TILE
    rows_s = bs * t_s
    rows_a = seq + rows_s

    xp = x_prompt.reshape(seq, d)
    xs = x_sample.reshape(rows_s, d)
    uz = _a_in_proj(xp, xs, a_norm[0].reshape(1, d), a_w_in[0], jpad * SSM_T, tn=1024)
    mt, bend, cin, a16r, a16i = _ssm_prep(a_lambda_re[0], a_lambda_im[0], a_log_dt[0],
                                          a_b_re[0], a_b_im[0], a_c_re[0], a_c_im[0])
    s_re, s_im = _ssm_state_contrib(uz, bend, jpad)
    h0r = state_ssm_re[:, 0].reshape(bs, g_cnt * p)
    h0i = state_ssm_im[:, 0].reshape(bs, g_cnt * p)
    hp_re, hp_im, pre, pim, sre, sim = _ssm_scan(s_re, s_im, a16r, a16i, h0r, h0i, jp)
    yg = _ssm_output(uz, mt, cin, hp_re, hp_im, a_d[0].reshape(1, -1), jpad)
    o_a = _a_glu(yg, uz, a_w_glu[0], rows_a)
    x1 = _a_out_proj(o_a, a_w_out[0], xp, xs)

    lambda_init = 0.8 - 0.6 * math.exp(-0.3 * n_a)
    w_q = b_w_in[0]
    kv_g = kv_norm.reshape(1, d)
    b_g = b_norm[0].reshape(1, d)
    pos_p = jnp.arange(seq, dtype=jnp.int32)
    pos_s = jnp.tile(past_len + jnp.arange(t_s, dtype=jnp.int32), bs)
    cos_p, sin_p = _rope_tables(pos_p)
    cos_s, sin_s = _rope_tables(pos_s)
    k_p, v_p, kvb_p = _kv_proj(x1, 0, seq, kv_g, w_kv, cos_p, sin_p)
    k_s, v_s, kvb_s = _kv_proj(x1, seq, rows_s, kv_g, w_kv, cos_s, sin_s)
    qb_p, z_p = _qz_proj(x1, 0, seq, b_g, w_q, cos_p, sin_p, tn=1024)
    qb_s, z_s = _qz_proj(x1, seq, rows_s, b_g, w_q, cos_s, sin_s, tn=1024)

    vecs = [a[0].reshape(1, HEAD_DIM) for a in (b_lambda_q1, b_lambda_k1, b_lambda_q2, b_lambda_k2)]
    subln = b_subln[0].reshape(1, 2 * HEAD_DIM)
    hw = 2 * HEAD_DIM
    og_p = _attn_prompt(qb_p, z_p, kvb_p, *vecs, subln, lambda_init, tq=1024, tk=512)
    og_s = _attn_sample(qb_s, z_s, kvb_s, cache_k.reshape(bs, past_len * N_HEADS, hw),
                        cache_v.reshape(bs, past_len * N_HEADS, hw), *vecs, subln, lambda_init,
                        t_s, tkv=1024)
    w_o = b_w_out[0]
    fg = final_norm.reshape(1, d)
    y_p = _b_out_proj(og_p, w_o, x1, 0, fg, ROW_TILE)
    y_s = _b_out_proj(og_s, w_o, x1, seq, fg, ROW_TILE)

    return (y_p.reshape(bp, seq, d), y_s.reshape(bs, t_s, d),
            pre.reshape(bp, n_a, g_cnt, p), pim.reshape(bp, n_a, g_cnt, p),
            k_p.reshape(bp, seq, N_HEADS, hw), v_p.reshape(bp, seq, N_HEADS, hw),
            sre.reshape(bs, n_a, g_cnt, p), sim.reshape(bs, n_a, g_cnt, p),
            k_s.reshape(bs, t_s, N_HEADS, hw), v_s.reshape(bs, t_s, N_HEADS, hw))
```

```python
import functools
import math

import jax
import jax.numpy as jnp
from jax import lax
from jax.experimental import pallas as pl
from jax.experimental.pallas import tpu as pltpu

F32 = jnp.float32
BF16 = jnp.bfloat16

CHUNK = 64
HEAD_DIM = 128
N_HEADS = 8
SSM_GROUP = 16
SSM_STATE = 64
SSM_T = 16
ROPE_THETA = 10000.0
EPS = 1e-6
NEG = -1e30

LANES = 128
SUBLANES = 8
GROUPS_PER_TILE = LANES // SSM_GROUP
CHUNK_TILE = 128
ROW_TILE = 512
STRIP = 32
VMEM_LIMIT = 56 * 1024 * 1024

_NT = (((1,), (1,)), ((), ()))


def _params(*sem, flags=None):
    return pltpu.CompilerParams(dimension_semantics=sem, vmem_limit_bytes=VMEM_LIMIT, flags=flags)


def _resident(shape, index_map):
    return pl.BlockSpec(shape, index_map, pipeline_mode=pl.Buffered(1))


def _rms_scale(x, g):
    ms = jnp.mean(x * x, axis=-1, keepdims=True)
    return x * lax.rsqrt(ms + EPS) * g


def _staged_weight_spec(d, tn, nb):
    return pl.BlockSpec((d, tn), lambda r, n: (0, jnp.where(r == 0, n, nb - 1)),
                        pipeline_mode=pl.Buffered(1))


def _row_block(r):
    return jnp.maximum(r - 1, 0)


def _col_block(r, n):
    return jnp.where(r == 0, 0, n)


def _a_in_kernel(xp_ref, xs_ref, g_ref, w_ref, o_ref, xn_ref, wb_ref, *, mp, nb):
    m, n = pl.program_id(0) - 1, pl.program_id(1)

    @pl.when(m < 0)
    def _():
        wb_ref[n] = w_ref[...].astype(BF16)

    @pl.when((n == 0) & (m >= 0) & (m < mp))
    def _():
        xn_ref[...] = _rms_scale(xp_ref[...], g_ref[...]).astype(BF16)

    @pl.when((n == 0) & (m == mp))
    def _():
        xn_ref[...] = _rms_scale(xs_ref[...], g_ref[...]).astype(BF16)

    for c in range(nb):
        @pl.when((m >= 0) & (m <= mp) & (n == c))
        def _():
            o_ref[...] = jnp.dot(xn_ref[...], wb_ref[c], preferred_element_type=F32)

    @pl.when(m > mp)
    def _():
        o_ref[...] = jnp.zeros(o_ref.shape, F32)


def _a_in_proj(xp, xs, g, w, rows_pad, tn):
    tm = ROW_TILE
    d = g.shape[-1]
    n_out = w.shape[1]
    mp = xp.shape[0] // tm
    nb = n_out // tn
    assert xs.shape[0] == tm and xp.shape[0] % tm == 0 and rows_pad % tm == 0
    return pl.pallas_call(
        functools.partial(_a_in_kernel, mp=mp, nb=nb),
        out_shape=jax.ShapeDtypeStruct((rows_pad, n_out), F32),
        grid=(1 + rows_pad // tm, nb),
        in_specs=[
            pl.BlockSpec((tm, d), lambda r, n: (jnp.minimum(_row_block(r), mp - 1), 0)),
            _resident((tm, d), lambda r, n: (0, 0)),
            _resident((1, d), lambda r, n: (0, 0)),
            _staged_weight_spec(d, tn, nb),
        ],
        out_specs=pl.BlockSpec((tm, tn), lambda r, n: (_row_block(r), _col_block(r, n))),
        scratch_shapes=[pltpu.VMEM((tm, d), BF16), pltpu.VMEM((nb, d, tn), BF16)],
        compiler_params=_params("arbitrary", "arbitrary"),
        name="a_in_proj",
    )(xp, xs, g, w)


PREP_PAIRS = LANES // SSM_T


def _ssm_prep_kernel(lrx_ref, lix_ref, dtx_ref, bre_ref, bim_ref, lrr_ref, lir_ref, dtr_ref,
                     cre_ref, cim_ref, mt_ref, bend_ref, cin_ref, a16r_ref, a16i_ref):
    p = SSM_STATE
    p2 = 2 * p
    w = SSM_T * SSM_GROUP
    hi = lax.Precision.HIGHEST

    lr, li = lrx_ref[...], lix_ref[...]
    dt = jnp.exp(dtx_ref[...])
    zr, zi = lr * dt, li * dt
    mag = jnp.exp(zr)
    n_re, n_im = mag * jnp.cos(zi) - 1.0, mag * jnp.sin(zi)
    den = lr * lr + li * li
    cf_re = (n_re * lr + n_im * li) / den
    cf_im = (n_im * lr - n_re * li) / den
    lane = lax.broadcasted_iota(jnp.int32, (p2, LANES), 1)
    e_end = (SSM_T - 1 - lane % SSM_T).astype(F32)
    pm = jnp.exp(zr * e_end)
    pw_re, pw_im = pm * jnp.cos(zi * e_end), pm * jnp.sin(zi * e_end)
    cp_re = cf_re * pw_re - cf_im * pw_im
    cp_im = cf_re * pw_im + cf_im * pw_re
    coef = jnp.concatenate([cp_re, cp_im, cf_re, cf_im], axis=0)
    sel_row = lax.broadcasted_iota(jnp.int32, (LANES, w), 0)
    sel_col = lax.broadcasted_iota(jnp.int32, (LANES, w), 1)

    tau = lax.broadcasted_iota(jnp.int32, (SSM_T + SUBLANES, p2), 0).astype(F32)
    lane2 = lax.broadcasted_iota(jnp.int32, (w, p2), 1)
    sblk = lax.broadcasted_iota(jnp.int32, (w, w), 1) // SSM_GROUP

    for j in range(PREP_PAIRS):
        own_pair = (sel_row // SSM_T) == j
        spread = jnp.where(own_pair & (sel_row % SSM_T == sel_col // SSM_GROUP), 1.0, 0.0)
        cpx = jnp.dot(coef[:2 * p2], spread, precision=hi, preferred_element_type=F32)
        cfx = jnp.broadcast_to(coef[2 * p2:, j * SSM_T:j * SSM_T + 1], (2 * p2, w))
        b_re, b_im = bre_ref[j], bim_ref[j]
        end_re = cpx[:p2] * b_re - cpx[p2:] * b_im
        end_im = cpx[:p2] * b_im + cpx[p2:] * b_re
        bb_re = cfx[:p2] * b_re - cfx[p2:] * b_im
        bb_im = cfx[:p2] * b_im + cfx[p2:] * b_re
        for r in range(2):
            sl = slice(r * p, (r + 1) * p)
            bend_ref[2 * j + r] = jnp.concatenate([end_re[sl], end_im[sl]], axis=0).astype(BF16)
        bst = jnp.concatenate([bb_re, bb_im], axis=0)

        lr2, li2 = lrr_ref[j], lir_ref[j]
        dt2 = jnp.exp(dtr_ref[j])
        zr2, zi2 = lr2 * dt2, li2 * dt2
        qm = jnp.exp(zr2 * tau)
        q_re, q_im = qm * jnp.cos(zi2 * tau), qm * jnp.sin(zi2 * tau)
        c_re, c_im = cre_ref[j], cim_ref[j]

        def c_times_powers(first):
            re = [c_re * q_re[t:t + 1] - c_im * q_im[t:t + 1] for t in range(first, first + SSM_T)]
            im = [c_re * q_im[t:t + 1] + c_im * q_re[t:t + 1] for t in range(first, first + SSM_T)]
            return jnp.concatenate(re, axis=0), jnp.concatenate(im, axis=0)

        g_re, g_im = c_times_powers(0)
        ci_re, ci_im = c_times_powers(1)
        for r in range(2):
            own = (lane2 // p) == r
            lhs = jnp.concatenate([jnp.where(own, g_re, 0.0), jnp.where(own, -g_im, 0.0)], axis=1)
            gen = jnp.dot(lhs, bst, preferred_element_type=F32)
            k = 0
            while (SSM_GROUP << k) < w:
                sh = SSM_GROUP << k
                shifted = jnp.concatenate([jnp.zeros((sh, w), F32), gen[:w - sh]], axis=0)
                gen = jnp.where(((sblk >> k) & 1) == 1, shifted, gen)
                k += 1
            mt_ref[2 * j + r] = gen.astype(BF16)
            cin_ref[2 * j + r] = jnp.concatenate(
                [jnp.where(own, ci_re, 0.0), jnp.where(own, -ci_im, 0.0)], axis=1).astype(BF16)
        a16r_ref[j] = q_re[SSM_T:SSM_T + 1]
        a16i_ref[j] = q_im[SSM_T:SSM_T + 1]


def _ssm_prep(lam_re, lam_im, log_dt, b_re, b_im, c_re, c_im):
    g, p = lam_re.shape
    gp = g // 2
    p2 = 2 * p
    w = SSM_T * SSM_GROUP
    pp = PREP_PAIRS
    assert gp % pp == 0
    colx = lambda a: jnp.repeat(a.reshape(gp, p2).T, SSM_T, axis=1)
    row = lambda a: a.reshape(gp, 1, p2)
    dt_full = jnp.broadcast_to(log_dt[:, None], (g, p))
    b_t = lambda a: jnp.tile(a.reshape(gp, p2, SSM_GROUP), (1, 1, SSM_T))
    c_pair = lambda a: a.reshape(gp, 2, SSM_GROUP, p).transpose(0, 2, 1, 3).reshape(gp, SSM_GROUP, p2)
    xspec = pl.BlockSpec((p2, LANES), lambda i: (0, i))
    rowspec = pl.BlockSpec((pp, 1, p2), lambda i: (i, 0, 0))
    bspec = pl.BlockSpec((pp, p2, w), lambda i: (i, 0, 0))
    cspec = pl.BlockSpec((pp, SSM_GROUP, p2), lambda i: (i, 0, 0))
    mt, bend, cin, a16r, a16i = pl.pallas_call(
        _ssm_prep_kernel,
        out_shape=[
            jax.ShapeDtypeStruct((g, w, w), BF16),
            jax.ShapeDtypeStruct((g, p2, w), BF16),
            jax.ShapeDtypeStruct((g, w, 2 * p2), BF16),
            jax.ShapeDtypeStruct((gp, 1, p2), F32),
            jax.ShapeDtypeStruct((gp, 1, p2), F32),
        ],
        grid=(gp // pp,),
        in_specs=[xspec, xspec, xspec, bspec, bspec, rowspec, rowspec, rowspec, cspec, cspec],
        out_specs=[
            pl.BlockSpec((2 * pp, w, w), lambda i: (i, 0, 0)),
            pl.BlockSpec((2 * pp, p2, w), lambda i: (i, 0, 0)),
            pl.BlockSpec((2 * pp, w, 2 * p2), lambda i: (i, 0, 0)),
            rowspec, rowspec,
        ],
        compiler_params=_params("arbitrary"),
        name="ssm_prep",
    )(colx(lam_re), colx(lam_im), colx(dt_full), b_t(b_re), b_t(b_im),
      row(lam_re), row(lam_im), row(dt_full), c_pair(c_re), c_pair(c_im))
    return mt, bend, cin, a16r.reshape(1, g * p), a16i.reshape(1, g * p)


def _step_rows(s):
    return pl.ds(s, CHUNK_TILE, stride=SSM_T)


def _build_ut(u_ref, ut_ref):
    for s in range(SSM_T):
        xt = u_ref[_step_rows(s), :].T.astype(BF16)
        for gl in range(GROUPS_PER_TILE):
            ut_ref[gl, s * SSM_GROUP:(s + 1) * SSM_GROUP, :] = xt[gl * SSM_GROUP:(gl + 1) * SSM_GROUP, :]


def _ssm_state_kernel(u_ref, bend_ref, sre_ref, sim_ref, ut_ref):
    _build_ut(u_ref, ut_ref)
    p = SSM_STATE
    for pr in range(GROUPS_PER_TILE // 2):
        st = [jnp.dot(bend_ref[2 * pr + r], ut_ref[2 * pr + r], preferred_element_type=F32)
              for r in range(2)]
        sre_ref[:, pr * 2 * p:(pr + 1) * 2 * p] = jnp.concatenate([st[0][:p], st[1][:p]], axis=0).T
        sim_ref[:, pr * 2 * p:(pr + 1) * 2 * p] = jnp.concatenate([st[0][p:], st[1][p:]], axis=0).T


def _ssm_state_contrib(uz, bend, jpad):
    g, p2, w = bend.shape
    ntile = g // GROUPS_PER_TILE
    sw = GROUPS_PER_TILE * SSM_STATE
    out = jax.ShapeDtypeStruct((jpad, g * SSM_STATE), F32)
    ospec = pl.BlockSpec((CHUNK_TILE, sw), lambda j, q: (j, q))
    return pl.pallas_call(
        _ssm_state_kernel,
        out_shape=[out, out],
        grid=(jpad // CHUNK_TILE, ntile),
        in_specs=[
            pl.BlockSpec((CHUNK_TILE * SSM_T, LANES), lambda j, q: (j, q)),
            pl.BlockSpec((GROUPS_PER_TILE, p2, w), lambda j, q: (q, 0, 0)),
        ],
        out_specs=[ospec, ospec],
        scratch_shapes=[pltpu.VMEM((GROUPS_PER_TILE, w, CHUNK_TILE), BF16)],
        compiler_params=_params("arbitrary", "arbitrary"),
        name="ssm_state_contrib",
    )(uz, bend)


def _ssm_scan_kernel(sre_ref, sim_ref, ar_ref, ai_ref, h0r_ref, h0i_ref,
                     hpr_ref, hpi_ref, pr_ref, pi_ref, sr_ref, si_ref, hr_sc, hi_sc, *, n_prompt):
    i = pl.program_id(0)
    rb = sre_ref.shape[0]
    ar, ai = ar_ref[...], ai_ref[...]

    @pl.when(i == 0)
    def _():
        hr_sc[...] = jnp.zeros_like(hr_sc)
        hi_sc[...] = jnp.zeros_like(hi_sc)

    @pl.when(i < n_prompt)
    def _():
        def body(j, c):
            row = pl.ds(j, 1)
            hr, hi = hr_sc[...], hi_sc[...]
            hpr_ref[row, :] = hr
            hpi_ref[row, :] = hi
            hr_sc[...] = ar * hr - ai * hi + sre_ref[row, :]
            hi_sc[...] = ar * hi + ai * hr + sim_ref[row, :]
            return c

        lax.fori_loop(0, rb, body, 0)

    @pl.when(i == n_prompt - 1)
    def _():
        pr_ref[...] = hr_sc[...]
        pi_ref[...] = hi_sc[...]

    @pl.when(i == n_prompt)
    def _():
        h0r, h0i = h0r_ref[...], h0i_ref[...]
        hpr_ref[...] = h0r
        hpi_ref[...] = h0i
        sr_ref[...] = ar * h0r - ai * h0i + sre_ref[...]
        si_ref[...] = ar * h0i + ai * h0r + sim_ref[...]

    @pl.when(i > n_prompt)
    def _():
        hpr_ref[...] = jnp.zeros_like(hpr_ref)
        hpi_ref[...] = jnp.zeros_like(hpi_ref)


def _ssm_scan(s_re, s_im, a16r, a16i, h0r, h0i, jp):
    jpad, n = s_re.shape
    rb = h0r.shape[0]
    assert jp % rb == 0 and jpad % rb == 0
    n_prompt = jp // rb
    rows = pl.BlockSpec((rb, n), lambda i: (i, 0))
    const1 = pl.BlockSpec((1, n), lambda i: (0, 0))
    constb = pl.BlockSpec((rb, n), lambda i: (0, 0))
    big = jax.ShapeDtypeStruct((jpad, n), F32)
    one = jax.ShapeDtypeStruct((1, n), F32)
    bat = jax.ShapeDtypeStruct((rb, n), F32)
    return pl.pallas_call(
        functools.partial(_ssm_scan_kernel, n_prompt=n_prompt),
        out_shape=[big, big, one, one, bat, bat],
        grid=(jpad // rb,),
        in_specs=[rows, rows, const1, const1, constb, constb],
        out_specs=[rows, rows, const1, const1, constb, constb],
        scratch_shapes=[pltpu.VMEM((1, n), F32), pltpu.VMEM((1, n), F32)],
        compiler_params=_params("arbitrary"),
        name="ssm_scan",
    )(s_re, s_im, a16r, a16i, h0r, h0i)


def _ssm_out_kernel(u_ref, mt_ref, cin_ref, hpr_ref, hpi_ref, d_ref, o_ref, ut_ref, yt_ref):
    _build_ut(u_ref, ut_ref)
    p2 = 2 * SSM_STATE
    for gl in range(GROUPS_PER_TILE):
        pr = gl // 2
        hp = jnp.concatenate([hpr_ref[:, pr * p2:(pr + 1) * p2], hpi_ref[:, pr * p2:(pr + 1) * p2]],
                             axis=1).astype(BF16)
        yt = jnp.dot(mt_ref[gl], ut_ref[gl], preferred_element_type=F32)
        yt = yt + lax.dot_general(cin_ref[gl], hp, _NT, preferred_element_type=F32)
        for t in range(SSM_T):
            yt_ref[t, gl * SSM_GROUP:(gl + 1) * SSM_GROUP, :] = yt[t * SSM_GROUP:(t + 1) * SSM_GROUP, :]
    d = d_ref[...]
    for t in range(SSM_T):
        y = yt_ref[t].T + d * u_ref[_step_rows(t), :]
        o_ref[_step_rows(t), :] = jax.nn.gelu(y)


def _ssm_output(uz, mt, cin, hp_re, hp_im, d_skip, jpad):
    g, w, _ = mt.shape
    w2 = cin.shape[-1]
    ntile = g // GROUPS_PER_TILE
    sw = GROUPS_PER_TILE * SSM_STATE
    tok = pl.BlockSpec((CHUNK_TILE * SSM_T, LANES), lambda j, q: (j, q))
    hspec = pl.BlockSpec((CHUNK_TILE, sw), lambda j, q: (j, q))
    return pl.pallas_call(
        _ssm_out_kernel,
        out_shape=jax.ShapeDtypeStruct((jpad * SSM_T, g * SSM_GROUP), F32),
        grid=(jpad // CHUNK_TILE, ntile),
        in_specs=[
            tok,
            pl.BlockSpec((GROUPS_PER_TILE, w, w), lambda j, q: (q, 0, 0)),
            pl.BlockSpec((GROUPS_PER_TILE, w, w2), lambda j, q: (q, 0, 0)),
            hspec, hspec,
            pl.BlockSpec((1, LANES), lambda j, q: (0, q)),
        ],
        out_specs=tok,
        scratch_shapes=[
            pltpu.VMEM((GROUPS_PER_TILE, w, CHUNK_TILE), BF16),
            pltpu.VMEM((SSM_T, LANES, CHUNK_TILE), F32),
        ],
        compiler_params=_params("arbitrary", "arbitrary"),
        name="ssm_output",
    )(uz, mt, cin, hp_re, hp_im, d_skip)


def _a_glu_kernel(y_ref, z_ref, w_ref, o_ref, wb_ref):
    r = pl.program_id(0)

    @pl.when(r == 0)
    def _():
        wb_ref[...] = w_ref[...].astype(BF16)

    @pl.when(r > 0)
    def _():
        y = y_ref[...]
        gate = jnp.dot(y.astype(BF16), wb_ref[...], preferred_element_type=F32)
        y2 = y * jax.nn.sigmoid(gate)
        o_ref[...] = (y2 * jax.nn.silu(z_ref[...])).astype(BF16)


def _a_glu(yg, uz, w_glu, rows):
    tm = ROW_TILE
    e = yg.shape[1]
    return pl.pallas_call(
        _a_glu_kernel,
        out_shape=jax.ShapeDtypeStruct((rows, e), BF16),
        grid=(1 + rows // tm,),
        in_specs=[
            pl.BlockSpec((tm, e), lambda r: (_row_block(r), 0)),
            pl.BlockSpec((tm, e), lambda r: (_row_block(r), 1)),
            _resident((e, e), lambda r: (0, 0)),
        ],
        out_specs=pl.BlockSpec((tm, e), lambda r: (_row_block(r), 0)),
        scratch_shapes=[pltpu.VMEM((e, e), BF16)],
        compiler_params=_params("arbitrary"),
        name="a_glu",
    )(yg, uz, w_glu)


def _a_out_kernel(o_ref, w_ref, xp_ref, xs_ref, x1_ref, wb_ref, *, mp):
    m = pl.program_id(0) - 1

    @pl.when(m < 0)
    def _():
        wb_ref[...] = w_ref[...].astype(BF16)

    @pl.when((m >= 0) & (m < mp))
    def _():
        x1_ref[...] = xp_ref[...] + jnp.dot(o_ref[...], wb_ref[...], preferred_element_type=F32)

    @pl.when(m == mp)
    def _():
        x1_ref[...] = xs_ref[...] + jnp.dot(o_ref[...], wb_ref[...], preferred_element_type=F32)


def _a_out_proj(o, w_out, xp, xs):
    tm = ROW_TILE
    rows, e = o.shape
    d = w_out.shape[1]
    mp = xp.shape[0] // tm
    return pl.pallas_call(
        functools.partial(_a_out_kernel, mp=mp),
        out_shape=jax.ShapeDtypeStruct((rows, d), F32),
        grid=(1 + rows // tm,),
        in_specs=[
            pl.BlockSpec((tm, e), lambda r: (_row_block(r), 0)),
            _resident((e, d), lambda r: (0, 0)),
            pl.BlockSpec((tm, d), lambda r: (jnp.minimum(_row_block(r), mp - 1), 0)),
            _resident((tm, d), lambda r: (0, 0)),
        ],
        out_specs=pl.BlockSpec((tm, d), lambda r: (_row_block(r), 0)),
        scratch_shapes=[pltpu.VMEM((e, d), BF16)],
        compiler_params=_params("arbitrary"),
        name="a_out_proj",
    )(o, w_out, xp, xs)


def _rope(x, cos, sin):
    outs = []
    for c in range(x.shape[1] // HEAD_DIM):
        xc = x[:, c * HEAD_DIM:(c + 1) * HEAD_DIM]
        outs.append(xc * cos + pltpu.roll(xc, HEAD_DIM // 2, axis=1) * sin)
    return jnp.concatenate(outs, axis=1) if len(outs) > 1 else outs[0]


def _store_heads(o_ref, val, tm):
    for h in range(N_HEADS):
        o_ref[pl.ds(h, tm, stride=N_HEADS), :] = val[:, h * HEAD_DIM:(h + 1) * HEAD_DIM]


def _kv_proj_kernel(x_ref, g_ref, w_ref, cos_ref, sin_ref, k_ref, v_ref, cb_ref, xn_ref, wb_ref, *, tm):
    m, n = pl.program_id(0) - 1, pl.program_id(1)
    hpb = w_ref.shape[1] // (2 * HEAD_DIM)

    for nn in range(4):
        @pl.when((m < 0) & (n == nn))
        def _():
            for hl in range(hpb):
                h = hpb * (nn % 2) + hl
                for half in range(2):
                    src = (2 * hl + half) * HEAD_DIM
                    wb_ref[2 * (nn // 2) + half, :, h * HEAD_DIM:(h + 1) * HEAD_DIM] = (
                        w_ref[:, src:src + HEAD_DIM].astype(BF16))

    @pl.when((m >= 0) & (n == 0))
    def _():
        xn_ref[...] = _rms_scale(x_ref[...], g_ref[...]).astype(BF16)

    for c in range(4):
        @pl.when((m >= 0) & (n == c))
        def _():
            acc = jnp.dot(xn_ref[...], wb_ref[c], preferred_element_type=F32)
            if c < 2:
                acc = _rope(acc, cos_ref[...], sin_ref[...])
            _store_heads(k_ref if c < 2 else v_ref, acc, tm)
            cb_ref[...] = acc.astype(BF16)


def _kv_proj(x1, row0, rows, g, w_kv, cos, sin):
    tm = ROW_TILE
    d = x1.shape[1]
    tn = N_HEADS * HEAD_DIM
    m0 = row0 // tm
    assert w_kv.shape[1] == 4 * tn
    out4 = jax.ShapeDtypeStruct((rows * N_HEADS, 2 * HEAD_DIM), F32)
    return pl.pallas_call(
        functools.partial(_kv_proj_kernel, tm=tm),
        out_shape=[out4, out4, jax.ShapeDtypeStruct((rows, 4 * tn), BF16)],
        grid=(1 + rows // tm, 4),
        in_specs=[
            pl.BlockSpec((tm, d), lambda r, n: (m0 + _row_block(r), 0)),
            _resident((1, d), lambda r, n: (0, 0)),
            _staged_weight_spec(d, tn, 4),
            pl.BlockSpec((tm, HEAD_DIM), lambda r, n: (_row_block(r), 0)),
            pl.BlockSpec((tm, HEAD_DIM), lambda r, n: (_row_block(r), 0)),
        ],
        out_specs=[
            pl.BlockSpec((tm * N_HEADS, HEAD_DIM),
                         lambda r, n: (_row_block(r), jnp.minimum(_col_block(r, n), 1))),
            pl.BlockSpec((tm * N_HEADS, HEAD_DIM),
                         lambda r, n: (_row_block(r), jnp.maximum(_col_block(r, n) - 2, 0))),
            pl.BlockSpec((tm, tn), lambda r, n: (_row_block(r), _col_block(r, n))),
        ],
        scratch_shapes=[pltpu.VMEM((tm, d), BF16), pltpu.VMEM((4, d, tn), BF16)],
        compiler_params=_params("arbitrary", "arbitrary"),
        name="kv_proj",
    )(x1, g, w_kv, cos, sin)


def _qz_proj_kernel(x_ref, g_ref, w_ref, cos_ref, sin_ref, q_ref, z_ref, xn_ref, wb_ref, *, nq):
    m, n = pl.program_id(0) - 1, pl.program_id(1)

    @pl.when(m < 0)
    def _():
        wb_ref[n] = w_ref[...].astype(BF16)

    @pl.when((m >= 0) & (n == 0))
    def _():
        xn_ref[...] = _rms_scale(x_ref[...], g_ref[...]).astype(BF16)

    for c in range(2 * nq):
        @pl.when((m >= 0) & (n == c))
        def _():
            acc = jnp.dot(xn_ref[...], wb_ref[c], preferred_element_type=F32)
            if c < nq:
                q_ref[...] = _rope(acc, cos_ref[...], sin_ref[...]).astype(BF16)
            else:
                z_ref[...] = acc


def _qz_proj(x1, row0, rows, g, w, cos, sin, tn):
    tm = ROW_TILE
    d = x1.shape[1]
    half = w.shape[1] // 2
    nq = half // tn
    m0 = row0 // tm
    return pl.pallas_call(
        functools.partial(_qz_proj_kernel, nq=nq),
        out_shape=[jax.ShapeDtypeStruct((rows, half), BF16), jax.ShapeDtypeStruct((rows, half), F32)],
        grid=(1 + rows // tm, 2 * nq),
        in_specs=[
            pl.BlockSpec((tm, d), lambda r, n: (m0 + _row_block(r), 0)),
            _resident((1, d), lambda r, n: (0, 0)),
            _staged_weight_spec(d, tn, 2 * nq),
            pl.BlockSpec((tm, HEAD_DIM), lambda r, n: (_row_block(r), 0)),
            pl.BlockSpec((tm, HEAD_DIM), lambda r, n: (_row_block(r), 0)),
        ],
        out_specs=[
            pl.BlockSpec((tm, tn), lambda r, n: (_row_block(r), jnp.minimum(_col_block(r, n), nq - 1))),
            pl.BlockSpec((tm, tn), lambda r, n: (_row_block(r), jnp.maximum(_col_block(r, n) - nq, 0))),
        ],
        scratch_shapes=[pltpu.VMEM((tm, d), BF16), pltpu.VMEM((2 * nq, d, tn), BF16)],
        compiler_params=_params("arbitrary", "arbitrary"),
        name="qz_proj",
    )(x1, g, w, cos, sin)


def _rope_tables(pos):
    inv_freq = ROPE_THETA ** (-jnp.arange(0, HEAD_DIM, 2, dtype=F32) / HEAD_DIM)
    ang = pos.astype(F32)[:, None] * inv_freq[None, :]
    c, s = jnp.cos(ang), jnp.sin(ang)
    return jnp.concatenate([c, c], axis=-1), jnp.concatenate([-s, s], axis=-1)


def _diff_lambda(lq1_ref, lk1_ref, lq2_ref, lk2_ref, lambda_init):
    s1 = jnp.sum(lq1_ref[...] * lk1_ref[...], axis=-1, keepdims=True)
    s2 = jnp.sum(lq2_ref[...] * lk2_ref[...], axis=-1, keepdims=True)
    return jnp.exp(s1) - jnp.exp(s2) + lambda_init


def _attn_finish(a1, l1, a2, l2, lam, subln, z, lambda_init):
    o = a1 / l1 - lam * (a2 / l2)
    o = _rms_scale(o, subln) * (1.0 - lambda_init)
    return (o * jax.nn.silu(z)).astype(BF16)


def _softmax_update(m_sc, l_sc, a_sc, idx, s, v):
    m_old = m_sc[idx]
    m_new = jnp.maximum(m_old, jnp.max(s, axis=-1, keepdims=True))
    p = jnp.exp(s - m_new)
    alpha = jnp.exp(m_old - m_new)
    l_sc[idx] = alpha * l_sc[idx] + jnp.sum(p, axis=-1, keepdims=True)
    a_sc[idx] = alpha * a_sc[idx] + jnp.dot(p.astype(BF16), v, preferred_element_type=F32)
    m_sc[idx] = m_new


def _attn_prompt_kernel(q_ref, k1_ref, k2_ref, v1_ref, v2_ref, z_ref,
                        lq1_ref, lk1_ref, lq2_ref, lk2_ref, sub_ref,
                        o_ref, m_sc, l_sc, al_sc, a_sc, s_sc, p_sc, *, tq, ts, tk, lambda_init):
    qi = pl.program_id(1)
    c2 = HEAD_DIM ** -0.5 * math.log2(math.e)
    k_refs = (k1_ref, k2_ref)
    nsub = tq // ts
    nc = tk // LANES
    m_sc[...] = jnp.full(m_sc.shape, NEG, F32)
    l_sc[...] = jnp.zeros(l_sc.shape, F32)
    a_sc[...] = jnp.zeros(a_sc.shape, F32)

    def softmax_strips(sub, n, col0):
        for r in range(ts // STRIP):
            row0 = sub * ts + r * STRIP
            lr = slice(r * STRIP, (r + 1) * STRIP)
            gr = slice(row0, row0 + STRIP)
            visible = tk if col0 is None else min(tk, (row0 // CHUNK + 1) * CHUNK - col0)
            assert visible > 0
            ncv = -(-visible // LANES)
            s = s_sc[2 * sub + n, lr, 0:ncv * LANES]
            if visible < ncv * LANES:
                col = lax.broadcasted_iota(jnp.int32, s.shape, 1)
                s = jnp.where(col < visible, s, NEG)
            cols = [s[:, c * LANES:(c + 1) * LANES] for c in range(ncv)]
            m_cur = jnp.max(functools.reduce(jnp.maximum, cols), axis=-1, keepdims=True)
            m_old = m_sc[n, gr, :]
            m_new = jnp.maximum(m_old, m_cur)
            ps = [jnp.exp2((c - m_new) * c2) for c in cols]
            l_cur = jnp.sum(functools.reduce(jnp.add, ps), axis=-1, keepdims=True)
            alpha = jnp.exp2((m_old - m_new) * c2)
            l_sc[n, gr, :] = alpha * l_sc[n, gr, :] + l_cur
            m_sc[n, gr, :] = m_new
            al_sc[n, gr, :] = alpha
            for c in range(nc):
                pc = ps[c].astype(BF16) if c < ncv else jnp.zeros((STRIP, LANES), BF16)
                p_sc[2 * sub + n, lr, c * LANES:(c + 1) * LANES] = pc

    def block(kb, col0, subs):
        rows = pl.ds(pl.multiple_of(kb * tk, tk), tk)
        vblk = jnp.concatenate([v1_ref[rows, :], v2_ref[rows, :]], axis=1)
        for sub in subs:
            for n in range(2):
                qn = q_ref[sub * ts:(sub + 1) * ts, n * HEAD_DIM:(n + 1) * HEAD_DIM]
                s_sc[2 * sub + n] = lax.dot_general(qn, k_refs[n][rows, :], _NT,
                                                    preferred_element_type=F32)
        for sub in subs:
            gr = slice(sub * ts, (sub + 1) * ts)
            for n in range(2):
                softmax_strips(sub, n, col0)
                pv = jnp.dot(p_sc[2 * sub + n], vblk, preferred_element_type=F32)
                alpha = al_sc[n, gr, :]
                a_sc[n, gr, :] = a_sc[n, gr, :] * jnp.concatenate([alpha, alpha], axis=1) + pv

    nd = tq // tk

    def body(kb, c):
        block(kb, None, range(nsub))
        return c

    lax.fori_loop(0, qi * nd, body, 0)
    for d in range(nd):
        block(qi * nd + d, d * tk, [sub for sub in range(nsub) if (sub + 1) * ts > d * tk])

    lam = _diff_lambda(lq1_ref, lk1_ref, lq2_ref, lk2_ref, lambda_init)
    l1 = jnp.concatenate([l_sc[0], l_sc[0]], axis=1)
    l2 = jnp.concatenate([l_sc[1], l_sc[1]], axis=1)
    o_ref[...] = _attn_finish(a_sc[0], l1, a_sc[1], l2, lam, sub_ref[...], z_ref[...], lambda_init)


def _attn_prompt(qb, z32, kvb, lq1, lk1, lq2, lk2, subln, lambda_init, tq, ts, tk):
    rows = qb.shape[0]
    hw = 2 * HEAD_DIM
    nsub = tq // ts
    assert tq % tk == 0 and tk % ts == 0 and ts % STRIP == 0 and CHUNK % STRIP == 0
    vec = pl.BlockSpec((1, HEAD_DIM), lambda h, i: (0, 0))
    kcol = lambda c: pl.BlockSpec((rows, HEAD_DIM), lambda h, i: (0, c * N_HEADS + h))
    stat = pltpu.VMEM((2, tq, LANES), F32)
    return pl.pallas_call(
        functools.partial(_attn_prompt_kernel, tq=tq, ts=ts, tk=tk, lambda_init=lambda_init),
        out_shape=jax.ShapeDtypeStruct((rows, N_HEADS * hw), BF16),
        grid=(N_HEADS, rows // tq),
        in_specs=[
            pl.BlockSpec((tq, hw), lambda h, i: (i, h)),
            kcol(0), kcol(1), kcol(2), kcol(3),
            pl.BlockSpec((tq, hw), lambda h, i: (i, h)),
            vec, vec, vec, vec,
            pl.BlockSpec((1, hw), lambda h, i: (0, 0)),
        ],
        out_specs=pl.BlockSpec((tq, hw), lambda h, i: (i, h)),
        scratch_shapes=[
            stat, stat, stat,
            pltpu.VMEM((2, tq, hw), F32),
            pltpu.VMEM((2 * nsub, ts, tk), F32),
            pltpu.VMEM((2 * nsub, ts, tk), BF16),
        ],
        compiler_params=_params("arbitrary", "arbitrary"),
        name="attn_prompt",
    )(qb, kvb, kvb, kvb, kvb, z32, lq1, lk1, lq2, lk2, subln)


def _attn_sample_kernel(q_ref, ck1_ref, ck2_ref, cv1_ref, cv2_ref, kvn_ref, z_ref,
                        lq1_ref, lk1_ref, lq2_ref, lk2_ref, sub_ref,
                        o_ref, m_sc, l_sc, a_sc, *, tkv, nkb, past_len, lambda_init):
    kb = pl.program_id(1)
    scale = HEAD_DIM ** -0.5
    t = q_ref.shape[0]
    hw = 2 * HEAD_DIM
    ck_refs = (ck1_ref, ck2_ref)
    assert (past_len - 1) // CHUNK <= past_len // CHUNK

    @pl.when(kb == 0)
    def _():
        m_sc[...] = jnp.full(m_sc.shape, NEG, F32)
        l_sc[...] = jnp.zeros(l_sc.shape, F32)
        a_sc[...] = jnp.zeros(a_sc.shape, F32)

    for h in range(N_HEADS):
        rows = pl.ds(h, tkv, stride=N_HEADS)
        v = jnp.concatenate([cv1_ref[rows, :], cv2_ref[rows, :]], axis=1).astype(BF16)
        for n in range(2):
            qn = q_ref[:, (2 * h + n) * HEAD_DIM:(2 * h + n + 1) * HEAD_DIM]
            k = ck_refs[n][rows, :].astype(BF16)
            s = lax.dot_general(qn, k, _NT, preferred_element_type=F32) * scale
            _softmax_update(m_sc, l_sc, a_sc, 2 * h + n, s, v)

    @pl.when(kb == nkb - 1)
    def _():
        row = lax.broadcasted_iota(jnp.int32, (t, t), 0) + past_len
        col = lax.broadcasted_iota(jnp.int32, (t, t), 1) + past_len
        new_mask = (col // CHUNK) <= (row // CHUNK)
        lam = _diff_lambda(lq1_ref, lk1_ref, lq2_ref, lk2_ref, lambda_init)
        sub = sub_ref[...]
        col_blk = lambda c, h: slice((c * N_HEADS + h) * HEAD_DIM, (c * N_HEADS + h + 1) * HEAD_DIM)
        for h in range(N_HEADS):
            vn = jnp.concatenate([kvn_ref[:, col_blk(2, h)], kvn_ref[:, col_blk(3, h)]], axis=1)
            for n in range(2):
                qn = q_ref[:, (2 * h + n) * HEAD_DIM:(2 * h + n + 1) * HEAD_DIM]
                s = lax.dot_general(qn, kvn_ref[:, col_blk(n, h)], _NT,
                                    preferred_element_type=F32) * scale
                _softmax_update(m_sc, l_sc, a_sc, 2 * h + n, jnp.where(new_mask, s, NEG), vn)
            o_ref[:, h * hw:(h + 1) * hw] = _attn_finish(
                a_sc[2 * h], l_sc[2 * h], a_sc[2 * h + 1], l_sc[2 * h + 1], lam, sub,
                z_ref[:, h * hw:(h + 1) * hw], lambda_init)


def _attn_sample(qb, z32, kvb, cache_k2, cache_v2, lq1, lk1, lq2, lk2, subln, lambda_init, t, tkv):
    bsz, rows8, hw = cache_k2.shape
    past_len = rows8 // N_HEADS
    nkb = past_len // tkv
    vec = pl.BlockSpec((1, HEAD_DIM), lambda b, k: (0, 0))
    cache = lambda c: pl.BlockSpec((None, tkv * N_HEADS, HEAD_DIM), lambda b, k: (b, k, c))
    full = lambda a: pl.BlockSpec((t, a.shape[1]), lambda b, k: (b, 0))
    return pl.pallas_call(
        functools.partial(_attn_sample_kernel, tkv=tkv, nkb=nkb, past_len=past_len,
                          lambda_init=lambda_init),
        out_shape=jax.ShapeDtypeStruct((bsz * t, N_HEADS * hw), BF16),
        grid=(bsz, nkb),
        in_specs=[
            full(qb), cache(0), cache(1), cache(0), cache(1), full(kvb), full(z32),
            vec, vec, vec, vec,
            pl.BlockSpec((1, hw), lambda b, k: (0, 0)),
        ],
        out_specs=pl.BlockSpec((t, N_HEADS * hw), lambda b, k: (b, 0)),
        scratch_shapes=[
            pltpu.VMEM((2 * N_HEADS, t, 1), F32),
            pltpu.VMEM((2 * N_HEADS, t, 1), F32),
            pltpu.VMEM((2 * N_HEADS, t, hw), F32),
        ],
        compiler_params=_params("arbitrary", "arbitrary"),
        name="attn_sample",
    )(qb, cache_k2, cache_k2, cache_v2, cache_v2, kvb, z32, lq1, lk1, lq2, lk2, subln)


def _b_out_kernel(o_ref, w_ref, x_ref, g_ref, y_ref, wb_ref):
    r = pl.program_id(0)

    @pl.when(r == 0)
    def _():
        wb_ref[...] = w_ref[...].astype(BF16)

    @pl.when(r > 0)
    def _():
        acc = jnp.dot(o_ref[...], wb_ref[...], preferred_element_type=F32)
        y_ref[...] = _rms_scale(x_ref[...] + acc, g_ref[...])


def _b_out_proj(og, w_out, x1, row0, g, tm):
    rows, e = og.shape
    d = w_out.shape[1]
    m0 = row0 // tm
    return pl.pallas_call(
        _b_out_kernel,
        out_shape=jax.ShapeDtypeStruct((rows, d), F32),
        grid=(1 + rows // tm,),
        in_specs=[
            pl.BlockSpec((tm, e), lambda r: (_row_block(r), 0)),
            _resident((e, d), lambda r: (0, 0)),
            pl.BlockSpec((tm, d), lambda r: (m0 + _row_block(r), 0)),
            _resident((1, d), lambda r: (0, 0)),
        ],
        out_specs=pl.BlockSpec((tm, d), lambda r: (_row_block(r), 0)),
        scratch_shapes=[pltpu.VMEM((e, d), BF16)],
        compiler_params=_params("arbitrary"),
        name="b_out_proj",
    )(og, w_out, x1, g)


def kernel(x_prompt, x_sample, state_ssm_re, state_ssm_im, cache_k, cache_v, a_norm, a_w_in, a_lambda_re, a_lambda_im, a_log_dt, a_b_re, a_b_im, a_c_re, a_c_im, a_d, a_w_glu, a_w_out, kv_norm, w_kv, b_norm, b_w_in, b_lambda_q1, b_lambda_k1, b_lambda_q2, b_lambda_k2, b_subln, b_w_out, final_norm):
    bp, seq, d = x_prompt.shape
    bs, t_s, _ = x_sample.shape
    past_len = cache_k.shape[1]
    n_a, n_b = a_norm.shape[0], b_norm.shape[0]
    g_cnt, p = a_lambda_re.shape[1:]
    assert bp == 1 and n_a == 1 and n_b == 1
    assert t_s == SSM_T and seq % (SSM_T * bs) == 0 and seq % CHUNK == 0
    assert cache_k.shape[2:] == (N_HEADS, 2 * HEAD_DIM)

    jp, js = seq // SSM_T, bs
    jpad = -(-(jp + js) // CHUNK_TILE) * CHUNK_TILE
    rows_s = bs * t_s
    rows_a = seq + rows_s

    xp = x_prompt.reshape(seq, d)
    xs = x_sample.reshape(rows_s, d)
    uz = _a_in_proj(xp, xs, a_norm[0].reshape(1, d), a_w_in[0], jpad * SSM_T, tn=1024)
    mt, bend, cin, a16r, a16i = _ssm_prep(a_lambda_re[0], a_lambda_im[0], a_log_dt[0],
                                          a_b_re[0], a_b_im[0], a_c_re[0], a_c_im[0])
    s_re, s_im = _ssm_state_contrib(uz, bend, jpad)
    h0r = state_ssm_re[:, 0].reshape(bs, g_cnt * p)
    h0i = state_ssm_im[:, 0].reshape(bs, g_cnt * p)
    hp_re, hp_im, pre, pim, sre, sim = _ssm_scan(s_re, s_im, a16r, a16i, h0r, h0i, jp)
    yg = _ssm_output(uz, mt, cin, hp_re, hp_im, a_d[0].reshape(1, -1), jpad)
    o_a = _a_glu(yg, uz, a_w_glu[0], rows_a)
    x1 = _a_out_proj(o_a, a_w_out[0], xp, xs)

    lambda_init = 0.8 - 0.6 * math.exp(-0.3 * n_a)
    w_q = b_w_in[0]
    kv_g = kv_norm.reshape(1, d)
    b_g = b_norm[0].reshape(1, d)
    pos_p = jnp.arange(seq, dtype=jnp.int32)
    pos_s = jnp.tile(past_len + jnp.arange(t_s, dtype=jnp.int32), bs)
    cos_p, sin_p = _rope_tables(pos_p)
    cos_s, sin_s = _rope_tables(pos_s)
    k_p, v_p, kvb_p = _kv_proj(x1, 0, seq, kv_g, w_kv, cos_p, sin_p)
    k_s, v_s, kvb_s = _kv_proj(x1, seq, rows_s, kv_g, w_kv, cos_s, sin_s)
    qb_p, z_p = _qz_proj(x1, 0, seq, b_g, w_q, cos_p, sin_p, tn=1024)
    qb_s, z_s = _qz_proj(x1, seq, rows_s, b_g, w_q, cos_s, sin_s, tn=1024)

    vecs = [a[0].reshape(1, HEAD_DIM) for a in (b_lambda_q1, b_lambda_k1, b_lambda_q2, b_lambda_k2)]
    subln = b_subln[0].reshape(1, 2 * HEAD_DIM)
    hw = 2 * HEAD_DIM
    og_p = _attn_prompt(qb_p, z_p, kvb_p, *vecs, subln, lambda_init, tq=1024, ts=512, tk=1024)
    og_s = _attn_sample(qb_s, z_s, kvb_s, cache_k.reshape(bs, past_len * N_HEADS, hw),
                        cache_v.reshape(bs, past_len * N_HEADS, hw), *vecs, subln, lambda_init,
                        t_s, tkv=1024)
    w_o = b_w_out[0]
    fg = final_norm.reshape(1, d)
    y_p = _b_out_proj(og_p, w_o, x1, 0, fg, ROW_TILE)
    y_s = _b_out_proj(og_s, w_o, x1, seq, fg, ROW_TILE)

    return (y_p.reshape(bp, seq, d), y_s.reshape(bs, t_s, d),
            pre.reshape(bp, n_a, g_cnt, p), pim.reshape(bp, n_a, g_cnt, p),
            k_p.reshape(bp, seq, N_HEADS, hw), v_p.reshape(bp, seq, N_HEADS, hw),
            sre.reshape(bs, n_a, g_cnt, p), sim.reshape(bs, n_a, g_cnt, p),
            k_s.reshape(bs, t_s, N_HEADS, hw), v_s.reshape(bs, t_s, N_HEADS, hw))
```

```python
import functools
import math

import jax
import jax.numpy as jnp
from jax import lax
from jax.experimental import pallas as pl
from jax.experimental.pallas import tpu as pltpu

F32 = jnp.float32
BF16 = jnp.bfloat16

CHUNK = 64
HEAD_DIM = 128
N_HEADS = 8
SSM_GROUP = 16
SSM_STATE = 64
SSM_T = 16
ROPE_THETA = 10000.0
EPS = 1e-6
NEG = -1e30

LANES = 128
SUBLANES = 8
GROUPS_PER_TILE = LANES // SSM_GROUP
CHUNK_TILE = 128
ROW_TILE = 512
STRIP = 32
VMEM_LIMIT = 56 * 1024 * 1024
KV_PROJ_VMEM_LIMIT = 58 * 1024 * 1024

_NT = (((1,), (1,)), ((), ()))


def _params(*sem, vmem_limit=VMEM_LIMIT):
    return pltpu.CompilerParams(dimension_semantics=sem, vmem_limit_bytes=vmem_limit)


def _resident(shape, index_map):
    return pl.BlockSpec(shape, index_map, pipeline_mode=pl.Buffered(1))


def _rms_scale(x, g):
    ms = jnp.mean(x * x, axis=-1, keepdims=True)
    return x * lax.rsqrt(ms + EPS) * g


def _staged_weight_spec(d, tn, nb):
    return pl.BlockSpec((d, tn), lambda r, n: (0, jnp.where(r == 0, n, nb - 1)),
                        pipeline_mode=pl.Buffered(1))


def _row_block(r):
    return jnp.maximum(r - 1, 0)


def _col_block(r, n):
    return jnp.where(r == 0, 0, n)


def _a_in_kernel(xp_ref, xs_ref, g_ref, w_ref, o_ref, xn_ref, wb_ref, *, mp, nb):
    m, n = pl.program_id(0) - 1, pl.program_id(1)

    @pl.when(m < 0)
    def _():
        wb_ref[n] = w_ref[...].astype(BF16)

    @pl.when((n == 0) & (m >= 0) & (m < mp))
    def _():
        xn_ref[...] = _rms_scale(xp_ref[...], g_ref[...]).astype(BF16)

    @pl.when((n == 0) & (m == mp))
    def _():
        xn_ref[...] = _rms_scale(xs_ref[...], g_ref[...]).astype(BF16)

    for c in range(nb):
        @pl.when((m >= 0) & (m <= mp) & (n == c))
        def _():
            o_ref[...] = jnp.dot(xn_ref[...], wb_ref[c], preferred_element_type=F32)

    @pl.when(m > mp)
    def _():
        o_ref[...] = jnp.zeros(o_ref.shape, F32)


def _a_in_proj(xp, xs, g, w, rows_pad, tn):
    tm = ROW_TILE
    d = g.shape[-1]
    n_out = w.shape[1]
    mp = xp.shape[0] // tm
    nb = n_out // tn
    assert xs.shape[0] == tm and xp.shape[0] % tm == 0 and rows_pad % tm == 0
    return pl.pallas_call(
        functools.partial(_a_in_kernel, mp=mp, nb=nb),
        out_shape=jax.ShapeDtypeStruct((rows_pad, n_out), F32),
        grid=(1 + rows_pad // tm, nb),
        in_specs=[
            pl.BlockSpec((tm, d), lambda r, n: (jnp.minimum(_row_block(r), mp - 1), 0)),
            _resident((tm, d), lambda r, n: (0, 0)),
            _resident((1, d), lambda r, n: (0, 0)),
            _staged_weight_spec(d, tn, nb),
        ],
        out_specs=pl.BlockSpec((tm, tn), lambda r, n: (_row_block(r), _col_block(r, n))),
        scratch_shapes=[pltpu.VMEM((tm, d), BF16), pltpu.VMEM((nb, d, tn), BF16)],
        compiler_params=_params("arbitrary", "arbitrary"),
        name="a_in_proj",
    )(xp, xs, g, w)


PREP_PAIRS = LANES // SSM_T


def _ssm_prep_kernel(lrx_ref, lix_ref, dtx_ref, bre_ref, bim_ref, lrr_ref, lir_ref, dtr_ref,
                     cre_ref, cim_ref, mt_ref, bend_ref, cin_ref, a16r_ref, a16i_ref):
    p = SSM_STATE
    p2 = 2 * p
    w = SSM_T * SSM_GROUP
    hi = lax.Precision.HIGHEST

    lr, li = lrx_ref[...], lix_ref[...]
    dt = jnp.exp(dtx_ref[...])
    zr, zi = lr * dt, li * dt
    mag = jnp.exp(zr)
    n_re, n_im = mag * jnp.cos(zi) - 1.0, mag * jnp.sin(zi)
    den = lr * lr + li * li
    cf_re = (n_re * lr + n_im * li) / den
    cf_im = (n_im * lr - n_re * li) / den
    lane = lax.broadcasted_iota(jnp.int32, (p2, LANES), 1)
    e_end = (SSM_T - 1 - lane % SSM_T).astype(F32)
    pm = jnp.exp(zr * e_end)
    pw_re, pw_im = pm * jnp.cos(zi * e_end), pm * jnp.sin(zi * e_end)
    cp_re = cf_re * pw_re - cf_im * pw_im
    cp_im = cf_re * pw_im + cf_im * pw_re
    coef = jnp.concatenate([cp_re, cp_im, cf_re, cf_im], axis=0)
    sel_row = lax.broadcasted_iota(jnp.int32, (LANES, w), 0)
    sel_col = lax.broadcasted_iota(jnp.int32, (LANES, w), 1)

    tau = lax.broadcasted_iota(jnp.int32, (SSM_T + SUBLANES, p2), 0).astype(F32)
    lane2 = lax.broadcasted_iota(jnp.int32, (w, p2), 1)
    sblk = lax.broadcasted_iota(jnp.int32, (w, w), 1) // SSM_GROUP

    for j in range(PREP_PAIRS):
        own_pair = (sel_row // SSM_T) == j
        spread = jnp.where(own_pair & (sel_row % SSM_T == sel_col // SSM_GROUP), 1.0, 0.0)
        cpx = jnp.dot(coef[:2 * p2], spread, precision=hi, preferred_element_type=F32)
        cfx = jnp.broadcast_to(coef[2 * p2:, j * SSM_T:j * SSM_T + 1], (2 * p2, w))
        b_re, b_im = bre_ref[j], bim_ref[j]
        end_re = cpx[:p2] * b_re - cpx[p2:] * b_im
        end_im = cpx[:p2] * b_im + cpx[p2:] * b_re
        bb_re = cfx[:p2] * b_re - cfx[p2:] * b_im
        bb_im = cfx[:p2] * b_im + cfx[p2:] * b_re
        for r in range(2):
            sl = slice(r * p, (r + 1) * p)
            bend_ref[2 * j + r] = jnp.concatenate([end_re[sl], end_im[sl]], axis=0).astype(BF16)
        bst = jnp.concatenate([bb_re, bb_im], axis=0)

        lr2, li2 = lrr_ref[j], lir_ref[j]
        dt2 = jnp.exp(dtr_ref[j])
        zr2, zi2 = lr2 * dt2, li2 * dt2
        qm = jnp.exp(zr2 * tau)
        q_re, q_im = qm * jnp.cos(zi2 * tau), qm * jnp.sin(zi2 * tau)
        c_re, c_im = cre_ref[j], cim_ref[j]

        def c_times_powers(first):
            re = [c_re * q_re[t:t + 1] - c_im * q_im[t:t + 1] for t in range(first, first + SSM_T)]
            im = [c_re * q_im[t:t + 1] + c_im * q_re[t:t + 1] for t in range(first, first + SSM_T)]
            return jnp.concatenate(re, axis=0), jnp.concatenate(im, axis=0)

        g_re, g_im = c_times_powers(0)
        ci_re, ci_im = c_times_powers(1)
        for r in range(2):
            own = (lane2 // p) == r
            lhs = jnp.concatenate([jnp.where(own, g_re, 0.0), jnp.where(own, -g_im, 0.0)], axis=1)
            gen = jnp.dot(lhs, bst, preferred_element_type=F32)
            k = 0
            while (SSM_GROUP << k) < w:
                sh = SSM_GROUP << k
                shifted = jnp.concatenate([jnp.zeros((sh, w), F32), gen[:w - sh]], axis=0)
                gen = jnp.where(((sblk >> k) & 1) == 1, shifted, gen)
                k += 1
            mt_ref[2 * j + r] = gen.astype(BF16)
            cin_ref[2 * j + r] = jnp.concatenate(
                [jnp.where(own, ci_re, 0.0), jnp.where(own, -ci_im, 0.0)], axis=1).astype(BF16)
        a16r_ref[j] = q_re[SSM_T:SSM_T + 1]
        a16i_ref[j] = q_im[SSM_T:SSM_T + 1]


def _ssm_prep(lam_re, lam_im, log_dt, b_re, b_im, c_re, c_im):
    g, p = lam_re.shape
    gp = g // 2
    p2 = 2 * p
    w = SSM_T * SSM_GROUP
    pp = PREP_PAIRS
    assert gp % pp == 0
    colx = lambda a: jnp.repeat(a.reshape(gp, p2).T, SSM_T, axis=1)
    row = lambda a: a.reshape(gp, 1, p2)
    dt_full = jnp.broadcast_to(log_dt[:, None], (g, p))
    b_t = lambda a: jnp.tile(a.reshape(gp, p2, SSM_GROUP), (1, 1, SSM_T))
    c_pair = lambda a: a.reshape(gp, 2, SSM_GROUP, p).transpose(0, 2, 1, 3).reshape(gp, SSM_GROUP, p2)
    xspec = pl.BlockSpec((p2, LANES), lambda i: (0, i))
    rowspec = pl.BlockSpec((pp, 1, p2), lambda i: (i, 0, 0))
    bspec = pl.BlockSpec((pp, p2, w), lambda i: (i, 0, 0))
    cspec = pl.BlockSpec((pp, SSM_GROUP, p2), lambda i: (i, 0, 0))
    mt, bend, cin, a16r, a16i = pl.pallas_call(
        _ssm_prep_kernel,
        out_shape=[
            jax.ShapeDtypeStruct((g, w, w), BF16),
            jax.ShapeDtypeStruct((g, p2, w), BF16),
            jax.ShapeDtypeStruct((g, w, 2 * p2), BF16),
            jax.ShapeDtypeStruct((gp, 1, p2), F32),
            jax.ShapeDtypeStruct((gp, 1, p2), F32),
        ],
        grid=(gp // pp,),
        in_specs=[xspec, xspec, xspec, bspec, bspec, rowspec, rowspec, rowspec, cspec, cspec],
        out_specs=[
            pl.BlockSpec((2 * pp, w, w), lambda i: (i, 0, 0)),
            pl.BlockSpec((2 * pp, p2, w), lambda i: (i, 0, 0)),
            pl.BlockSpec((2 * pp, w, 2 * p2), lambda i: (i, 0, 0)),
            rowspec, rowspec,
        ],
        compiler_params=_params("arbitrary"),
        name="ssm_prep",
    )(colx(lam_re), colx(lam_im), colx(dt_full), b_t(b_re), b_t(b_im),
      row(lam_re), row(lam_im), row(dt_full), c_pair(c_re), c_pair(c_im))
    return mt, bend, cin, a16r.reshape(1, g * p), a16i.reshape(1, g * p)


def _step_rows(s):
    return pl.ds(s, CHUNK_TILE, stride=SSM_T)


def _build_ut(u_ref, ut_ref):
    for s in range(SSM_T):
        xt = u_ref[_step_rows(s), :].T.astype(BF16)
        for gl in range(GROUPS_PER_TILE):
            ut_ref[gl, s * SSM_GROUP:(s + 1) * SSM_GROUP, :] = xt[gl * SSM_GROUP:(gl + 1) * SSM_GROUP, :]


STATE_TILES = 2


def _ssm_state_kernel(*refs):
    u_refs = refs[:STATE_TILES]
    bend_ref, sre_ref, sim_ref = refs[STATE_TILES:STATE_TILES + 3]
    ut_refs = refs[STATE_TILES + 3:]
    p = SSM_STATE
    for tile in range(STATE_TILES):
        _build_ut(u_refs[tile], ut_refs[tile])
        for pr in range(GROUPS_PER_TILE // 2):
            st = [jnp.dot(bend_ref[tile * GROUPS_PER_TILE + 2 * pr + r], ut_refs[tile][2 * pr + r],
                          preferred_element_type=F32)
                  for r in range(2)]
            cols = slice((tile * GROUPS_PER_TILE // 2 + pr) * 2 * p,
                         (tile * GROUPS_PER_TILE // 2 + pr + 1) * 2 * p)
            sre_ref[:, cols] = jnp.concatenate([st[0][:p], st[1][:p]], axis=0).T
            sim_ref[:, cols] = jnp.concatenate([st[0][p:], st[1][p:]], axis=0).T


def _ssm_state_contrib(uz, bend, jpad):
    g, p2, w = bend.shape
    gstep = STATE_TILES * GROUPS_PER_TILE
    assert g % gstep == 0
    sw = gstep * SSM_STATE
    out = jax.ShapeDtypeStruct((jpad, g * SSM_STATE), F32)
    ospec = pl.BlockSpec((CHUNK_TILE, sw), lambda j, q: (j, q))
    utile = lambda t: pl.BlockSpec((CHUNK_TILE * SSM_T, LANES), lambda j, q: (j, STATE_TILES * q + t))
    return pl.pallas_call(
        _ssm_state_kernel,
        out_shape=[out, out],
        grid=(jpad // CHUNK_TILE, g // gstep),
        in_specs=[*[utile(t) for t in range(STATE_TILES)],
                  pl.BlockSpec((gstep, p2, w), lambda j, q: (q, 0, 0))],
        out_specs=[ospec, ospec],
        scratch_shapes=[pltpu.VMEM((GROUPS_PER_TILE, w, CHUNK_TILE), BF16) for _ in range(STATE_TILES)],
        compiler_params=_params("arbitrary", "arbitrary"),
        name="ssm_state_contrib",
    )(*[uz] * STATE_TILES, bend)


def _ssm_scan_kernel(sre_ref, sim_ref, ar_ref, ai_ref, h0r_ref, h0i_ref,
                     hpr_ref, hpi_ref, pr_ref, pi_ref, sr_ref, si_ref, hr_sc, hi_sc, *, n_prompt):
    i = pl.program_id(0)
    rb = sre_ref.shape[0]
    ar, ai = ar_ref[...], ai_ref[...]

    @pl.when(i == 0)
    def _():
        hr_sc[...] = jnp.zeros_like(hr_sc)
        hi_sc[...] = jnp.zeros_like(hi_sc)

    @pl.when(i < n_prompt)
    def _():
        def body(j, c):
            row = pl.ds(j, 1)
            hr, hi = hr_sc[...], hi_sc[...]
            hpr_ref[row, :] = hr
            hpi_ref[row, :] = hi
            hr_sc[...] = ar * hr - ai * hi + sre_ref[row, :]
            hi_sc[...] = ar * hi + ai * hr + sim_ref[row, :]
            return c

        lax.fori_loop(0, rb, body, 0)

    @pl.when(i == n_prompt - 1)
    def _():
        pr_ref[...] = hr_sc[...]
        pi_ref[...] = hi_sc[...]

    @pl.when(i == n_prompt)
    def _():
        h0r, h0i = h0r_ref[...], h0i_ref[...]
        hpr_ref[...] = h0r
        hpi_ref[...] = h0i
        sr_ref[...] = ar * h0r - ai * h0i + sre_ref[...]
        si_ref[...] = ar * h0i + ai * h0r + sim_ref[...]

    @pl.when(i > n_prompt)
    def _():
        hpr_ref[...] = jnp.zeros_like(hpr_ref)
        hpi_ref[...] = jnp.zeros_like(hpi_ref)


def _ssm_scan(s_re, s_im, a16r, a16i, h0r, h0i, jp):
    jpad, n = s_re.shape
    rb = h0r.shape[0]
    assert jp % rb == 0 and jpad % rb == 0
    n_prompt = jp // rb
    rows = pl.BlockSpec((rb, n), lambda i: (i, 0))
    const1 = pl.BlockSpec((1, n), lambda i: (0, 0))
    constb = pl.BlockSpec((rb, n), lambda i: (0, 0))
    big = jax.ShapeDtypeStruct((jpad, n), F32)
    one = jax.ShapeDtypeStruct((1, n), F32)
    bat = jax.ShapeDtypeStruct((rb, n), F32)
    return pl.pallas_call(
        functools.partial(_ssm_scan_kernel, n_prompt=n_prompt),
        out_shape=[big, big, one, one, bat, bat],
        grid=(jpad // rb,),
        in_specs=[rows, rows, const1, const1, constb, constb],
        out_specs=[rows, rows, const1, const1, constb, constb],
        scratch_shapes=[pltpu.VMEM((1, n), F32), pltpu.VMEM((1, n), F32)],
        compiler_params=_params("arbitrary"),
        name="ssm_scan",
    )(s_re, s_im, a16r, a16i, h0r, h0i)


def _ssm_out_kernel(u_ref, mt_ref, cin_ref, hpr_ref, hpi_ref, d_ref, o_ref, ut_ref, yt_ref):
    _build_ut(u_ref, ut_ref)
    p2 = 2 * SSM_STATE
    for gl in range(GROUPS_PER_TILE):
        pr = gl // 2
        hp = jnp.concatenate([hpr_ref[:, pr * p2:(pr + 1) * p2], hpi_ref[:, pr * p2:(pr + 1) * p2]],
                             axis=1).astype(BF16)
        yt = jnp.dot(mt_ref[gl], ut_ref[gl], preferred_element_type=F32)
        yt = yt + lax.dot_general(cin_ref[gl], hp, _NT, preferred_element_type=F32)
        for t in range(SSM_T):
            yt_ref[t, gl * SSM_GROUP:(gl + 1) * SSM_GROUP, :] = yt[t * SSM_GROUP:(t + 1) * SSM_GROUP, :]
    d = d_ref[...]
    for t in range(SSM_T):
        y = yt_ref[t].T + d * u_ref[_step_rows(t), :]
        o_ref[_step_rows(t), :] = jax.nn.gelu(y)


def _ssm_output(uz, mt, cin, hp_re, hp_im, d_skip, jpad):
    g, w, _ = mt.shape
    w2 = cin.shape[-1]
    ntile = g // GROUPS_PER_TILE
    sw = GROUPS_PER_TILE * SSM_STATE
    tok = pl.BlockSpec((CHUNK_TILE * SSM_T, LANES), lambda j, q: (j, q))
    hspec = pl.BlockSpec((CHUNK_TILE, sw), lambda j, q: (j, q))
    return pl.pallas_call(
        _ssm_out_kernel,
        out_shape=jax.ShapeDtypeStruct((jpad * SSM_T, g * SSM_GROUP), F32),
        grid=(jpad // CHUNK_TILE, ntile),
        in_specs=[
            tok,
            pl.BlockSpec((GROUPS_PER_TILE, w, w), lambda j, q: (q, 0, 0)),
            pl.BlockSpec((GROUPS_PER_TILE, w, w2), lambda j, q: (q, 0, 0)),
            hspec, hspec,
            pl.BlockSpec((1, LANES), lambda j, q: (0, q)),
        ],
        out_specs=tok,
        scratch_shapes=[
            pltpu.VMEM((GROUPS_PER_TILE, w, CHUNK_TILE), BF16),
            pltpu.VMEM((SSM_T, LANES, CHUNK_TILE), F32),
        ],
        compiler_params=_params("arbitrary", "arbitrary"),
        name="ssm_output",
    )(uz, mt, cin, hp_re, hp_im, d_skip)


def _a_glu_kernel(y_ref, z_ref, w_ref, o_ref, wb_ref):
    r = pl.program_id(0)

    @pl.when(r == 0)
    def _():
        wb_ref[...] = w_ref[...].astype(BF16)

    @pl.when(r > 0)
    def _():
        y = y_ref[...]
        gate = jnp.dot(y.astype(BF16), wb_ref[...], preferred_element_type=F32)
        y2 = y * jax.nn.sigmoid(gate)
        o_ref[...] = (y2 * jax.nn.silu(z_ref[...])).astype(BF16)


def _a_glu(yg, uz, w_glu, rows):
    tm = ROW_TILE
    e = yg.shape[1]
    return pl.pallas_call(
        _a_glu_kernel,
        out_shape=jax.ShapeDtypeStruct((rows, e), BF16),
        grid=(1 + rows // tm,),
        in_specs=[
            pl.BlockSpec((tm, e), lambda r: (_row_block(r), 0)),
            pl.BlockSpec((tm, e), lambda r: (_row_block(r), 1)),
            _resident((e, e), lambda r: (0, 0)),
        ],
        out_specs=pl.BlockSpec((tm, e), lambda r: (_row_block(r), 0)),
        scratch_shapes=[pltpu.VMEM((e, e), BF16)],
        compiler_params=_params("arbitrary"),
        name="a_glu",
    )(yg, uz, w_glu)


def _a_out_kernel(o_ref, w_ref, xp_ref, xs_ref, x1_ref, wb_ref, *, mp):
    m = pl.program_id(0) - 1

    @pl.when(m < 0)
    def _():
        wb_ref[...] = w_ref[...].astype(BF16)

    @pl.when((m >= 0) & (m < mp))
    def _():
        x1_ref[...] = xp_ref[...] + jnp.dot(o_ref[...], wb_ref[...], preferred_element_type=F32)

    @pl.when(m == mp)
    def _():
        x1_ref[...] = xs_ref[...] + jnp.dot(o_ref[...], wb_ref[...], preferred_element_type=F32)


def _a_out_proj(o, w_out, xp, xs):
    tm = ROW_TILE
    rows, e = o.shape
    d = w_out.shape[1]
    mp = xp.shape[0] // tm
    return pl.pallas_call(
        functools.partial(_a_out_kernel, mp=mp),
        out_shape=jax.ShapeDtypeStruct((rows, d), F32),
        grid=(1 + rows // tm,),
        in_specs=[
            pl.BlockSpec((tm, e), lambda r: (_row_block(r), 0)),
            _resident((e, d), lambda r: (0, 0)),
            pl.BlockSpec((tm, d), lambda r: (jnp.minimum(_row_block(r), mp - 1), 0)),
            _resident((tm, d), lambda r: (0, 0)),
        ],
        out_specs=pl.BlockSpec((tm, d), lambda r: (_row_block(r), 0)),
        scratch_shapes=[pltpu.VMEM((e, d), BF16)],
        compiler_params=_params("arbitrary"),
        name="a_out_proj",
    )(o, w_out, xp, xs)


def _rope(x, cos, sin):
    outs = []
    for c in range(x.shape[1] // HEAD_DIM):
        xc = x[:, c * HEAD_DIM:(c + 1) * HEAD_DIM]
        outs.append(xc * cos + pltpu.roll(xc, HEAD_DIM // 2, axis=1) * sin)
    return jnp.concatenate(outs, axis=1) if len(outs) > 1 else outs[0]


def _store_heads(o_ref, val, tm):
    for h in range(N_HEADS):
        o_ref[pl.ds(h, tm, stride=N_HEADS), :] = val[:, h * HEAD_DIM:(h + 1) * HEAD_DIM]


def _prompt_col(r, mp, col, last):
    return jnp.where(_row_block(r) >= mp, last, col)


def _sample_col(r, mp, col):
    return jnp.where(_row_block(r) == mp, col, 0)


def _kv_proj_kernel(x_ref, g_ref, w_ref, cos_ref, sin_ref, kp_ref, vp_ref, cbp_ref,
                    ks_ref, vs_ref, cbs_ref, xn_ref, wb_ref, *, tm, mp):
    m, n = pl.program_id(0) - 1, pl.program_id(1)
    hpb = w_ref.shape[1] // (2 * HEAD_DIM)

    for nn in range(4):
        @pl.when((m < 0) & (n == nn))
        def _():
            for hl in range(hpb):
                h = hpb * (nn % 2) + hl
                for half in range(2):
                    src = (2 * hl + half) * HEAD_DIM
                    wb_ref[2 * (nn // 2) + half, :, h * HEAD_DIM:(h + 1) * HEAD_DIM] = (
                        w_ref[:, src:src + HEAD_DIM].astype(BF16))

    @pl.when((m >= 0) & (n == 0))
    def _():
        xn_ref[...] = _rms_scale(x_ref[...], g_ref[...]).astype(BF16)

    for c in range(4):
        @pl.when((m >= 0) & (n == c))
        def _():
            acc = jnp.dot(xn_ref[...], wb_ref[c], preferred_element_type=F32)
            if c < 2:
                acc = _rope(acc, cos_ref[...], sin_ref[...])

            @pl.when(m < mp)
            def _():
                _store_heads(kp_ref if c < 2 else vp_ref, acc, tm)
                cbp_ref[...] = acc.astype(BF16)

            @pl.when(m == mp)
            def _():
                _store_heads(ks_ref if c < 2 else vs_ref, acc, tm)
                cbs_ref[...] = acc.astype(BF16)


def _kv_proj(x1, rows_p, g, w_kv, cos, sin):
    tm = ROW_TILE
    rows, d = x1.shape
    tn = N_HEADS * HEAD_DIM
    mp = rows_p // tm
    assert w_kv.shape[1] == 4 * tn and rows == rows_p + tm and rows_p % tm == 0
    out4 = lambda nrows: jax.ShapeDtypeStruct((nrows * N_HEADS, 2 * HEAD_DIM), F32)
    outb = lambda nrows: jax.ShapeDtypeStruct((nrows, 4 * tn), BF16)
    prow = lambda r: jnp.minimum(_row_block(r), mp - 1)
    col = _col_block
    return pl.pallas_call(
        functools.partial(_kv_proj_kernel, tm=tm, mp=mp),
        out_shape=[out4(rows_p), out4(rows_p), outb(rows_p), out4(tm), out4(tm), outb(tm)],
        grid=(1 + rows // tm, 4),
        in_specs=[
            pl.BlockSpec((tm, d), lambda r, n: (_row_block(r), 0)),
            _resident((1, d), lambda r, n: (0, 0)),
            _staged_weight_spec(d, tn, 4),
            pl.BlockSpec((tm, HEAD_DIM), lambda r, n: (_row_block(r), 0)),
            pl.BlockSpec((tm, HEAD_DIM), lambda r, n: (_row_block(r), 0)),
        ],
        out_specs=[
            pl.BlockSpec((tm * N_HEADS, HEAD_DIM),
                         lambda r, n: (prow(r), _prompt_col(r, mp, jnp.minimum(col(r, n), 1), 1))),
            pl.BlockSpec((tm * N_HEADS, HEAD_DIM),
                         lambda r, n: (prow(r), _prompt_col(r, mp, jnp.maximum(col(r, n) - 2, 0), 1))),
            pl.BlockSpec((tm, tn), lambda r, n: (prow(r), _prompt_col(r, mp, col(r, n), 3))),
            pl.BlockSpec((tm * N_HEADS, HEAD_DIM),
                         lambda r, n: (0, _sample_col(r, mp, jnp.minimum(col(r, n), 1)))),
            pl.BlockSpec((tm * N_HEADS, HEAD_DIM),
                         lambda r, n: (0, _sample_col(r, mp, jnp.maximum(col(r, n) - 2, 0)))),
            pl.BlockSpec((tm, tn), lambda r, n: (0, _sample_col(r, mp, col(r, n)))),
        ],
        scratch_shapes=[pltpu.VMEM((tm, d), BF16), pltpu.VMEM((4, d, tn), BF16)],
        compiler_params=_params("arbitrary", "arbitrary", vmem_limit=KV_PROJ_VMEM_LIMIT),
        name="kv_proj",
    )(x1, g, w_kv, cos, sin)


def _qz_proj_kernel(x_ref, g_ref, w_ref, cos_ref, sin_ref, qp_ref, zp_ref, qs_ref, zs_ref,
                    xn_ref, wb_ref, *, nq, mp):
    m, n = pl.program_id(0) - 1, pl.program_id(1)

    @pl.when(m < 0)
    def _():
        wb_ref[n] = w_ref[...].astype(BF16)

    @pl.when((m >= 0) & (n == 0))
    def _():
        xn_ref[...] = _rms_scale(x_ref[...], g_ref[...]).astype(BF16)

    for c in range(2 * nq):
        @pl.when((m >= 0) & (n == c))
        def _():
            acc = jnp.dot(xn_ref[...], wb_ref[c], preferred_element_type=F32)
            val = _rope(acc, cos_ref[...], sin_ref[...]).astype(BF16) if c < nq else acc

            @pl.when(m < mp)
            def _():
                (qp_ref if c < nq else zp_ref)[...] = val

            @pl.when(m == mp)
            def _():
                (qs_ref if c < nq else zs_ref)[...] = val


def _qz_proj(x1, rows_p, g, w, cos, sin, tn):
    tm = ROW_TILE
    rows, d = x1.shape
    half = w.shape[1] // 2
    nq = half // tn
    mp = rows_p // tm
    assert rows == rows_p + tm and rows_p % tm == 0
    prow = lambda r: jnp.minimum(_row_block(r), mp - 1)
    qcol = lambda r, n: jnp.minimum(_col_block(r, n), nq - 1)
    zcol = lambda r, n: jnp.maximum(_col_block(r, n) - nq, 0)
    return pl.pallas_call(
        functools.partial(_qz_proj_kernel, nq=nq, mp=mp),
        out_shape=[jax.ShapeDtypeStruct((rows_p, half), BF16), jax.ShapeDtypeStruct((rows_p, half), F32),
                   jax.ShapeDtypeStruct((tm, half), BF16), jax.ShapeDtypeStruct((tm, half), F32)],
        grid=(1 + rows // tm, 2 * nq),
        in_specs=[
            pl.BlockSpec((tm, d), lambda r, n: (_row_block(r), 0)),
            _resident((1, d), lambda r, n: (0, 0)),
            _staged_weight_spec(d, tn, 2 * nq),
            pl.BlockSpec((tm, HEAD_DIM), lambda r, n: (_row_block(r), 0)),
            pl.BlockSpec((tm, HEAD_DIM), lambda r, n: (_row_block(r), 0)),
        ],
        out_specs=[
            pl.BlockSpec((tm, tn), lambda r, n: (prow(r), _prompt_col(r, mp, qcol(r, n), nq - 1))),
            pl.BlockSpec((tm, tn), lambda r, n: (prow(r), _prompt_col(r, mp, zcol(r, n), nq - 1))),
            pl.BlockSpec((tm, tn), lambda r, n: (0, _sample_col(r, mp, qcol(r, n)))),
            pl.BlockSpec((tm, tn), lambda r, n: (0, _sample_col(r, mp, zcol(r, n)))),
        ],
        scratch_shapes=[pltpu.VMEM((tm, d), BF16), pltpu.VMEM((2 * nq, d, tn), BF16)],
        compiler_params=_params("arbitrary", "arbitrary"),
        name="qz_proj",
    )(x1, g, w, cos, sin)


def _rope_tables(pos):
    inv_freq = ROPE_THETA ** (-jnp.arange(0, HEAD_DIM, 2, dtype=F32) / HEAD_DIM)
    ang = pos.astype(F32)[:, None] * inv_freq[None, :]
    c, s = jnp.cos(ang), jnp.sin(ang)
    return jnp.concatenate([c, c], axis=-1), jnp.concatenate([-s, s], axis=-1)


def _diff_lambda(lq1_ref, lk1_ref, lq2_ref, lk2_ref, lambda_init):
    s1 = jnp.sum(lq1_ref[...] * lk1_ref[...], axis=-1, keepdims=True)
    s2 = jnp.sum(lq2_ref[...] * lk2_ref[...], axis=-1, keepdims=True)
    return jnp.exp(s1) - jnp.exp(s2) + lambda_init


def _attn_finish(a1, l1, a2, l2, lam, subln, z, lambda_init):
    o = a1 / l1 - lam * (a2 / l2)
    o = _rms_scale(o, subln) * (1.0 - lambda_init)
    return (o * jax.nn.silu(z)).astype(BF16)


def _softmax_update(m_sc, l_sc, a_sc, idx, s, v):
    m_old = m_sc[idx]
    m_new = jnp.maximum(m_old, jnp.max(s, axis=-1, keepdims=True))
    p = jnp.exp(s - m_new)
    alpha = jnp.exp(m_old - m_new)
    l_sc[idx] = alpha * l_sc[idx] + jnp.sum(p, axis=-1, keepdims=True)
    a_sc[idx] = alpha * a_sc[idx] + jnp.dot(p.astype(BF16), v, preferred_element_type=F32)
    m_sc[idx] = m_new


def _attn_prompt_kernel(q_ref, k1_ref, k2_ref, v1_ref, v2_ref, z_ref,
                        lq1_ref, lk1_ref, lq2_ref, lk2_ref, sub_ref,
                        o_ref, m_sc, l_sc, al_sc, a_sc, *sp_scs, tq, ts, tk, lambda_init):
    qi = pl.program_id(1)
    c2 = HEAD_DIM ** -0.5 * math.log2(math.e)
    k_refs = (k1_ref, k2_ref)
    nsub = tq // ts
    nc = tk // LANES
    s_scs, p_scs = sp_scs[:2 * nsub], sp_scs[2 * nsub:]
    m_sc[...] = jnp.full(m_sc.shape, NEG, F32)
    l_sc[...] = jnp.zeros(l_sc.shape, F32)
    a_sc[...] = jnp.zeros(a_sc.shape, F32)

    def softmax_strips(sub, n, col0):
        for r in range(ts // STRIP):
            row0 = sub * ts + r * STRIP
            lr = slice(r * STRIP, (r + 1) * STRIP)
            gr = slice(row0, row0 + STRIP)
            visible = tk if col0 is None else min(tk, (row0 // CHUNK + 1) * CHUNK - col0)
            assert visible > 0
            ncv = -(-visible // LANES)
            s = s_scs[2 * sub + n][lr, 0:ncv * LANES]
            if visible < ncv * LANES:
                col = lax.broadcasted_iota(jnp.int32, s.shape, 1)
                s = jnp.where(col < visible, s, NEG)
            cols = [s[:, c * LANES:(c + 1) * LANES] for c in range(ncv)]
            m_cur = jnp.max(functools.reduce(jnp.maximum, cols), axis=-1, keepdims=True)
            m_old = m_sc[n, gr, :]
            m_new = jnp.maximum(m_old, m_cur)
            ps = [jnp.exp2((c - m_new) * c2) for c in cols]
            l_cur = jnp.sum(functools.reduce(jnp.add, ps), axis=-1, keepdims=True)
            alpha = jnp.exp2((m_old - m_new) * c2)
            l_sc[n, gr, :] = alpha * l_sc[n, gr, :] + l_cur
            m_sc[n, gr, :] = m_new
            al_sc[n, gr, :] = alpha
            for c in range(nc):
                pc = ps[c].astype(BF16) if c < ncv else jnp.zeros((STRIP, LANES), BF16)
                p_scs[2 * sub + n][lr, c * LANES:(c + 1) * LANES] = pc

    def block(kb, col0, subs):
        rows = pl.ds(pl.multiple_of(kb * tk, tk), tk)
        vblk = jnp.concatenate([v1_ref[rows, :], v2_ref[rows, :]], axis=1)
        for sub in subs:
            for n in range(2):
                qn = q_ref[sub * ts:(sub + 1) * ts, n * HEAD_DIM:(n + 1) * HEAD_DIM]
                s_scs[2 * sub + n][...] = lax.dot_general(qn, k_refs[n][rows, :], _NT,
                                                          preferred_element_type=F32)
        for sub in subs:
            gr = slice(sub * ts, (sub + 1) * ts)
            for n in range(2):
                softmax_strips(sub, n, col0)
                pv = jnp.dot(p_scs[2 * sub + n][...], vblk, preferred_element_type=F32)
                alpha = al_sc[n, gr, :]
                a_sc[n, gr, :] = a_sc[n, gr, :] * jnp.concatenate([alpha, alpha], axis=1) + pv

    nd = tq // tk

    def body(kb, c):
        block(kb, None, range(nsub))
        return c

    lax.fori_loop(0, qi * nd, body, 0)
    for d in range(nd):
        block(qi * nd + d, d * tk, [sub for sub in range(nsub) if (sub + 1) * ts > d * tk])

    lam = _diff_lambda(lq1_ref, lk1_ref, lq2_ref, lk2_ref, lambda_init)
    l1 = jnp.concatenate([l_sc[0], l_sc[0]], axis=1)
    l2 = jnp.concatenate([l_sc[1], l_sc[1]], axis=1)
    o_ref[...] = _attn_finish(a_sc[0], l1, a_sc[1], l2, lam, sub_ref[...], z_ref[...], lambda_init)


def _attn_prompt(qb, z32, kvb, lq1, lk1, lq2, lk2, subln, lambda_init, tq, ts, tk):
    rows = qb.shape[0]
    hw = 2 * HEAD_DIM
    nsub = tq // ts
    assert tq % tk == 0 and tk % ts == 0 and ts % STRIP == 0 and CHUNK % STRIP == 0
    vec = pl.BlockSpec((1, HEAD_DIM), lambda h, i: (0, 0))
    kcol = lambda c: pl.BlockSpec((rows, HEAD_DIM), lambda h, i: (0, c * N_HEADS + h))
    stat = pltpu.VMEM((2, tq, LANES), F32)
    return pl.pallas_call(
        functools.partial(_attn_prompt_kernel, tq=tq, ts=ts, tk=tk, lambda_init=lambda_init),
        out_shape=jax.ShapeDtypeStruct((rows, N_HEADS * hw), BF16),
        grid=(N_HEADS, rows // tq),
        in_specs=[
            pl.BlockSpec((tq, hw), lambda h, i: (i, h)),
            kcol(0), kcol(1), kcol(2), kcol(3),
            pl.BlockSpec((tq, hw), lambda h, i: (i, h)),
            vec, vec, vec, vec,
            pl.BlockSpec((1, hw), lambda h, i: (0, 0)),
        ],
        out_specs=pl.BlockSpec((tq, hw), lambda h, i: (i, h)),
        scratch_shapes=[
            stat, stat, stat,
            pltpu.VMEM((2, tq, hw), F32),
            *[pltpu.VMEM((ts, tk), F32) for _ in range(2 * nsub)],
            *[pltpu.VMEM((ts, tk), BF16) for _ in range(2 * nsub)],
        ],
        compiler_params=_params("arbitrary", "arbitrary"),
        name="attn_prompt",
    )(qb, kvb, kvb, kvb, kvb, z32, lq1, lk1, lq2, lk2, subln)


def _attn_sample_kernel(q_ref, ck1_ref, ck2_ref, cv1_ref, cv2_ref, kvn_ref, z_ref,
                        lq1_ref, lk1_ref, lq2_ref, lk2_ref, sub_ref,
                        o_ref, m_sc, l_sc, a_sc, *, tkv, nkb, past_len, lambda_init):
    kb = pl.program_id(1)
    scale = HEAD_DIM ** -0.5
    t = q_ref.shape[0]
    hw = 2 * HEAD_DIM
    ck_refs = (ck1_ref, ck2_ref)
    assert (past_len - 1) // CHUNK <= past_len // CHUNK

    @pl.when(kb == 0)
    def _():
        m_sc[...] = jnp.full(m_sc.shape, NEG, F32)
        l_sc[...] = jnp.zeros(l_sc.shape, F32)
        a_sc[...] = jnp.zeros(a_sc.shape, F32)

    for h in range(N_HEADS):
        rows = pl.ds(h, tkv, stride=N_HEADS)
        v = jnp.concatenate([cv1_ref[rows, :], cv2_ref[rows, :]], axis=1).astype(BF16)
        for n in range(2):
            qn = q_ref[:, (2 * h + n) * HEAD_DIM:(2 * h + n + 1) * HEAD_DIM]
            k = ck_refs[n][rows, :].astype(BF16)
            s = lax.dot_general(qn, k, _NT, preferred_element_type=F32) * scale
            _softmax_update(m_sc, l_sc, a_sc, 2 * h + n, s, v)

    @pl.when(kb == nkb - 1)
    def _():
        row = lax.broadcasted_iota(jnp.int32, (t, t), 0) + past_len
        col = lax.broadcasted_iota(jnp.int32, (t, t), 1) + past_len
        new_mask = (col // CHUNK) <= (row // CHUNK)
        lam = _diff_lambda(lq1_ref, lk1_ref, lq2_ref, lk2_ref, lambda_init)
        sub = sub_ref[...]
        col_blk = lambda c, h: slice((c * N_HEADS + h) * HEAD_DIM, (c * N_HEADS + h + 1) * HEAD_DIM)
        for h in range(N_HEADS):
            vn = jnp.concatenate([kvn_ref[:, col_blk(2, h)], kvn_ref[:, col_blk(3, h)]], axis=1)
            for n in range(2):
                qn = q_ref[:, (2 * h + n) * HEAD_DIM:(2 * h + n + 1) * HEAD_DIM]
                s = lax.dot_general(qn, kvn_ref[:, col_blk(n, h)], _NT,
                                    preferred_element_type=F32) * scale
                _softmax_update(m_sc, l_sc, a_sc, 2 * h + n, jnp.where(new_mask, s, NEG), vn)
            o_ref[:, h * hw:(h + 1) * hw] = _attn_finish(
                a_sc[2 * h], l_sc[2 * h], a_sc[2 * h + 1], l_sc[2 * h + 1], lam, sub,
                z_ref[:, h * hw:(h + 1) * hw], lambda_init)


def _attn_sample(qb, z32, kvb, cache_k2, cache_v2, lq1, lk1, lq2, lk2, subln, lambda_init, t, tkv):
    bsz, rows8, hw = cache_k2.shape
    past_len = rows8 // N_HEADS
    nkb = past_len // tkv
    vec = pl.BlockSpec((1, HEAD_DIM), lambda b, k: (0, 0))
    cache = lambda c: pl.BlockSpec((None, tkv * N_HEADS, HEAD_DIM), lambda b, k: (b, k, c))
    full = lambda a: pl.BlockSpec((t, a.shape[1]), lambda b, k: (b, 0))
    return pl.pallas_call(
        functools.partial(_attn_sample_kernel, tkv=tkv, nkb=nkb, past_len=past_len,
                          lambda_init=lambda_init),
        out_shape=jax.ShapeDtypeStruct((bsz * t, N_HEADS * hw), BF16),
        grid=(bsz, nkb),
        in_specs=[
            full(qb), cache(0), cache(1), cache(0), cache(1), full(kvb), full(z32),
            vec, vec, vec, vec,
            pl.BlockSpec((1, hw), lambda b, k: (0, 0)),
        ],
        out_specs=pl.BlockSpec((t, N_HEADS * hw), lambda b, k: (b, 0)),
        scratch_shapes=[
            pltpu.VMEM((2 * N_HEADS, t, 1), F32),
            pltpu.VMEM((2 * N_HEADS, t, 1), F32),
            pltpu.VMEM((2 * N_HEADS, t, hw), F32),
        ],
        compiler_params=_params("arbitrary", "arbitrary"),
        name="attn_sample",
    )(qb, cache_k2, cache_k2, cache_v2, cache_v2, kvb, z32, lq1, lk1, lq2, lk2, subln)


def _b_out_kernel(op_ref, os_ref, w_ref, x_ref, g_ref, yp_ref, ys_ref, wb_ref, *, mp):
    m = pl.program_id(0) - 1

    @pl.when(m < 0)
    def _():
        wb_ref[...] = w_ref[...].astype(BF16)

    def finish(o_ref, y_ref):
        acc = jnp.dot(o_ref[...], wb_ref[...], preferred_element_type=F32)
        y_ref[...] = _rms_scale(x_ref[...] + acc, g_ref[...])

    @pl.when((m >= 0) & (m < mp))
    def _():
        finish(op_ref, yp_ref)

    @pl.when(m == mp)
    def _():
        finish(os_ref, ys_ref)


def _b_out_proj(og_p, og_s, w_out, x1, g):
    tm = ROW_TILE
    rows_p, e = og_p.shape
    rows, d = x1.shape
    mp = rows_p // tm
    assert og_s.shape[0] == tm and rows == rows_p + tm and rows_p % tm == 0
    prow = lambda r: jnp.minimum(_row_block(r), mp - 1)
    return pl.pallas_call(
        functools.partial(_b_out_kernel, mp=mp),
        out_shape=[jax.ShapeDtypeStruct((rows_p, d), F32), jax.ShapeDtypeStruct((tm, d), F32)],
        grid=(1 + rows // tm,),
        in_specs=[
            pl.BlockSpec((tm, e), lambda r: (prow(r), 0)),
            _resident((tm, e), lambda r: (0, 0)),
            _resident((e, d), lambda r: (0, 0)),
            pl.BlockSpec((tm, d), lambda r: (_row_block(r), 0)),
            _resident((1, d), lambda r: (0, 0)),
        ],
        out_specs=[pl.BlockSpec((tm, d), lambda r: (prow(r), 0)),
                   pl.BlockSpec((tm, d), lambda r: (0, 0))],
        scratch_shapes=[pltpu.VMEM((e, d), BF16)],
        compiler_params=_params("arbitrary"),
        name="b_out_proj",
    )(og_p, og_s, w_out, x1, g)


def kernel(x_prompt, x_sample, state_ssm_re, state_ssm_im, cache_k, cache_v, a_norm, a_w_in, a_lambda_re, a_lambda_im, a_log_dt, a_b_re, a_b_im, a_c_re, a_c_im, a_d, a_w_glu, a_w_out, kv_norm, w_kv, b_norm, b_w_in, b_lambda_q1, b_lambda_k1, b_lambda_q2, b_lambda_k2, b_subln, b_w_out, final_norm):
    bp, seq, d = x_prompt.shape
    bs, t_s, _ = x_sample.shape
    past_len = cache_k.shape[1]
    n_a, n_b = a_norm.shape[0], b_norm.shape[0]
    g_cnt, p = a_lambda_re.shape[1:]
    assert bp == 1 and n_a == 1 and n_b == 1
    assert t_s == SSM_T and seq % (SSM_T * bs) == 0 and seq % CHUNK == 0
    assert cache_k.shape[2:] == (N_HEADS, 2 * HEAD_DIM)

    jp, js = seq // SSM_T, bs
    jpad = -(-(jp + js) // CHUNK_TILE) * CHUNK_TILE
    rows_s = bs * t_s
    rows_a = seq + rows_s

    xp = x_prompt.reshape(seq, d)
    xs = x_sample.reshape(rows_s, d)
    uz = _a_in_proj(xp, xs, a_norm[0].reshape(1, d), a_w_in[0], jpad * SSM_T, tn=1024)
    mt, bend, cin, a16r, a16i = _ssm_prep(a_lambda_re[0], a_lambda_im[0], a_log_dt[0],
                                          a_b_re[0], a_b_im[0], a_c_re[0], a_c_im[0])
    s_re, s_im = _ssm_state_contrib(uz, bend, jpad)
    h0r = state_ssm_re[:, 0].reshape(bs, g_cnt * p)
    h0i = state_ssm_im[:, 0].reshape(bs, g_cnt * p)
    hp_re, hp_im, pre, pim, sre, sim = _ssm_scan(s_re, s_im, a16r, a16i, h0r, h0i, jp)
    yg = _ssm_output(uz, mt, cin, hp_re, hp_im, a_d[0].reshape(1, -1), jpad)
    o_a = _a_glu(yg, uz, a_w_glu[0], rows_a)
    x1 = _a_out_proj(o_a, a_w_out[0], xp, xs)

    lambda_init = 0.8 - 0.6 * math.exp(-0.3 * n_a)
    w_q = b_w_in[0]
    kv_g = kv_norm.reshape(1, d)
    b_g = b_norm[0].reshape(1, d)
    pos = jnp.concatenate([jnp.arange(seq, dtype=jnp.int32),
                           jnp.tile(past_len + jnp.arange(t_s, dtype=jnp.int32), bs)])
    cos, sin = _rope_tables(pos)
    k_p, v_p, kvb_p, k_s, v_s, kvb_s = _kv_proj(x1, seq, kv_g, w_kv, cos, sin)
    qb_p, z_p, qb_s, z_s = _qz_proj(x1, seq, b_g, w_q, cos, sin, tn=1024)

    vecs = [a[0].reshape(1, HEAD_DIM) for a in (b_lambda_q1, b_lambda_k1, b_lambda_q2, b_lambda_k2)]
    subln = b_subln[0].reshape(1, 2 * HEAD_DIM)
    hw = 2 * HEAD_DIM
    og_p = _attn_prompt(qb_p, z_p, kvb_p, *vecs, subln, lambda_init, tq=1024, ts=512, tk=1024)
    og_s = _attn_sample(qb_s, z_s, kvb_s, cache_k.reshape(bs, past_len * N_HEADS, hw),
                        cache_v.reshape(bs, past_len * N_HEADS, hw), *vecs, subln, lambda_init,
                        t_s, tkv=1024)
    w_o = b_w_out[0]
    fg = final_norm.reshape(1, d)
    y_p, y_s = _b_out_proj(og_p, og_s, w_o, x1, fg)

    return (y_p.reshape(bp, seq, d), y_s.reshape(bs, t_s, d),
            pre.reshape(bp, n_a, g_cnt, p), pim.reshape(bp, n_a, g_cnt, p),
            k_p.reshape(bp, seq, N_HEADS, hw), v_p.reshape(bp, seq, N_HEADS, hw),
            sre.reshape(bs, n_a, g_cnt, p), sim.reshape(bs, n_a, g_cnt, p),
            k_s.reshape(bs, t_s, N_HEADS, hw), v_s.reshape(bs, t_s, N_HEADS, hw))
```

```python
import functools
import math

import jax
import jax.numpy as jnp
from jax import lax
from jax.experimental import pallas as pl
from jax.experimental.pallas import tpu as pltpu

F32 = jnp.float32
BF16 = jnp.bfloat16

CHUNK = 64
HEAD_DIM = 128
N_HEADS = 8
SSM_GROUP = 16
SSM_STATE = 64
SSM_T = 16
ROPE_THETA = 10000.0
EPS = 1e-6
NEG = -1e30

LANES = 128
SUBLANES = 8
GROUPS_PER_TILE = LANES // SSM_GROUP
CHUNK_TILE = 128
ROW_TILE = 512
STRIP = 32
VMEM_LIMIT = 56 * 1024 * 1024
KV_PROJ_VMEM_LIMIT = 58 * 1024 * 1024

_NT = (((1,), (1,)), ((), ()))


def _params(*sem, vmem_limit=VMEM_LIMIT):
    return pltpu.CompilerParams(dimension_semantics=sem, vmem_limit_bytes=vmem_limit)


def _resident(shape, index_map):
    return pl.BlockSpec(shape, index_map, pipeline_mode=pl.Buffered(1))


def _rms_scale(x, g):
    ms = jnp.mean(x * x, axis=-1, keepdims=True)
    return x * lax.rsqrt(ms + EPS) * g


def _staged_weight_spec(d, tn, nb):
    return pl.BlockSpec((d, tn), lambda r, n: (0, jnp.where(r == 0, n, nb - 1)),
                        pipeline_mode=pl.Buffered(1))


def _row_block(r):
    return jnp.maximum(r - 1, 0)


def _col_block(r, n):
    return jnp.where(r == 0, 0, n)


def _a_in_kernel(xp_ref, xs_ref, g_ref, w_ref, o_ref, xn_ref, wb_ref, *, mp, nb):
    m, n = pl.program_id(0) - 1, pl.program_id(1)

    @pl.when(m < 0)
    def _():
        wb_ref[n] = w_ref[...].astype(BF16)

    @pl.when((n == 0) & (m >= 0) & (m < mp))
    def _():
        xn_ref[...] = _rms_scale(xp_ref[...], g_ref[...]).astype(BF16)

    @pl.when((n == 0) & (m == mp))
    def _():
        xn_ref[...] = _rms_scale(xs_ref[...], g_ref[...]).astype(BF16)

    for c in range(nb):
        @pl.when((m >= 0) & (m <= mp) & (n == c))
        def _():
            o_ref[...] = jnp.dot(xn_ref[...], wb_ref[c], preferred_element_type=F32)

    @pl.when(m > mp)
    def _():
        o_ref[...] = jnp.zeros(o_ref.shape, F32)


def _a_in_proj(xp, xs, g, w, rows_pad, tn):
    tm = ROW_TILE
    d = g.shape[-1]
    n_out = w.shape[1]
    mp = xp.shape[0] // tm
    nb = n_out // tn
    assert xs.shape[0] == tm and xp.shape[0] % tm == 0 and rows_pad % tm == 0
    return pl.pallas_call(
        functools.partial(_a_in_kernel, mp=mp, nb=nb),
        out_shape=jax.ShapeDtypeStruct((rows_pad, n_out), F32),
        grid=(1 + rows_pad // tm, nb),
        in_specs=[
            pl.BlockSpec((tm, d), lambda r, n: (jnp.minimum(_row_block(r), mp - 1), 0)),
            _resident((tm, d), lambda r, n: (0, 0)),
            _resident((1, d), lambda r, n: (0, 0)),
            _staged_weight_spec(d, tn, nb),
        ],
        out_specs=pl.BlockSpec((tm, tn), lambda r, n: (_row_block(r), _col_block(r, n))),
        scratch_shapes=[pltpu.VMEM((tm, d), BF16), pltpu.VMEM((nb, d, tn), BF16)],
        compiler_params=_params("arbitrary", "arbitrary"),
        name="a_in_proj",
    )(xp, xs, g, w)


PREP_PAIRS = LANES // SSM_T


def _ssm_prep_kernel(lrx_ref, lix_ref, dtx_ref, bre_ref, bim_ref, lrr_ref, lir_ref, dtr_ref,
                     cre_ref, cim_ref, mt_ref, bend_ref, cin_ref, a16r_ref, a16i_ref):
    p = SSM_STATE
    p2 = 2 * p
    w = SSM_T * SSM_GROUP
    hi = lax.Precision.HIGHEST

    lr, li = lrx_ref[...], lix_ref[...]
    dt = jnp.exp(dtx_ref[...])
    zr, zi = lr * dt, li * dt
    mag = jnp.exp(zr)
    n_re, n_im = mag * jnp.cos(zi) - 1.0, mag * jnp.sin(zi)
    den = lr * lr + li * li
    cf_re = (n_re * lr + n_im * li) / den
    cf_im = (n_im * lr - n_re * li) / den
    lane = lax.broadcasted_iota(jnp.int32, (p2, LANES), 1)
    e_end = (SSM_T - 1 - lane % SSM_T).astype(F32)
    pm = jnp.exp(zr * e_end)
    pw_re, pw_im = pm * jnp.cos(zi * e_end), pm * jnp.sin(zi * e_end)
    cp_re = cf_re * pw_re - cf_im * pw_im
    cp_im = cf_re * pw_im + cf_im * pw_re
    coef = jnp.concatenate([cp_re, cp_im, cf_re, cf_im], axis=0)
    sel_row = lax.broadcasted_iota(jnp.int32, (LANES, w), 0)
    sel_col = lax.broadcasted_iota(jnp.int32, (LANES, w), 1)

    tau = lax.broadcasted_iota(jnp.int32, (SSM_T + SUBLANES, p2), 0).astype(F32)
    lane2 = lax.broadcasted_iota(jnp.int32, (w, p2), 1)
    sblk = lax.broadcasted_iota(jnp.int32, (w, w), 1) // SSM_GROUP

    for j in range(PREP_PAIRS):
        own_pair = (sel_row // SSM_T) == j
        spread = jnp.where(own_pair & (sel_row % SSM_T == sel_col // SSM_GROUP), 1.0, 0.0)
        cpx = jnp.dot(coef[:2 * p2], spread, precision=hi, preferred_element_type=F32)
        cfx = jnp.broadcast_to(coef[2 * p2:, j * SSM_T:j * SSM_T + 1], (2 * p2, w))
        b_re, b_im = bre_ref[j], bim_ref[j]
        end_re = cpx[:p2] * b_re - cpx[p2:] * b_im
        end_im = cpx[:p2] * b_im + cpx[p2:] * b_re
        bb_re = cfx[:p2] * b_re - cfx[p2:] * b_im
        bb_im = cfx[:p2] * b_im + cfx[p2:] * b_re
        for r in range(2):
            sl = slice(r * p, (r + 1) * p)
            bend_ref[2 * j + r] = jnp.concatenate([end_re[sl], end_im[sl]], axis=0).astype(BF16)
        bst = jnp.concatenate([bb_re, bb_im], axis=0)

        lr2, li2 = lrr_ref[j], lir_ref[j]
        dt2 = jnp.exp(dtr_ref[j])
        zr2, zi2 = lr2 * dt2, li2 * dt2
        qm = jnp.exp(zr2 * tau)
        q_re, q_im = qm * jnp.cos(zi2 * tau), qm * jnp.sin(zi2 * tau)
        c_re, c_im = cre_ref[j], cim_ref[j]

        def c_times_powers(first):
            re = [c_re * q_re[t:t + 1] - c_im * q_im[t:t + 1] for t in range(first, first + SSM_T)]
            im = [c_re * q_im[t:t + 1] + c_im * q_re[t:t + 1] for t in range(first, first + SSM_T)]
            return jnp.concatenate(re, axis=0), jnp.concatenate(im, axis=0)

        g_re, g_im = c_times_powers(0)
        ci_re, ci_im = c_times_powers(1)
        for r in range(2):
            own = (lane2 // p) == r
            lhs = jnp.concatenate([jnp.where(own, g_re, 0.0), jnp.where(own, -g_im, 0.0)], axis=1)
            gen = jnp.dot(lhs, bst, preferred_element_type=F32)
            k = 0
            while (SSM_GROUP << k) < w:
                sh = SSM_GROUP << k
                shifted = jnp.concatenate([jnp.zeros((sh, w), F32), gen[:w - sh]], axis=0)
                gen = jnp.where(((sblk >> k) & 1) == 1, shifted, gen)
                k += 1
            mt_ref[2 * j + r] = gen.astype(BF16)
            cin_ref[2 * j + r] = jnp.concatenate(
                [jnp.where(own, ci_re, 0.0), jnp.where(own, -ci_im, 0.0)], axis=1).astype(BF16)
        a16r_ref[j] = q_re[SSM_T:SSM_T + 1]
        a16i_ref[j] = q_im[SSM_T:SSM_T + 1]


def _ssm_prep(lam_re, lam_im, log_dt, b_re, b_im, c_re, c_im):
    g, p = lam_re.shape
    gp = g // 2
    p2 = 2 * p
    w = SSM_T * SSM_GROUP
    pp = PREP_PAIRS
    assert gp % pp == 0
    colx = lambda a: jnp.repeat(a.reshape(gp, p2).T, SSM_T, axis=1)
    row = lambda a: a.reshape(gp, 1, p2)
    dt_full = jnp.broadcast_to(log_dt[:, None], (g, p))
    b_t = lambda a: jnp.tile(a.reshape(gp, p2, SSM_GROUP), (1, 1, SSM_T))
    c_pair = lambda a: a.reshape(gp, 2, SSM_GROUP, p).transpose(0, 2, 1, 3).reshape(gp, SSM_GROUP, p2)
    xspec = pl.BlockSpec((p2, LANES), lambda i: (0, i))
    rowspec = pl.BlockSpec((pp, 1, p2), lambda i: (i, 0, 0))
    bspec = pl.BlockSpec((pp, p2, w), lambda i: (i, 0, 0))
    cspec = pl.BlockSpec((pp, SSM_GROUP, p2), lambda i: (i, 0, 0))
    mt, bend, cin, a16r, a16i = pl.pallas_call(
        _ssm_prep_kernel,
        out_shape=[
            jax.ShapeDtypeStruct((g, w, w), BF16),
            jax.ShapeDtypeStruct((g, p2, w), BF16),
            jax.ShapeDtypeStruct((g, w, 2 * p2), BF16),
            jax.ShapeDtypeStruct((gp, 1, p2), F32),
            jax.ShapeDtypeStruct((gp, 1, p2), F32),
        ],
        grid=(gp // pp,),
        in_specs=[xspec, xspec, xspec, bspec, bspec, rowspec, rowspec, rowspec, cspec, cspec],
        out_specs=[
            pl.BlockSpec((2 * pp, w, w), lambda i: (i, 0, 0)),
            pl.BlockSpec((2 * pp, p2, w), lambda i: (i, 0, 0)),
            pl.BlockSpec((2 * pp, w, 2 * p2), lambda i: (i, 0, 0)),
            rowspec, rowspec,
        ],
        compiler_params=_params("arbitrary"),
        name="ssm_prep",
    )(colx(lam_re), colx(lam_im), colx(dt_full), b_t(b_re), b_t(b_im),
      row(lam_re), row(lam_im), row(dt_full), c_pair(c_re), c_pair(c_im))
    return mt, bend, cin, a16r.reshape(1, g * p), a16i.reshape(1, g * p)


def _step_rows(s):
    return pl.ds(s, CHUNK_TILE, stride=SSM_T)


def _build_ut(u_ref, ut_ref):
    for s in range(SSM_T):
        xt = u_ref[_step_rows(s), :].T.astype(BF16)
        for gl in range(GROUPS_PER_TILE):
            ut_ref[gl, s * SSM_GROUP:(s + 1) * SSM_GROUP, :] = xt[gl * SSM_GROUP:(gl + 1) * SSM_GROUP, :]


STATE_TILES = 2


def _ssm_state_kernel(*refs):
    u_refs = refs[:STATE_TILES]
    bend_ref, sre_ref, sim_ref = refs[STATE_TILES:STATE_TILES + 3]
    ut_refs = refs[STATE_TILES + 3:]
    p = SSM_STATE
    for tile in range(STATE_TILES):
        _build_ut(u_refs[tile], ut_refs[tile])
        for pr in range(GROUPS_PER_TILE // 2):
            st = [jnp.dot(bend_ref[tile * GROUPS_PER_TILE + 2 * pr + r], ut_refs[tile][2 * pr + r],
                          preferred_element_type=F32)
                  for r in range(2)]
            cols = slice((tile * GROUPS_PER_TILE // 2 + pr) * 2 * p,
                         (tile * GROUPS_PER_TILE // 2 + pr + 1) * 2 * p)
            sre_ref[:, cols] = jnp.concatenate([st[0][:p], st[1][:p]], axis=0).T
            sim_ref[:, cols] = jnp.concatenate([st[0][p:], st[1][p:]], axis=0).T


def _ssm_state_contrib(uz, bend, jpad):
    g, p2, w = bend.shape
    gstep = STATE_TILES * GROUPS_PER_TILE
    assert g % gstep == 0
    sw = gstep * SSM_STATE
    out = jax.ShapeDtypeStruct((jpad, g * SSM_STATE), F32)
    ospec = pl.BlockSpec((CHUNK_TILE, sw), lambda j, q: (j, q))
    utile = lambda t: pl.BlockSpec((CHUNK_TILE * SSM_T, LANES), lambda j, q: (j, STATE_TILES * q + t))
    return pl.pallas_call(
        _ssm_state_kernel,
        out_shape=[out, out],
        grid=(jpad // CHUNK_TILE, g // gstep),
        in_specs=[*[utile(t) for t in range(STATE_TILES)],
                  pl.BlockSpec((gstep, p2, w), lambda j, q: (q, 0, 0))],
        out_specs=[ospec, ospec],
        scratch_shapes=[pltpu.VMEM((GROUPS_PER_TILE, w, CHUNK_TILE), BF16) for _ in range(STATE_TILES)],
        compiler_params=_params("arbitrary", "arbitrary"),
        name="ssm_state_contrib",
    )(*[uz] * STATE_TILES, bend)


def _ssm_scan_kernel(sre_ref, sim_ref, ar_ref, ai_ref, h0r_ref, h0i_ref,
                     hpr_ref, hpi_ref, pr_ref, pi_ref, sr_ref, si_ref, hr_sc, hi_sc, *, n_prompt):
    i = pl.program_id(0)
    rb = sre_ref.shape[0]
    ar, ai = ar_ref[...], ai_ref[...]

    @pl.when(i == 0)
    def _():
        hr_sc[...] = jnp.zeros_like(hr_sc)
        hi_sc[...] = jnp.zeros_like(hi_sc)

    @pl.when(i < n_prompt)
    def _():
        def body(j, c):
            row = pl.ds(j, 1)
            hr, hi = hr_sc[...], hi_sc[...]
            hpr_ref[row, :] = hr
            hpi_ref[row, :] = hi
            hr_sc[...] = ar * hr - ai * hi + sre_ref[row, :]
            hi_sc[...] = ar * hi + ai * hr + sim_ref[row, :]
            return c

        lax.fori_loop(0, rb, body, 0)

    @pl.when(i == n_prompt - 1)
    def _():
        pr_ref[...] = hr_sc[...]
        pi_ref[...] = hi_sc[...]

    @pl.when(i == n_prompt)
    def _():
        h0r, h0i = h0r_ref[...], h0i_ref[...]
        hpr_ref[...] = h0r
        hpi_ref[...] = h0i
        sr_ref[...] = ar * h0r - ai * h0i + sre_ref[...]
        si_ref[...] = ar * h0i + ai * h0r + sim_ref[...]

    @pl.when(i > n_prompt)
    def _():
        hpr_ref[...] = jnp.zeros_like(hpr_ref)
        hpi_ref[...] = jnp.zeros_like(hpi_ref)


def _ssm_scan(s_re, s_im, a16r, a16i, h0r, h0i, jp):
    jpad, n = s_re.shape
    rb = h0r.shape[0]
    assert jp % rb == 0 and jpad % rb == 0
    n_prompt = jp // rb
    rows = pl.BlockSpec((rb, n), lambda i: (i, 0))
    const1 = pl.BlockSpec((1, n), lambda i: (0, 0))
    constb = pl.BlockSpec((rb, n), lambda i: (0, 0))
    big = jax.ShapeDtypeStruct((jpad, n), F32)
    one = jax.ShapeDtypeStruct((1, n), F32)
    bat = jax.ShapeDtypeStruct((rb, n), F32)
    return pl.pallas_call(
        functools.partial(_ssm_scan_kernel, n_prompt=n_prompt),
        out_shape=[big, big, one, one, bat, bat],
        grid=(jpad // rb,),
        in_specs=[rows, rows, const1, const1, constb, constb],
        out_specs=[rows, rows, const1, const1, constb, constb],
        scratch_shapes=[pltpu.VMEM((1, n), F32), pltpu.VMEM((1, n), F32)],
        compiler_params=_params("arbitrary"),
        name="ssm_scan",
    )(s_re, s_im, a16r, a16i, h0r, h0i)


def _ssm_out_kernel(u_ref, mt_ref, cin_ref, hpr_ref, hpi_ref, d_ref, o_ref, ut_ref, yt_ref):
    _build_ut(u_ref, ut_ref)
    p2 = 2 * SSM_STATE
    for gl in range(GROUPS_PER_TILE):
        pr = gl // 2
        hp = jnp.concatenate([hpr_ref[:, pr * p2:(pr + 1) * p2], hpi_ref[:, pr * p2:(pr + 1) * p2]],
                             axis=1).astype(BF16)
        yt = jnp.dot(mt_ref[gl], ut_ref[gl], preferred_element_type=F32)
        yt = yt + lax.dot_general(cin_ref[gl], hp, _NT, preferred_element_type=F32)
        for t in range(SSM_T):
            yt_ref[t, gl * SSM_GROUP:(gl + 1) * SSM_GROUP, :] = yt[t * SSM_GROUP:(t + 1) * SSM_GROUP, :]
    d = d_ref[...]
    for t in range(SSM_T):
        y = yt_ref[t].T + d * u_ref[_step_rows(t), :]
        o_ref[_step_rows(t), :] = jax.nn.gelu(y)


def _ssm_output(uz, mt, cin, hp_re, hp_im, d_skip, jpad):
    g, w, _ = mt.shape
    w2 = cin.shape[-1]
    ntile = g // GROUPS_PER_TILE
    sw = GROUPS_PER_TILE * SSM_STATE
    tok = pl.BlockSpec((CHUNK_TILE * SSM_T, LANES), lambda j, q: (j, q))
    hspec = pl.BlockSpec((CHUNK_TILE, sw), lambda j, q: (j, q))
    return pl.pallas_call(
        _ssm_out_kernel,
        out_shape=jax.ShapeDtypeStruct((jpad * SSM_T, g * SSM_GROUP), F32),
        grid=(jpad // CHUNK_TILE, ntile),
        in_specs=[
            tok,
            pl.BlockSpec((GROUPS_PER_TILE, w, w), lambda j, q: (q, 0, 0)),
            pl.BlockSpec((GROUPS_PER_TILE, w, w2), lambda j, q: (q, 0, 0)),
            hspec, hspec,
            pl.BlockSpec((1, LANES), lambda j, q: (0, q)),
        ],
        out_specs=tok,
        scratch_shapes=[
            pltpu.VMEM((GROUPS_PER_TILE, w, CHUNK_TILE), BF16),
            pltpu.VMEM((SSM_T, LANES, CHUNK_TILE), F32),
        ],
        compiler_params=_params("arbitrary", "arbitrary"),
        name="ssm_output",
    )(uz, mt, cin, hp_re, hp_im, d_skip)


def _a_glu_kernel(y_ref, z_ref, w_ref, o_ref, wb_ref):
    r = pl.program_id(0)

    @pl.when(r == 0)
    def _():
        wb_ref[...] = w_ref[...].astype(BF16)

    @pl.when(r > 0)
    def _():
        y = y_ref[...]
        gate = jnp.dot(y.astype(BF16), wb_ref[...], preferred_element_type=F32)
        y2 = y * jax.nn.sigmoid(gate)
        o_ref[...] = (y2 * jax.nn.silu(z_ref[...])).astype(BF16)


def _a_glu(yg, uz, w_glu, rows):
    tm = ROW_TILE
    e = yg.shape[1]
    return pl.pallas_call(
        _a_glu_kernel,
        out_shape=jax.ShapeDtypeStruct((rows, e), BF16),
        grid=(1 + rows // tm,),
        in_specs=[
            pl.BlockSpec((tm, e), lambda r: (_row_block(r), 0)),
            pl.BlockSpec((tm, e), lambda r: (_row_block(r), 1)),
            _resident((e, e), lambda r: (0, 0)),
        ],
        out_specs=pl.BlockSpec((tm, e), lambda r: (_row_block(r), 0)),
        scratch_shapes=[pltpu.VMEM((e, e), BF16)],
        compiler_params=_params("arbitrary"),
        name="a_glu",
    )(yg, uz, w_glu)


def _a_out_kernel(o_ref, w_ref, xp_ref, xs_ref, x1_ref, wb_ref, *, mp):
    m = pl.program_id(0) - 1

    @pl.when(m < 0)
    def _():
        wb_ref[...] = w_ref[...].astype(BF16)

    @pl.when((m >= 0) & (m < mp))
    def _():
        x1_ref[...] = xp_ref[...] + jnp.dot(o_ref[...], wb_ref[...], preferred_element_type=F32)

    @pl.when(m == mp)
    def _():
        x1_ref[...] = xs_ref[...] + jnp.dot(o_ref[...], wb_ref[...], preferred_element_type=F32)


def _a_out_proj(o, w_out, xp, xs):
    tm = ROW_TILE
    rows, e = o.shape
    d = w_out.shape[1]
    mp = xp.shape[0] // tm
    return pl.pallas_call(
        functools.partial(_a_out_kernel, mp=mp),
        out_shape=jax.ShapeDtypeStruct((rows, d), F32),
        grid=(1 + rows // tm,),
        in_specs=[
            pl.BlockSpec((tm, e), lambda r: (_row_block(r), 0)),
            _resident((e, d), lambda r: (0, 0)),
            pl.BlockSpec((tm, d), lambda r: (jnp.minimum(_row_block(r), mp - 1), 0)),
            _resident((tm, d), lambda r: (0, 0)),
        ],
        out_specs=pl.BlockSpec((tm, d), lambda r: (_row_block(r), 0)),
        scratch_shapes=[pltpu.VMEM((e, d), BF16)],
        compiler_params=_params("arbitrary"),
        name="a_out_proj",
    )(o, w_out, xp, xs)


def _rope(x, cos, sin):
    outs = []
    for c in range(x.shape[1] // HEAD_DIM):
        xc = x[:, c * HEAD_DIM:(c + 1) * HEAD_DIM]
        outs.append(xc * cos + pltpu.roll(xc, HEAD_DIM // 2, axis=1) * sin)
    return jnp.concatenate(outs, axis=1) if len(outs) > 1 else outs[0]


def _store_heads(o_ref, val, tm):
    for h in range(N_HEADS):
        o_ref[pl.ds(h, tm, stride=N_HEADS), :] = val[:, h * HEAD_DIM:(h + 1) * HEAD_DIM]


def _prompt_col(r, mp, col, last):
    return jnp.where(_row_block(r) >= mp, last, col)


def _sample_col(r, mp, col):
    return jnp.where(_row_block(r) == mp, col, 0)


def _kv_proj_kernel(x_ref, g_ref, w_ref, cos_ref, sin_ref, kp_ref, vp_ref, cbp_ref,
                    ks_ref, vs_ref, cbs_ref, xn_ref, wb_ref, *, tm, mp):
    m, n = pl.program_id(0) - 1, pl.program_id(1)
    hpb = w_ref.shape[1] // (2 * HEAD_DIM)

    for nn in range(4):
        @pl.when((m < 0) & (n == nn))
        def _():
            for hl in range(hpb):
                h = hpb * (nn % 2) + hl
                for half in range(2):
                    src = (2 * hl + half) * HEAD_DIM
                    wb_ref[2 * (nn // 2) + half, :, h * HEAD_DIM:(h + 1) * HEAD_DIM] = (
                        w_ref[:, src:src + HEAD_DIM].astype(BF16))

    @pl.when((m >= 0) & (n == 0))
    def _():
        xn_ref[...] = _rms_scale(x_ref[...], g_ref[...]).astype(BF16)

    for c in range(4):
        @pl.when((m >= 0) & (n == c))
        def _():
            acc = jnp.dot(xn_ref[...], wb_ref[c], preferred_element_type=F32)
            if c < 2:
                acc = _rope(acc, cos_ref[...], sin_ref[...])

            @pl.when(m < mp)
            def _():
                _store_heads(kp_ref if c < 2 else vp_ref, acc, tm)
                cbp_ref[...] = acc.astype(BF16)

            @pl.when(m == mp)
            def _():
                _store_heads(ks_ref if c < 2 else vs_ref, acc, tm)
                cbs_ref[...] = acc.astype(BF16)


def _kv_proj(x1, rows_p, g, w_kv, cos, sin):
    tm = ROW_TILE
    rows, d = x1.shape
    tn = N_HEADS * HEAD_DIM
    mp = rows_p // tm
    assert w_kv.shape[1] == 4 * tn and rows == rows_p + tm and rows_p % tm == 0
    out4 = lambda nrows: jax.ShapeDtypeStruct((nrows * N_HEADS, 2 * HEAD_DIM), F32)
    outb = lambda nrows: jax.ShapeDtypeStruct((nrows, 4 * tn), BF16)
    prow = lambda r: jnp.minimum(_row_block(r), mp - 1)
    col = _col_block
    return pl.pallas_call(
        functools.partial(_kv_proj_kernel, tm=tm, mp=mp),
        out_shape=[out4(rows_p), out4(rows_p), outb(rows_p), out4(tm), out4(tm), outb(tm)],
        grid=(1 + rows // tm, 4),
        in_specs=[
            pl.BlockSpec((tm, d), lambda r, n: (_row_block(r), 0)),
            _resident((1, d), lambda r, n: (0, 0)),
            _staged_weight_spec(d, tn, 4),
            pl.BlockSpec((tm, HEAD_DIM), lambda r, n: (_row_block(r), 0)),
            pl.BlockSpec((tm, HEAD_DIM), lambda r, n: (_row_block(r), 0)),
        ],
        out_specs=[
            pl.BlockSpec((tm * N_HEADS, HEAD_DIM),
                         lambda r, n: (prow(r), _prompt_col(r, mp, jnp.minimum(col(r, n), 1), 1))),
            pl.BlockSpec((tm * N_HEADS, HEAD_DIM),
                         lambda r, n: (prow(r), _prompt_col(r, mp, jnp.maximum(col(r, n) - 2, 0), 1))),
            pl.BlockSpec((tm, tn), lambda r, n: (prow(r), _prompt_col(r, mp, col(r, n), 3))),
            pl.BlockSpec((tm * N_HEADS, HEAD_DIM),
                         lambda r, n: (0, _sample_col(r, mp, jnp.minimum(col(r, n), 1)))),
            pl.BlockSpec((tm * N_HEADS, HEAD_DIM),
                         lambda r, n: (0, _sample_col(r, mp, jnp.maximum(col(r, n) - 2, 0)))),
            pl.BlockSpec((tm, tn), lambda r, n: (0, _sample_col(r, mp, col(r, n)))),
        ],
        scratch_shapes=[pltpu.VMEM((tm, d), BF16), pltpu.VMEM((4, d, tn), BF16)],
        compiler_params=_params("arbitrary", "arbitrary", vmem_limit=KV_PROJ_VMEM_LIMIT),
        name="kv_proj",
    )(x1, g, w_kv, cos, sin)


def _qz_proj_kernel(x_ref, g_ref, w_ref, cos_ref, sin_ref, qp_ref, zp_ref, qs_ref, zs_ref,
                    xn_ref, wb_ref, *, nq, mp):
    m, n = pl.program_id(0) - 1, pl.program_id(1)

    @pl.when(m < 0)
    def _():
        wb_ref[n] = w_ref[...].astype(BF16)

    @pl.when((m >= 0) & (n == 0))
    def _():
        xn_ref[...] = _rms_scale(x_ref[...], g_ref[...]).astype(BF16)

    for c in range(2 * nq):
        @pl.when((m >= 0) & (n == c))
        def _():
            acc = jnp.dot(xn_ref[...], wb_ref[c], preferred_element_type=F32)
            val = _rope(acc, cos_ref[...], sin_ref[...]).astype(BF16) if c < nq else acc

            @pl.when(m < mp)
            def _():
                (qp_ref if c < nq else zp_ref)[...] = val

            @pl.when(m == mp)
            def _():
                (qs_ref if c < nq else zs_ref)[...] = val


def _qz_proj(x1, rows_p, g, w, cos, sin, tn):
    tm = ROW_TILE
    rows, d = x1.shape
    half = w.shape[1] // 2
    nq = half // tn
    mp = rows_p // tm
    assert rows == rows_p + tm and rows_p % tm == 0
    prow = lambda r: jnp.minimum(_row_block(r), mp - 1)
    qcol = lambda r, n: jnp.minimum(_col_block(r, n), nq - 1)
    zcol = lambda r, n: jnp.maximum(_col_block(r, n) - nq, 0)
    return pl.pallas_call(
        functools.partial(_qz_proj_kernel, nq=nq, mp=mp),
        out_shape=[jax.ShapeDtypeStruct((rows_p, half), BF16), jax.ShapeDtypeStruct((rows_p, half), F32),
                   jax.ShapeDtypeStruct((tm, half), BF16), jax.ShapeDtypeStruct((tm, half), F32)],
        grid=(1 + rows // tm, 2 * nq),
        in_specs=[
            pl.BlockSpec((tm, d), lambda r, n: (_row_block(r), 0)),
            _resident((1, d), lambda r, n: (0, 0)),
            _staged_weight_spec(d, tn, 2 * nq),
            pl.BlockSpec((tm, HEAD_DIM), lambda r, n: (_row_block(r), 0)),
            pl.BlockSpec((tm, HEAD_DIM), lambda r, n: (_row_block(r), 0)),
        ],
        out_specs=[
            pl.BlockSpec((tm, tn), lambda r, n: (prow(r), _prompt_col(r, mp, qcol(r, n), nq - 1))),
            pl.BlockSpec((tm, tn), lambda r, n: (prow(r), _prompt_col(r, mp, zcol(r, n), nq - 1))),
            pl.BlockSpec((tm, tn), lambda r, n: (0, _sample_col(r, mp, qcol(r, n)))),
            pl.BlockSpec((tm, tn), lambda r, n: (0, _sample_col(r, mp, zcol(r, n)))),
        ],
        scratch_shapes=[pltpu.VMEM((tm, d), BF16), pltpu.VMEM((2 * nq, d, tn), BF16)],
        compiler_params=_params("arbitrary", "arbitrary"),
        name="qz_proj",
    )(x1, g, w, cos, sin)


def _rope_tables(pos):
    inv_freq = ROPE_THETA ** (-jnp.arange(0, HEAD_DIM, 2, dtype=F32) / HEAD_DIM)
    ang = pos.astype(F32)[:, None] * inv_freq[None, :]
    c, s = jnp.cos(ang), jnp.sin(ang)
    return jnp.concatenate([c, c], axis=-1), jnp.concatenate([-s, s], axis=-1)


def _diff_lambda(lq1_ref, lk1_ref, lq2_ref, lk2_ref, lambda_init):
    s1 = jnp.sum(lq1_ref[...] * lk1_ref[...], axis=-1, keepdims=True)
    s2 = jnp.sum(lq2_ref[...] * lk2_ref[...], axis=-1, keepdims=True)
    return jnp.exp(s1) - jnp.exp(s2) + lambda_init


def _attn_finish(a1, l1, a2, l2, lam, subln, z, lambda_init):
    o = a1 / l1 - lam * (a2 / l2)
    o = _rms_scale(o, subln) * (1.0 - lambda_init)
    return (o * jax.nn.silu(z)).astype(BF16)


def _attn_prompt_kernel(q_ref, k1_ref, k2_ref, v1_ref, v2_ref, z_ref,
                        lq1_ref, lk1_ref, lq2_ref, lk2_ref, sub_ref,
                        o_ref, m_sc, l_sc, al_sc, a_sc, *sp_scs, tq, ts, tk, lambda_init):
    qi = pl.program_id(1)
    c2 = HEAD_DIM ** -0.5 * math.log2(math.e)
    k_refs = (k1_ref, k2_ref)
    nsub = tq // ts
    s_scs, p_scs = sp_scs[:2 * nsub], sp_scs[2 * nsub:]
    m_sc[...] = jnp.full(m_sc.shape, NEG, F32)
    l_sc[...] = jnp.zeros(l_sc.shape, F32)
    a_sc[...] = jnp.zeros(a_sc.shape, F32)

    def softmax_strips(sub, n, col0):
        for r in range(ts // STRIP):
            row0 = sub * ts + r * STRIP
            lr = slice(r * STRIP, (r + 1) * STRIP)
            gr = slice(row0, row0 + STRIP)
            visible = tk if col0 is None else min(tk, (row0 // CHUNK + 1) * CHUNK - col0)
            assert visible > 0
            ncv = -(-visible // LANES)
            s = s_scs[2 * sub + n][lr, 0:ncv * LANES]
            if visible < ncv * LANES:
                col = lax.broadcasted_iota(jnp.int32, s.shape, 1)
                s = jnp.where(col < visible, s, NEG)
            cols = [s[:, c * LANES:(c + 1) * LANES] for c in range(ncv)]
            m_cur = jnp.max(functools.reduce(jnp.maximum, cols), axis=-1, keepdims=True)
            m_old = m_sc[n, gr, :]
            m_new = jnp.maximum(m_old, m_cur)
            ps = [jnp.exp2((c - m_new) * c2) for c in cols]
            l_cur = jnp.sum(functools.reduce(jnp.add, ps), axis=-1, keepdims=True)
            alpha = jnp.exp2((m_old - m_new) * c2)
            l_sc[n, gr, :] = alpha * l_sc[n, gr, :] + l_cur
            m_sc[n, gr, :] = m_new
            al_sc[n, gr, :] = alpha
            for c in range(keys_seen(sub, col0) // LANES):
                pc = ps[c].astype(BF16) if c < ncv else jnp.zeros((STRIP, LANES), BF16)
                p_scs[2 * sub + n][lr, c * LANES:(c + 1) * LANES] = pc

    def keys_seen(sub, col0):
        return tk if col0 is None else min(tk, (sub + 1) * ts - col0)

    def block(kb, col0, subs):
        start = pl.multiple_of(kb * tk, tk)
        for sub in subs:
            rows = pl.ds(start, keys_seen(sub, col0))
            for n in range(2):
                qn = q_ref[sub * ts:(sub + 1) * ts, n * HEAD_DIM:(n + 1) * HEAD_DIM]
                s_scs[2 * sub + n][:, 0:rows.size] = lax.dot_general(
                    qn, k_refs[n][rows, :], _NT, preferred_element_type=F32)
        for sub in subs:
            gr = slice(sub * ts, (sub + 1) * ts)
            rows = pl.ds(start, keys_seen(sub, col0))
            vblk = jnp.concatenate([v1_ref[rows, :], v2_ref[rows, :]], axis=1)
            for n in range(2):
                softmax_strips(sub, n, col0)
                pv = jnp.dot(p_scs[2 * sub + n][:, 0:rows.size], vblk, preferred_element_type=F32)
                alpha = al_sc[n, gr, :]
                a_sc[n, gr, :] = a_sc[n, gr, :] * jnp.concatenate([alpha, alpha], axis=1) + pv

    nd = tq // tk

    def body(kb, c):
        block(kb, None, range(nsub))
        return c

    lax.fori_loop(0, qi * nd, body, 0)
    for d in range(nd):
        block(qi * nd + d, d * tk, [sub for sub in range(nsub) if (sub + 1) * ts > d * tk])

    lam = _diff_lambda(lq1_ref, lk1_ref, lq2_ref, lk2_ref, lambda_init)
    l1 = jnp.concatenate([l_sc[0], l_sc[0]], axis=1)
    l2 = jnp.concatenate([l_sc[1], l_sc[1]], axis=1)
    o_ref[...] = _attn_finish(a_sc[0], l1, a_sc[1], l2, lam, sub_ref[...], z_ref[...], lambda_init)


def _attn_prompt(qb, z32, kvb, lq1, lk1, lq2, lk2, subln, lambda_init, tq, ts, tk):
    rows = qb.shape[0]
    hw = 2 * HEAD_DIM
    nsub = tq // ts
    assert tq % tk == 0 and tk % ts == 0 and ts % STRIP == 0 and CHUNK % STRIP == 0
    vec = pl.BlockSpec((1, HEAD_DIM), lambda h, i: (0, 0))
    kcol = lambda c: pl.BlockSpec((rows, HEAD_DIM), lambda h, i: (0, c * N_HEADS + h))
    stat = pltpu.VMEM((2, tq, LANES), F32)
    return pl.pallas_call(
        functools.partial(_attn_prompt_kernel, tq=tq, ts=ts, tk=tk, lambda_init=lambda_init),
        out_shape=jax.ShapeDtypeStruct((rows, N_HEADS * hw), BF16),
        grid=(N_HEADS, rows // tq),
        in_specs=[
            pl.BlockSpec((tq, hw), lambda h, i: (i, h)),
            kcol(0), kcol(1), kcol(2), kcol(3),
            pl.BlockSpec((tq, hw), lambda h, i: (i, h)),
            vec, vec, vec, vec,
            pl.BlockSpec((1, hw), lambda h, i: (0, 0)),
        ],
        out_specs=pl.BlockSpec((tq, hw), lambda h, i: (i, h)),
        scratch_shapes=[
            stat, stat, stat,
            pltpu.VMEM((2, tq, hw), F32),
            *[pltpu.VMEM((ts, tk), F32) for _ in range(2 * nsub)],
            *[pltpu.VMEM((ts, tk), BF16) for _ in range(2 * nsub)],
        ],
        compiler_params=_params("arbitrary", "arbitrary"),
        name="attn_prompt",
    )(qb, kvb, kvb, kvb, kvb, z32, lq1, lk1, lq2, lk2, subln)


def _attn_sample_kernel(q_ref, ck1_ref, ck2_ref, cv1_ref, cv2_ref, kvn_ref, z_ref,
                        lq1_ref, lk1_ref, lq2_ref, lk2_ref, sub_ref,
                        o_ref, m_sc, l_sc, a_sc, s_sc, p_sc, *, tkv, nkb, past_len, lambda_init):
    kb = pl.program_id(1)
    c2 = HEAD_DIM ** -0.5 * math.log2(math.e)
    t = q_ref.shape[0]
    hw = 2 * HEAD_DIM
    rows_all = 2 * N_HEADS * t
    strip = 4 * t
    ck_refs = (ck1_ref, ck2_ref)
    assert (past_len - 1) // CHUNK <= past_len // CHUNK

    @pl.when(kb == 0)
    def _():
        m_sc[...] = jnp.full(m_sc.shape, NEG, F32)
        l_sc[...] = jnp.zeros(l_sc.shape, F32)
        a_sc[...] = jnp.zeros(a_sc.shape, F32)

    col_blk = lambda c, h: slice((c * N_HEADS + h) * HEAD_DIM, (c * N_HEADS + h + 1) * HEAD_DIM)
    head_rows = lambda h: pl.ds(h, tkv, stride=N_HEADS)

    def update(with_new):
        ncols = tkv + (LANES if with_new else 0)
        if with_new:
            row = lax.broadcasted_iota(jnp.int32, (t, LANES), 0) + past_len
            col = lax.broadcasted_iota(jnp.int32, (t, LANES), 1)
            new_mask = (col < t) & (((col + past_len) // CHUNK) <= (row // CHUNK))
            pad_k = jnp.zeros((LANES - t, HEAD_DIM), BF16)
        for h in range(N_HEADS):
            for n in range(2):
                qn = q_ref[:, (2 * h + n) * HEAD_DIM:(2 * h + n + 1) * HEAD_DIM]
                rr = slice((2 * h + n) * t, (2 * h + n + 1) * t)
                k = ck_refs[n][head_rows(h), :].astype(BF16)
                s_sc[rr, 0:tkv] = lax.dot_general(qn, k, _NT, preferred_element_type=F32)
                if with_new:
                    kn = jnp.concatenate([kvn_ref[:, col_blk(n, h)], pad_k], axis=0)
                    sn = lax.dot_general(qn, kn, _NT, preferred_element_type=F32)
                    s_sc[rr, tkv:ncols] = jnp.where(new_mask, sn, NEG)
        for r in range(rows_all // strip):
            rr = slice(r * strip, (r + 1) * strip)
            cols = [s_sc[rr, c * LANES:(c + 1) * LANES] for c in range(ncols // LANES)]
            m_cur = jnp.max(functools.reduce(jnp.maximum, cols), axis=-1, keepdims=True)
            m_old = m_sc[rr, :]
            m_new = jnp.maximum(m_old, m_cur)
            ps = [jnp.exp2((c - m_new) * c2) for c in cols]
            l_cur = jnp.sum(functools.reduce(jnp.add, ps), axis=-1, keepdims=True)
            alpha = jnp.exp2((m_old - m_new) * c2)
            l_sc[rr, :] = alpha * l_sc[rr, :] + l_cur
            m_sc[rr, :] = m_new
            a_sc[rr, :] = a_sc[rr, :] * jnp.concatenate([alpha, alpha], axis=1)
            for c, pc in enumerate(ps):
                p_sc[rr, c * LANES:(c + 1) * LANES] = pc.astype(BF16)
        for h in range(N_HEADS):
            rr = slice(2 * h * t, (2 * h + 2) * t)
            v = jnp.concatenate([cv1_ref[head_rows(h), :], cv2_ref[head_rows(h), :]],
                                axis=1).astype(BF16)
            pv = jnp.dot(p_sc[rr, 0:tkv], v, preferred_element_type=F32)
            if with_new:
                vn = jnp.concatenate([kvn_ref[:, col_blk(2, h)], kvn_ref[:, col_blk(3, h)]], axis=1)
                vn = jnp.concatenate([vn, jnp.zeros((LANES - t, hw), BF16)], axis=0)
                pv = pv + jnp.dot(p_sc[rr, tkv:ncols], vn, preferred_element_type=F32)
            a_sc[rr, :] = a_sc[rr, :] + pv

    @pl.when(kb < nkb - 1)
    def _():
        update(False)

    @pl.when(kb == nkb - 1)
    def _():
        update(True)
        lam = _diff_lambda(lq1_ref, lk1_ref, lq2_ref, lk2_ref, lambda_init)
        sub = sub_ref[...]
        for h in range(N_HEADS):
            r1 = slice(2 * h * t, (2 * h + 1) * t)
            r2 = slice((2 * h + 1) * t, (2 * h + 2) * t)
            l1 = jnp.concatenate([l_sc[r1, :], l_sc[r1, :]], axis=1)
            l2 = jnp.concatenate([l_sc[r2, :], l_sc[r2, :]], axis=1)
            o_ref[:, h * hw:(h + 1) * hw] = _attn_finish(
                a_sc[r1, :], l1, a_sc[r2, :], l2, lam, sub, z_ref[:, h * hw:(h + 1) * hw], lambda_init)


def _attn_sample(qb, z32, kvb, cache_k2, cache_v2, lq1, lk1, lq2, lk2, subln, lambda_init, t, tkv):
    bsz, rows8, hw = cache_k2.shape
    past_len = rows8 // N_HEADS
    nkb = past_len // tkv
    rows_all = 2 * N_HEADS * t
    assert t <= LANES
    vec = pl.BlockSpec((1, HEAD_DIM), lambda b, k: (0, 0))
    cache = lambda c: pl.BlockSpec((None, tkv * N_HEADS, HEAD_DIM), lambda b, k: (b, k, c))
    full = lambda a: pl.BlockSpec((t, a.shape[1]), lambda b, k: (b, 0))
    return pl.pallas_call(
        functools.partial(_attn_sample_kernel, tkv=tkv, nkb=nkb, past_len=past_len,
                          lambda_init=lambda_init),
        out_shape=jax.ShapeDtypeStruct((bsz * t, N_HEADS * hw), BF16),
        grid=(bsz, nkb),
        in_specs=[
            full(qb), cache(0), cache(1), cache(0), cache(1), full(kvb), full(z32),
            vec, vec, vec, vec,
            pl.BlockSpec((1, hw), lambda b, k: (0, 0)),
        ],
        out_specs=pl.BlockSpec((t, N_HEADS * hw), lambda b, k: (b, 0)),
        scratch_shapes=[
            pltpu.VMEM((rows_all, LANES), F32),
            pltpu.VMEM((rows_all, LANES), F32),
            pltpu.VMEM((rows_all, hw), F32),
            pltpu.VMEM((rows_all, tkv + LANES), F32),
            pltpu.VMEM((rows_all, tkv + LANES), BF16),
        ],
        compiler_params=_params("arbitrary", "arbitrary"),
        name="attn_sample",
    )(qb, cache_k2, cache_k2, cache_v2, cache_v2, kvb, z32, lq1, lk1, lq2, lk2, subln)


def _b_out_kernel(op_ref, os_ref, w_ref, x_ref, g_ref, yp_ref, ys_ref, wb_ref, *, mp):
    m = pl.program_id(0) - 1

    @pl.when(m < 0)
    def _():
        wb_ref[...] = w_ref[...].astype(BF16)

    def finish(o_ref, y_ref):
        acc = jnp.dot(o_ref[...], wb_ref[...], preferred_element_type=F32)
        y_ref[...] = _rms_scale(x_ref[...] + acc, g_ref[...])

    @pl.when((m >= 0) & (m < mp))
    def _():
        finish(op_ref, yp_ref)

    @pl.when(m == mp)
    def _():
        finish(os_ref, ys_ref)


def _b_out_proj(og_p, og_s, w_out, x1, g):
    tm = ROW_TILE
    rows_p, e = og_p.shape
    rows, d = x1.shape
    mp = rows_p // tm
    assert og_s.shape[0] == tm and rows == rows_p + tm and rows_p % tm == 0
    prow = lambda r: jnp.minimum(_row_block(r), mp - 1)
    return pl.pallas_call(
        functools.partial(_b_out_kernel, mp=mp),
        out_shape=[jax.ShapeDtypeStruct((rows_p, d), F32), jax.ShapeDtypeStruct((tm, d), F32)],
        grid=(1 + rows // tm,),
        in_specs=[
            pl.BlockSpec((tm, e), lambda r: (prow(r), 0)),
            _resident((tm, e), lambda r: (0, 0)),
            _resident((e, d), lambda r: (0, 0)),
            pl.BlockSpec((tm, d), lambda r: (_row_block(r), 0)),
            _resident((1, d), lambda r: (0, 0)),
        ],
        out_specs=[pl.BlockSpec((tm, d), lambda r: (prow(r), 0)),
                   pl.BlockSpec((tm, d), lambda r: (0, 0))],
        scratch_shapes=[pltpu.VMEM((e, d), BF16)],
        compiler_params=_params("arbitrary"),
        name="b_out_proj",
    )(og_p, og_s, w_out, x1, g)


def kernel(x_prompt, x_sample, state_ssm_re, state_ssm_im, cache_k, cache_v, a_norm, a_w_in, a_lambda_re, a_lambda_im, a_log_dt, a_b_re, a_b_im, a_c_re, a_c_im, a_d, a_w_glu, a_w_out, kv_norm, w_kv, b_norm, b_w_in, b_lambda_q1, b_lambda_k1, b_lambda_q2, b_lambda_k2, b_subln, b_w_out, final_norm):
    bp, seq, d = x_prompt.shape
    bs, t_s, _ = x_sample.shape
    past_len = cache_k.shape[1]
    n_a, n_b = a_norm.shape[0], b_norm.shape[0]
    g_cnt, p = a_lambda_re.shape[1:]
    assert bp == 1 and n_a == 1 and n_b == 1
    assert t_s == SSM_T and seq % (SSM_T * bs) == 0 and seq % CHUNK == 0
    assert cache_k.shape[2:] == (N_HEADS, 2 * HEAD_DIM)

    jp, js = seq // SSM_T, bs
    jpad = -(-(jp + js) // CHUNK_TILE) * CHUNK_TILE
    rows_s = bs * t_s
    rows_a = seq + rows_s

    xp = x_prompt.reshape(seq, d)
    xs = x_sample.reshape(rows_s, d)
    uz = _a_in_proj(xp, xs, a_norm[0].reshape(1, d), a_w_in[0], jpad * SSM_T, tn=1024)
    mt, bend, cin, a16r, a16i = _ssm_prep(a_lambda_re[0], a_lambda_im[0], a_log_dt[0],
                                          a_b_re[0], a_b_im[0], a_c_re[0], a_c_im[0])
    s_re, s_im = _ssm_state_contrib(uz, bend, jpad)
    h0r = state_ssm_re[:, 0].reshape(bs, g_cnt * p)
    h0i = state_ssm_im[:, 0].reshape(bs, g_cnt * p)
    hp_re, hp_im, pre, pim, sre, sim = _ssm_scan(s_re, s_im, a16r, a16i, h0r, h0i, jp)
    yg = _ssm_output(uz, mt, cin, hp_re, hp_im, a_d[0].reshape(1, -1), jpad)
    o_a = _a_glu(yg, uz, a_w_glu[0], rows_a)
    x1 = _a_out_proj(o_a, a_w_out[0], xp, xs)

    lambda_init = 0.8 - 0.6 * math.exp(-0.3 * n_a)
    w_q = b_w_in[0]
    kv_g = kv_norm.reshape(1, d)
    b_g = b_norm[0].reshape(1, d)
    pos = jnp.concatenate([jnp.arange(seq, dtype=jnp.int32),
                           jnp.tile(past_len + jnp.arange(t_s, dtype=jnp.int32), bs)])
    cos, sin = _rope_tables(pos)
    k_p, v_p, kvb_p, k_s, v_s, kvb_s = _kv_proj(x1, seq, kv_g, w_kv, cos, sin)
    qb_p, z_p, qb_s, z_s = _qz_proj(x1, seq, b_g, w_q, cos, sin, tn=1024)

    vecs = [a[0].reshape(1, HEAD_DIM) for a in (b_lambda_q1, b_lambda_k1, b_lambda_q2, b_lambda_k2)]
    subln = b_subln[0].reshape(1, 2 * HEAD_DIM)
    hw = 2 * HEAD_DIM
    og_p = _attn_prompt(qb_p, z_p, kvb_p, *vecs, subln, lambda_init, tq=1024, ts=512, tk=1024)
    og_s = _attn_sample(qb_s, z_s, kvb_s, cache_k.reshape(bs, past_len * N_HEADS, hw),
                        cache_v.reshape(bs, past_len * N_HEADS, hw), *vecs, subln, lambda_init,
                        t_s, tkv=1024)
    w_o = b_w_out[0]
    fg = final_norm.reshape(1, d)
    y_p, y_s = _b_out_proj(og_p, og_s, w_o, x1, fg)

    return (y_p.reshape(bp, seq, d), y_s.reshape(bs, t_s, d),
            pre.reshape(bp, n_a, g_cnt, p), pim.reshape(bp, n_a, g_cnt, p),
            k_p.reshape(bp, seq, N_HEADS, hw), v_p.reshape(bp, seq, N_HEADS, hw),
            sre.reshape(bs, n_a, g_cnt, p), sim.reshape(bs, n_a, g_cnt, p),
            k_s.reshape(bs, t_s, N_HEADS, hw), v_s.reshape(bs, t_s, N_HEADS, hw))
```

```python
import functools
import math

import jax
import jax.numpy as jnp
from jax import lax
from jax.experimental import pallas as pl
from jax.experimental.pallas import tpu as pltpu

F32 = jnp.float32
BF16 = jnp.bfloat16

CHUNK = 64
HEAD_DIM = 128
N_HEADS = 8
SSM_GROUP = 16
SSM_STATE = 64
SSM_T = 16
ROPE_THETA = 10000.0
Q_SCALE = HEAD_DIM ** -0.5 * math.log2(math.e)
EPS = 1e-6
NEG = -1e30

LANES = 128
SUBLANES = 8
GROUPS_PER_TILE = LANES // SSM_GROUP
CHUNK_TILE = 128
ROW_TILE = 512
STRIP = 32
VMEM_LIMIT = 56 * 1024 * 1024
KV_PROJ_VMEM_LIMIT = 58 * 1024 * 1024

_NT = (((1,), (1,)), ((), ()))


def _params(*sem, vmem_limit=VMEM_LIMIT):
    return pltpu.CompilerParams(dimension_semantics=sem, vmem_limit_bytes=vmem_limit)


def _resident(shape, index_map):
    return pl.BlockSpec(shape, index_map, pipeline_mode=pl.Buffered(1))


def _rms_scale(x, g):
    ms = jnp.mean(x * x, axis=-1, keepdims=True)
    return x * lax.rsqrt(ms + EPS) * g


def _staged_weight_spec(d, tn, nb):
    return pl.BlockSpec((d, tn), lambda r, n: (0, jnp.where(r == 0, n, nb - 1)),
                        pipeline_mode=pl.Buffered(1))


def _row_block(r):
    return jnp.maximum(r - 1, 0)


def _col_block(r, n):
    return jnp.where(r == 0, 0, n)


def _a_in_kernel(xp_ref, xs_ref, g_ref, w_ref, o_ref, xn_ref, wb_ref, *, mp, nb):
    m, n = pl.program_id(0) - 1, pl.program_id(1)

    @pl.when(m < 0)
    def _():
        wb_ref[n] = w_ref[...].astype(BF16)

    @pl.when((n == 0) & (m >= 0) & (m < mp))
    def _():
        xn_ref[...] = _rms_scale(xp_ref[...], g_ref[...]).astype(BF16)

    @pl.when((n == 0) & (m == mp))
    def _():
        xn_ref[...] = _rms_scale(xs_ref[...], g_ref[...]).astype(BF16)

    for c in range(nb):
        @pl.when((m >= 0) & (m <= mp) & (n == c))
        def _():
            o_ref[...] = jnp.dot(xn_ref[...], wb_ref[c], preferred_element_type=F32)

    @pl.when(m > mp)
    def _():
        o_ref[...] = jnp.zeros(o_ref.shape, F32)


def _a_in_proj(xp, xs, g, w, rows_pad, tn):
    tm = ROW_TILE
    d = g.shape[-1]
    n_out = w.shape[1]
    mp = xp.shape[0] // tm
    nb = n_out // tn
    assert xs.shape[0] == tm and xp.shape[0] % tm == 0 and rows_pad % tm == 0
    return pl.pallas_call(
        functools.partial(_a_in_kernel, mp=mp, nb=nb),
        out_shape=jax.ShapeDtypeStruct((rows_pad, n_out), F32),
        grid=(1 + rows_pad // tm, nb),
        in_specs=[
            pl.BlockSpec((tm, d), lambda r, n: (jnp.minimum(_row_block(r), mp - 1), 0)),
            _resident((tm, d), lambda r, n: (0, 0)),
            _resident((1, d), lambda r, n: (0, 0)),
            _staged_weight_spec(d, tn, nb),
        ],
        out_specs=pl.BlockSpec((tm, tn), lambda r, n: (_row_block(r), _col_block(r, n))),
        scratch_shapes=[pltpu.VMEM((tm, d), BF16), pltpu.VMEM((nb, d, tn), BF16)],
        compiler_params=_params("arbitrary", "arbitrary"),
        name="a_in_proj",
    )(xp, xs, g, w)


PREP_PAIRS = LANES // SSM_T


def _ssm_prep_kernel(lrx_ref, lix_ref, dtx_ref, bre_ref, bim_ref, lrr_ref, lir_ref, dtr_ref,
                     cre_ref, cim_ref, mt_ref, bend_ref, cin_ref, a16r_ref, a16i_ref):
    p = SSM_STATE
    p2 = 2 * p
    w = SSM_T * SSM_GROUP
    hi = lax.Precision.HIGHEST

    lr, li = lrx_ref[...], lix_ref[...]
    dt = jnp.exp(dtx_ref[...])
    zr, zi = lr * dt, li * dt
    mag = jnp.exp(zr)
    n_re, n_im = mag * jnp.cos(zi) - 1.0, mag * jnp.sin(zi)
    den = lr * lr + li * li
    cf_re = (n_re * lr + n_im * li) / den
    cf_im = (n_im * lr - n_re * li) / den
    lane = lax.broadcasted_iota(jnp.int32, (p2, LANES), 1)
    e_end = (SSM_T - 1 - lane % SSM_T).astype(F32)
    pm = jnp.exp(zr * e_end)
    pw_re, pw_im = pm * jnp.cos(zi * e_end), pm * jnp.sin(zi * e_end)
    cp_re = cf_re * pw_re - cf_im * pw_im
    cp_im = cf_re * pw_im + cf_im * pw_re
    coef = jnp.concatenate([cp_re, cp_im, cf_re, cf_im], axis=0)
    sel_row = lax.broadcasted_iota(jnp.int32, (LANES, w), 0)
    sel_col = lax.broadcasted_iota(jnp.int32, (LANES, w), 1)

    tau = lax.broadcasted_iota(jnp.int32, (SSM_T + SUBLANES, p2), 0).astype(F32)
    lane2 = lax.broadcasted_iota(jnp.int32, (w, p2), 1)
    sblk = lax.broadcasted_iota(jnp.int32, (w, w), 1) // SSM_GROUP

    for j in range(PREP_PAIRS):
        own_pair = (sel_row // SSM_T) == j
        spread = jnp.where(own_pair & (sel_row % SSM_T == sel_col // SSM_GROUP), 1.0, 0.0)
        cpx = jnp.dot(coef[:2 * p2], spread, precision=hi, preferred_element_type=F32)
        cfx = jnp.broadcast_to(coef[2 * p2:, j * SSM_T:j * SSM_T + 1], (2 * p2, w))
        b_re, b_im = bre_ref[j], bim_ref[j]
        end_re = cpx[:p2] * b_re - cpx[p2:] * b_im
        end_im = cpx[:p2] * b_im + cpx[p2:] * b_re
        bb_re = cfx[:p2] * b_re - cfx[p2:] * b_im
        bb_im = cfx[:p2] * b_im + cfx[p2:] * b_re
        for r in range(2):
            sl = slice(r * p, (r + 1) * p)
            bend_ref[2 * j + r] = jnp.concatenate([end_re[sl], end_im[sl]], axis=0).astype(BF16)
        bst = jnp.concatenate([bb_re, bb_im], axis=0)

        lr2, li2 = lrr_ref[j], lir_ref[j]
        dt2 = jnp.exp(dtr_ref[j])
        zr2, zi2 = lr2 * dt2, li2 * dt2
        qm = jnp.exp(zr2 * tau)
        q_re, q_im = qm * jnp.cos(zi2 * tau), qm * jnp.sin(zi2 * tau)
        c_re, c_im = cre_ref[j], cim_ref[j]

        def c_times_powers(first):
            re = [c_re * q_re[t:t + 1] - c_im * q_im[t:t + 1] for t in range(first, first + SSM_T)]
            im = [c_re * q_im[t:t + 1] + c_im * q_re[t:t + 1] for t in range(first, first + SSM_T)]
            return jnp.concatenate(re, axis=0), jnp.concatenate(im, axis=0)

        g_re, g_im = c_times_powers(0)
        ci_re, ci_im = c_times_powers(1)
        for r in range(2):
            own = (lane2 // p) == r
            lhs = jnp.concatenate([jnp.where(own, g_re, 0.0), jnp.where(own, -g_im, 0.0)], axis=1)
            gen = jnp.dot(lhs, bst, preferred_element_type=F32)
            k = 0
            while (SSM_GROUP << k) < w:
                sh = SSM_GROUP << k
                shifted = jnp.concatenate([jnp.zeros((sh, w), F32), gen[:w - sh]], axis=0)
                gen = jnp.where(((sblk >> k) & 1) == 1, shifted, gen)
                k += 1
            mt_ref[2 * j + r] = gen.astype(BF16)
            cin_ref[2 * j + r] = jnp.concatenate(
                [jnp.where(own, ci_re, 0.0), jnp.where(own, -ci_im, 0.0)], axis=1).astype(BF16)
        a16r_ref[j] = q_re[SSM_T:SSM_T + 1]
        a16i_ref[j] = q_im[SSM_T:SSM_T + 1]


def _ssm_prep(lam_re, lam_im, log_dt, b_re, b_im, c_re, c_im):
    g, p = lam_re.shape
    gp = g // 2
    p2 = 2 * p
    w = SSM_T * SSM_GROUP
    pp = PREP_PAIRS
    assert gp % pp == 0
    colx = lambda a: jnp.repeat(a.reshape(gp, p2).T, SSM_T, axis=1)
    row = lambda a: a.reshape(gp, 1, p2)
    dt_full = jnp.broadcast_to(log_dt[:, None], (g, p))
    b_t = lambda a: jnp.tile(a.reshape(gp, p2, SSM_GROUP), (1, 1, SSM_T))
    c_pair = lambda a: a.reshape(gp, 2, SSM_GROUP, p).transpose(0, 2, 1, 3).reshape(gp, SSM_GROUP, p2)
    xspec = pl.BlockSpec((p2, LANES), lambda i: (0, i))
    rowspec = pl.BlockSpec((pp, 1, p2), lambda i: (i, 0, 0))
    bspec = pl.BlockSpec((pp, p2, w), lambda i: (i, 0, 0))
    cspec = pl.BlockSpec((pp, SSM_GROUP, p2), lambda i: (i, 0, 0))
    mt, bend, cin, a16r, a16i = pl.pallas_call(
        _ssm_prep_kernel,
        out_shape=[
            jax.ShapeDtypeStruct((g, w, w), BF16),
            jax.ShapeDtypeStruct((g, p2, w), BF16),
            jax.ShapeDtypeStruct((g, w, 2 * p2), BF16),
            jax.ShapeDtypeStruct((gp, 1, p2), F32),
            jax.ShapeDtypeStruct((gp, 1, p2), F32),
        ],
        grid=(gp // pp,),
        in_specs=[xspec, xspec, xspec, bspec, bspec, rowspec, rowspec, rowspec, cspec, cspec],
        out_specs=[
            pl.BlockSpec((2 * pp, w, w), lambda i: (i, 0, 0)),
            pl.BlockSpec((2 * pp, p2, w), lambda i: (i, 0, 0)),
            pl.BlockSpec((2 * pp, w, 2 * p2), lambda i: (i, 0, 0)),
            rowspec, rowspec,
        ],
        compiler_params=_params("arbitrary"),
        name="ssm_prep",
    )(colx(lam_re), colx(lam_im), colx(dt_full), b_t(b_re), b_t(b_im),
      row(lam_re), row(lam_im), row(dt_full), c_pair(c_re), c_pair(c_im))
    return mt, bend, cin, a16r.reshape(1, g * p), a16i.reshape(1, g * p)


def _step_rows(s):
    return pl.ds(s, CHUNK_TILE, stride=SSM_T)


def _build_ut(u_ref, ut_ref):
    for s in range(SSM_T):
        xt = u_ref[_step_rows(s), :].T.astype(BF16)
        for gl in range(GROUPS_PER_TILE):
            ut_ref[gl, s * SSM_GROUP:(s + 1) * SSM_GROUP, :] = xt[gl * SSM_GROUP:(gl + 1) * SSM_GROUP, :]


STATE_TILES = 2


def _ssm_state_kernel(*refs):
    u_refs = refs[:STATE_TILES]
    bend_ref, sre_ref, sim_ref = refs[STATE_TILES:STATE_TILES + 3]
    ut_refs = refs[STATE_TILES + 3:]
    p = SSM_STATE
    for tile in range(STATE_TILES):
        _build_ut(u_refs[tile], ut_refs[tile])
        for pr in range(GROUPS_PER_TILE // 2):
            st = [jnp.dot(bend_ref[tile * GROUPS_PER_TILE + 2 * pr + r], ut_refs[tile][2 * pr + r],
                          preferred_element_type=F32)
                  for r in range(2)]
            cols = slice((tile * GROUPS_PER_TILE // 2 + pr) * 2 * p,
                         (tile * GROUPS_PER_TILE // 2 + pr + 1) * 2 * p)
            sre_ref[:, cols] = jnp.concatenate([st[0][:p], st[1][:p]], axis=0).T
            sim_ref[:, cols] = jnp.concatenate([st[0][p:], st[1][p:]], axis=0).T


def _ssm_state_contrib(uz, bend, jpad):
    g, p2, w = bend.shape
    gstep = STATE_TILES * GROUPS_PER_TILE
    assert g % gstep == 0
    sw = gstep * SSM_STATE
    out = jax.ShapeDtypeStruct((jpad, g * SSM_STATE), F32)
    ospec = pl.BlockSpec((CHUNK_TILE, sw), lambda j, q: (j, q))
    utile = lambda t: pl.BlockSpec((CHUNK_TILE * SSM_T, LANES), lambda j, q: (j, STATE_TILES * q + t))
    return pl.pallas_call(
        _ssm_state_kernel,
        out_shape=[out, out],
        grid=(jpad // CHUNK_TILE, g // gstep),
        in_specs=[*[utile(t) for t in range(STATE_TILES)],
                  pl.BlockSpec((gstep, p2, w), lambda j, q: (q, 0, 0))],
        out_specs=[ospec, ospec],
        scratch_shapes=[pltpu.VMEM((GROUPS_PER_TILE, w, CHUNK_TILE), BF16) for _ in range(STATE_TILES)],
        compiler_params=_params("arbitrary", "arbitrary"),
        name="ssm_state_contrib",
    )(*[uz] * STATE_TILES, bend)


def _ssm_scan_kernel(sre_ref, sim_ref, ar_ref, ai_ref, h0r_ref, h0i_ref,
                     hpr_ref, hpi_ref, pr_ref, pi_ref, sr_ref, si_ref, hr_sc, hi_sc, *, n_prompt):
    i = pl.program_id(0)
    rb = sre_ref.shape[0]
    ar, ai = ar_ref[...], ai_ref[...]

    @pl.when(i == 0)
    def _():
        hr_sc[...] = jnp.zeros_like(hr_sc)
        hi_sc[...] = jnp.zeros_like(hi_sc)

    @pl.when(i < n_prompt)
    def _():
        def body(j, c):
            row = pl.ds(j, 1)
            hr, hi = hr_sc[...], hi_sc[...]
            hpr_ref[row, :] = hr
            hpi_ref[row, :] = hi
            hr_sc[...] = ar * hr - ai * hi + sre_ref[row, :]
            hi_sc[...] = ar * hi + ai * hr + sim_ref[row, :]
            return c

        lax.fori_loop(0, rb, body, 0)

    @pl.when(i == n_prompt - 1)
    def _():
        pr_ref[...] = hr_sc[...]
        pi_ref[...] = hi_sc[...]

    @pl.when(i == n_prompt)
    def _():
        h0r, h0i = h0r_ref[...], h0i_ref[...]
        hpr_ref[...] = h0r
        hpi_ref[...] = h0i
        sr_ref[...] = ar * h0r - ai * h0i + sre_ref[...]
        si_ref[...] = ar * h0i + ai * h0r + sim_ref[...]

    @pl.when(i > n_prompt)
    def _():
        hpr_ref[...] = jnp.zeros_like(hpr_ref)
        hpi_ref[...] = jnp.zeros_like(hpi_ref)


def _ssm_scan(s_re, s_im, a16r, a16i, h0r, h0i, jp):
    jpad, n = s_re.shape
    rb = h0r.shape[0]
    assert jp % rb == 0 and jpad % rb == 0
    n_prompt = jp // rb
    rows = pl.BlockSpec((rb, n), lambda i: (i, 0))
    const1 = pl.BlockSpec((1, n), lambda i: (0, 0))
    constb = pl.BlockSpec((rb, n), lambda i: (0, 0))
    big = jax.ShapeDtypeStruct((jpad, n), F32)
    one = jax.ShapeDtypeStruct((1, n), F32)
    bat = jax.ShapeDtypeStruct((rb, n), F32)
    return pl.pallas_call(
        functools.partial(_ssm_scan_kernel, n_prompt=n_prompt),
        out_shape=[big, big, one, one, bat, bat],
        grid=(jpad // rb,),
        in_specs=[rows, rows, const1, const1, constb, constb],
        out_specs=[rows, rows, const1, const1, constb, constb],
        scratch_shapes=[pltpu.VMEM((1, n), F32), pltpu.VMEM((1, n), F32)],
        compiler_params=_params("arbitrary"),
        name="ssm_scan",
    )(s_re, s_im, a16r, a16i, h0r, h0i)


def _ssm_out_kernel(u_ref, mt_ref, cin_ref, hpr_ref, hpi_ref, d_ref, o_ref, ut_ref, yt_ref):
    _build_ut(u_ref, ut_ref)
    p2 = 2 * SSM_STATE
    for gl in range(GROUPS_PER_TILE):
        pr = gl // 2
        hp = jnp.concatenate([hpr_ref[:, pr * p2:(pr + 1) * p2], hpi_ref[:, pr * p2:(pr + 1) * p2]],
                             axis=1).astype(BF16)
        yt = jnp.dot(mt_ref[gl], ut_ref[gl], preferred_element_type=F32)
        yt = yt + lax.dot_general(cin_ref[gl], hp, _NT, preferred_element_type=F32)
        for t in range(SSM_T):
            yt_ref[t, gl * SSM_GROUP:(gl + 1) * SSM_GROUP, :] = yt[t * SSM_GROUP:(t + 1) * SSM_GROUP, :]
    d = d_ref[...]
    for t in range(SSM_T):
        y = yt_ref[t].T + d * u_ref[_step_rows(t), :]
        o_ref[_step_rows(t), :] = jax.nn.gelu(y)


def _ssm_output(uz, mt, cin, hp_re, hp_im, d_skip, jpad):
    g, w, _ = mt.shape
    w2 = cin.shape[-1]
    ntile = g // GROUPS_PER_TILE
    sw = GROUPS_PER_TILE * SSM_STATE
    tok = pl.BlockSpec((CHUNK_TILE * SSM_T, LANES), lambda j, q: (j, q))
    hspec = pl.BlockSpec((CHUNK_TILE, sw), lambda j, q: (j, q))
    return pl.pallas_call(
        _ssm_out_kernel,
        out_shape=jax.ShapeDtypeStruct((jpad * SSM_T, g * SSM_GROUP), F32),
        grid=(jpad // CHUNK_TILE, ntile),
        in_specs=[
            tok,
            pl.BlockSpec((GROUPS_PER_TILE, w, w), lambda j, q: (q, 0, 0)),
            pl.BlockSpec((GROUPS_PER_TILE, w, w2), lambda j, q: (q, 0, 0)),
            hspec, hspec,
            pl.BlockSpec((1, LANES), lambda j, q: (0, q)),
        ],
        out_specs=tok,
        scratch_shapes=[
            pltpu.VMEM((GROUPS_PER_TILE, w, CHUNK_TILE), BF16),
            pltpu.VMEM((SSM_T, LANES, CHUNK_TILE), F32),
        ],
        compiler_params=_params("arbitrary", "arbitrary"),
        name="ssm_output",
    )(uz, mt, cin, hp_re, hp_im, d_skip)


def _a_glu_kernel(y_ref, z_ref, w_ref, o_ref, wb_ref):
    r = pl.program_id(0)

    @pl.when(r == 0)
    def _():
        wb_ref[...] = w_ref[...].astype(BF16)

    @pl.when(r > 0)
    def _():
        y = y_ref[...]
        gate = jnp.dot(y.astype(BF16), wb_ref[...], preferred_element_type=F32)
        y2 = y * jax.nn.sigmoid(gate)
        o_ref[...] = (y2 * jax.nn.silu(z_ref[...])).astype(BF16)


def _a_glu(yg, uz, w_glu, rows):
    tm = ROW_TILE
    e = yg.shape[1]
    return pl.pallas_call(
        _a_glu_kernel,
        out_shape=jax.ShapeDtypeStruct((rows, e), BF16),
        grid=(1 + rows // tm,),
        in_specs=[
            pl.BlockSpec((tm, e), lambda r: (_row_block(r), 0)),
            pl.BlockSpec((tm, e), lambda r: (_row_block(r), 1)),
            _resident((e, e), lambda r: (0, 0)),
        ],
        out_specs=pl.BlockSpec((tm, e), lambda r: (_row_block(r), 0)),
        scratch_shapes=[pltpu.VMEM((e, e), BF16)],
        compiler_params=_params("arbitrary"),
        name="a_glu",
    )(yg, uz, w_glu)


def _a_out_kernel(o_ref, w_ref, xp_ref, xs_ref, x1_ref, wb_ref, *, mp):
    m = pl.program_id(0) - 1

    @pl.when(m < 0)
    def _():
        wb_ref[...] = w_ref[...].astype(BF16)

    @pl.when((m >= 0) & (m < mp))
    def _():
        x1_ref[...] = xp_ref[...] + jnp.dot(o_ref[...], wb_ref[...], preferred_element_type=F32)

    @pl.when(m == mp)
    def _():
        x1_ref[...] = xs_ref[...] + jnp.dot(o_ref[...], wb_ref[...], preferred_element_type=F32)


def _a_out_proj(o, w_out, xp, xs):
    tm = ROW_TILE
    rows, e = o.shape
    d = w_out.shape[1]
    mp = xp.shape[0] // tm
    return pl.pallas_call(
        functools.partial(_a_out_kernel, mp=mp),
        out_shape=jax.ShapeDtypeStruct((rows, d), F32),
        grid=(1 + rows // tm,),
        in_specs=[
            pl.BlockSpec((tm, e), lambda r: (_row_block(r), 0)),
            _resident((e, d), lambda r: (0, 0)),
            pl.BlockSpec((tm, d), lambda r: (jnp.minimum(_row_block(r), mp - 1), 0)),
            _resident((tm, d), lambda r: (0, 0)),
        ],
        out_specs=pl.BlockSpec((tm, d), lambda r: (_row_block(r), 0)),
        scratch_shapes=[pltpu.VMEM((e, d), BF16)],
        compiler_params=_params("arbitrary"),
        name="a_out_proj",
    )(o, w_out, xp, xs)


def _rope(x, cos, sin):
    outs = []
    for c in range(x.shape[1] // HEAD_DIM):
        xc = x[:, c * HEAD_DIM:(c + 1) * HEAD_DIM]
        outs.append(xc * cos + pltpu.roll(xc, HEAD_DIM // 2, axis=1) * sin)
    return jnp.concatenate(outs, axis=1) if len(outs) > 1 else outs[0]


def _store_heads(o_ref, val, tm):
    for h in range(N_HEADS):
        o_ref[pl.ds(h, tm, stride=N_HEADS), :] = val[:, h * HEAD_DIM:(h + 1) * HEAD_DIM]


def _prompt_col(r, mp, col, last):
    return jnp.where(_row_block(r) >= mp, last, col)


def _sample_col(r, mp, col):
    return jnp.where(_row_block(r) == mp, col, 0)


def _kv_proj_kernel(x_ref, g_ref, w_ref, cos_ref, sin_ref, kp_ref, vp_ref, cbp_ref,
                    ks_ref, vs_ref, cbs_ref, xn_ref, wb_ref, *, tm, mp):
    m, n = pl.program_id(0) - 1, pl.program_id(1)
    hpb = w_ref.shape[1] // (2 * HEAD_DIM)

    for nn in range(4):
        @pl.when((m < 0) & (n == nn))
        def _():
            for hl in range(hpb):
                h = hpb * (nn % 2) + hl
                for half in range(2):
                    src = (2 * hl + half) * HEAD_DIM
                    wb_ref[2 * (nn // 2) + half, :, h * HEAD_DIM:(h + 1) * HEAD_DIM] = (
                        w_ref[:, src:src + HEAD_DIM].astype(BF16))

    @pl.when((m >= 0) & (n == 0))
    def _():
        xn_ref[...] = _rms_scale(x_ref[...], g_ref[...]).astype(BF16)

    def column(c, k_ref, v_ref, cb_ref):
        acc = jnp.dot(xn_ref[...], wb_ref[c], preferred_element_type=F32)
        if c < 2:
            acc = _rope(acc, cos_ref[...], sin_ref[...])
        _store_heads(k_ref if c < 2 else v_ref, acc, tm)
        cb_ref[...] = acc.astype(BF16)

    for c in range(4):
        @pl.when((m >= 0) & (m < mp) & (n == c))
        def _():
            column(c, kp_ref, vp_ref, cbp_ref)

        @pl.when((m == mp) & (n == c))
        def _():
            column(c, ks_ref, vs_ref, cbs_ref)


def _kv_proj(x1, rows_p, g, w_kv, cos, sin):
    tm = ROW_TILE
    rows, d = x1.shape
    tn = N_HEADS * HEAD_DIM
    mp = rows_p // tm
    assert w_kv.shape[1] == 4 * tn and rows == rows_p + tm and rows_p % tm == 0
    out4 = lambda nrows: jax.ShapeDtypeStruct((nrows * N_HEADS, 2 * HEAD_DIM), F32)
    outb = lambda nrows: jax.ShapeDtypeStruct((nrows, 4 * tn), BF16)
    prow = lambda r: jnp.minimum(_row_block(r), mp - 1)
    col = _col_block
    return pl.pallas_call(
        functools.partial(_kv_proj_kernel, tm=tm, mp=mp),
        out_shape=[out4(rows_p), out4(rows_p), outb(rows_p), out4(tm), out4(tm), outb(tm)],
        grid=(1 + rows // tm, 4),
        in_specs=[
            pl.BlockSpec((tm, d), lambda r, n: (_row_block(r), 0)),
            _resident((1, d), lambda r, n: (0, 0)),
            _staged_weight_spec(d, tn, 4),
            pl.BlockSpec((tm, HEAD_DIM), lambda r, n: (_row_block(r), 0)),
            pl.BlockSpec((tm, HEAD_DIM), lambda r, n: (_row_block(r), 0)),
        ],
        out_specs=[
            pl.BlockSpec((tm * N_HEADS, HEAD_DIM),
                         lambda r, n: (prow(r), _prompt_col(r, mp, jnp.minimum(col(r, n), 1), 1))),
            pl.BlockSpec((tm * N_HEADS, HEAD_DIM),
                         lambda r, n: (prow(r), _prompt_col(r, mp, jnp.maximum(col(r, n) - 2, 0), 1))),
            pl.BlockSpec((tm, tn), lambda r, n: (prow(r), _prompt_col(r, mp, col(r, n), 3))),
            pl.BlockSpec((tm * N_HEADS, HEAD_DIM),
                         lambda r, n: (0, _sample_col(r, mp, jnp.minimum(col(r, n), 1)))),
            pl.BlockSpec((tm * N_HEADS, HEAD_DIM),
                         lambda r, n: (0, _sample_col(r, mp, jnp.maximum(col(r, n) - 2, 0)))),
            pl.BlockSpec((tm, tn), lambda r, n: (0, _sample_col(r, mp, col(r, n)))),
        ],
        scratch_shapes=[pltpu.VMEM((tm, d), BF16), pltpu.VMEM((4, d, tn), BF16)],
        compiler_params=_params("arbitrary", "arbitrary", vmem_limit=KV_PROJ_VMEM_LIMIT),
        name="kv_proj",
    )(x1, g, w_kv, cos, sin)


def _qz_proj_kernel(x_ref, g_ref, w_ref, cos_ref, sin_ref, qp_ref, zp_ref, qs_ref, zs_ref,
                    xn_ref, wb_ref, *, nq, mp):
    m, n = pl.program_id(0) - 1, pl.program_id(1)

    @pl.when(m < 0)
    def _():
        wb_ref[n] = w_ref[...].astype(BF16)

    @pl.when((m >= 0) & (n == 0))
    def _():
        xn_ref[...] = _rms_scale(x_ref[...], g_ref[...]).astype(BF16)

    def column(c, q_ref, z_ref):
        acc = jnp.dot(xn_ref[...], wb_ref[c], preferred_element_type=F32)
        if c < nq:
            q_ref[...] = (_rope(acc, cos_ref[...], sin_ref[...]) * Q_SCALE).astype(BF16)
        else:
            z_ref[...] = acc

    for c in range(2 * nq):
        @pl.when((m >= 0) & (m < mp) & (n == c))
        def _():
            column(c, qp_ref, zp_ref)

        @pl.when((m == mp) & (n == c))
        def _():
            column(c, qs_ref, zs_ref)


def _qz_proj(x1, rows_p, g, w, cos, sin, tn):
    tm = ROW_TILE
    rows, d = x1.shape
    half = w.shape[1] // 2
    nq = half // tn
    mp = rows_p // tm
    assert rows == rows_p + tm and rows_p % tm == 0
    prow = lambda r: jnp.minimum(_row_block(r), mp - 1)
    qcol = lambda r, n: jnp.minimum(_col_block(r, n), nq - 1)
    zcol = lambda r, n: jnp.maximum(_col_block(r, n) - nq, 0)
    return pl.pallas_call(
        functools.partial(_qz_proj_kernel, nq=nq, mp=mp),
        out_shape=[jax.ShapeDtypeStruct((rows_p, half), BF16), jax.ShapeDtypeStruct((rows_p, half), F32),
                   jax.ShapeDtypeStruct((tm, half), BF16), jax.ShapeDtypeStruct((tm, half), F32)],
        grid=(1 + rows // tm, 2 * nq),
        in_specs=[
            pl.BlockSpec((tm, d), lambda r, n: (_row_block(r), 0)),
            _resident((1, d), lambda r, n: (0, 0)),
            _staged_weight_spec(d, tn, 2 * nq),
            pl.BlockSpec((tm, HEAD_DIM), lambda r, n: (_row_block(r), 0)),
            pl.BlockSpec((tm, HEAD_DIM), lambda r, n: (_row_block(r), 0)),
        ],
        out_specs=[
            pl.BlockSpec((tm, tn), lambda r, n: (prow(r), _prompt_col(r, mp, qcol(r, n), nq - 1))),
            pl.BlockSpec((tm, tn), lambda r, n: (prow(r), _prompt_col(r, mp, zcol(r, n), nq - 1))),
            pl.BlockSpec((tm, tn), lambda r, n: (0, _sample_col(r, mp, qcol(r, n)))),
            pl.BlockSpec((tm, tn), lambda r, n: (0, _sample_col(r, mp, zcol(r, n)))),
        ],
        scratch_shapes=[pltpu.VMEM((tm, d), BF16), pltpu.VMEM((2 * nq, d, tn), BF16)],
        compiler_params=_params("arbitrary", "arbitrary"),
        name="qz_proj",
    )(x1, g, w, cos, sin)


def _rope_tables(pos):
    inv_freq = ROPE_THETA ** (-jnp.arange(0, HEAD_DIM, 2, dtype=F32) / HEAD_DIM)
    ang = pos.astype(F32)[:, None] * inv_freq[None, :]
    c, s = jnp.cos(ang), jnp.sin(ang)
    return jnp.concatenate([c, c], axis=-1), jnp.concatenate([-s, s], axis=-1)


def _diff_lambda(lq1_ref, lk1_ref, lq2_ref, lk2_ref, lambda_init):
    s1 = jnp.sum(lq1_ref[...] * lk1_ref[...], axis=-1, keepdims=True)
    s2 = jnp.sum(lq2_ref[...] * lk2_ref[...], axis=-1, keepdims=True)
    return jnp.exp(s1) - jnp.exp(s2) + lambda_init


def _attn_finish(a1, l1, a2, l2, lam, subln, z, lambda_init):
    o = a1 / l1 - lam * (a2 / l2)
    o = _rms_scale(o, subln) * (1.0 - lambda_init)
    return (o * jax.nn.silu(z)).astype(BF16)


def _attn_prompt_kernel(q_ref, k1_ref, k2_ref, v1_ref, v2_ref, z_ref,
                        lq1_ref, lk1_ref, lq2_ref, lk2_ref, sub_ref,
                        o_ref, m_sc, l_sc, al_sc, a_sc, *sp_scs, tq, ts, tk, lambda_init):
    qi = pl.program_id(1)
    k_refs = (k1_ref, k2_ref)
    nsub = tq // ts
    s_scs, p_scs = sp_scs[:2 * nsub], sp_scs[2 * nsub:]
    m_sc[...] = jnp.full(m_sc.shape, NEG, F32)
    l_sc[...] = jnp.zeros(l_sc.shape, F32)
    a_sc[...] = jnp.zeros(a_sc.shape, F32)

    def softmax_strips(sub, n, col0):
        for r in range(ts // STRIP):
            row0 = sub * ts + r * STRIP
            lr = slice(r * STRIP, (r + 1) * STRIP)
            gr = slice(row0, row0 + STRIP)
            visible = tk if col0 is None else min(tk, (row0 // CHUNK + 1) * CHUNK - col0)
            assert visible > 0
            ncv = -(-visible // LANES)
            s = s_scs[2 * sub + n][lr, 0:ncv * LANES]
            if visible < ncv * LANES:
                col = lax.broadcasted_iota(jnp.int32, s.shape, 1)
                s = jnp.where(col < visible, s, NEG)
            cols = [s[:, c * LANES:(c + 1) * LANES] for c in range(ncv)]
            m_cur = jnp.max(functools.reduce(jnp.maximum, cols), axis=-1, keepdims=True)
            m_old = m_sc[n, gr, :]
            m_new = jnp.maximum(m_old, m_cur)
            ps = [jnp.exp2(c - m_new) for c in cols]
            l_cur = jnp.sum(functools.reduce(jnp.add, ps), axis=-1, keepdims=True)
            alpha = jnp.exp2(m_old - m_new)
            l_sc[n, gr, :] = alpha * l_sc[n, gr, :] + l_cur
            m_sc[n, gr, :] = m_new
            al_sc[n, gr, :] = alpha
            for c in range(keys_seen(sub, col0) // LANES):
                pc = ps[c].astype(BF16) if c < ncv else jnp.zeros((STRIP, LANES), BF16)
                p_scs[2 * sub + n][lr, c * LANES:(c + 1) * LANES] = pc

    def keys_seen(sub, col0):
        return tk if col0 is None else min(tk, (sub + 1) * ts - col0)

    def block(kb, col0, subs):
        start = pl.multiple_of(kb * tk, tk)
        for sub in subs:
            rows = pl.ds(start, keys_seen(sub, col0))
            for n in range(2):
                qn = q_ref[sub * ts:(sub + 1) * ts, n * HEAD_DIM:(n + 1) * HEAD_DIM]
                s_scs[2 * sub + n][:, 0:rows.size] = lax.dot_general(
                    qn, k_refs[n][rows, :], _NT, preferred_element_type=F32)
        for sub in subs:
            gr = slice(sub * ts, (sub + 1) * ts)
            rows = pl.ds(start, keys_seen(sub, col0))
            vblk = jnp.concatenate([v1_ref[rows, :], v2_ref[rows, :]], axis=1)
            for n in range(2):
                softmax_strips(sub, n, col0)
                pv = jnp.dot(p_scs[2 * sub + n][:, 0:rows.size], vblk, preferred_element_type=F32)
                alpha = al_sc[n, gr, :]
                a_sc[n, gr, :] = a_sc[n, gr, :] * jnp.concatenate([alpha, alpha], axis=1) + pv

    nd = tq // tk

    def body(kb, c):
        block(kb, None, range(nsub))
        return c

    lax.fori_loop(0, qi * nd, body, 0)
    for d in range(nd):
        block(qi * nd + d, d * tk, [sub for sub in range(nsub) if (sub + 1) * ts > d * tk])

    lam = _diff_lambda(lq1_ref, lk1_ref, lq2_ref, lk2_ref, lambda_init)
    l1 = jnp.concatenate([l_sc[0], l_sc[0]], axis=1)
    l2 = jnp.concatenate([l_sc[1], l_sc[1]], axis=1)
    o_ref[...] = _attn_finish(a_sc[0], l1, a_sc[1], l2, lam, sub_ref[...], z_ref[...], lambda_init)


def _attn_prompt(qb, z32, kvb, lq1, lk1, lq2, lk2, subln, lambda_init, tq, ts, tk):
    rows = qb.shape[0]
    hw = 2 * HEAD_DIM
    nsub = tq // ts
    assert tq % tk == 0 and tk % ts == 0 and ts % STRIP == 0 and CHUNK % STRIP == 0
    vec = pl.BlockSpec((1, HEAD_DIM), lambda h, i: (0, 0))
    kcol = lambda c: pl.BlockSpec((rows, HEAD_DIM), lambda h, i: (0, c * N_HEADS + h))
    stat = pltpu.VMEM((2, tq, LANES), F32)
    return pl.pallas_call(
        functools.partial(_attn_prompt_kernel, tq=tq, ts=ts, tk=tk, lambda_init=lambda_init),
        out_shape=jax.ShapeDtypeStruct((rows, N_HEADS * hw), BF16),
        grid=(N_HEADS, rows // tq),
        in_specs=[
            pl.BlockSpec((tq, hw), lambda h, i: (i, h)),
            kcol(0), kcol(1), kcol(2), kcol(3),
            pl.BlockSpec((tq, hw), lambda h, i: (i, h)),
            vec, vec, vec, vec,
            pl.BlockSpec((1, hw), lambda h, i: (0, 0)),
        ],
        out_specs=pl.BlockSpec((tq, hw), lambda h, i: (i, h)),
        scratch_shapes=[
            stat, stat, stat,
            pltpu.VMEM((2, tq, hw), F32),
            *[pltpu.VMEM((ts, tk), F32) for _ in range(2 * nsub)],
            *[pltpu.VMEM((ts, tk), BF16) for _ in range(2 * nsub)],
        ],
        compiler_params=_params("arbitrary", "arbitrary"),
        name="attn_prompt",
    )(qb, kvb, kvb, kvb, kvb, z32, lq1, lk1, lq2, lk2, subln)


def _attn_sample_kernel(q_ref, ck1_ref, ck2_ref, cv1_ref, cv2_ref, kvn_ref, z_ref,
                        lq1_ref, lk1_ref, lq2_ref, lk2_ref, sub_ref,
                        o_ref, m_sc, l_sc, a_sc, s_sc, p_sc, *, tkv, nkb, past_len, lambda_init):
    kb = pl.program_id(1)
    t = q_ref.shape[0]
    hw = 2 * HEAD_DIM
    rows_all = 2 * N_HEADS * t
    strip = 4 * t
    ck_refs = (ck1_ref, ck2_ref)
    assert (past_len - 1) // CHUNK <= past_len // CHUNK

    @pl.when(kb == 0)
    def _():
        m_sc[...] = jnp.full(m_sc.shape, NEG, F32)
        l_sc[...] = jnp.zeros(l_sc.shape, F32)
        a_sc[...] = jnp.zeros(a_sc.shape, F32)

    col_blk = lambda c, h: slice((c * N_HEADS + h) * HEAD_DIM, (c * N_HEADS + h + 1) * HEAD_DIM)
    head_rows = lambda h: pl.ds(h, tkv, stride=N_HEADS)

    def update(with_new):
        ncols = tkv + (LANES if with_new else 0)
        if with_new:
            row = lax.broadcasted_iota(jnp.int32, (t, LANES), 0) + past_len
            col = lax.broadcasted_iota(jnp.int32, (t, LANES), 1)
            new_mask = (col < t) & (((col + past_len) // CHUNK) <= (row // CHUNK))
            pad_k = jnp.zeros((LANES - t, HEAD_DIM), BF16)
        for h in range(N_HEADS):
            for n in range(2):
                qn = q_ref[:, (2 * h + n) * HEAD_DIM:(2 * h + n + 1) * HEAD_DIM]
                rr = slice((2 * h + n) * t, (2 * h + n + 1) * t)
                k = ck_refs[n][head_rows(h), :].astype(BF16)
                s_sc[rr, 0:tkv] = lax.dot_general(qn, k, _NT, preferred_element_type=F32)
                if with_new:
                    kn = jnp.concatenate([kvn_ref[:, col_blk(n, h)], pad_k], axis=0)
                    sn = lax.dot_general(qn, kn, _NT, preferred_element_type=F32)
                    s_sc[rr, tkv:ncols] = jnp.where(new_mask, sn, NEG)
        for r in range(rows_all // strip):
            rr = slice(r * strip, (r + 1) * strip)
            cols = [s_sc[rr, c * LANES:(c + 1) * LANES] for c in range(ncols // LANES)]
            m_cur = jnp.max(functools.reduce(jnp.maximum, cols), axis=-1, keepdims=True)
            m_old = m_sc[rr, :]
            m_new = jnp.maximum(m_old, m_cur)
            ps = [jnp.exp2(c - m_new) for c in cols]
            l_cur = jnp.sum(functools.reduce(jnp.add, ps), axis=-1, keepdims=True)
            alpha = jnp.exp2(m_old - m_new)
            l_sc[rr, :] = alpha * l_sc[rr, :] + l_cur
            m_sc[rr, :] = m_new
            a_sc[rr, :] = a_sc[rr, :] * jnp.concatenate([alpha, alpha], axis=1)
            for c, pc in enumerate(ps):
                p_sc[rr, c * LANES:(c + 1) * LANES] = pc.astype(BF16)
        for h in range(N_HEADS):
            rr = slice(2 * h * t, (2 * h + 2) * t)
            v = jnp.concatenate([cv1_ref[head_rows(h), :], cv2_ref[head_rows(h), :]],
                                axis=1).astype(BF16)
            pv = jnp.dot(p_sc[rr, 0:tkv], v, preferred_element_type=F32)
            if with_new:
                vn = jnp.concatenate([kvn_ref[:, col_blk(2, h)], kvn_ref[:, col_blk(3, h)]], axis=1)
                vn = jnp.concatenate([vn, jnp.zeros((LANES - t, hw), BF16)], axis=0)
                pv = pv + jnp.dot(p_sc[rr, tkv:ncols], vn, preferred_element_type=F32)
            a_sc[rr, :] = a_sc[rr, :] + pv

    @pl.when(kb < nkb - 1)
    def _():
        update(False)

    @pl.when(kb == nkb - 1)
    def _():
        update(True)
        lam = _diff_lambda(lq1_ref, lk1_ref, lq2_ref, lk2_ref, lambda_init)
        sub = sub_ref[...]
        for h in range(N_HEADS):
            r1 = slice(2 * h * t, (2 * h + 1) * t)
            r2 = slice((2 * h + 1) * t, (2 * h + 2) * t)
            l1 = jnp.concatenate([l_sc[r1, :], l_sc[r1, :]], axis=1)
            l2 = jnp.concatenate([l_sc[r2, :], l_sc[r2, :]], axis=1)
            o_ref[:, h * hw:(h + 1) * hw] = _attn_finish(
                a_sc[r1, :], l1, a_sc[r2, :], l2, lam, sub, z_ref[:, h * hw:(h + 1) * hw], lambda_init)


def _attn_sample(qb, z32, kvb, cache_k2, cache_v2, lq1, lk1, lq2, lk2, subln, lambda_init, t, tkv):
    bsz, rows8, hw = cache_k2.shape
    past_len = rows8 // N_HEADS
    nkb = past_len // tkv
    rows_all = 2 * N_HEADS * t
    assert t <= LANES
    vec = pl.BlockSpec((1, HEAD_DIM), lambda b, k: (0, 0))
    cache = lambda c: pl.BlockSpec((None, tkv * N_HEADS, HEAD_DIM), lambda b, k: (b, k, c))
    full = lambda a: pl.BlockSpec((t, a.shape[1]), lambda b, k: (b, 0))
    return pl.pallas_call(
        functools.partial(_attn_sample_kernel, tkv=tkv, nkb=nkb, past_len=past_len,
                          lambda_init=lambda_init),
        out_shape=jax.ShapeDtypeStruct((bsz * t, N_HEADS * hw), BF16),
        grid=(bsz, nkb),
        in_specs=[
            full(qb), cache(0), cache(1), cache(0), cache(1), full(kvb), full(z32),
            vec, vec, vec, vec,
            pl.BlockSpec((1, hw), lambda b, k: (0, 0)),
        ],
        out_specs=pl.BlockSpec((t, N_HEADS * hw), lambda b, k: (b, 0)),
        scratch_shapes=[
            pltpu.VMEM((rows_all, LANES), F32),
            pltpu.VMEM((rows_all, LANES), F32),
            pltpu.VMEM((rows_all, hw), F32),
            pltpu.VMEM((rows_all, tkv + LANES), F32),
            pltpu.VMEM((rows_all, tkv + LANES), BF16),
        ],
        compiler_params=_params("arbitrary", "arbitrary"),
        name="attn_sample",
    )(qb, cache_k2, cache_k2, cache_v2, cache_v2, kvb, z32, lq1, lk1, lq2, lk2, subln)


def _b_out_kernel(op_ref, os_ref, w_ref, x_ref, g_ref, yp_ref, ys_ref, wb_ref, *, mp):
    m = pl.program_id(0) - 1

    @pl.when(m < 0)
    def _():
        wb_ref[...] = w_ref[...].astype(BF16)

    def finish(o_ref, y_ref):
        acc = jnp.dot(o_ref[...], wb_ref[...], preferred_element_type=F32)
        y_ref[...] = _rms_scale(x_ref[...] + acc, g_ref[...])

    @pl.when((m >= 0) & (m < mp))
    def _():
        finish(op_ref, yp_ref)

    @pl.when(m == mp)
    def _():
        finish(os_ref, ys_ref)


def _b_out_proj(og_p, og_s, w_out, x1, g):
    tm = ROW_TILE
    rows_p, e = og_p.shape
    rows, d = x1.shape
    mp = rows_p // tm
    assert og_s.shape[0] == tm and rows == rows_p + tm and rows_p % tm == 0
    prow = lambda r: jnp.minimum(_row_block(r), mp - 1)
    return pl.pallas_call(
        functools.partial(_b_out_kernel, mp=mp),
        out_shape=[jax.ShapeDtypeStruct((rows_p, d), F32), jax.ShapeDtypeStruct((tm, d), F32)],
        grid=(1 + rows // tm,),
        in_specs=[
            pl.BlockSpec((tm, e), lambda r: (prow(r), 0)),
            _resident((tm, e), lambda r: (0, 0)),
            _resident((e, d), lambda r: (0, 0)),
            pl.BlockSpec((tm, d), lambda r: (_row_block(r), 0)),
            _resident((1, d), lambda r: (0, 0)),
        ],
        out_specs=[pl.BlockSpec((tm, d), lambda r: (prow(r), 0)),
                   pl.BlockSpec((tm, d), lambda r: (0, 0))],
        scratch_shapes=[pltpu.VMEM((e, d), BF16)],
        compiler_params=_params("arbitrary"),
        name="b_out_proj",
    )(og_p, og_s, w_out, x1, g)


def kernel(x_prompt, x_sample, state_ssm_re, state_ssm_im, cache_k, cache_v, a_norm, a_w_in, a_lambda_re, a_lambda_im, a_log_dt, a_b_re, a_b_im, a_c_re, a_c_im, a_d, a_w_glu, a_w_out, kv_norm, w_kv, b_norm, b_w_in, b_lambda_q1, b_lambda_k1, b_lambda_q2, b_lambda_k2, b_subln, b_w_out, final_norm):
    bp, seq, d = x_prompt.shape
    bs, t_s, _ = x_sample.shape
    past_len = cache_k.shape[1]
    n_a, n_b = a_norm.shape[0], b_norm.shape[0]
    g_cnt, p = a_lambda_re.shape[1:]
    assert bp == 1 and n_a == 1 and n_b == 1
    assert t_s == SSM_T and seq % (SSM_T * bs) == 0 and seq % CHUNK == 0
    assert cache_k.shape[2:] == (N_HEADS, 2 * HEAD_DIM)

    jp, js = seq // SSM_T, bs
    jpad = -(-(jp + js) // CHUNK_TILE) * CHUNK_TILE
    rows_s = bs * t_s
    rows_a = seq + rows_s

    xp = x_prompt.reshape(seq, d)
    xs = x_sample.reshape(rows_s, d)
    uz = _a_in_proj(xp, xs, a_norm[0].reshape(1, d), a_w_in[0], jpad * SSM_T, tn=1024)
    mt, bend, cin, a16r, a16i = _ssm_prep(a_lambda_re[0], a_lambda_im[0], a_log_dt[0],
                                          a_b_re[0], a_b_im[0], a_c_re[0], a_c_im[0])
    s_re, s_im = _ssm_state_contrib(uz, bend, jpad)
    h0r = state_ssm_re[:, 0].reshape(bs, g_cnt * p)
    h0i = state_ssm_im[:, 0].reshape(bs, g_cnt * p)
    hp_re, hp_im, pre, pim, sre, sim = _ssm_scan(s_re, s_im, a16r, a16i, h0r, h0i, jp)
    yg = _ssm_output(uz, mt, cin, hp_re, hp_im, a_d[0].reshape(1, -1), jpad)
    o_a = _a_glu(yg, uz, a_w_glu[0], rows_a)
    x1 = _a_out_proj(o_a, a_w_out[0], xp, xs)

    lambda_init = 0.8 - 0.6 * math.exp(-0.3 * n_a)
    w_q = b_w_in[0]
    kv_g = kv_norm.reshape(1, d)
    b_g = b_norm[0].reshape(1, d)
    pos = jnp.concatenate([jnp.arange(seq, dtype=jnp.int32),
                           jnp.tile(past_len + jnp.arange(t_s, dtype=jnp.int32), bs)])
    cos, sin = _rope_tables(pos)
    k_p, v_p, kvb_p, k_s, v_s, kvb_s = _kv_proj(x1, seq, kv_g, w_kv, cos, sin)
    qb_p, z_p, qb_s, z_s = _qz_proj(x1, seq, b_g, w_q, cos, sin, tn=1024)

    vecs = [a[0].reshape(1, HEAD_DIM) for a in (b_lambda_q1, b_lambda_k1, b_lambda_q2, b_lambda_k2)]
    subln = b_subln[0].reshape(1, 2 * HEAD_DIM)
    hw = 2 * HEAD_DIM
    og_p = _attn_prompt(qb_p, z_p, kvb_p, *vecs, subln, lambda_init, tq=1024, ts=512, tk=1024)
    og_s = _attn_sample(qb_s, z_s, kvb_s, cache_k.reshape(bs, past_len * N_HEADS, hw),
                        cache_v.reshape(bs, past_len * N_HEADS, hw), *vecs, subln, lambda_init,
                        t_s, tkv=1024)
    w_o = b_w_out[0]
    fg = final_norm.reshape(1, d)
    y_p, y_s = _b_out_proj(og_p, og_s, w_o, x1, fg)

    return (y_p.reshape(bp, seq, d), y_s.reshape(bs, t_s, d),
            pre.reshape(bp, n_a, g_cnt, p), pim.reshape(bp, n_a, g_cnt, p),
            k_p.reshape(bp, seq, N_HEADS, hw), v_p.reshape(bp, seq, N_HEADS, hw),
            sre.reshape(bs, n_a, g_cnt, p), sim.reshape(bs, n_a, g_cnt, p),
            k_s.reshape(bs, t_s, N_HEADS, hw), v_s.reshape(bs, t_s, N_HEADS, hw))
```

```python
import functools
import math

import jax
import jax.numpy as jnp
from jax import lax
from jax.experimental import pallas as pl
from jax.experimental.pallas import tpu as pltpu

F32 = jnp.float32
BF16 = jnp.bfloat16

CHUNK = 64
HEAD_DIM = 128
N_HEADS = 8
SSM_GROUP = 16
SSM_STATE = 64
SSM_T = 16
ROPE_THETA = 10000.0
Q_SCALE = HEAD_DIM ** -0.5 * math.log2(math.e)
EPS = 1e-6
NEG = -1e30

LANES = 128
SUBLANES = 8
GROUPS_PER_TILE = LANES // SSM_GROUP
CHUNK_TILE = 128
ROW_TILE = 512
STRIP = 32
VMEM_LIMIT = 56 * 1024 * 1024
KV_PROJ_VMEM_LIMIT = 58 * 1024 * 1024

_NT = (((1,), (1,)), ((), ()))


def _params(*sem, vmem_limit=VMEM_LIMIT):
    return pltpu.CompilerParams(dimension_semantics=sem, vmem_limit_bytes=vmem_limit)


def _resident(shape, index_map):
    return pl.BlockSpec(shape, index_map, pipeline_mode=pl.Buffered(1))


def _rms_scale(x, g):
    ms = jnp.mean(x * x, axis=-1, keepdims=True)
    return x * lax.rsqrt(ms + EPS) * g


def _staged_weight_spec(d, tn, nb):
    return pl.BlockSpec((d, tn), lambda r, n: (0, jnp.where(r == 0, n, nb - 1)),
                        pipeline_mode=pl.Buffered(1))


def _row_block(r):
    return jnp.maximum(r - 1, 0)


def _col_block(r, n):
    return jnp.where(r == 0, 0, n)


def _a_in_kernel(xp_ref, xs_ref, g_ref, w_ref, o_ref, xn_ref, wb_ref, *, mp, nb):
    m, n = pl.program_id(0) - 1, pl.program_id(1)

    @pl.when(m < 0)
    def _():
        wb_ref[n] = w_ref[...].astype(BF16)

    @pl.when((n == 0) & (m >= 0) & (m < mp))
    def _():
        xn_ref[...] = _rms_scale(xp_ref[...], g_ref[...]).astype(BF16)

    @pl.when((n == 0) & (m == mp))
    def _():
        xn_ref[...] = _rms_scale(xs_ref[...], g_ref[...]).astype(BF16)

    for c in range(nb):
        @pl.when((m >= 0) & (m <= mp) & (n == c))
        def _():
            o_ref[...] = jnp.dot(xn_ref[...], wb_ref[c], preferred_element_type=F32)

    @pl.when(m > mp)
    def _():
        o_ref[...] = jnp.zeros(o_ref.shape, F32)


def _a_in_proj(xp, xs, g, w, rows_pad, tn):
    tm = ROW_TILE
    d = g.shape[-1]
    n_out = w.shape[1]
    mp = xp.shape[0] // tm
    nb = n_out // tn
    assert xs.shape[0] == tm and xp.shape[0] % tm == 0 and rows_pad % tm == 0
    return pl.pallas_call(
        functools.partial(_a_in_kernel, mp=mp, nb=nb),
        out_shape=jax.ShapeDtypeStruct((rows_pad, n_out), F32),
        grid=(1 + rows_pad // tm, nb),
        in_specs=[
            pl.BlockSpec((tm, d), lambda r, n: (jnp.minimum(_row_block(r), mp - 1), 0)),
            _resident((tm, d), lambda r, n: (0, 0)),
            _resident((1, d), lambda r, n: (0, 0)),
            _staged_weight_spec(d, tn, nb),
        ],
        out_specs=pl.BlockSpec((tm, tn), lambda r, n: (_row_block(r), _col_block(r, n))),
        scratch_shapes=[pltpu.VMEM((tm, d), BF16), pltpu.VMEM((nb, d, tn), BF16)],
        compiler_params=_params("arbitrary", "arbitrary"),
        name="a_in_proj",
    )(xp, xs, g, w)


PREP_PAIRS = LANES // SSM_T


def _ssm_prep_kernel(lrx_ref, lix_ref, dtx_ref, bre_ref, bim_ref, lrr_ref, lir_ref, dtr_ref,
                     cre_ref, cim_ref, mt_ref, bend_ref, cin_ref, a16r_ref, a16i_ref):
    p = SSM_STATE
    p2 = 2 * p
    w = SSM_T * SSM_GROUP
    hi = lax.Precision.HIGHEST

    lr, li = lrx_ref[...], lix_ref[...]
    dt = jnp.exp(dtx_ref[...])
    zr, zi = lr * dt, li * dt
    mag = jnp.exp(zr)
    n_re, n_im = mag * jnp.cos(zi) - 1.0, mag * jnp.sin(zi)
    den = lr * lr + li * li
    cf_re = (n_re * lr + n_im * li) / den
    cf_im = (n_im * lr - n_re * li) / den
    lane = lax.broadcasted_iota(jnp.int32, (p2, LANES), 1)
    e_end = (SSM_T - 1 - lane % SSM_T).astype(F32)
    pm = jnp.exp(zr * e_end)
    pw_re, pw_im = pm * jnp.cos(zi * e_end), pm * jnp.sin(zi * e_end)
    cp_re = cf_re * pw_re - cf_im * pw_im
    cp_im = cf_re * pw_im + cf_im * pw_re
    coef = jnp.concatenate([cp_re, cp_im, cf_re, cf_im], axis=0)
    sel_row = lax.broadcasted_iota(jnp.int32, (LANES, w), 0)
    sel_col = lax.broadcasted_iota(jnp.int32, (LANES, w), 1)

    tau = lax.broadcasted_iota(jnp.int32, (SSM_T + SUBLANES, p2), 0).astype(F32)
    lane2 = lax.broadcasted_iota(jnp.int32, (w, p2), 1)
    sblk = lax.broadcasted_iota(jnp.int32, (w, w), 1) // SSM_GROUP

    for j in range(PREP_PAIRS):
        own_pair = (sel_row // SSM_T) == j
        spread = jnp.where(own_pair & (sel_row % SSM_T == sel_col // SSM_GROUP), 1.0, 0.0)
        cpx = jnp.dot(coef[:2 * p2], spread, precision=hi, preferred_element_type=F32)
        cfx = jnp.broadcast_to(coef[2 * p2:, j * SSM_T:j * SSM_T + 1], (2 * p2, w))
        b_re, b_im = bre_ref[j], bim_ref[j]
        end_re = cpx[:p2] * b_re - cpx[p2:] * b_im
        end_im = cpx[:p2] * b_im + cpx[p2:] * b_re
        bb_re = cfx[:p2] * b_re - cfx[p2:] * b_im
        bb_im = cfx[:p2] * b_im + cfx[p2:] * b_re
        for r in range(2):
            sl = slice(r * p, (r + 1) * p)
            bend_ref[2 * j + r] = jnp.concatenate([end_re[sl], end_im[sl]], axis=0).astype(BF16)
        bst = jnp.concatenate([bb_re, bb_im], axis=0)

        lr2, li2 = lrr_ref[j], lir_ref[j]
        dt2 = jnp.exp(dtr_ref[j])
        zr2, zi2 = lr2 * dt2, li2 * dt2
        qm = jnp.exp(zr2 * tau)
        q_re, q_im = qm * jnp.cos(zi2 * tau), qm * jnp.sin(zi2 * tau)
        c_re, c_im = cre_ref[j], cim_ref[j]

        def c_times_powers(first):
            re = [c_re * q_re[t:t + 1] - c_im * q_im[t:t + 1] for t in range(first, first + SSM_T)]
            im = [c_re * q_im[t:t + 1] + c_im * q_re[t:t + 1] for t in range(first, first + SSM_T)]
            return jnp.concatenate(re, axis=0), jnp.concatenate(im, axis=0)

        g_re, g_im = c_times_powers(0)
        ci_re, ci_im = c_times_powers(1)
        for r in range(2):
            own = (lane2 // p) == r
            lhs = jnp.concatenate([jnp.where(own, g_re, 0.0), jnp.where(own, -g_im, 0.0)], axis=1)
            gen = jnp.dot(lhs, bst, preferred_element_type=F32)
            k = 0
            while (SSM_GROUP << k) < w:
                sh = SSM_GROUP << k
                shifted = jnp.concatenate([jnp.zeros((sh, w), F32), gen[:w - sh]], axis=0)
                gen = jnp.where(((sblk >> k) & 1) == 1, shifted, gen)
                k += 1
            mt_ref[2 * j + r] = gen.astype(BF16)
            cin_ref[2 * j + r] = jnp.concatenate(
                [jnp.where(own, ci_re, 0.0), jnp.where(own, -ci_im, 0.0)], axis=1).astype(BF16)
        a16r_ref[j] = q_re[SSM_T:SSM_T + 1]
        a16i_ref[j] = q_im[SSM_T:SSM_T + 1]


def _ssm_prep(lam_re, lam_im, log_dt, b_re, b_im, c_re, c_im):
    g, p = lam_re.shape
    gp = g // 2
    p2 = 2 * p
    w = SSM_T * SSM_GROUP
    pp = PREP_PAIRS
    assert gp % pp == 0
    colx = lambda a: jnp.repeat(a.reshape(gp, p2).T, SSM_T, axis=1)
    row = lambda a: a.reshape(gp, 1, p2)
    dt_full = jnp.broadcast_to(log_dt[:, None], (g, p))
    b_t = lambda a: jnp.tile(a.reshape(gp, p2, SSM_GROUP), (1, 1, SSM_T))
    c_pair = lambda a: a.reshape(gp, 2, SSM_GROUP, p).transpose(0, 2, 1, 3).reshape(gp, SSM_GROUP, p2)
    xspec = pl.BlockSpec((p2, LANES), lambda i: (0, i))
    rowspec = pl.BlockSpec((pp, 1, p2), lambda i: (i, 0, 0))
    bspec = pl.BlockSpec((pp, p2, w), lambda i: (i, 0, 0))
    cspec = pl.BlockSpec((pp, SSM_GROUP, p2), lambda i: (i, 0, 0))
    mt, bend, cin, a16r, a16i = pl.pallas_call(
        _ssm_prep_kernel,
        out_shape=[
            jax.ShapeDtypeStruct((g, w, w), BF16),
            jax.ShapeDtypeStruct((g, p2, w), BF16),
            jax.ShapeDtypeStruct((g, w, 2 * p2), BF16),
            jax.ShapeDtypeStruct((gp, 1, p2), F32),
            jax.ShapeDtypeStruct((gp, 1, p2), F32),
        ],
        grid=(gp // pp,),
        in_specs=[xspec, xspec, xspec, bspec, bspec, rowspec, rowspec, rowspec, cspec, cspec],
        out_specs=[
            pl.BlockSpec((2 * pp, w, w), lambda i: (i, 0, 0)),
            pl.BlockSpec((2 * pp, p2, w), lambda i: (i, 0, 0)),
            pl.BlockSpec((2 * pp, w, 2 * p2), lambda i: (i, 0, 0)),
            rowspec, rowspec,
        ],
        compiler_params=_params("arbitrary"),
        name="ssm_prep",
    )(colx(lam_re), colx(lam_im), colx(dt_full), b_t(b_re), b_t(b_im),
      row(lam_re), row(lam_im), row(dt_full), c_pair(c_re), c_pair(c_im))
    return mt, bend, cin, a16r.reshape(1, g * p), a16i.reshape(1, g * p)


def _step_rows(s):
    return pl.ds(s, CHUNK_TILE, stride=SSM_T)


def _build_ut(u_ref, ut_ref):
    for s in range(SSM_T):
        xt = u_ref[_step_rows(s), :].T.astype(BF16)
        for gl in range(GROUPS_PER_TILE):
            ut_ref[gl, s * SSM_GROUP:(s + 1) * SSM_GROUP, :] = xt[gl * SSM_GROUP:(gl + 1) * SSM_GROUP, :]


STATE_TILES = 2


def _ssm_state_kernel(*refs):
    u_refs = refs[:STATE_TILES]
    bend_ref, sre_ref, sim_ref = refs[STATE_TILES:STATE_TILES + 3]
    ut_refs = refs[STATE_TILES + 3:]
    p = SSM_STATE
    for tile in range(STATE_TILES):
        _build_ut(u_refs[tile], ut_refs[tile])
        for pr in range(GROUPS_PER_TILE // 2):
            st = [jnp.dot(bend_ref[tile * GROUPS_PER_TILE + 2 * pr + r], ut_refs[tile][2 * pr + r],
                          preferred_element_type=F32)
                  for r in range(2)]
            cols = slice((tile * GROUPS_PER_TILE // 2 + pr) * 2 * p,
                         (tile * GROUPS_PER_TILE // 2 + pr + 1) * 2 * p)
            sre_ref[:, cols] = jnp.concatenate([st[0][:p], st[1][:p]], axis=0).T
            sim_ref[:, cols] = jnp.concatenate([st[0][p:], st[1][p:]], axis=0).T


def _ssm_state_contrib(uz, bend, jpad):
    g, p2, w = bend.shape
    gstep = STATE_TILES * GROUPS_PER_TILE
    assert g % gstep == 0
    sw = gstep * SSM_STATE
    out = jax.ShapeDtypeStruct((jpad, g * SSM_STATE), F32)
    ospec = pl.BlockSpec((CHUNK_TILE, sw), lambda j, q: (j, q))
    utile = lambda t: pl.BlockSpec((CHUNK_TILE * SSM_T, LANES), lambda j, q: (j, STATE_TILES * q + t))
    return pl.pallas_call(
        _ssm_state_kernel,
        out_shape=[out, out],
        grid=(jpad // CHUNK_TILE, g // gstep),
        in_specs=[*[utile(t) for t in range(STATE_TILES)],
                  pl.BlockSpec((gstep, p2, w), lambda j, q: (q, 0, 0))],
        out_specs=[ospec, ospec],
        scratch_shapes=[pltpu.VMEM((GROUPS_PER_TILE, w, CHUNK_TILE), BF16) for _ in range(STATE_TILES)],
        compiler_params=_params("arbitrary", "arbitrary"),
        name="ssm_state_contrib",
    )(*[uz] * STATE_TILES, bend)


def _ssm_scan_kernel(sre_ref, sim_ref, ar_ref, ai_ref, h0r_ref, h0i_ref,
                     hpr_ref, hpi_ref, pr_ref, pi_ref, sr_ref, si_ref, hr_sc, hi_sc, *, n_prompt):
    i = pl.program_id(0)
    rb = sre_ref.shape[0]
    ar, ai = ar_ref[...], ai_ref[...]

    @pl.when(i == 0)
    def _():
        hr_sc[...] = jnp.zeros_like(hr_sc)
        hi_sc[...] = jnp.zeros_like(hi_sc)

    @pl.when(i < n_prompt)
    def _():
        def body(j, c):
            row = pl.ds(j, 1)
            hr, hi = hr_sc[...], hi_sc[...]
            hpr_ref[row, :] = hr
            hpi_ref[row, :] = hi
            hr_sc[...] = ar * hr - ai * hi + sre_ref[row, :]
            hi_sc[...] = ar * hi + ai * hr + sim_ref[row, :]
            return c

        lax.fori_loop(0, rb, body, 0)

    @pl.when(i == n_prompt - 1)
    def _():
        pr_ref[...] = hr_sc[...]
        pi_ref[...] = hi_sc[...]

    @pl.when(i == n_prompt)
    def _():
        h0r, h0i = h0r_ref[...], h0i_ref[...]
        hpr_ref[...] = h0r
        hpi_ref[...] = h0i
        sr_ref[...] = ar * h0r - ai * h0i + sre_ref[...]
        si_ref[...] = ar * h0i + ai * h0r + sim_ref[...]

    @pl.when(i > n_prompt)
    def _():
        hpr_ref[...] = jnp.zeros_like(hpr_ref)
        hpi_ref[...] = jnp.zeros_like(hpi_ref)


def _ssm_scan(s_re, s_im, a16r, a16i, h0r, h0i, jp):
    jpad, n = s_re.shape
    rb = h0r.shape[0]
    assert jp % rb == 0 and jpad % rb == 0
    n_prompt = jp // rb
    rows = pl.BlockSpec((rb, n), lambda i: (i, 0))
    const1 = pl.BlockSpec((1, n), lambda i: (0, 0))
    constb = pl.BlockSpec((rb, n), lambda i: (0, 0))
    big = jax.ShapeDtypeStruct((jpad, n), F32)
    one = jax.ShapeDtypeStruct((1, n), F32)
    bat = jax.ShapeDtypeStruct((rb, n), F32)
    return pl.pallas_call(
        functools.partial(_ssm_scan_kernel, n_prompt=n_prompt),
        out_shape=[big, big, one, one, bat, bat],
        grid=(jpad // rb,),
        in_specs=[rows, rows, const1, const1, constb, constb],
        out_specs=[rows, rows, const1, const1, constb, constb],
        scratch_shapes=[pltpu.VMEM((1, n), F32), pltpu.VMEM((1, n), F32)],
        compiler_params=_params("arbitrary"),
        name="ssm_scan",
    )(s_re, s_im, a16r, a16i, h0r, h0i)


def _ssm_out_kernel(u_ref, mt_ref, cin_ref, hpr_ref, hpi_ref, d_ref, o_ref, ut_ref, uf_ref, yt_ref):
    for s in range(SSM_T):
        xt = u_ref[_step_rows(s), :].T
        xb = xt.astype(BF16)
        for gl in range(GROUPS_PER_TILE):
            rows = slice(gl * SSM_GROUP, (gl + 1) * SSM_GROUP)
            ut_ref[gl, s * SSM_GROUP:(s + 1) * SSM_GROUP, :] = xb[rows, :]
            uf_ref[gl, s * SSM_GROUP:(s + 1) * SSM_GROUP, :] = xt[rows, :]
    p2 = 2 * SSM_STATE
    for gl in range(GROUPS_PER_TILE):
        pr = gl // 2
        hp = jnp.concatenate([hpr_ref[:, pr * p2:(pr + 1) * p2], hpi_ref[:, pr * p2:(pr + 1) * p2]],
                             axis=1).astype(BF16)
        yt = jnp.dot(mt_ref[gl], ut_ref[gl], preferred_element_type=F32)
        yt = yt + lax.dot_general(cin_ref[gl], hp, _NT, preferred_element_type=F32)
        yt = jax.nn.gelu(yt + d_ref[gl] * uf_ref[gl])
        for t in range(SSM_T):
            yt_ref[t, gl * SSM_GROUP:(gl + 1) * SSM_GROUP, :] = yt[t * SSM_GROUP:(t + 1) * SSM_GROUP, :]
    for t in range(SSM_T):
        o_ref[_step_rows(t), :] = yt_ref[t].T


def _ssm_output(uz, mt, cin, hp_re, hp_im, d_skip, jpad):
    g, w, _ = mt.shape
    w2 = cin.shape[-1]
    ntile = g // GROUPS_PER_TILE
    sw = GROUPS_PER_TILE * SSM_STATE
    tok = pl.BlockSpec((CHUNK_TILE * SSM_T, LANES), lambda j, q: (j, q))
    hspec = pl.BlockSpec((CHUNK_TILE, sw), lambda j, q: (j, q))
    return pl.pallas_call(
        _ssm_out_kernel,
        out_shape=jax.ShapeDtypeStruct((jpad * SSM_T, g * SSM_GROUP), F32),
        grid=(jpad // CHUNK_TILE, ntile),
        in_specs=[
            tok,
            pl.BlockSpec((GROUPS_PER_TILE, w, w), lambda j, q: (q, 0, 0)),
            pl.BlockSpec((GROUPS_PER_TILE, w, w2), lambda j, q: (q, 0, 0)),
            hspec, hspec,
            pl.BlockSpec((GROUPS_PER_TILE, w, 1), lambda j, q: (q, 0, 0)),
        ],
        out_specs=tok,
        scratch_shapes=[
            pltpu.VMEM((GROUPS_PER_TILE, w, CHUNK_TILE), BF16),
            pltpu.VMEM((GROUPS_PER_TILE, w, CHUNK_TILE), F32),
            pltpu.VMEM((SSM_T, LANES, CHUNK_TILE), F32),
        ],
        compiler_params=_params("arbitrary", "arbitrary"),
        name="ssm_output",
    )(uz, mt, cin, hp_re, hp_im, d_skip)


def _a_glu_kernel(y_ref, z_ref, w_ref, o_ref, wb_ref):
    r = pl.program_id(0)

    @pl.when(r == 0)
    def _():
        wb_ref[...] = w_ref[...].astype(BF16)

    @pl.when(r > 0)
    def _():
        y = y_ref[...]
        gate = jnp.dot(y.astype(BF16), wb_ref[...], preferred_element_type=F32)
        y2 = y * jax.nn.sigmoid(gate)
        o_ref[...] = (y2 * jax.nn.silu(z_ref[...])).astype(BF16)


def _a_glu(yg, uz, w_glu, rows):
    tm = ROW_TILE
    e = yg.shape[1]
    return pl.pallas_call(
        _a_glu_kernel,
        out_shape=jax.ShapeDtypeStruct((rows, e), BF16),
        grid=(1 + rows // tm,),
        in_specs=[
            pl.BlockSpec((tm, e), lambda r: (_row_block(r), 0)),
            pl.BlockSpec((tm, e), lambda r: (_row_block(r), 1)),
            _resident((e, e), lambda r: (0, 0)),
        ],
        out_specs=pl.BlockSpec((tm, e), lambda r: (_row_block(r), 0)),
        scratch_shapes=[pltpu.VMEM((e, e), BF16)],
        compiler_params=_params("arbitrary"),
        name="a_glu",
    )(yg, uz, w_glu)


def _a_out_kernel(o_ref, w_ref, xp_ref, xs_ref, x1_ref, wb_ref, *, mp):
    m = pl.program_id(0) - 1

    @pl.when(m < 0)
    def _():
        wb_ref[...] = w_ref[...].astype(BF16)

    @pl.when((m >= 0) & (m < mp))
    def _():
        x1_ref[...] = xp_ref[...] + jnp.dot(o_ref[...], wb_ref[...], preferred_element_type=F32)

    @pl.when(m == mp)
    def _():
        x1_ref[...] = xs_ref[...] + jnp.dot(o_ref[...], wb_ref[...], preferred_element_type=F32)


def _a_out_proj(o, w_out, xp, xs):
    tm = ROW_TILE
    rows, e = o.shape
    d = w_out.shape[1]
    mp = xp.shape[0] // tm
    return pl.pallas_call(
        functools.partial(_a_out_kernel, mp=mp),
        out_shape=jax.ShapeDtypeStruct((rows, d), F32),
        grid=(1 + rows // tm,),
        in_specs=[
            pl.BlockSpec((tm, e), lambda r: (_row_block(r), 0)),
            _resident((e, d), lambda r: (0, 0)),
            pl.BlockSpec((tm, d), lambda r: (jnp.minimum(_row_block(r), mp - 1), 0)),
            _resident((tm, d), lambda r: (0, 0)),
        ],
        out_specs=pl.BlockSpec((tm, d), lambda r: (_row_block(r), 0)),
        scratch_shapes=[pltpu.VMEM((e, d), BF16)],
        compiler_params=_params("arbitrary"),
        name="a_out_proj",
    )(o, w_out, xp, xs)


def _rope(x, cos, sin):
    outs = []
    for c in range(x.shape[1] // HEAD_DIM):
        xc = x[:, c * HEAD_DIM:(c + 1) * HEAD_DIM]
        outs.append(xc * cos + pltpu.roll(xc, HEAD_DIM // 2, axis=1) * sin)
    return jnp.concatenate(outs, axis=1) if len(outs) > 1 else outs[0]


def _store_heads(o_ref, val, tm):
    for h in range(N_HEADS):
        o_ref[pl.ds(h, tm, stride=N_HEADS), :] = val[:, h * HEAD_DIM:(h + 1) * HEAD_DIM]


def _prompt_col(r, mp, col, last):
    return jnp.where(_row_block(r) >= mp, last, col)


def _sample_col(r, mp, col):
    return jnp.where(_row_block(r) == mp, col, 0)


def _kv_proj_kernel(x_ref, g_ref, w_ref, cos_ref, sin_ref, kp_ref, vp_ref, cbp_ref,
                    ks_ref, vs_ref, cbs_ref, xn_ref, wb_ref, *, tm, mp):
    m, n = pl.program_id(0) - 1, pl.program_id(1)
    hpb = w_ref.shape[1] // (2 * HEAD_DIM)

    for nn in range(4):
        @pl.when((m < 0) & (n == nn))
        def _():
            for hl in range(hpb):
                h = hpb * (nn % 2) + hl
                for half in range(2):
                    src = (2 * hl + half) * HEAD_DIM
                    wb_ref[2 * (nn // 2) + half, :, h * HEAD_DIM:(h + 1) * HEAD_DIM] = (
                        w_ref[:, src:src + HEAD_DIM].astype(BF16))

    @pl.when((m >= 0) & (n == 0))
    def _():
        xn_ref[...] = _rms_scale(x_ref[...], g_ref[...]).astype(BF16)

    def column(c, k_ref, v_ref, cb_ref):
        acc = jnp.dot(xn_ref[...], wb_ref[c], preferred_element_type=F32)
        if c < 2:
            acc = _rope(acc, cos_ref[...], sin_ref[...])
        _store_heads(k_ref if c < 2 else v_ref, acc, tm)
        cb_ref[...] = acc.astype(BF16)

    for c in range(4):
        @pl.when((m >= 0) & (m < mp) & (n == c))
        def _():
            column(c, kp_ref, vp_ref, cbp_ref)

        @pl.when((m == mp) & (n == c))
        def _():
            column(c, ks_ref, vs_ref, cbs_ref)


def _kv_proj(x1, rows_p, g, w_kv, cos, sin):
    tm = ROW_TILE
    rows, d = x1.shape
    tn = N_HEADS * HEAD_DIM
    mp = rows_p // tm
    assert w_kv.shape[1] == 4 * tn and rows == rows_p + tm and rows_p % tm == 0
    out4 = lambda nrows: jax.ShapeDtypeStruct((nrows * N_HEADS, 2 * HEAD_DIM), F32)
    outb = lambda nrows: jax.ShapeDtypeStruct((nrows, 4 * tn), BF16)
    prow = lambda r: jnp.minimum(_row_block(r), mp - 1)
    col = _col_block
    return pl.pallas_call(
        functools.partial(_kv_proj_kernel, tm=tm, mp=mp),
        out_shape=[out4(rows_p), out4(rows_p), outb(rows_p), out4(tm), out4(tm), outb(tm)],
        grid=(1 + rows // tm, 4),
        in_specs=[
            pl.BlockSpec((tm, d), lambda r, n: (_row_block(r), 0)),
            _resident((1, d), lambda r, n: (0, 0)),
            _staged_weight_spec(d, tn, 4),
            pl.BlockSpec((tm, HEAD_DIM), lambda r, n: (_row_block(r), 0)),
            pl.BlockSpec((tm, HEAD_DIM), lambda r, n: (_row_block(r), 0)),
        ],
        out_specs=[
            pl.BlockSpec((tm * N_HEADS, HEAD_DIM),
                         lambda r, n: (prow(r), _prompt_col(r, mp, jnp.minimum(col(r, n), 1), 1))),
            pl.BlockSpec((tm * N_HEADS, HEAD_DIM),
                         lambda r, n: (prow(r), _prompt_col(r, mp, jnp.maximum(col(r, n) - 2, 0), 1))),
            pl.BlockSpec((tm, tn), lambda r, n: (prow(r), _prompt_col(r, mp, col(r, n), 3))),
            pl.BlockSpec((tm * N_HEADS, HEAD_DIM),
                         lambda r, n: (0, _sample_col(r, mp, jnp.minimum(col(r, n), 1)))),
            pl.BlockSpec((tm * N_HEADS, HEAD_DIM),
                         lambda r, n: (0, _sample_col(r, mp, jnp.maximum(col(r, n) - 2, 0)))),
            pl.BlockSpec((tm, tn), lambda r, n: (0, _sample_col(r, mp, col(r, n)))),
        ],
        scratch_shapes=[pltpu.VMEM((tm, d), BF16), pltpu.VMEM((4, d, tn), BF16)],
        compiler_params=_params("arbitrary", "arbitrary", vmem_limit=KV_PROJ_VMEM_LIMIT),
        name="kv_proj",
    )(x1, g, w_kv, cos, sin)


def _qz_proj_kernel(x_ref, g_ref, w_ref, cos_ref, sin_ref, qp_ref, zp_ref, qs_ref, zs_ref,
                    xn_ref, wb_ref, *, nq, mp):
    m, n = pl.program_id(0) - 1, pl.program_id(1)

    @pl.when(m < 0)
    def _():
        wb_ref[n] = w_ref[...].astype(BF16)

    @pl.when((m >= 0) & (n == 0))
    def _():
        xn_ref[...] = _rms_scale(x_ref[...], g_ref[...]).astype(BF16)

    def column(c, q_ref, z_ref):
        acc = jnp.dot(xn_ref[...], wb_ref[c], preferred_element_type=F32)
        if c < nq:
            q_ref[...] = (_rope(acc, cos_ref[...], sin_ref[...]) * Q_SCALE).astype(BF16)
        else:
            z_ref[...] = acc

    for c in range(2 * nq):
        @pl.when((m >= 0) & (m < mp) & (n == c))
        def _():
            column(c, qp_ref, zp_ref)

        @pl.when((m == mp) & (n == c))
        def _():
            column(c, qs_ref, zs_ref)


def _qz_proj(x1, rows_p, g, w, cos, sin, tn):
    tm = ROW_TILE
    rows, d = x1.shape
    half = w.shape[1] // 2
    nq = half // tn
    mp = rows_p // tm
    assert rows == rows_p + tm and rows_p % tm == 0
    prow = lambda r: jnp.minimum(_row_block(r), mp - 1)
    qcol = lambda r, n: jnp.minimum(_col_block(r, n), nq - 1)
    zcol = lambda r, n: jnp.maximum(_col_block(r, n) - nq, 0)
    return pl.pallas_call(
        functools.partial(_qz_proj_kernel, nq=nq, mp=mp),
        out_shape=[jax.ShapeDtypeStruct((rows_p, half), BF16), jax.ShapeDtypeStruct((rows_p, half), F32),
                   jax.ShapeDtypeStruct((tm, half), BF16), jax.ShapeDtypeStruct((tm, half), F32)],
        grid=(1 + rows // tm, 2 * nq),
        in_specs=[
            pl.BlockSpec((tm, d), lambda r, n: (_row_block(r), 0)),
            _resident((1, d), lambda r, n: (0, 0)),
            _staged_weight_spec(d, tn, 2 * nq),
            pl.BlockSpec((tm, HEAD_DIM), lambda r, n: (_row_block(r), 0)),
            pl.BlockSpec((tm, HEAD_DIM), lambda r, n: (_row_block(r), 0)),
        ],
        out_specs=[
            pl.BlockSpec((tm, tn), lambda r, n: (prow(r), _prompt_col(r, mp, qcol(r, n), nq - 1))),
            pl.BlockSpec((tm, tn), lambda r, n: (prow(r), _prompt_col(r, mp, zcol(r, n), nq - 1))),
            pl.BlockSpec((tm, tn), lambda r, n: (0, _sample_col(r, mp, qcol(r, n)))),
            pl.BlockSpec((tm, tn), lambda r, n: (0, _sample_col(r, mp, zcol(r, n)))),
        ],
        scratch_shapes=[pltpu.VMEM((tm, d), BF16), pltpu.VMEM((2 * nq, d, tn), BF16)],
        compiler_params=_params("arbitrary", "arbitrary"),
        name="qz_proj",
    )(x1, g, w, cos, sin)


def _rope_tables(pos):
    inv_freq = ROPE_THETA ** (-jnp.arange(0, HEAD_DIM, 2, dtype=F32) / HEAD_DIM)
    ang = pos.astype(F32)[:, None] * inv_freq[None, :]
    c, s = jnp.cos(ang), jnp.sin(ang)
    return jnp.concatenate([c, c], axis=-1), jnp.concatenate([-s, s], axis=-1)


def _diff_lambda(lq1_ref, lk1_ref, lq2_ref, lk2_ref, lambda_init):
    s1 = jnp.sum(lq1_ref[...] * lk1_ref[...], axis=-1, keepdims=True)
    s2 = jnp.sum(lq2_ref[...] * lk2_ref[...], axis=-1, keepdims=True)
    return jnp.exp(s1) - jnp.exp(s2) + lambda_init


def _attn_finish(a1, l1, a2, l2, lam, subln, z, lambda_init):
    o = a1 / l1 - lam * (a2 / l2)
    o = _rms_scale(o, subln) * (1.0 - lambda_init)
    return (o * jax.nn.silu(z)).astype(BF16)


def _attn_prompt_kernel(q_ref, k1_ref, k2_ref, v1_ref, v2_ref, z_ref,
                        lq1_ref, lk1_ref, lq2_ref, lk2_ref, sub_ref,
                        o_ref, m_sc, l_sc, al_sc, a_sc, *sp_scs, tq, ts, tk, lambda_init):
    qi = pl.program_id(1)
    k_refs = (k1_ref, k2_ref)
    nsub = tq // ts
    s_scs, p_scs = sp_scs[:2 * nsub], sp_scs[2 * nsub:]
    m_sc[...] = jnp.full(m_sc.shape, NEG, F32)
    l_sc[...] = jnp.zeros(l_sc.shape, F32)
    a_sc[...] = jnp.zeros(a_sc.shape, F32)

    def softmax_strips(sub, n, col0):
        for r in range(ts // STRIP):
            row0 = sub * ts + r * STRIP
            lr = slice(r * STRIP, (r + 1) * STRIP)
            gr = slice(row0, row0 + STRIP)
            visible = tk if col0 is None else min(tk, (row0 // CHUNK + 1) * CHUNK - col0)
            assert visible > 0
            ncv = -(-visible // LANES)
            s = s_scs[2 * sub + n][lr, 0:ncv * LANES]
            if visible < ncv * LANES:
                col = lax.broadcasted_iota(jnp.int32, s.shape, 1)
                s = jnp.where(col < visible, s, NEG)
            cols = [s[:, c * LANES:(c + 1) * LANES] for c in range(ncv)]
            m_cur = jnp.max(functools.reduce(jnp.maximum, cols), axis=-1, keepdims=True)
            m_old = m_sc[n, gr, :]
            m_new = jnp.maximum(m_old, m_cur)
            ps = [jnp.exp2(c - m_new) for c in cols]
            l_cur = jnp.sum(functools.reduce(jnp.add, ps), axis=-1, keepdims=True)
            alpha = jnp.exp2(m_old - m_new)
            l_sc[n, gr, :] = alpha * l_sc[n, gr, :] + l_cur
            m_sc[n, gr, :] = m_new
            al_sc[n, gr, :] = alpha
            for c in range(keys_seen(sub, col0) // LANES):
                pc = ps[c].astype(BF16) if c < ncv else jnp.zeros((STRIP, LANES), BF16)
                p_scs[2 * sub + n][lr, c * LANES:(c + 1) * LANES] = pc

    def keys_seen(sub, col0):
        return tk if col0 is None else min(tk, (sub + 1) * ts - col0)

    def block(kb, col0, subs):
        start = pl.multiple_of(kb * tk, tk)
        for sub in subs:
            rows = pl.ds(start, keys_seen(sub, col0))
            for n in range(2):
                qn = q_ref[sub * ts:(sub + 1) * ts, n * HEAD_DIM:(n + 1) * HEAD_DIM]
                s_scs[2 * sub + n][:, 0:rows.size] = lax.dot_general(
                    qn, k_refs[n][rows, :], _NT, preferred_element_type=F32)
        for sub in subs:
            gr = slice(sub * ts, (sub + 1) * ts)
            rows = pl.ds(start, keys_seen(sub, col0))
            vblk = jnp.concatenate([v1_ref[rows, :], v2_ref[rows, :]], axis=1)
            for n in range(2):
                softmax_strips(sub, n, col0)
                pv = jnp.dot(p_scs[2 * sub + n][:, 0:rows.size], vblk, preferred_element_type=F32)
                alpha = al_sc[n, gr, :]
                a_sc[n, gr, :] = a_sc[n, gr, :] * jnp.concatenate([alpha, alpha], axis=1) + pv

    nd = tq // tk

    def body(kb, c):
        block(kb, None, range(nsub))
        return c

    lax.fori_loop(0, qi * nd, body, 0)
    for d in range(nd):
        block(qi * nd + d, d * tk, [sub for sub in range(nsub) if (sub + 1) * ts > d * tk])

    lam = _diff_lambda(lq1_ref, lk1_ref, lq2_ref, lk2_ref, lambda_init)
    l1 = jnp.concatenate([l_sc[0], l_sc[0]], axis=1)
    l2 = jnp.concatenate([l_sc[1], l_sc[1]], axis=1)
    o_ref[...] = _attn_finish(a_sc[0], l1, a_sc[1], l2, lam, sub_ref[...], z_ref[...], lambda_init)


def _attn_prompt(qb, z32, kvb, lq1, lk1, lq2, lk2, subln, lambda_init, tq, ts, tk):
    rows = qb.shape[0]
    hw = 2 * HEAD_DIM
    nsub = tq // ts
    assert tq % tk == 0 and tk % ts == 0 and ts % STRIP == 0 and CHUNK % STRIP == 0
    vec = pl.BlockSpec((1, HEAD_DIM), lambda h, i: (0, 0))
    kcol = lambda c: pl.BlockSpec((rows, HEAD_DIM), lambda h, i: (0, c * N_HEADS + h))
    stat = pltpu.VMEM((2, tq, LANES), F32)
    return pl.pallas_call(
        functools.partial(_attn_prompt_kernel, tq=tq, ts=ts, tk=tk, lambda_init=lambda_init),
        out_shape=jax.ShapeDtypeStruct((rows, N_HEADS * hw), BF16),
        grid=(N_HEADS, rows // tq),
        in_specs=[
            pl.BlockSpec((tq, hw), lambda h, i: (i, h)),
            kcol(0), kcol(1), kcol(2), kcol(3),
            pl.BlockSpec((tq, hw), lambda h, i: (i, h)),
            vec, vec, vec, vec,
            pl.BlockSpec((1, hw), lambda h, i: (0, 0)),
        ],
        out_specs=pl.BlockSpec((tq, hw), lambda h, i: (i, h)),
        scratch_shapes=[
            stat, stat, stat,
            pltpu.VMEM((2, tq, hw), F32),
            *[pltpu.VMEM((ts, tk), F32) for _ in range(2 * nsub)],
            *[pltpu.VMEM((ts, tk), BF16) for _ in range(2 * nsub)],
        ],
        compiler_params=_params("arbitrary", "arbitrary"),
        name="attn_prompt",
    )(qb, kvb, kvb, kvb, kvb, z32, lq1, lk1, lq2, lk2, subln)


def _attn_sample_kernel(q_ref, ck1_ref, ck2_ref, cv1_ref, cv2_ref, kvn_ref, z_ref,
                        lq1_ref, lk1_ref, lq2_ref, lk2_ref, sub_ref,
                        o_ref, m_sc, l_sc, a_sc, s_sc, p_sc, *, tkv, nkb, past_len, lambda_init):
    kb = pl.program_id(1)
    t = q_ref.shape[0]
    hw = 2 * HEAD_DIM
    rows_all = 2 * N_HEADS * t
    strip = 4 * t
    ck_refs = (ck1_ref, ck2_ref)
    assert (past_len - 1) // CHUNK <= past_len // CHUNK

    @pl.when(kb == 0)
    def _():
        m_sc[...] = jnp.full(m_sc.shape, NEG, F32)
        l_sc[...] = jnp.zeros(l_sc.shape, F32)
        a_sc[...] = jnp.zeros(a_sc.shape, F32)

    col_blk = lambda c, h: slice((c * N_HEADS + h) * HEAD_DIM, (c * N_HEADS + h + 1) * HEAD_DIM)
    head_rows = lambda h: pl.ds(h, tkv, stride=N_HEADS)

    def update(with_new):
        ncols = tkv + (LANES if with_new else 0)
        if with_new:
            row = lax.broadcasted_iota(jnp.int32, (t, LANES), 0) + past_len
            col = lax.broadcasted_iota(jnp.int32, (t, LANES), 1)
            new_mask = (col < t) & (((col + past_len) // CHUNK) <= (row // CHUNK))
            pad_k = jnp.zeros((LANES - t, HEAD_DIM), BF16)
        for h in range(N_HEADS):
            for n in range(2):
                qn = q_ref[:, (2 * h + n) * HEAD_DIM:(2 * h + n + 1) * HEAD_DIM]
                rr = slice((2 * h + n) * t, (2 * h + n + 1) * t)
                k = ck_refs[n][head_rows(h), :].astype(BF16)
                s_sc[rr, 0:tkv] = lax.dot_general(qn, k, _NT, preferred_element_type=F32)
                if with_new:
                    kn = jnp.concatenate([kvn_ref[:, col_blk(n, h)], pad_k], axis=0)
                    sn = lax.dot_general(qn, kn, _NT, preferred_element_type=F32)
                    s_sc[rr, tkv:ncols] = jnp.where(new_mask, sn, NEG)
        for r in range(rows_all // strip):
            rr = slice(r * strip, (r + 1) * strip)
            cols = [s_sc[rr, c * LANES:(c + 1) * LANES] for c in range(ncols // LANES)]
            m_cur = jnp.max(functools.reduce(jnp.maximum, cols), axis=-1, keepdims=True)
            m_old = m_sc[rr, :]
            m_new = jnp.maximum(m_old, m_cur)
            ps = [jnp.exp2(c - m_new) for c in cols]
            l_cur = jnp.sum(functools.reduce(jnp.add, ps), axis=-1, keepdims=True)
            alpha = jnp.exp2(m_old - m_new)
            l_sc[rr, :] = alpha * l_sc[rr, :] + l_cur
            m_sc[rr, :] = m_new
            a_sc[rr, :] = a_sc[rr, :] * jnp.concatenate([alpha, alpha], axis=1)
            for c, pc in enumerate(ps):
                p_sc[rr, c * LANES:(c + 1) * LANES] = pc.astype(BF16)
        for h in range(N_HEADS):
            rr = slice(2 * h * t, (2 * h + 2) * t)
            v = jnp.concatenate([cv1_ref[head_rows(h), :], cv2_ref[head_rows(h), :]],
                                axis=1).astype(BF16)
            pv = jnp.dot(p_sc[rr, 0:tkv], v, preferred_element_type=F32)
            if with_new:
                vn = jnp.concatenate([kvn_ref[:, col_blk(2, h)], kvn_ref[:, col_blk(3, h)]], axis=1)
                vn = jnp.concatenate([vn, jnp.zeros((LANES - t, hw), BF16)], axis=0)
                pv = pv + jnp.dot(p_sc[rr, tkv:ncols], vn, preferred_element_type=F32)
            a_sc[rr, :] = a_sc[rr, :] + pv

    @pl.when(kb < nkb - 1)
    def _():
        update(False)

    @pl.when(kb == nkb - 1)
    def _():
        update(True)
        lam = _diff_lambda(lq1_ref, lk1_ref, lq2_ref, lk2_ref, lambda_init)
        sub = sub_ref[...]
        for h in range(N_HEADS):
            r1 = slice(2 * h * t, (2 * h + 1) * t)
            r2 = slice((2 * h + 1) * t, (2 * h + 2) * t)
            l1 = jnp.concatenate([l_sc[r1, :], l_sc[r1, :]], axis=1)
            l2 = jnp.concatenate([l_sc[r2, :], l_sc[r2, :]], axis=1)
            o_ref[:, h * hw:(h + 1) * hw] = _attn_finish(
                a_sc[r1, :], l1, a_sc[r2, :], l2, lam, sub, z_ref[:, h * hw:(h + 1) * hw], lambda_init)


def _attn_sample(qb, z32, kvb, cache_k2, cache_v2, lq1, lk1, lq2, lk2, subln, lambda_init, t, tkv):
    bsz, rows8, hw = cache_k2.shape
    past_len = rows8 // N_HEADS
    nkb = past_len // tkv
    rows_all = 2 * N_HEADS * t
    assert t <= LANES
    vec = pl.BlockSpec((1, HEAD_DIM), lambda b, k: (0, 0))
    cache = lambda c: pl.BlockSpec((None, tkv * N_HEADS, HEAD_DIM), lambda b, k: (b, k, c))
    full = lambda a: pl.BlockSpec((t, a.shape[1]), lambda b, k: (b, 0))
    return pl.pallas_call(
        functools.partial(_attn_sample_kernel, tkv=tkv, nkb=nkb, past_len=past_len,
                          lambda_init=lambda_init),
        out_shape=jax.ShapeDtypeStruct((bsz * t, N_HEADS * hw), BF16),
        grid=(bsz, nkb),
        in_specs=[
            full(qb), cache(0), cache(1), cache(0), cache(1), full(kvb), full(z32),
            vec, vec, vec, vec,
            pl.BlockSpec((1, hw), lambda b, k: (0, 0)),
        ],
        out_specs=pl.BlockSpec((t, N_HEADS * hw), lambda b, k: (b, 0)),
        scratch_shapes=[
            pltpu.VMEM((rows_all, LANES), F32),
            pltpu.VMEM((rows_all, LANES), F32),
            pltpu.VMEM((rows_all, hw), F32),
            pltpu.VMEM((rows_all, tkv + LANES), F32),
            pltpu.VMEM((rows_all, tkv + LANES), BF16),
        ],
        compiler_params=_params("arbitrary", "arbitrary"),
        name="attn_sample",
    )(qb, cache_k2, cache_k2, cache_v2, cache_v2, kvb, z32, lq1, lk1, lq2, lk2, subln)


def _b_out_kernel(op_ref, os_ref, w_ref, x_ref, g_ref, yp_ref, ys_ref, wb_ref, *, mp):
    m = pl.program_id(0) - 1

    @pl.when(m < 0)
    def _():
        wb_ref[...] = w_ref[...].astype(BF16)

    def finish(o_ref, y_ref):
        acc = jnp.dot(o_ref[...], wb_ref[...], preferred_element_type=F32)
        y_ref[...] = _rms_scale(x_ref[...] + acc, g_ref[...])

    @pl.when((m >= 0) & (m < mp))
    def _():
        finish(op_ref, yp_ref)

    @pl.when(m == mp)
    def _():
        finish(os_ref, ys_ref)


def _b_out_proj(og_p, og_s, w_out, x1, g):
    tm = ROW_TILE
    rows_p, e = og_p.shape
    rows, d = x1.shape
    mp = rows_p // tm
    assert og_s.shape[0] == tm and rows == rows_p + tm and rows_p % tm == 0
    prow = lambda r: jnp.minimum(_row_block(r), mp - 1)
    return pl.pallas_call(
        functools.partial(_b_out_kernel, mp=mp),
        out_shape=[jax.ShapeDtypeStruct((rows_p, d), F32), jax.ShapeDtypeStruct((tm, d), F32)],
        grid=(1 + rows // tm,),
        in_specs=[
            pl.BlockSpec((tm, e), lambda r: (prow(r), 0)),
            _resident((tm, e), lambda r: (0, 0)),
            _resident((e, d), lambda r: (0, 0)),
            pl.BlockSpec((tm, d), lambda r: (_row_block(r), 0)),
            _resident((1, d), lambda r: (0, 0)),
        ],
        out_specs=[pl.BlockSpec((tm, d), lambda r: (prow(r), 0)),
                   pl.BlockSpec((tm, d), lambda r: (0, 0))],
        scratch_shapes=[pltpu.VMEM((e, d), BF16)],
        compiler_params=_params("arbitrary"),
        name="b_out_proj",
    )(og_p, og_s, w_out, x1, g)


def kernel(x_prompt, x_sample, state_ssm_re, state_ssm_im, cache_k, cache_v, a_norm, a_w_in, a_lambda_re, a_lambda_im, a_log_dt, a_b_re, a_b_im, a_c_re, a_c_im, a_d, a_w_glu, a_w_out, kv_norm, w_kv, b_norm, b_w_in, b_lambda_q1, b_lambda_k1, b_lambda_q2, b_lambda_k2, b_subln, b_w_out, final_norm):
    bp, seq, d = x_prompt.shape
    bs, t_s, _ = x_sample.shape
    past_len = cache_k.shape[1]
    n_a, n_b = a_norm.shape[0], b_norm.shape[0]
    g_cnt, p = a_lambda_re.shape[1:]
    assert bp == 1 and n_a == 1 and n_b == 1
    assert t_s == SSM_T and seq % (SSM_T * bs) == 0 and seq % CHUNK == 0
    assert cache_k.shape[2:] == (N_HEADS, 2 * HEAD_DIM)

    jp, js = seq // SSM_T, bs
    jpad = -(-(jp + js) // CHUNK_TILE) * CHUNK_TILE
    rows_s = bs * t_s
    rows_a = seq + rows_s

    xp = x_prompt.reshape(seq, d)
    xs = x_sample.reshape(rows_s, d)
    uz = _a_in_proj(xp, xs, a_norm[0].reshape(1, d), a_w_in[0], jpad * SSM_T, tn=1024)
    mt, bend, cin, a16r, a16i = _ssm_prep(a_lambda_re[0], a_lambda_im[0], a_log_dt[0],
                                          a_b_re[0], a_b_im[0], a_c_re[0], a_c_im[0])
    s_re, s_im = _ssm_state_contrib(uz, bend, jpad)
    h0r = state_ssm_re[:, 0].reshape(bs, g_cnt * p)
    h0i = state_ssm_im[:, 0].reshape(bs, g_cnt * p)
    hp_re, hp_im, pre, pim, sre, sim = _ssm_scan(s_re, s_im, a16r, a16i, h0r, h0i, jp)
    d_col = jnp.tile(a_d[0].reshape(g_cnt, 1, SSM_GROUP), (1, SSM_T, 1)).reshape(g_cnt, -1, 1)
    yg = _ssm_output(uz, mt, cin, hp_re, hp_im, d_col, jpad)
    o_a = _a_glu(yg, uz, a_w_glu[0], rows_a)
    x1 = _a_out_proj(o_a, a_w_out[0], xp, xs)

    lambda_init = 0.8 - 0.6 * math.exp(-0.3 * n_a)
    w_q = b_w_in[0]
    kv_g = kv_norm.reshape(1, d)
    b_g = b_norm[0].reshape(1, d)
    pos = jnp.concatenate([jnp.arange(seq, dtype=jnp.int32),
                           jnp.tile(past_len + jnp.arange(t_s, dtype=jnp.int32), bs)])
    cos, sin = _rope_tables(pos)
    k_p, v_p, kvb_p, k_s, v_s, kvb_s = _kv_proj(x1, seq, kv_g, w_kv, cos, sin)
    qb_p, z_p, qb_s, z_s = _qz_proj(x1, seq, b_g, w_q, cos, sin, tn=1024)

    vecs = [a[0].reshape(1, HEAD_DIM) for a in (b_lambda_q1, b_lambda_k1, b_lambda_q2, b_lambda_k2)]
    subln = b_subln[0].reshape(1, 2 * HEAD_DIM)
    hw = 2 * HEAD_DIM
    og_p = _attn_prompt(qb_p, z_p, kvb_p, *vecs, subln, lambda_init, tq=1024, ts=512, tk=1024)
    og_s = _attn_sample(qb_s, z_s, kvb_s, cache_k.reshape(bs, past_len * N_HEADS, hw),
                        cache_v.reshape(bs, past_len * N_HEADS, hw), *vecs, subln, lambda_init,
                        t_s, tkv=1024)
    w_o = b_w_out[0]
    fg = final_norm.reshape(1, d)
    y_p, y_s = _b_out_proj(og_p, og_s, w_o, x1, fg)

    return (y_p.reshape(bp, seq, d), y_s.reshape(bs, t_s, d),
            pre.reshape(bp, n_a, g_cnt, p), pim.reshape(bp, n_a, g_cnt, p),
            k_p.reshape(bp, seq, N_HEADS, hw), v_p.reshape(bp, seq, N_HEADS, hw),
            sre.reshape(bs, n_a, g_cnt, p), sim.reshape(bs, n_a, g_cnt, p),
            k_s.reshape(bs, t_s, N_HEADS, hw), v_s.reshape(bs, t_s, N_HEADS, hw))
```

```python
import functools
import math

import jax
import jax.numpy as jnp
from jax import lax
from jax.experimental import pallas as pl
from jax.experimental.pallas import tpu as pltpu

F32 = jnp.float32
BF16 = jnp.bfloat16

CHUNK = 64
HEAD_DIM = 128
N_HEADS = 8
SSM_GROUP = 16
SSM_STATE = 64
SSM_T = 16
ROPE_THETA = 10000.0
Q_SCALE = HEAD_DIM ** -0.5 * math.log2(math.e)
EPS = 1e-6
NEG = -1e30

LANES = 128
SUBLANES = 8
GROUPS_PER_TILE = LANES // SSM_GROUP
CHUNK_TILE = 128
ROW_TILE = 512
STRIP = 32
VMEM_LIMIT = 56 * 1024 * 1024
KV_PROJ_VMEM_LIMIT = 58 * 1024 * 1024

_NT = (((1,), (1,)), ((), ()))


def _params(*sem, vmem_limit=VMEM_LIMIT):
    return pltpu.CompilerParams(dimension_semantics=sem, vmem_limit_bytes=vmem_limit)


def _resident(shape, index_map):
    return pl.BlockSpec(shape, index_map, pipeline_mode=pl.Buffered(1))


def _rms_scale(x, g):
    ms = jnp.mean(x * x, axis=-1, keepdims=True)
    return x * lax.rsqrt(ms + EPS) * g


def _staged_weight_spec(d, tn, nb):
    return pl.BlockSpec((d, tn), lambda r, n: (0, jnp.where(r == 0, n, nb - 1)),
                        pipeline_mode=pl.Buffered(1))


def _row_block(r):
    return jnp.maximum(r - 1, 0)


def _col_block(r, n):
    return jnp.where(r == 0, 0, n)


def _a_in_kernel(xp_ref, xs_ref, g_ref, w_ref, o_ref, xn_ref, wb_ref, *, mp, nb):
    m, n = pl.program_id(0) - 1, pl.program_id(1)

    @pl.when(m < 0)
    def _():
        wb_ref[n] = w_ref[...].astype(BF16)

    @pl.when((n == 0) & (m >= 0) & (m < mp))
    def _():
        xn_ref[...] = _rms_scale(xp_ref[...], g_ref[...]).astype(BF16)

    @pl.when((n == 0) & (m == mp))
    def _():
        xn_ref[...] = _rms_scale(xs_ref[...], g_ref[...]).astype(BF16)

    for c in range(nb):
        @pl.when((m >= 0) & (m <= mp) & (n == c))
        def _():
            o_ref[...] = jnp.dot(xn_ref[...], wb_ref[c], preferred_element_type=F32)

    @pl.when(m > mp)
    def _():
        o_ref[...] = jnp.zeros(o_ref.shape, F32)


def _a_in_proj(xp, xs, g, w, rows_pad, tn):
    tm = ROW_TILE
    d = g.shape[-1]
    n_out = w.shape[1]
    mp = xp.shape[0] // tm
    nb = n_out // tn
    assert xs.shape[0] == tm and xp.shape[0] % tm == 0 and rows_pad % tm == 0
    return pl.pallas_call(
        functools.partial(_a_in_kernel, mp=mp, nb=nb),
        out_shape=jax.ShapeDtypeStruct((rows_pad, n_out), F32),
        grid=(1 + rows_pad // tm, nb),
        in_specs=[
            pl.BlockSpec((tm, d), lambda r, n: (jnp.minimum(_row_block(r), mp - 1), 0)),
            _resident((tm, d), lambda r, n: (0, 0)),
            _resident((1, d), lambda r, n: (0, 0)),
            _staged_weight_spec(d, tn, nb),
        ],
        out_specs=pl.BlockSpec((tm, tn), lambda r, n: (_row_block(r), _col_block(r, n))),
        scratch_shapes=[pltpu.VMEM((tm, d), BF16), pltpu.VMEM((nb, d, tn), BF16)],
        compiler_params=_params("arbitrary", "arbitrary"),
        name="a_in_proj",
    )(xp, xs, g, w)


PREP_PAIRS = LANES // SSM_T


def _ssm_prep_kernel(lrx_ref, lix_ref, dtx_ref, bre_ref, bim_ref, lrr_ref, lir_ref, dtr_ref,
                     cre_ref, cim_ref, mt_ref, bend_ref, cin_ref, a16r_ref, a16i_ref):
    p = SSM_STATE
    p2 = 2 * p
    w = SSM_T * SSM_GROUP
    hi = lax.Precision.HIGHEST

    lr, li = lrx_ref[...], lix_ref[...]
    dt = jnp.exp(dtx_ref[...])
    zr, zi = lr * dt, li * dt
    mag = jnp.exp(zr)
    n_re, n_im = mag * jnp.cos(zi) - 1.0, mag * jnp.sin(zi)
    den = lr * lr + li * li
    cf_re = (n_re * lr + n_im * li) / den
    cf_im = (n_im * lr - n_re * li) / den
    lane = lax.broadcasted_iota(jnp.int32, (p2, LANES), 1)
    e_end = (SSM_T - 1 - lane % SSM_T).astype(F32)
    pm = jnp.exp(zr * e_end)
    pw_re, pw_im = pm * jnp.cos(zi * e_end), pm * jnp.sin(zi * e_end)
    cp_re = cf_re * pw_re - cf_im * pw_im
    cp_im = cf_re * pw_im + cf_im * pw_re
    coef = jnp.concatenate([cp_re, cp_im, cf_re, cf_im], axis=0)
    sel_row = lax.broadcasted_iota(jnp.int32, (LANES, w), 0)
    sel_col = lax.broadcasted_iota(jnp.int32, (LANES, w), 1)

    tau = lax.broadcasted_iota(jnp.int32, (SSM_T + SUBLANES, p2), 0).astype(F32)
    lane2 = lax.broadcasted_iota(jnp.int32, (w, p2), 1)
    sblk = lax.broadcasted_iota(jnp.int32, (w, w), 1) // SSM_GROUP

    for j in range(PREP_PAIRS):
        own_pair = (sel_row // SSM_T) == j
        spread = jnp.where(own_pair & (sel_row % SSM_T == sel_col // SSM_GROUP), 1.0, 0.0)
        cpx = jnp.dot(coef[:2 * p2], spread, precision=hi, preferred_element_type=F32)
        cfx = jnp.broadcast_to(coef[2 * p2:, j * SSM_T:j * SSM_T + 1], (2 * p2, w))
        b_re, b_im = bre_ref[j], bim_ref[j]
        end_re = cpx[:p2] * b_re - cpx[p2:] * b_im
        end_im = cpx[:p2] * b_im + cpx[p2:] * b_re
        bb_re = cfx[:p2] * b_re - cfx[p2:] * b_im
        bb_im = cfx[:p2] * b_im + cfx[p2:] * b_re
        for r in range(2):
            sl = slice(r * p, (r + 1) * p)
            bend_ref[2 * j + r] = jnp.concatenate([end_re[sl], end_im[sl]], axis=0).astype(BF16)
        bst = jnp.concatenate([bb_re, bb_im], axis=0)

        lr2, li2 = lrr_ref[j], lir_ref[j]
        dt2 = jnp.exp(dtr_ref[j])
        zr2, zi2 = lr2 * dt2, li2 * dt2
        qm = jnp.exp(zr2 * tau)
        q_re, q_im = qm * jnp.cos(zi2 * tau), qm * jnp.sin(zi2 * tau)
        c_re, c_im = cre_ref[j], cim_ref[j]

        def c_times_powers(first):
            re = [c_re * q_re[t:t + 1] - c_im * q_im[t:t + 1] for t in range(first, first + SSM_T)]
            im = [c_re * q_im[t:t + 1] + c_im * q_re[t:t + 1] for t in range(first, first + SSM_T)]
            return jnp.concatenate(re, axis=0), jnp.concatenate(im, axis=0)

        g_re, g_im = c_times_powers(0)
        ci_re, ci_im = c_times_powers(1)
        for r in range(2):
            own = (lane2 // p) == r
            lhs = jnp.concatenate([jnp.where(own, g_re, 0.0), jnp.where(own, -g_im, 0.0)], axis=1)
            gen = jnp.dot(lhs, bst, preferred_element_type=F32)
            k = 0
            while (SSM_GROUP << k) < w:
                sh = SSM_GROUP << k
                shifted = jnp.concatenate([jnp.zeros((sh, w), F32), gen[:w - sh]], axis=0)
                gen = jnp.where(((sblk >> k) & 1) == 1, shifted, gen)
                k += 1
            mt_ref[2 * j + r] = gen.astype(BF16)
            cin_ref[2 * j + r] = jnp.concatenate(
                [jnp.where(own, ci_re, 0.0), jnp.where(own, -ci_im, 0.0)], axis=1).astype(BF16)
        a16r_ref[j] = q_re[SSM_T:SSM_T + 1]
        a16i_ref[j] = q_im[SSM_T:SSM_T + 1]


def _ssm_prep(lam_re, lam_im, log_dt, b_re, b_im, c_re, c_im):
    g, p = lam_re.shape
    gp = g // 2
    p2 = 2 * p
    w = SSM_T * SSM_GROUP
    pp = PREP_PAIRS
    assert gp % pp == 0
    colx = lambda a: jnp.repeat(a.reshape(gp, p2).T, SSM_T, axis=1)
    row = lambda a: a.reshape(gp, 1, p2)
    dt_full = jnp.broadcast_to(log_dt[:, None], (g, p))
    b_t = lambda a: jnp.tile(a.reshape(gp, p2, SSM_GROUP), (1, 1, SSM_T))
    c_pair = lambda a: a.reshape(gp, 2, SSM_GROUP, p).transpose(0, 2, 1, 3).reshape(gp, SSM_GROUP, p2)
    xspec = pl.BlockSpec((p2, LANES), lambda i: (0, i))
    rowspec = pl.BlockSpec((pp, 1, p2), lambda i: (i, 0, 0))
    bspec = pl.BlockSpec((pp, p2, w), lambda i: (i, 0, 0))
    cspec = pl.BlockSpec((pp, SSM_GROUP, p2), lambda i: (i, 0, 0))
    mt, bend, cin, a16r, a16i = pl.pallas_call(
        _ssm_prep_kernel,
        out_shape=[
            jax.ShapeDtypeStruct((g, w, w), BF16),
            jax.ShapeDtypeStruct((g, p2, w), BF16),
            jax.ShapeDtypeStruct((g, w, 2 * p2), BF16),
            jax.ShapeDtypeStruct((gp, 1, p2), F32),
            jax.ShapeDtypeStruct((gp, 1, p2), F32),
        ],
        grid=(gp // pp,),
        in_specs=[xspec, xspec, xspec, bspec, bspec, rowspec, rowspec, rowspec, cspec, cspec],
        out_specs=[
            pl.BlockSpec((2 * pp, w, w), lambda i: (i, 0, 0)),
            pl.BlockSpec((2 * pp, p2, w), lambda i: (i, 0, 0)),
            pl.BlockSpec((2 * pp, w, 2 * p2), lambda i: (i, 0, 0)),
            rowspec, rowspec,
        ],
        compiler_params=_params("arbitrary"),
        name="ssm_prep",
    )(colx(lam_re), colx(lam_im), colx(dt_full), b_t(b_re), b_t(b_im),
      row(lam_re), row(lam_im), row(dt_full), c_pair(c_re), c_pair(c_im))
    return mt, bend, cin, a16r.reshape(1, g * p), a16i.reshape(1, g * p)


def _step_rows(s):
    return pl.ds(s, CHUNK_TILE, stride=SSM_T)


def _build_ut(u_ref, ut_ref):
    for s in range(SSM_T):
        xt = u_ref[_step_rows(s), :].T.astype(BF16)
        for gl in range(GROUPS_PER_TILE):
            ut_ref[gl, s * SSM_GROUP:(s + 1) * SSM_GROUP, :] = xt[gl * SSM_GROUP:(gl + 1) * SSM_GROUP, :]


STATE_TILES = 2


def _ssm_state_kernel(*refs):
    u_refs = refs[:STATE_TILES]
    bend_ref, sre_ref, sim_ref = refs[STATE_TILES:STATE_TILES + 3]
    ut_refs = refs[STATE_TILES + 3:]
    p = SSM_STATE
    for tile in range(STATE_TILES):
        _build_ut(u_refs[tile], ut_refs[tile])
        for pr in range(GROUPS_PER_TILE // 2):
            st = [jnp.dot(bend_ref[tile * GROUPS_PER_TILE + 2 * pr + r], ut_refs[tile][2 * pr + r],
                          preferred_element_type=F32)
                  for r in range(2)]
            cols = slice((tile * GROUPS_PER_TILE // 2 + pr) * 2 * p,
                         (tile * GROUPS_PER_TILE // 2 + pr + 1) * 2 * p)
            sre_ref[:, cols] = jnp.concatenate([st[0][:p], st[1][:p]], axis=0).T
            sim_ref[:, cols] = jnp.concatenate([st[0][p:], st[1][p:]], axis=0).T


def _ssm_state_contrib(uz, bend, jpad):
    g, p2, w = bend.shape
    gstep = STATE_TILES * GROUPS_PER_TILE
    assert g % gstep == 0
    sw = gstep * SSM_STATE
    out = jax.ShapeDtypeStruct((jpad, g * SSM_STATE), F32)
    ospec = pl.BlockSpec((CHUNK_TILE, sw), lambda q, j: (j, q))
    utile = lambda t: pl.BlockSpec((CHUNK_TILE * SSM_T, LANES), lambda q, j: (j, STATE_TILES * q + t))
    return pl.pallas_call(
        _ssm_state_kernel,
        out_shape=[out, out],
        grid=(g // gstep, jpad // CHUNK_TILE),
        in_specs=[*[utile(t) for t in range(STATE_TILES)],
                  pl.BlockSpec((gstep, p2, w), lambda q, j: (q, 0, 0))],
        out_specs=[ospec, ospec],
        scratch_shapes=[pltpu.VMEM((GROUPS_PER_TILE, w, CHUNK_TILE), BF16) for _ in range(STATE_TILES)],
        compiler_params=_params("arbitrary", "arbitrary"),
        name="ssm_state_contrib",
    )(*[uz] * STATE_TILES, bend)


def _ssm_scan_kernel(sre_ref, sim_ref, ar_ref, ai_ref, h0r_ref, h0i_ref,
                     hpr_ref, hpi_ref, pr_ref, pi_ref, sr_ref, si_ref, hr_sc, hi_sc, *, n_prompt):
    i = pl.program_id(0)
    rb = sre_ref.shape[0]
    ar, ai = ar_ref[...], ai_ref[...]

    @pl.when(i == 0)
    def _():
        hr_sc[...] = jnp.zeros_like(hr_sc)
        hi_sc[...] = jnp.zeros_like(hi_sc)

    @pl.when(i < n_prompt)
    def _():
        def body(j, c):
            row = pl.ds(j, 1)
            hr, hi = hr_sc[...], hi_sc[...]
            hpr_ref[row, :] = hr
            hpi_ref[row, :] = hi
            hr_sc[...] = ar * hr - ai * hi + sre_ref[row, :]
            hi_sc[...] = ar * hi + ai * hr + sim_ref[row, :]
            return c

        lax.fori_loop(0, rb, body, 0)

    @pl.when(i == n_prompt - 1)
    def _():
        pr_ref[...] = hr_sc[...]
        pi_ref[...] = hi_sc[...]

    @pl.when(i == n_prompt)
    def _():
        h0r, h0i = h0r_ref[...], h0i_ref[...]
        hpr_ref[...] = h0r
        hpi_ref[...] = h0i
        sr_ref[...] = ar * h0r - ai * h0i + sre_ref[...]
        si_ref[...] = ar * h0i + ai * h0r + sim_ref[...]

    @pl.when(i > n_prompt)
    def _():
        hpr_ref[...] = jnp.zeros_like(hpr_ref)
        hpi_ref[...] = jnp.zeros_like(hpi_ref)


def _ssm_scan(s_re, s_im, a16r, a16i, h0r, h0i, jp):
    jpad, n = s_re.shape
    rb = h0r.shape[0]
    assert jp % rb == 0 and jpad % rb == 0
    n_prompt = jp // rb
    rows = pl.BlockSpec((rb, n), lambda i: (i, 0))
    const1 = pl.BlockSpec((1, n), lambda i: (0, 0))
    constb = pl.BlockSpec((rb, n), lambda i: (0, 0))
    big = jax.ShapeDtypeStruct((jpad, n), F32)
    one = jax.ShapeDtypeStruct((1, n), F32)
    bat = jax.ShapeDtypeStruct((rb, n), F32)
    return pl.pallas_call(
        functools.partial(_ssm_scan_kernel, n_prompt=n_prompt),
        out_shape=[big, big, one, one, bat, bat],
        grid=(jpad // rb,),
        in_specs=[rows, rows, const1, const1, constb, constb],
        out_specs=[rows, rows, const1, const1, constb, constb],
        scratch_shapes=[pltpu.VMEM((1, n), F32), pltpu.VMEM((1, n), F32)],
        compiler_params=_params("arbitrary"),
        name="ssm_scan",
    )(s_re, s_im, a16r, a16i, h0r, h0i)


def _ssm_out_kernel(u_ref, mt_ref, cin_ref, hpr_ref, hpi_ref, d_ref, o_ref, ut_ref, uf_ref, yt_ref):
    for s in range(SSM_T):
        xt = u_ref[_step_rows(s), :].T
        xb = xt.astype(BF16)
        for gl in range(GROUPS_PER_TILE):
            rows = slice(gl * SSM_GROUP, (gl + 1) * SSM_GROUP)
            ut_ref[gl, s * SSM_GROUP:(s + 1) * SSM_GROUP, :] = xb[rows, :]
            uf_ref[gl, s * SSM_GROUP:(s + 1) * SSM_GROUP, :] = xt[rows, :]
    p2 = 2 * SSM_STATE
    for gl in range(GROUPS_PER_TILE):
        pr = gl // 2
        hp = jnp.concatenate([hpr_ref[:, pr * p2:(pr + 1) * p2], hpi_ref[:, pr * p2:(pr + 1) * p2]],
                             axis=1).astype(BF16)
        yt = jnp.dot(mt_ref[gl], ut_ref[gl], preferred_element_type=F32)
        yt = yt + lax.dot_general(cin_ref[gl], hp, _NT, preferred_element_type=F32)
        yt = jax.nn.gelu(yt + d_ref[gl] * uf_ref[gl])
        for t in range(SSM_T):
            yt_ref[t, gl * SSM_GROUP:(gl + 1) * SSM_GROUP, :] = yt[t * SSM_GROUP:(t + 1) * SSM_GROUP, :]
    for t in range(SSM_T):
        o_ref[_step_rows(t), :] = yt_ref[t].T


def _ssm_output(uz, mt, cin, hp_re, hp_im, d_skip, jpad):
    g, w, _ = mt.shape
    w2 = cin.shape[-1]
    ntile = g // GROUPS_PER_TILE
    sw = GROUPS_PER_TILE * SSM_STATE
    tok = pl.BlockSpec((CHUNK_TILE * SSM_T, LANES), lambda q, j: (j, q))
    hspec = pl.BlockSpec((CHUNK_TILE, sw), lambda q, j: (j, q))
    return pl.pallas_call(
        _ssm_out_kernel,
        out_shape=jax.ShapeDtypeStruct((jpad * SSM_T, g * SSM_GROUP), F32),
        grid=(ntile, jpad // CHUNK_TILE),
        in_specs=[
            tok,
            pl.BlockSpec((GROUPS_PER_TILE, w, w), lambda q, j: (q, 0, 0)),
            pl.BlockSpec((GROUPS_PER_TILE, w, w2), lambda q, j: (q, 0, 0)),
            hspec, hspec,
            pl.BlockSpec((GROUPS_PER_TILE, w, 1), lambda q, j: (q, 0, 0)),
        ],
        out_specs=tok,
        scratch_shapes=[
            pltpu.VMEM((GROUPS_PER_TILE, w, CHUNK_TILE), BF16),
            pltpu.VMEM((GROUPS_PER_TILE, w, CHUNK_TILE), F32),
            pltpu.VMEM((SSM_T, LANES, CHUNK_TILE), F32),
        ],
        compiler_params=_params("arbitrary", "arbitrary"),
        name="ssm_output",
    )(uz, mt, cin, hp_re, hp_im, d_skip)


def _a_glu_kernel(y_ref, z_ref, w_ref, o_ref, wb_ref):
    r = pl.program_id(0)

    @pl.when(r == 0)
    def _():
        wb_ref[...] = w_ref[...].astype(BF16)

    @pl.when(r > 0)
    def _():
        y = y_ref[...]
        gate = jnp.dot(y.astype(BF16), wb_ref[...], preferred_element_type=F32)
        y2 = y * jax.nn.sigmoid(gate)
        o_ref[...] = (y2 * jax.nn.silu(z_ref[...])).astype(BF16)


def _a_glu(yg, uz, w_glu, rows):
    tm = ROW_TILE
    e = yg.shape[1]
    return pl.pallas_call(
        _a_glu_kernel,
        out_shape=jax.ShapeDtypeStruct((rows, e), BF16),
        grid=(1 + rows // tm,),
        in_specs=[
            pl.BlockSpec((tm, e), lambda r: (_row_block(r), 0)),
            pl.BlockSpec((tm, e), lambda r: (_row_block(r), 1)),
            _resident((e, e), lambda r: (0, 0)),
        ],
        out_specs=pl.BlockSpec((tm, e), lambda r: (_row_block(r), 0)),
        scratch_shapes=[pltpu.VMEM((e, e), BF16)],
        compiler_params=_params("arbitrary"),
        name="a_glu",
    )(yg, uz, w_glu)


def _a_out_kernel(o_ref, w_ref, xp_ref, xs_ref, x1_ref, wb_ref, *, mp):
    m = pl.program_id(0) - 1

    @pl.when(m < 0)
    def _():
        wb_ref[...] = w_ref[...].astype(BF16)

    @pl.when((m >= 0) & (m < mp))
    def _():
        x1_ref[...] = xp_ref[...] + jnp.dot(o_ref[...], wb_ref[...], preferred_element_type=F32)

    @pl.when(m == mp)
    def _():
        x1_ref[...] = xs_ref[...] + jnp.dot(o_ref[...], wb_ref[...], preferred_element_type=F32)


def _a_out_proj(o, w_out, xp, xs):
    tm = ROW_TILE
    rows, e = o.shape
    d = w_out.shape[1]
    mp = xp.shape[0] // tm
    return pl.pallas_call(
        functools.partial(_a_out_kernel, mp=mp),
        out_shape=jax.ShapeDtypeStruct((rows, d), F32),
        grid=(1 + rows // tm,),
        in_specs=[
            pl.BlockSpec((tm, e), lambda r: (_row_block(r), 0)),
            _resident((e, d), lambda r: (0, 0)),
            pl.BlockSpec((tm, d), lambda r: (jnp.minimum(_row_block(r), mp - 1), 0)),
            _resident((tm, d), lambda r: (0, 0)),
        ],
        out_specs=pl.BlockSpec((tm, d), lambda r: (_row_block(r), 0)),
        scratch_shapes=[pltpu.VMEM((e, d), BF16)],
        compiler_params=_params("arbitrary"),
        name="a_out_proj",
    )(o, w_out, xp, xs)


def _rope(x, cos, sin):
    outs = []
    for c in range(x.shape[1] // HEAD_DIM):
        xc = x[:, c * HEAD_DIM:(c + 1) * HEAD_DIM]
        outs.append(xc * cos + pltpu.roll(xc, HEAD_DIM // 2, axis=1) * sin)
    return jnp.concatenate(outs, axis=1) if len(outs) > 1 else outs[0]


def _store_heads(o_ref, val, tm):
    for h in range(N_HEADS):
        o_ref[pl.ds(h, tm, stride=N_HEADS), :] = val[:, h * HEAD_DIM:(h + 1) * HEAD_DIM]


def _prompt_col(r, mp, col, last):
    return jnp.where(_row_block(r) >= mp, last, col)


def _sample_col(r, mp, col):
    return jnp.where(_row_block(r) == mp, col, 0)


def _kv_proj_kernel(x_ref, g_ref, w_ref, cos_ref, sin_ref, kp_ref, vp_ref, cbp_ref,
                    ks_ref, vs_ref, cbs_ref, xn_ref, wb_ref, *, tm, mp):
    m, n = pl.program_id(0) - 1, pl.program_id(1)
    hpb = w_ref.shape[1] // (2 * HEAD_DIM)

    for nn in range(4):
        @pl.when((m < 0) & (n == nn))
        def _():
            for hl in range(hpb):
                h = hpb * (nn % 2) + hl
                for half in range(2):
                    src = (2 * hl + half) * HEAD_DIM
                    wb_ref[2 * (nn // 2) + half, :, h * HEAD_DIM:(h + 1) * HEAD_DIM] = (
                        w_ref[:, src:src + HEAD_DIM].astype(BF16))

    @pl.when((m >= 0) & (n == 0))
    def _():
        xn_ref[...] = _rms_scale(x_ref[...], g_ref[...]).astype(BF16)

    def column(c, k_ref, v_ref, cb_ref):
        acc = jnp.dot(xn_ref[...], wb_ref[c], preferred_element_type=F32)
        if c < 2:
            acc = _rope(acc, cos_ref[...], sin_ref[...])
        _store_heads(k_ref if c < 2 else v_ref, acc, tm)
        cb_ref[...] = acc.astype(BF16)

    for c in range(4):
        @pl.when((m >= 0) & (m < mp) & (n == c))
        def _():
            column(c, kp_ref, vp_ref, cbp_ref)

        @pl.when((m == mp) & (n == c))
        def _():
            column(c, ks_ref, vs_ref, cbs_ref)


def _kv_proj(x1, rows_p, g, w_kv, cos, sin):
    tm = ROW_TILE
    rows, d = x1.shape
    tn = N_HEADS * HEAD_DIM
    mp = rows_p // tm
    assert w_kv.shape[1] == 4 * tn and rows == rows_p + tm and rows_p % tm == 0
    out4 = lambda nrows: jax.ShapeDtypeStruct((nrows * N_HEADS, 2 * HEAD_DIM), F32)
    outb = lambda nrows: jax.ShapeDtypeStruct((nrows, 4 * tn), BF16)
    prow = lambda r: jnp.minimum(_row_block(r), mp - 1)
    col = _col_block
    return pl.pallas_call(
        functools.partial(_kv_proj_kernel, tm=tm, mp=mp),
        out_shape=[out4(rows_p), out4(rows_p), outb(rows_p), out4(tm), out4(tm), outb(tm)],
        grid=(1 + rows // tm, 4),
        in_specs=[
            pl.BlockSpec((tm, d), lambda r, n: (_row_block(r), 0)),
            _resident((1, d), lambda r, n: (0, 0)),
            _staged_weight_spec(d, tn, 4),
            pl.BlockSpec((tm, HEAD_DIM), lambda r, n: (_row_block(r), 0)),
            pl.BlockSpec((tm, HEAD_DIM), lambda r, n: (_row_block(r), 0)),
        ],
        out_specs=[
            pl.BlockSpec((tm * N_HEADS, HEAD_DIM),
                         lambda r, n: (prow(r), _prompt_col(r, mp, jnp.minimum(col(r, n), 1), 1))),
            pl.BlockSpec((tm * N_HEADS, HEAD_DIM),
                         lambda r, n: (prow(r), _prompt_col(r, mp, jnp.maximum(col(r, n) - 2, 0), 1))),
            pl.BlockSpec((tm, tn), lambda r, n: (prow(r), _prompt_col(r, mp, col(r, n), 3))),
            pl.BlockSpec((tm * N_HEADS, HEAD_DIM),
                         lambda r, n: (0, _sample_col(r, mp, jnp.minimum(col(r, n), 1)))),
            pl.BlockSpec((tm * N_HEADS, HEAD_DIM),
                         lambda r, n: (0, _sample_col(r, mp, jnp.maximum(col(r, n) - 2, 0)))),
            pl.BlockSpec((tm, tn), lambda r, n: (0, _sample_col(r, mp, col(r, n)))),
        ],
        scratch_shapes=[pltpu.VMEM((tm, d), BF16), pltpu.VMEM((4, d, tn), BF16)],
        compiler_params=_params("arbitrary", "arbitrary", vmem_limit=KV_PROJ_VMEM_LIMIT),
        name="kv_proj",
    )(x1, g, w_kv, cos, sin)


def _qz_proj_kernel(x_ref, g_ref, w_ref, cos_ref, sin_ref, qp_ref, zp_ref, qs_ref, zs_ref,
                    xn_ref, wb_ref, *, nq, mp):
    m, n = pl.program_id(0) - 1, pl.program_id(1)

    @pl.when(m < 0)
    def _():
        wb_ref[n] = w_ref[...].astype(BF16)

    @pl.when((m >= 0) & (n == 0))
    def _():
        xn_ref[...] = _rms_scale(x_ref[...], g_ref[...]).astype(BF16)

    def column(c, q_ref, z_ref):
        acc = jnp.dot(xn_ref[...], wb_ref[c], preferred_element_type=F32)
        if c < nq:
            q_ref[...] = (_rope(acc, cos_ref[...], sin_ref[...]) * Q_SCALE).astype(BF16)
        else:
            z_ref[...] = acc

    for c in range(2 * nq):
        @pl.when((m >= 0) & (m < mp) & (n == c))
        def _():
            column(c, qp_ref, zp_ref)

        @pl.when((m == mp) & (n == c))
        def _():
            column(c, qs_ref, zs_ref)


def _qz_proj(x1, rows_p, g, w, cos, sin, tn):
    tm = ROW_TILE
    rows, d = x1.shape
    half = w.shape[1] // 2
    nq = half // tn
    mp = rows_p // tm
    assert rows == rows_p + tm and rows_p % tm == 0
    prow = lambda r: jnp.minimum(_row_block(r), mp - 1)
    qcol = lambda r, n: jnp.minimum(_col_block(r, n), nq - 1)
    zcol = lambda r, n: jnp.maximum(_col_block(r, n) - nq, 0)
    return pl.pallas_call(
        functools.partial(_qz_proj_kernel, nq=nq, mp=mp),
        out_shape=[jax.ShapeDtypeStruct((rows_p, half), BF16), jax.ShapeDtypeStruct((rows_p, half), F32),
                   jax.ShapeDtypeStruct((tm, half), BF16), jax.ShapeDtypeStruct((tm, half), F32)],
        grid=(1 + rows // tm, 2 * nq),
        in_specs=[
            pl.BlockSpec((tm, d), lambda r, n: (_row_block(r), 0)),
            _resident((1, d), lambda r, n: (0, 0)),
            _staged_weight_spec(d, tn, 2 * nq),
            pl.BlockSpec((tm, HEAD_DIM), lambda r, n: (_row_block(r), 0)),
            pl.BlockSpec((tm, HEAD_DIM), lambda r, n: (_row_block(r), 0)),
        ],
        out_specs=[
            pl.BlockSpec((tm, tn), lambda r, n: (prow(r), _prompt_col(r, mp, qcol(r, n), nq - 1))),
            pl.BlockSpec((tm, tn), lambda r, n: (prow(r), _prompt_col(r, mp, zcol(r, n), nq - 1))),
            pl.BlockSpec((tm, tn), lambda r, n: (0, _sample_col(r, mp, qcol(r, n)))),
            pl.BlockSpec((tm, tn), lambda r, n: (0, _sample_col(r, mp, zcol(r, n)))),
        ],
        scratch_shapes=[pltpu.VMEM((tm, d), BF16), pltpu.VMEM((2 * nq, d, tn), BF16)],
        compiler_params=_params("arbitrary", "arbitrary"),
        name="qz_proj",
    )(x1, g, w, cos, sin)


def _rope_tables(pos):
    inv_freq = ROPE_THETA ** (-jnp.arange(0, HEAD_DIM, 2, dtype=F32) / HEAD_DIM)
    ang = pos.astype(F32)[:, None] * inv_freq[None, :]
    c, s = jnp.cos(ang), jnp.sin(ang)
    return jnp.concatenate([c, c], axis=-1), jnp.concatenate([-s, s], axis=-1)


def _diff_lambda(lq1_ref, lk1_ref, lq2_ref, lk2_ref, lambda_init):
    s1 = jnp.sum(lq1_ref[...] * lk1_ref[...], axis=-1, keepdims=True)
    s2 = jnp.sum(lq2_ref[...] * lk2_ref[...], axis=-1, keepdims=True)
    return jnp.exp(s1) - jnp.exp(s2) + lambda_init


def _attn_finish(a1, l1, a2, l2, lam, subln, z, lambda_init):
    o = a1 / l1 - lam * (a2 / l2)
    o = _rms_scale(o, subln) * (1.0 - lambda_init)
    return (o * jax.nn.silu(z)).astype(BF16)


def _attn_prompt_kernel(q_ref, k1_ref, k2_ref, v1_ref, v2_ref, z_ref,
                        lq1_ref, lk1_ref, lq2_ref, lk2_ref, sub_ref,
                        o_ref, m_sc, l_sc, al_sc, a_sc, *sp_scs, tq, ts, tk, lambda_init):
    qi = pl.program_id(1)
    k_refs = (k1_ref, k2_ref)
    nsub = tq // ts
    s_scs, p_scs = sp_scs[:2 * nsub], sp_scs[2 * nsub:]
    m_sc[...] = jnp.full(m_sc.shape, NEG, F32)
    l_sc[...] = jnp.zeros(l_sc.shape, F32)
    a_sc[...] = jnp.zeros(a_sc.shape, F32)

    def softmax_strips(sub, n, col0):
        for r in range(ts // STRIP):
            row0 = sub * ts + r * STRIP
            lr = slice(r * STRIP, (r + 1) * STRIP)
            gr = slice(row0, row0 + STRIP)
            visible = tk if col0 is None else min(tk, (row0 // CHUNK + 1) * CHUNK - col0)
            assert visible > 0
            ncv = -(-visible // LANES)
            s = s_scs[2 * sub + n][lr, 0:ncv * LANES]
            if visible < ncv * LANES:
                col = lax.broadcasted_iota(jnp.int32, s.shape, 1)
                s = jnp.where(col < visible, s, NEG)
            cols = [s[:, c * LANES:(c + 1) * LANES] for c in range(ncv)]
            m_cur = jnp.max(functools.reduce(jnp.maximum, cols), axis=-1, keepdims=True)
            m_old = m_sc[n, gr, :]
            m_new = jnp.maximum(m_old, m_cur)
            ps = [jnp.exp2(c - m_new) for c in cols]
            l_cur = jnp.sum(functools.reduce(jnp.add, ps), axis=-1, keepdims=True)
            alpha = jnp.exp2(m_old - m_new)
            l_sc[n, gr, :] = alpha * l_sc[n, gr, :] + l_cur
            m_sc[n, gr, :] = m_new
            al_sc[n, gr, :] = alpha
            for c in range(keys_seen(sub, col0) // LANES):
                pc = ps[c].astype(BF16) if c < ncv else jnp.zeros((STRIP, LANES), BF16)
                p_scs[2 * sub + n][lr, c * LANES:(c + 1) * LANES] = pc

    def keys_seen(sub, col0):
        return tk if col0 is None else min(tk, (sub + 1) * ts - col0)

    def block(kb, col0, subs):
        start = pl.multiple_of(kb * tk, tk)
        for sub in subs:
            rows = pl.ds(start, keys_seen(sub, col0))
            for n in range(2):
                qn = q_ref[sub * ts:(sub + 1) * ts, n * HEAD_DIM:(n + 1) * HEAD_DIM]
                s_scs[2 * sub + n][:, 0:rows.size] = lax.dot_general(
                    qn, k_refs[n][rows, :], _NT, preferred_element_type=F32)
        for sub in subs:
            gr = slice(sub * ts, (sub + 1) * ts)
            rows = pl.ds(start, keys_seen(sub, col0))
            vblk = jnp.concatenate([v1_ref[rows, :], v2_ref[rows, :]], axis=1)
            for n in range(2):
                softmax_strips(sub, n, col0)
                pv = jnp.dot(p_scs[2 * sub + n][:, 0:rows.size], vblk, preferred_element_type=F32)
                alpha = al_sc[n, gr, :]
                a_sc[n, gr, :] = a_sc[n, gr, :] * jnp.concatenate([alpha, alpha], axis=1) + pv

    nd = tq // tk

    def body(kb, c):
        block(kb, None, range(nsub))
        return c

    lax.fori_loop(0, qi * nd, body, 0)
    for d in range(nd):
        block(qi * nd + d, d * tk, [sub for sub in range(nsub) if (sub + 1) * ts > d * tk])

    lam = _diff_lambda(lq1_ref, lk1_ref, lq2_ref, lk2_ref, lambda_init)
    l1 = jnp.concatenate([l_sc[0], l_sc[0]], axis=1)
    l2 = jnp.concatenate([l_sc[1], l_sc[1]], axis=1)
    o_ref[...] = _attn_finish(a_sc[0], l1, a_sc[1], l2, lam, sub_ref[...], z_ref[...], lambda_init)


def _attn_prompt(qb, z32, kvb, lq1, lk1, lq2, lk2, subln, lambda_init, tq, ts, tk):
    rows = qb.shape[0]
    hw = 2 * HEAD_DIM
    nsub = tq // ts
    assert tq % tk == 0 and tk % ts == 0 and ts % STRIP == 0 and CHUNK % STRIP == 0
    vec = pl.BlockSpec((1, HEAD_DIM), lambda h, i: (0, 0))
    kcol = lambda c: pl.BlockSpec((rows, HEAD_DIM), lambda h, i: (0, c * N_HEADS + h))
    stat = pltpu.VMEM((2, tq, LANES), F32)
    return pl.pallas_call(
        functools.partial(_attn_prompt_kernel, tq=tq, ts=ts, tk=tk, lambda_init=lambda_init),
        out_shape=jax.ShapeDtypeStruct((rows, N_HEADS * hw), BF16),
        grid=(N_HEADS, rows // tq),
        in_specs=[
            pl.BlockSpec((tq, hw), lambda h, i: (i, h)),
            kcol(0), kcol(1), kcol(2), kcol(3),
            pl.BlockSpec((tq, hw), lambda h, i: (i, h)),
            vec, vec, vec, vec,
            pl.BlockSpec((1, hw), lambda h, i: (0, 0)),
        ],
        out_specs=pl.BlockSpec((tq, hw), lambda h, i: (i, h)),
        scratch_shapes=[
            stat, stat, stat,
            pltpu.VMEM((2, tq, hw), F32),
            *[pltpu.VMEM((ts, tk), F32) for _ in range(2 * nsub)],
            *[pltpu.VMEM((ts, tk), BF16) for _ in range(2 * nsub)],
        ],
        compiler_params=_params("arbitrary", "arbitrary"),
        name="attn_prompt",
    )(qb, kvb, kvb, kvb, kvb, z32, lq1, lk1, lq2, lk2, subln)


def _attn_sample_kernel(q_ref, ck1_ref, ck2_ref, cv1_ref, cv2_ref, kvn_ref, z_ref,
                        lq1_ref, lk1_ref, lq2_ref, lk2_ref, sub_ref,
                        o_ref, m_sc, l_sc, a_sc, s_sc, p_sc, *, tkv, nkb, past_len, lambda_init):
    kb = pl.program_id(1)
    t = q_ref.shape[0]
    hw = 2 * HEAD_DIM
    rows_all = 2 * N_HEADS * t
    strip = 4 * t
    ck_refs = (ck1_ref, ck2_ref)
    assert (past_len - 1) // CHUNK <= past_len // CHUNK

    @pl.when(kb == 0)
    def _():
        m_sc[...] = jnp.full(m_sc.shape, NEG, F32)
        l_sc[...] = jnp.zeros(l_sc.shape, F32)
        a_sc[...] = jnp.zeros(a_sc.shape, F32)

    col_blk = lambda c, h: slice((c * N_HEADS + h) * HEAD_DIM, (c * N_HEADS + h + 1) * HEAD_DIM)
    head_rows = lambda h: pl.ds(h, tkv, stride=N_HEADS)

    def update(with_new):
        ncols = tkv + (LANES if with_new else 0)
        if with_new:
            row = lax.broadcasted_iota(jnp.int32, (t, LANES), 0) + past_len
            col = lax.broadcasted_iota(jnp.int32, (t, LANES), 1)
            new_mask = (col < t) & (((col + past_len) // CHUNK) <= (row // CHUNK))
            pad_k = jnp.zeros((LANES - t, HEAD_DIM), BF16)
        for h in range(N_HEADS):
            for n in range(2):
                qn = q_ref[:, (2 * h + n) * HEAD_DIM:(2 * h + n + 1) * HEAD_DIM]
                rr = slice((2 * h + n) * t, (2 * h + n + 1) * t)
                k = ck_refs[n][head_rows(h), :].astype(BF16)
                s_sc[rr, 0:tkv] = lax.dot_general(qn, k, _NT, preferred_element_type=F32)
                if with_new:
                    kn = jnp.concatenate([kvn_ref[:, col_blk(n, h)], pad_k], axis=0)
                    sn = lax.dot_general(qn, kn, _NT, preferred_element_type=F32)
                    s_sc[rr, tkv:ncols] = jnp.where(new_mask, sn, NEG)
        for r in range(rows_all // strip):
            rr = slice(r * strip, (r + 1) * strip)
            cols = [s_sc[rr, c * LANES:(c + 1) * LANES] for c in range(ncols // LANES)]
            m_cur = jnp.max(functools.reduce(jnp.maximum, cols), axis=-1, keepdims=True)
            m_old = m_sc[rr, :]
            m_new = jnp.maximum(m_old, m_cur)
            ps = [jnp.exp2(c - m_new) for c in cols]
            l_cur = jnp.sum(functools.reduce(jnp.add, ps), axis=-1, keepdims=True)
            alpha = jnp.exp2(m_old - m_new)
            l_sc[rr, :] = alpha * l_sc[rr, :] + l_cur
            m_sc[rr, :] = m_new
            a_sc[rr, :] = a_sc[rr, :] * jnp.concatenate([alpha, alpha], axis=1)
            for c, pc in enumerate(ps):
                p_sc[rr, c * LANES:(c + 1) * LANES] = pc.astype(BF16)
        for h in range(N_HEADS):
            rr = slice(2 * h * t, (2 * h + 2) * t)
            v = jnp.concatenate([cv1_ref[head_rows(h), :], cv2_ref[head_rows(h), :]],
                                axis=1).astype(BF16)
            pv = jnp.dot(p_sc[rr, 0:tkv], v, preferred_element_type=F32)
            if with_new:
                vn = jnp.concatenate([kvn_ref[:, col_blk(2, h)], kvn_ref[:, col_blk(3, h)]], axis=1)
                vn = jnp.concatenate([vn, jnp.zeros((LANES - t, hw), BF16)], axis=0)
                pv = pv + jnp.dot(p_sc[rr, tkv:ncols], vn, preferred_element_type=F32)
            a_sc[rr, :] = a_sc[rr, :] + pv

    @pl.when(kb < nkb - 1)
    def _():
        update(False)

    @pl.when(kb == nkb - 1)
    def _():
        update(True)
        lam = _diff_lambda(lq1_ref, lk1_ref, lq2_ref, lk2_ref, lambda_init)
        sub = sub_ref[...]
        for h in range(N_HEADS):
            r1 = slice(2 * h * t, (2 * h + 1) * t)
            r2 = slice((2 * h + 1) * t, (2 * h + 2) * t)
            l1 = jnp.concatenate([l_sc[r1, :], l_sc[r1, :]], axis=1)
            l2 = jnp.concatenate([l_sc[r2, :], l_sc[r2, :]], axis=1)
            o_ref[:, h * hw:(h + 1) * hw] = _attn_finish(
                a_sc[r1, :], l1, a_sc[r2, :], l2, lam, sub, z_ref[:, h * hw:(h + 1) * hw], lambda_init)


def _attn_sample(qb, z32, kvb, cache_k2, cache_v2, lq1, lk1, lq2, lk2, subln, lambda_init, t, tkv):
    bsz, rows8, hw = cache_k2.shape
    past_len = rows8 // N_HEADS
    nkb = past_len // tkv
    rows_all = 2 * N_HEADS * t
    assert t <= LANES
    vec = pl.BlockSpec((1, HEAD_DIM), lambda b, k: (0, 0))
    cache = lambda c: pl.BlockSpec((None, tkv * N_HEADS, HEAD_DIM), lambda b, k: (b, k, c))
    full = lambda a: pl.BlockSpec((t, a.shape[1]), lambda b, k: (b, 0))
    return pl.pallas_call(
        functools.partial(_attn_sample_kernel, tkv=tkv, nkb=nkb, past_len=past_len,
                          lambda_init=lambda_init),
        out_shape=jax.ShapeDtypeStruct((bsz * t, N_HEADS * hw), BF16),
        grid=(bsz, nkb),
        in_specs=[
            full(qb), cache(0), cache(1), cache(0), cache(1), full(kvb), full(z32),
            vec, vec, vec, vec,
            pl.BlockSpec((1, hw), lambda b, k: (0, 0)),
        ],
        out_specs=pl.BlockSpec((t, N_HEADS * hw), lambda b, k: (b, 0)),
        scratch_shapes=[
            pltpu.VMEM((rows_all, LANES), F32),
            pltpu.VMEM((rows_all, LANES), F32),
            pltpu.VMEM((rows_all, hw), F32),
            pltpu.VMEM((rows_all, tkv + LANES), F32),
            pltpu.VMEM((rows_all, tkv + LANES), BF16),
        ],
        compiler_params=_params("arbitrary", "arbitrary"),
        name="attn_sample",
    )(qb, cache_k2, cache_k2, cache_v2, cache_v2, kvb, z32, lq1, lk1, lq2, lk2, subln)


def _b_out_kernel(op_ref, os_ref, w_ref, x_ref, g_ref, yp_ref, ys_ref, wb_ref, *, mp):
    m = pl.program_id(0) - 1

    @pl.when(m < 0)
    def _():
        wb_ref[...] = w_ref[...].astype(BF16)

    def finish(o_ref, y_ref):
        acc = jnp.dot(o_ref[...], wb_ref[...], preferred_element_type=F32)
        y_ref[...] = _rms_scale(x_ref[...] + acc, g_ref[...])

    @pl.when((m >= 0) & (m < mp))
    def _():
        finish(op_ref, yp_ref)

    @pl.when(m == mp)
    def _():
        finish(os_ref, ys_ref)


def _b_out_proj(og_p, og_s, w_out, x1, g):
    tm = ROW_TILE
    rows_p, e = og_p.shape
    rows, d = x1.shape
    mp = rows_p // tm
    assert og_s.shape[0] == tm and rows == rows_p + tm and rows_p % tm == 0
    prow = lambda r: jnp.minimum(_row_block(r), mp - 1)
    return pl.pallas_call(
        functools.partial(_b_out_kernel, mp=mp),
        out_shape=[jax.ShapeDtypeStruct((rows_p, d), F32), jax.ShapeDtypeStruct((tm, d), F32)],
        grid=(1 + rows // tm,),
        in_specs=[
            pl.BlockSpec((tm, e), lambda r: (prow(r), 0)),
            _resident((tm, e), lambda r: (0, 0)),
            _resident((e, d), lambda r: (0, 0)),
            pl.BlockSpec((tm, d), lambda r: (_row_block(r), 0)),
            _resident((1, d), lambda r: (0, 0)),
        ],
        out_specs=[pl.BlockSpec((tm, d), lambda r: (prow(r), 0)),
                   pl.BlockSpec((tm, d), lambda r: (0, 0))],
        scratch_shapes=[pltpu.VMEM((e, d), BF16)],
        compiler_params=_params("arbitrary"),
        name="b_out_proj",
    )(og_p, og_s, w_out, x1, g)


def kernel(x_prompt, x_sample, state_ssm_re, state_ssm_im, cache_k, cache_v, a_norm, a_w_in, a_lambda_re, a_lambda_im, a_log_dt, a_b_re, a_b_im, a_c_re, a_c_im, a_d, a_w_glu, a_w_out, kv_norm, w_kv, b_norm, b_w_in, b_lambda_q1, b_lambda_k1, b_lambda_q2, b_lambda_k2, b_subln, b_w_out, final_norm):
    bp, seq, d = x_prompt.shape
    bs, t_s, _ = x_sample.shape
    past_len = cache_k.shape[1]
    n_a, n_b = a_norm.shape[0], b_norm.shape[0]
    g_cnt, p = a_lambda_re.shape[1:]
    assert bp == 1 and n_a == 1 and n_b == 1
    assert t_s == SSM_T and seq % (SSM_T * bs) == 0 and seq % CHUNK == 0
    assert cache_k.shape[2:] == (N_HEADS, 2 * HEAD_DIM)

    jp, js = seq // SSM_T, bs
    jpad = -(-(jp + js) // CHUNK_TILE) * CHUNK_TILE
    rows_s = bs * t_s
    rows_a = seq + rows_s

    xp = x_prompt.reshape(seq, d)
    xs = x_sample.reshape(rows_s, d)
    uz = _a_in_proj(xp, xs, a_norm[0].reshape(1, d), a_w_in[0], jpad * SSM_T, tn=1024)
    mt, bend, cin, a16r, a16i = _ssm_prep(a_lambda_re[0], a_lambda_im[0], a_log_dt[0],
                                          a_b_re[0], a_b_im[0], a_c_re[0], a_c_im[0])
    s_re, s_im = _ssm_state_contrib(uz, bend, jpad)
    h0r = state_ssm_re[:, 0].reshape(bs, g_cnt * p)
    h0i = state_ssm_im[:, 0].reshape(bs, g_cnt * p)
    hp_re, hp_im, pre, pim, sre, sim = _ssm_scan(s_re, s_im, a16r, a16i, h0r, h0i, jp)
    d_col = jnp.tile(a_d[0].reshape(g_cnt, 1, SSM_GROUP), (1, SSM_T, 1)).reshape(g_cnt, -1, 1)
    yg = _ssm_output(uz, mt, cin, hp_re, hp_im, d_col, jpad)
    o_a = _a_glu(yg, uz, a_w_glu[0], rows_a)
    x1 = _a_out_proj(o_a, a_w_out[0], xp, xs)

    lambda_init = 0.8 - 0.6 * math.exp(-0.3 * n_a)
    w_q = b_w_in[0]
    kv_g = kv_norm.reshape(1, d)
    b_g = b_norm[0].reshape(1, d)
    pos = jnp.concatenate([jnp.arange(seq, dtype=jnp.int32),
                           jnp.tile(past_len + jnp.arange(t_s, dtype=jnp.int32), bs)])
    cos, sin = _rope_tables(pos)
    k_p, v_p, kvb_p, k_s, v_s, kvb_s = _kv_proj(x1, seq, kv_g, w_kv, cos, sin)
    qb_p, z_p, qb_s, z_s = _qz_proj(x1, seq, b_g, w_q, cos, sin, tn=1024)

    vecs = [a[0].reshape(1, HEAD_DIM) for a in (b_lambda_q1, b_lambda_k1, b_lambda_q2, b_lambda_k2)]
    subln = b_subln[0].reshape(1, 2 * HEAD_DIM)
    hw = 2 * HEAD_DIM
    og_p = _attn_prompt(qb_p, z_p, kvb_p, *vecs, subln, lambda_init, tq=1024, ts=512, tk=1024)
    og_s = _attn_sample(qb_s, z_s, kvb_s, cache_k.reshape(bs, past_len * N_HEADS, hw),
                        cache_v.reshape(bs, past_len * N_HEADS, hw), *vecs, subln, lambda_init,
                        t_s, tkv=1024)
    w_o = b_w_out[0]
    fg = final_norm.reshape(1, d)
    y_p, y_s = _b_out_proj(og_p, og_s, w_o, x1, fg)

    return (y_p.reshape(bp, seq, d), y_s.reshape(bs, t_s, d),
            pre.reshape(bp, n_a, g_cnt, p), pim.reshape(bp, n_a, g_cnt, p),
            k_p.reshape(bp, seq, N_HEADS, hw), v_p.reshape(bp, seq, N_HEADS, hw),
            sre.reshape(bs, n_a, g_cnt, p), sim.reshape(bs, n_a, g_cnt, p),
            k_s.reshape(bs, t_s, N_HEADS, hw), v_s.reshape(bs, t_s, N_HEADS, hw))
```

```python
import functools
import math

import jax
import jax.numpy as jnp
from jax import lax
from jax.experimental import pallas as pl
from jax.experimental.pallas import tpu as pltpu

F32 = jnp.float32
BF16 = jnp.bfloat16

CHUNK = 64
HEAD_DIM = 128
N_HEADS = 8
SSM_GROUP = 16
SSM_STATE = 64
SSM_T = 16
ROPE_THETA = 10000.0
Q_SCALE = HEAD_DIM ** -0.5 * math.log2(math.e)
EPS = 1e-6
NEG = -1e30

LANES = 128
SUBLANES = 8
GROUPS_PER_TILE = LANES // SSM_GROUP
CHUNK_TILE = 128
ROW_TILE = 512
STRIP = 32
VMEM_LIMIT = 56 * 1024 * 1024
KV_PROJ_VMEM_LIMIT = 58 * 1024 * 1024

_NT = (((1,), (1,)), ((), ()))


def _params(*sem, vmem_limit=VMEM_LIMIT):
    return pltpu.CompilerParams(dimension_semantics=sem, vmem_limit_bytes=vmem_limit)


def _resident(shape, index_map):
    return pl.BlockSpec(shape, index_map, pipeline_mode=pl.Buffered(1))


def _rms_scale(x, g):
    ms = jnp.mean(x * x, axis=-1, keepdims=True)
    return x * lax.rsqrt(ms + EPS) * g


def _staged_weight_spec(d, tn, nb):
    return pl.BlockSpec((d, tn), lambda r, n: (0, jnp.where(r == 0, n, nb - 1)),
                        pipeline_mode=pl.Buffered(1))


def _row_block(r):
    return jnp.maximum(r - 1, 0)


def _col_block(r, n):
    return jnp.where(r == 0, 0, n)


def _a_in_kernel(xp_ref, xs_ref, g_ref, w_ref, o_ref, xn_ref, wb_ref, *, mp, nb):
    m, n = pl.program_id(0) - 1, pl.program_id(1)

    @pl.when(m < 0)
    def _():
        wb_ref[n] = w_ref[...].astype(BF16)

    @pl.when((n == 0) & (m >= 0) & (m < mp))
    def _():
        xn_ref[...] = _rms_scale(xp_ref[...], g_ref[...]).astype(BF16)

    @pl.when((n == 0) & (m == mp))
    def _():
        xn_ref[...] = _rms_scale(xs_ref[...], g_ref[...]).astype(BF16)

    for c in range(nb):
        @pl.when((m >= 0) & (m <= mp) & (n == c))
        def _():
            o_ref[...] = jnp.dot(xn_ref[...], wb_ref[c], preferred_element_type=F32)

    @pl.when(m > mp)
    def _():
        o_ref[...] = jnp.zeros(o_ref.shape, F32)


def _a_in_proj(xp, xs, g, w, rows_pad, tn):
    tm = ROW_TILE
    d = g.shape[-1]
    n_out = w.shape[1]
    mp = xp.shape[0] // tm
    nb = n_out // tn
    assert xs.shape[0] == tm and xp.shape[0] % tm == 0 and rows_pad % tm == 0
    return pl.pallas_call(
        functools.partial(_a_in_kernel, mp=mp, nb=nb),
        out_shape=jax.ShapeDtypeStruct((rows_pad, n_out), F32),
        grid=(1 + rows_pad // tm, nb),
        in_specs=[
            pl.BlockSpec((tm, d), lambda r, n: (jnp.minimum(_row_block(r), mp - 1), 0)),
            _resident((tm, d), lambda r, n: (0, 0)),
            _resident((1, d), lambda r, n: (0, 0)),
            _staged_weight_spec(d, tn, nb),
        ],
        out_specs=pl.BlockSpec((tm, tn), lambda r, n: (_row_block(r), _col_block(r, n))),
        scratch_shapes=[pltpu.VMEM((tm, d), BF16), pltpu.VMEM((nb, d, tn), BF16)],
        compiler_params=_params("arbitrary", "arbitrary"),
        name="a_in_proj",
    )(xp, xs, g, w)


PREP_PAIRS = LANES // SSM_T


def _ssm_prep_kernel(lrx_ref, lix_ref, dtx_ref, bre_ref, bim_ref, lrr_ref, lir_ref, dtr_ref,
                     cre_ref, cim_ref, mt_ref, bend_ref, cin_ref, a16r_ref, a16i_ref):
    p = SSM_STATE
    p2 = 2 * p
    w = SSM_T * SSM_GROUP
    hi = lax.Precision.HIGHEST

    lr, li = lrx_ref[...], lix_ref[...]
    dt = jnp.exp(dtx_ref[...])
    zr, zi = lr * dt, li * dt
    mag = jnp.exp(zr)
    n_re, n_im = mag * jnp.cos(zi) - 1.0, mag * jnp.sin(zi)
    den = lr * lr + li * li
    cf_re = (n_re * lr + n_im * li) / den
    cf_im = (n_im * lr - n_re * li) / den
    lane = lax.broadcasted_iota(jnp.int32, (p2, LANES), 1)
    e_end = (SSM_T - 1 - lane % SSM_T).astype(F32)
    pm = jnp.exp(zr * e_end)
    pw_re, pw_im = pm * jnp.cos(zi * e_end), pm * jnp.sin(zi * e_end)
    cp_re = cf_re * pw_re - cf_im * pw_im
    cp_im = cf_re * pw_im + cf_im * pw_re
    coef = jnp.concatenate([cp_re, cp_im, cf_re, cf_im], axis=0)
    sel_row = lax.broadcasted_iota(jnp.int32, (LANES, w), 0)
    sel_col = lax.broadcasted_iota(jnp.int32, (LANES, w), 1)

    tau = lax.broadcasted_iota(jnp.int32, (SSM_T + SUBLANES, p2), 0).astype(F32)
    lane2 = lax.broadcasted_iota(jnp.int32, (w, p2), 1)
    sblk = lax.broadcasted_iota(jnp.int32, (w, w), 1) // SSM_GROUP

    for j in range(PREP_PAIRS):
        own_pair = (sel_row // SSM_T) == j
        spread = jnp.where(own_pair & (sel_row % SSM_T == sel_col // SSM_GROUP), 1.0, 0.0)
        cpx = jnp.dot(coef[:2 * p2], spread, precision=hi, preferred_element_type=F32)
        cfx = jnp.broadcast_to(coef[2 * p2:, j * SSM_T:j * SSM_T + 1], (2 * p2, w))
        b_re, b_im = jnp.tile(bre_ref[j], (1, SSM_T)), jnp.tile(bim_ref[j], (1, SSM_T))
        end_re = cpx[:p2] * b_re - cpx[p2:] * b_im
        end_im = cpx[:p2] * b_im + cpx[p2:] * b_re
        bb_re = cfx[:p2] * b_re - cfx[p2:] * b_im
        bb_im = cfx[:p2] * b_im + cfx[p2:] * b_re
        for r in range(2):
            sl = slice(r * p, (r + 1) * p)
            bend_ref[2 * j + r] = jnp.concatenate([end_re[sl], end_im[sl]], axis=0).astype(BF16)
        bst = jnp.concatenate([bb_re, bb_im], axis=0)

        lr2, li2 = lrr_ref[j], lir_ref[j]
        dt2 = jnp.exp(dtr_ref[j])
        zr2, zi2 = lr2 * dt2, li2 * dt2
        qm = jnp.exp(zr2 * tau)
        q_re, q_im = qm * jnp.cos(zi2 * tau), qm * jnp.sin(zi2 * tau)
        c_re, c_im = cre_ref[j], cim_ref[j]

        def c_times_powers(first):
            re = [c_re * q_re[t:t + 1] - c_im * q_im[t:t + 1] for t in range(first, first + SSM_T)]
            im = [c_re * q_im[t:t + 1] + c_im * q_re[t:t + 1] for t in range(first, first + SSM_T)]
            return jnp.concatenate(re, axis=0), jnp.concatenate(im, axis=0)

        g_re, g_im = c_times_powers(0)
        ci_re, ci_im = c_times_powers(1)
        for r in range(2):
            own = (lane2 // p) == r
            lhs = jnp.concatenate([jnp.where(own, g_re, 0.0), jnp.where(own, -g_im, 0.0)], axis=1)
            gen = jnp.dot(lhs, bst, preferred_element_type=F32)
            k = 0
            while (SSM_GROUP << k) < w:
                sh = SSM_GROUP << k
                shifted = jnp.concatenate([jnp.zeros((sh, w), F32), gen[:w - sh]], axis=0)
                gen = jnp.where(((sblk >> k) & 1) == 1, shifted, gen)
                k += 1
            mt_ref[2 * j + r] = gen.astype(BF16)
            cin_ref[2 * j + r] = jnp.concatenate(
                [jnp.where(own, ci_re, 0.0), jnp.where(own, -ci_im, 0.0)], axis=1).astype(BF16)
        a16r_ref[j] = q_re[SSM_T:SSM_T + 1]
        a16i_ref[j] = q_im[SSM_T:SSM_T + 1]


def _ssm_prep(lam_re, lam_im, log_dt, b_re, b_im, c_re, c_im):
    g, p = lam_re.shape
    gp = g // 2
    p2 = 2 * p
    w = SSM_T * SSM_GROUP
    pp = PREP_PAIRS
    assert gp % pp == 0
    colx = lambda a: jnp.repeat(a.reshape(gp, p2).T, SSM_T, axis=1)
    row = lambda a: a.reshape(gp, 1, p2)
    dt_full = jnp.broadcast_to(log_dt[:, None], (g, p))
    b_t = lambda a: a.reshape(gp, p2, SSM_GROUP)
    c_pair = lambda a: a.reshape(gp, 2, SSM_GROUP, p).transpose(0, 2, 1, 3).reshape(gp, SSM_GROUP, p2)
    xspec = pl.BlockSpec((p2, LANES), lambda i: (0, i))
    rowspec = pl.BlockSpec((pp, 1, p2), lambda i: (i, 0, 0))
    bspec = pl.BlockSpec((pp, p2, SSM_GROUP), lambda i: (i, 0, 0))
    cspec = pl.BlockSpec((pp, SSM_GROUP, p2), lambda i: (i, 0, 0))
    mt, bend, cin, a16r, a16i = pl.pallas_call(
        _ssm_prep_kernel,
        out_shape=[
            jax.ShapeDtypeStruct((g, w, w), BF16),
            jax.ShapeDtypeStruct((g, p2, w), BF16),
            jax.ShapeDtypeStruct((g, w, 2 * p2), BF16),
            jax.ShapeDtypeStruct((gp, 1, p2), F32),
            jax.ShapeDtypeStruct((gp, 1, p2), F32),
        ],
        grid=(gp // pp,),
        in_specs=[xspec, xspec, xspec, bspec, bspec, rowspec, rowspec, rowspec, cspec, cspec],
        out_specs=[
            pl.BlockSpec((2 * pp, w, w), lambda i: (i, 0, 0)),
            pl.BlockSpec((2 * pp, p2, w), lambda i: (i, 0, 0)),
            pl.BlockSpec((2 * pp, w, 2 * p2), lambda i: (i, 0, 0)),
            rowspec, rowspec,
        ],
        compiler_params=_params("arbitrary"),
        name="ssm_prep",
    )(colx(lam_re), colx(lam_im), colx(dt_full), b_t(b_re), b_t(b_im),
      row(lam_re), row(lam_im), row(dt_full), c_pair(c_re), c_pair(c_im))
    return mt, bend, cin, a16r.reshape(1, g * p), a16i.reshape(1, g * p)


def _step_rows(s):
    return pl.ds(s, CHUNK_TILE, stride=SSM_T)


def _build_ut(u_ref, ut_ref):
    for s in range(SSM_T):
        xt = u_ref[_step_rows(s), :].T.astype(BF16)
        for gl in range(GROUPS_PER_TILE):
            ut_ref[gl, s * SSM_GROUP:(s + 1) * SSM_GROUP, :] = xt[gl * SSM_GROUP:(gl + 1) * SSM_GROUP, :]


STATE_TILES = 2


def _ssm_state_kernel(*refs):
    u_refs = refs[:STATE_TILES]
    bend_ref, sre_ref, sim_ref = refs[STATE_TILES:STATE_TILES + 3]
    ut_refs = refs[STATE_TILES + 3:]
    p = SSM_STATE
    for tile in range(STATE_TILES):
        _build_ut(u_refs[tile], ut_refs[tile])
        for pr in range(GROUPS_PER_TILE // 2):
            st = [jnp.dot(bend_ref[tile * GROUPS_PER_TILE + 2 * pr + r], ut_refs[tile][2 * pr + r],
                          preferred_element_type=F32)
                  for r in range(2)]
            cols = slice((tile * GROUPS_PER_TILE // 2 + pr) * 2 * p,
                         (tile * GROUPS_PER_TILE // 2 + pr + 1) * 2 * p)
            sre_ref[:, cols] = jnp.concatenate([st[0][:p], st[1][:p]], axis=0).T
            sim_ref[:, cols] = jnp.concatenate([st[0][p:], st[1][p:]], axis=0).T


def _ssm_state_contrib(uz, bend, jpad):
    g, p2, w = bend.shape
    gstep = STATE_TILES * GROUPS_PER_TILE
    assert g % gstep == 0
    sw = gstep * SSM_STATE
    out = jax.ShapeDtypeStruct((jpad, g * SSM_STATE), F32)
    ospec = pl.BlockSpec((CHUNK_TILE, sw), lambda q, j: (j, q))
    utile = lambda t: pl.BlockSpec((CHUNK_TILE * SSM_T, LANES), lambda q, j: (j, STATE_TILES * q + t))
    return pl.pallas_call(
        _ssm_state_kernel,
        out_shape=[out, out],
        grid=(g // gstep, jpad // CHUNK_TILE),
        in_specs=[*[utile(t) for t in range(STATE_TILES)],
                  pl.BlockSpec((gstep, p2, w), lambda q, j: (q, 0, 0))],
        out_specs=[ospec, ospec],
        scratch_shapes=[pltpu.VMEM((GROUPS_PER_TILE, w, CHUNK_TILE), BF16) for _ in range(STATE_TILES)],
        compiler_params=_params("arbitrary", "arbitrary"),
        name="ssm_state_contrib",
    )(*[uz] * STATE_TILES, bend)


def _ssm_scan_kernel(sre_ref, sim_ref, ar_ref, ai_ref, h0r_ref, h0i_ref,
                     hpr_ref, hpi_ref, pr_ref, pi_ref, sr_ref, si_ref, hr_sc, hi_sc, *, n_prompt):
    i = pl.program_id(0)
    rb = sre_ref.shape[0]
    ar, ai = ar_ref[...], ai_ref[...]

    @pl.when(i == 0)
    def _():
        hr_sc[...] = jnp.zeros_like(hr_sc)
        hi_sc[...] = jnp.zeros_like(hi_sc)

    @pl.when(i < n_prompt)
    def _():
        def body(j, c):
            row = pl.ds(j, 1)
            hr, hi = hr_sc[...], hi_sc[...]
            hpr_ref[row, :] = hr
            hpi_ref[row, :] = hi
            hr_sc[...] = ar * hr - ai * hi + sre_ref[row, :]
            hi_sc[...] = ar * hi + ai * hr + sim_ref[row, :]
            return c

        lax.fori_loop(0, rb, body, 0)

    @pl.when(i == n_prompt - 1)
    def _():
        pr_ref[...] = hr_sc[...]
        pi_ref[...] = hi_sc[...]

    @pl.when(i == n_prompt)
    def _():
        h0r, h0i = h0r_ref[...], h0i_ref[...]
        hpr_ref[...] = h0r
        hpi_ref[...] = h0i
        sr_ref[...] = ar * h0r - ai * h0i + sre_ref[...]
        si_ref[...] = ar * h0i + ai * h0r + sim_ref[...]

    @pl.when(i > n_prompt)
    def _():
        hpr_ref[...] = jnp.zeros_like(hpr_ref)
        hpi_ref[...] = jnp.zeros_like(hpi_ref)


def _ssm_scan(s_re, s_im, a16r, a16i, h0r, h0i, jp):
    jpad, n = s_re.shape
    rb = h0r.shape[0]
    assert jp % rb == 0 and jpad % rb == 0
    n_prompt = jp // rb
    rows = pl.BlockSpec((rb, n), lambda i: (i, 0))
    const1 = pl.BlockSpec((1, n), lambda i: (0, 0))
    constb = pl.BlockSpec((rb, n), lambda i: (0, 0))
    big = jax.ShapeDtypeStruct((jpad, n), F32)
    one = jax.ShapeDtypeStruct((1, n), F32)
    bat = jax.ShapeDtypeStruct((rb, n), F32)
    return pl.pallas_call(
        functools.partial(_ssm_scan_kernel, n_prompt=n_prompt),
        out_shape=[big, big, one, one, bat, bat],
        grid=(jpad // rb,),
        in_specs=[rows, rows, const1, const1, constb, constb],
        out_specs=[rows, rows, const1, const1, constb, constb],
        scratch_shapes=[pltpu.VMEM((1, n), F32), pltpu.VMEM((1, n), F32)],
        compiler_params=_params("arbitrary"),
        name="ssm_scan",
    )(s_re, s_im, a16r, a16i, h0r, h0i)


def _ssm_out_kernel(u_ref, mt_ref, cin_ref, hpr_ref, hpi_ref, d_ref, o_ref, ut_ref, uf_ref, yt_ref):
    for s in range(SSM_T):
        xt = u_ref[_step_rows(s), :].T
        xb = xt.astype(BF16)
        for gl in range(GROUPS_PER_TILE):
            rows = slice(gl * SSM_GROUP, (gl + 1) * SSM_GROUP)
            ut_ref[gl, s * SSM_GROUP:(s + 1) * SSM_GROUP, :] = xb[rows, :]
            uf_ref[gl, s * SSM_GROUP:(s + 1) * SSM_GROUP, :] = xt[rows, :]
    p2 = 2 * SSM_STATE
    for gl in range(GROUPS_PER_TILE):
        pr = gl // 2
        hp = jnp.concatenate([hpr_ref[:, pr * p2:(pr + 1) * p2], hpi_ref[:, pr * p2:(pr + 1) * p2]],
                             axis=1).astype(BF16)
        yt = jnp.dot(mt_ref[gl], ut_ref[gl], preferred_element_type=F32)
        yt = yt + lax.dot_general(cin_ref[gl], hp, _NT, preferred_element_type=F32)
        yt = jax.nn.gelu(yt + d_ref[gl] * uf_ref[gl])
        for t in range(SSM_T):
            yt_ref[t, gl * SSM_GROUP:(gl + 1) * SSM_GROUP, :] = yt[t * SSM_GROUP:(t + 1) * SSM_GROUP, :]
    for t in range(SSM_T):
        o_ref[_step_rows(t), :] = yt_ref[t].T


def _ssm_output(uz, mt, cin, hp_re, hp_im, d_skip, jpad):
    g, w, _ = mt.shape
    w2 = cin.shape[-1]
    ntile = g // GROUPS_PER_TILE
    sw = GROUPS_PER_TILE * SSM_STATE
    tok = pl.BlockSpec((CHUNK_TILE * SSM_T, LANES), lambda q, j: (j, q))
    hspec = pl.BlockSpec((CHUNK_TILE, sw), lambda q, j: (j, q))
    return pl.pallas_call(
        _ssm_out_kernel,
        out_shape=jax.ShapeDtypeStruct((jpad * SSM_T, g * SSM_GROUP), F32),
        grid=(ntile, jpad // CHUNK_TILE),
        in_specs=[
            tok,
            pl.BlockSpec((GROUPS_PER_TILE, w, w), lambda q, j: (q, 0, 0)),
            pl.BlockSpec((GROUPS_PER_TILE, w, w2), lambda q, j: (q, 0, 0)),
            hspec, hspec,
            pl.BlockSpec((GROUPS_PER_TILE, w, 1), lambda q, j: (q, 0, 0)),
        ],
        out_specs=tok,
        scratch_shapes=[
            pltpu.VMEM((GROUPS_PER_TILE, w, CHUNK_TILE), BF16),
            pltpu.VMEM((GROUPS_PER_TILE, w, CHUNK_TILE), F32),
            pltpu.VMEM((SSM_T, LANES, CHUNK_TILE), F32),
        ],
        compiler_params=_params("arbitrary", "arbitrary"),
        name="ssm_output",
    )(uz, mt, cin, hp_re, hp_im, d_skip)


def _a_glu_kernel(y_ref, z_ref, w_ref, o_ref, wb_ref):
    r = pl.program_id(0)

    @pl.when(r == 0)
    def _():
        wb_ref[...] = w_ref[...].astype(BF16)

    @pl.when(r > 0)
    def _():
        y = y_ref[...]
        gate = jnp.dot(y.astype(BF16), wb_ref[...], preferred_element_type=F32)
        y2 = y * jax.nn.sigmoid(gate)
        o_ref[...] = (y2 * jax.nn.silu(z_ref[...])).astype(BF16)


def _a_glu(yg, uz, w_glu, rows):
    tm = ROW_TILE
    e = yg.shape[1]
    return pl.pallas_call(
        _a_glu_kernel,
        out_shape=jax.ShapeDtypeStruct((rows, e), BF16),
        grid=(1 + rows // tm,),
        in_specs=[
            pl.BlockSpec((tm, e), lambda r: (_row_block(r), 0)),
            pl.BlockSpec((tm, e), lambda r: (_row_block(r), 1)),
            _resident((e, e), lambda r: (0, 0)),
        ],
        out_specs=pl.BlockSpec((tm, e), lambda r: (_row_block(r), 0)),
        scratch_shapes=[pltpu.VMEM((e, e), BF16)],
        compiler_params=_params("arbitrary"),
        name="a_glu",
    )(yg, uz, w_glu)


def _a_out_kernel(o_ref, w_ref, xp_ref, xs_ref, x1_ref, wb_ref, *, mp):
    m = pl.program_id(0) - 1

    @pl.when(m < 0)
    def _():
        wb_ref[...] = w_ref[...].astype(BF16)

    @pl.when((m >= 0) & (m < mp))
    def _():
        x1_ref[...] = xp_ref[...] + jnp.dot(o_ref[...], wb_ref[...], preferred_element_type=F32)

    @pl.when(m == mp)
    def _():
        x1_ref[...] = xs_ref[...] + jnp.dot(o_ref[...], wb_ref[...], preferred_element_type=F32)


def _a_out_proj(o, w_out, xp, xs):
    tm = ROW_TILE
    rows, e = o.shape
    d = w_out.shape[1]
    mp = xp.shape[0] // tm
    return pl.pallas_call(
        functools.partial(_a_out_kernel, mp=mp),
        out_shape=jax.ShapeDtypeStruct((rows, d), F32),
        grid=(1 + rows // tm,),
        in_specs=[
            pl.BlockSpec((tm, e), lambda r: (_row_block(r), 0)),
            _resident((e, d), lambda r: (0, 0)),
            pl.BlockSpec((tm, d), lambda r: (jnp.minimum(_row_block(r), mp - 1), 0)),
            _resident((tm, d), lambda r: (0, 0)),
        ],
        out_specs=pl.BlockSpec((tm, d), lambda r: (_row_block(r), 0)),
        scratch_shapes=[pltpu.VMEM((e, d), BF16)],
        compiler_params=_params("arbitrary"),
        name="a_out_proj",
    )(o, w_out, xp, xs)


def _rope(x, cos, sin):
    outs = []
    for c in range(x.shape[1] // HEAD_DIM):
        xc = x[:, c * HEAD_DIM:(c + 1) * HEAD_DIM]
        outs.append(xc * cos + pltpu.roll(xc, HEAD_DIM // 2, axis=1) * sin)
    return jnp.concatenate(outs, axis=1) if len(outs) > 1 else outs[0]


def _rope_block(co_ref, so_ref, cb_ref, sb_ref):
    co, so = co_ref[...], so_ref[...]
    cb, sb = cb_ref[...], sb_ref[...]
    lane = lax.broadcasted_iota(jnp.int32, (1, HEAD_DIM), 1)
    sign = jnp.where(lane < HEAD_DIM // 2, -1.0, 1.0)
    return cb * co - sb * so, (sb * co + cb * so) * sign


def _store_heads(o_ref, val, tm):
    for h in range(N_HEADS):
        o_ref[pl.ds(h, tm, stride=N_HEADS), :] = val[:, h * HEAD_DIM:(h + 1) * HEAD_DIM]


def _prompt_col(r, mp, col, last):
    return jnp.where(_row_block(r) >= mp, last, col)


def _sample_col(r, mp, col):
    return jnp.where(_row_block(r) == mp, col, 0)


def _kv_proj_kernel(x_ref, g_ref, w_ref, co_ref, so_ref, cb_ref, sb_ref, kp_ref, vp_ref, cbp_ref,
                    ks_ref, vs_ref, cbs_ref, xn_ref, wb_ref, *, tm, mp):
    m, n = pl.program_id(0) - 1, pl.program_id(1)
    hpb = w_ref.shape[1] // (2 * HEAD_DIM)

    for nn in range(4):
        @pl.when((m < 0) & (n == nn))
        def _():
            for hl in range(hpb):
                h = hpb * (nn % 2) + hl
                for half in range(2):
                    src = (2 * hl + half) * HEAD_DIM
                    wb_ref[2 * (nn // 2) + half, :, h * HEAD_DIM:(h + 1) * HEAD_DIM] = (
                        w_ref[:, src:src + HEAD_DIM].astype(BF16))

    @pl.when((m >= 0) & (n == 0))
    def _():
        xn_ref[...] = _rms_scale(x_ref[...], g_ref[...]).astype(BF16)

    def column(c, k_ref, v_ref, copy_ref):
        acc = jnp.dot(xn_ref[...], wb_ref[c], preferred_element_type=F32)
        if c < 2:
            acc = _rope(acc, *_rope_block(co_ref, so_ref, cb_ref, sb_ref))
        _store_heads(k_ref if c < 2 else v_ref, acc, tm)
        copy_ref[...] = acc.astype(BF16)

    for c in range(4):
        @pl.when((m >= 0) & (m < mp) & (n == c))
        def _():
            column(c, kp_ref, vp_ref, cbp_ref)

        @pl.when((m == mp) & (n == c))
        def _():
            column(c, ks_ref, vs_ref, cbs_ref)


def _kv_proj(x1, rows_p, g, w_kv, rope):
    tm = ROW_TILE
    rows, d = x1.shape
    tn = N_HEADS * HEAD_DIM
    mp = rows_p // tm
    assert w_kv.shape[1] == 4 * tn and rows == rows_p + tm and rows_p % tm == 0
    out4 = lambda nrows: jax.ShapeDtypeStruct((nrows * N_HEADS, 2 * HEAD_DIM), F32)
    outb = lambda nrows: jax.ShapeDtypeStruct((nrows, 4 * tn), BF16)
    prow = lambda r: jnp.minimum(_row_block(r), mp - 1)
    col = _col_block
    return pl.pallas_call(
        functools.partial(_kv_proj_kernel, tm=tm, mp=mp),
        out_shape=[out4(rows_p), out4(rows_p), outb(rows_p), out4(tm), out4(tm), outb(tm)],
        grid=(1 + rows // tm, 4),
        in_specs=[
            pl.BlockSpec((tm, d), lambda r, n: (_row_block(r), 0)),
            _resident((1, d), lambda r, n: (0, 0)),
            _staged_weight_spec(d, tn, 4),
            *_rope_specs(tm, mp),
        ],
        out_specs=[
            pl.BlockSpec((tm * N_HEADS, HEAD_DIM),
                         lambda r, n: (prow(r), _prompt_col(r, mp, jnp.minimum(col(r, n), 1), 1))),
            pl.BlockSpec((tm * N_HEADS, HEAD_DIM),
                         lambda r, n: (prow(r), _prompt_col(r, mp, jnp.maximum(col(r, n) - 2, 0), 1))),
            pl.BlockSpec((tm, tn), lambda r, n: (prow(r), _prompt_col(r, mp, col(r, n), 3))),
            pl.BlockSpec((tm * N_HEADS, HEAD_DIM),
                         lambda r, n: (0, _sample_col(r, mp, jnp.minimum(col(r, n), 1)))),
            pl.BlockSpec((tm * N_HEADS, HEAD_DIM),
                         lambda r, n: (0, _sample_col(r, mp, jnp.maximum(col(r, n) - 2, 0)))),
            pl.BlockSpec((tm, tn), lambda r, n: (0, _sample_col(r, mp, col(r, n)))),
        ],
        scratch_shapes=[pltpu.VMEM((tm, d), BF16), pltpu.VMEM((4, d, tn), BF16)],
        compiler_params=_params("arbitrary", "arbitrary", vmem_limit=KV_PROJ_VMEM_LIMIT),
        name="kv_proj",
    )(x1, g, w_kv, *rope)


def _qz_proj_kernel(x_ref, g_ref, w_ref, co_ref, so_ref, cb_ref, sb_ref, qp_ref, zp_ref, qs_ref, zs_ref,
                    xn_ref, wb_ref, *, nq, mp):
    m, n = pl.program_id(0) - 1, pl.program_id(1)

    @pl.when(m < 0)
    def _():
        wb_ref[n] = w_ref[...].astype(BF16)

    @pl.when((m >= 0) & (n == 0))
    def _():
        xn_ref[...] = _rms_scale(x_ref[...], g_ref[...]).astype(BF16)

    def column(c, q_ref, z_ref):
        acc = jnp.dot(xn_ref[...], wb_ref[c], preferred_element_type=F32)
        if c < nq:
            rot = _rope(acc, *_rope_block(co_ref, so_ref, cb_ref, sb_ref))
            q_ref[...] = (rot * Q_SCALE).astype(BF16)
        else:
            z_ref[...] = acc

    for c in range(2 * nq):
        @pl.when((m >= 0) & (m < mp) & (n == c))
        def _():
            column(c, qp_ref, zp_ref)

        @pl.when((m == mp) & (n == c))
        def _():
            column(c, qs_ref, zs_ref)


def _qz_proj(x1, rows_p, g, w, rope, tn):
    tm = ROW_TILE
    rows, d = x1.shape
    half = w.shape[1] // 2
    nq = half // tn
    mp = rows_p // tm
    assert rows == rows_p + tm and rows_p % tm == 0
    prow = lambda r: jnp.minimum(_row_block(r), mp - 1)
    qcol = lambda r, n: jnp.minimum(_col_block(r, n), nq - 1)
    zcol = lambda r, n: jnp.maximum(_col_block(r, n) - nq, 0)
    return pl.pallas_call(
        functools.partial(_qz_proj_kernel, nq=nq, mp=mp),
        out_shape=[jax.ShapeDtypeStruct((rows_p, half), BF16), jax.ShapeDtypeStruct((rows_p, half), F32),
                   jax.ShapeDtypeStruct((tm, half), BF16), jax.ShapeDtypeStruct((tm, half), F32)],
        grid=(1 + rows // tm, 2 * nq),
        in_specs=[
            pl.BlockSpec((tm, d), lambda r, n: (_row_block(r), 0)),
            _resident((1, d), lambda r, n: (0, 0)),
            _staged_weight_spec(d, tn, 2 * nq),
            *_rope_specs(tm, mp),
        ],
        out_specs=[
            pl.BlockSpec((tm, tn), lambda r, n: (prow(r), _prompt_col(r, mp, qcol(r, n), nq - 1))),
            pl.BlockSpec((tm, tn), lambda r, n: (prow(r), _prompt_col(r, mp, zcol(r, n), nq - 1))),
            pl.BlockSpec((tm, tn), lambda r, n: (0, _sample_col(r, mp, qcol(r, n)))),
            pl.BlockSpec((tm, tn), lambda r, n: (0, _sample_col(r, mp, zcol(r, n)))),
        ],
        scratch_shapes=[pltpu.VMEM((tm, d), BF16), pltpu.VMEM((2 * nq, d, tn), BF16)],
        compiler_params=_params("arbitrary", "arbitrary"),
        name="qz_proj",
    )(x1, g, w, *rope)


def _rope_specs(tm, mp):
    off = pl.BlockSpec((None, tm, HEAD_DIM), lambda r, n: (jnp.where(_row_block(r) == mp, 1, 0), 0, 0))
    base = pl.BlockSpec((None, 1, HEAD_DIM), lambda r, n: (_row_block(r), 0, 0))
    return [off, off, base, base]


def _rope_tables(rows_p, past_len, t_s, tm):
    inv_freq = ROPE_THETA ** (-jnp.arange(0, HEAD_DIM, 2, dtype=F32) / HEAD_DIM)
    full = lambda a: jnp.concatenate([a, a], axis=-1)
    rows = jnp.arange(tm, dtype=jnp.int32)
    off = jnp.stack([rows, rows % t_s]).astype(F32)[..., None] * inv_freq
    base = jnp.concatenate([jnp.arange(0, rows_p, tm, dtype=jnp.int32),
                            jnp.full((1,), past_len, jnp.int32)]).astype(F32)[:, None, None] * inv_freq
    return full(jnp.cos(off)), full(jnp.sin(off)), full(jnp.cos(base)), full(jnp.sin(base))


def _diff_lambda(lq1_ref, lk1_ref, lq2_ref, lk2_ref, lambda_init):
    s1 = jnp.sum(lq1_ref[...] * lk1_ref[...], axis=-1, keepdims=True)
    s2 = jnp.sum(lq2_ref[...] * lk2_ref[...], axis=-1, keepdims=True)
    return jnp.exp(s1) - jnp.exp(s2) + lambda_init


def _attn_finish(a1, l1, a2, l2, lam, subln, z, lambda_init):
    o = a1 / l1 - lam * (a2 / l2)
    o = _rms_scale(o, subln) * (1.0 - lambda_init)
    return (o * jax.nn.silu(z)).astype(BF16)


def _attn_prompt_kernel(q_ref, k1_ref, k2_ref, v1_ref, v2_ref, z_ref,
                        lq1_ref, lk1_ref, lq2_ref, lk2_ref, sub_ref,
                        o_ref, m_sc, l_sc, al_sc, a_sc, *sp_scs, tq, ts, tk, lambda_init):
    qi = pl.program_id(1)
    k_refs = (k1_ref, k2_ref)
    nsub = tq // ts
    s_scs, p_scs = sp_scs[:2 * nsub], sp_scs[2 * nsub:]
    m_sc[...] = jnp.full(m_sc.shape, NEG, F32)
    l_sc[...] = jnp.zeros(l_sc.shape, F32)
    a_sc[...] = jnp.zeros(a_sc.shape, F32)

    def softmax_strips(sub, n, col0):
        for r in range(ts // STRIP):
            row0 = sub * ts + r * STRIP
            lr = slice(r * STRIP, (r + 1) * STRIP)
            gr = slice(row0, row0 + STRIP)
            visible = tk if col0 is None else min(tk, (row0 // CHUNK + 1) * CHUNK - col0)
            assert visible > 0
            ncv = -(-visible // LANES)
            s = s_scs[2 * sub + n][lr, 0:ncv * LANES]
            if visible < ncv * LANES:
                col = lax.broadcasted_iota(jnp.int32, s.shape, 1)
                s = jnp.where(col < visible, s, NEG)
            cols = [s[:, c * LANES:(c + 1) * LANES] for c in range(ncv)]
            m_cur = jnp.max(functools.reduce(jnp.maximum, cols), axis=-1, keepdims=True)
            m_old = m_sc[n, gr, :]
            m_new = jnp.maximum(m_old, m_cur)
            ps = [jnp.exp2(c - m_new) for c in cols]
            l_cur = jnp.sum(functools.reduce(jnp.add, ps), axis=-1, keepdims=True)
            alpha = jnp.exp2(m_old - m_new)
            l_sc[n, gr, :] = alpha * l_sc[n, gr, :] + l_cur
            m_sc[n, gr, :] = m_new
            al_sc[n, gr, :] = alpha
            for c in range(keys_seen(sub, col0) // LANES):
                pc = ps[c].astype(BF16) if c < ncv else jnp.zeros((STRIP, LANES), BF16)
                p_scs[2 * sub + n][lr, c * LANES:(c + 1) * LANES] = pc

    def keys_seen(sub, col0):
        return tk if col0 is None else min(tk, (sub + 1) * ts - col0)

    def block(kb, col0, subs):
        start = pl.multiple_of(kb * tk, tk)
        for sub in subs:
            rows = pl.ds(start, keys_seen(sub, col0))
            for n in range(2):
                qn = q_ref[sub * ts:(sub + 1) * ts, n * HEAD_DIM:(n + 1) * HEAD_DIM]
                s_scs[2 * sub + n][:, 0:rows.size] = lax.dot_general(
                    qn, k_refs[n][rows, :], _NT, preferred_element_type=F32)
        for sub in subs:
            gr = slice(sub * ts, (sub + 1) * ts)
            rows = pl.ds(start, keys_seen(sub, col0))
            vblk = jnp.concatenate([v1_ref[rows, :], v2_ref[rows, :]], axis=1)
            for n in range(2):
                softmax_strips(sub, n, col0)
                pv = jnp.dot(p_scs[2 * sub + n][:, 0:rows.size], vblk, preferred_element_type=F32)
                alpha = al_sc[n, gr, :]
                a_sc[n, gr, :] = a_sc[n, gr, :] * jnp.concatenate([alpha, alpha], axis=1) + pv

    nd = tq // tk

    def body(kb, c):
        block(kb, None, range(nsub))
        return c

    lax.fori_loop(0, qi * nd, body, 0)
    for d in range(nd):
        block(qi * nd + d, d * tk, [sub for sub in range(nsub) if (sub + 1) * ts > d * tk])

    lam = _diff_lambda(lq1_ref, lk1_ref, lq2_ref, lk2_ref, lambda_init)
    l1 = jnp.concatenate([l_sc[0], l_sc[0]], axis=1)
    l2 = jnp.concatenate([l_sc[1], l_sc[1]], axis=1)
    o_ref[...] = _attn_finish(a_sc[0], l1, a_sc[1], l2, lam, sub_ref[...], z_ref[...], lambda_init)


def _attn_prompt(qb, z32, kvb, lq1, lk1, lq2, lk2, subln, lambda_init, tq, ts, tk):
    rows = qb.shape[0]
    hw = 2 * HEAD_DIM
    nsub = tq // ts
    assert tq % tk == 0 and tk % ts == 0 and ts % STRIP == 0 and CHUNK % STRIP == 0
    vec = pl.BlockSpec((1, HEAD_DIM), lambda h, i: (0, 0))
    kcol = lambda c: pl.BlockSpec((rows, HEAD_DIM), lambda h, i: (0, c * N_HEADS + h))
    stat = pltpu.VMEM((2, tq, LANES), F32)
    return pl.pallas_call(
        functools.partial(_attn_prompt_kernel, tq=tq, ts=ts, tk=tk, lambda_init=lambda_init),
        out_shape=jax.ShapeDtypeStruct((rows, N_HEADS * hw), BF16),
        grid=(N_HEADS, rows // tq),
        in_specs=[
            pl.BlockSpec((tq, hw), lambda h, i: (i, h)),
            kcol(0), kcol(1), kcol(2), kcol(3),
            pl.BlockSpec((tq, hw), lambda h, i: (i, h)),
            vec, vec, vec, vec,
            pl.BlockSpec((1, hw), lambda h, i: (0, 0)),
        ],
        out_specs=pl.BlockSpec((tq, hw), lambda h, i: (i, h)),
        scratch_shapes=[
            stat, stat, stat,
            pltpu.VMEM((2, tq, hw), F32),
            *[pltpu.VMEM((ts, tk), F32) for _ in range(2 * nsub)],
            *[pltpu.VMEM((ts, tk), BF16) for _ in range(2 * nsub)],
        ],
        compiler_params=_params("arbitrary", "arbitrary"),
        name="attn_prompt",
    )(qb, kvb, kvb, kvb, kvb, z32, lq1, lk1, lq2, lk2, subln)


def _attn_sample_kernel(q_ref, ck1_ref, ck2_ref, cv1_ref, cv2_ref, kvn_ref, z_ref,
                        lq1_ref, lk1_ref, lq2_ref, lk2_ref, sub_ref,
                        o_ref, m_sc, l_sc, a_sc, s_sc, p_sc, *, tkv, nkb, past_len, lambda_init):
    kb = pl.program_id(1)
    t = q_ref.shape[0]
    hw = 2 * HEAD_DIM
    rows_all = 2 * N_HEADS * t
    strip = 4 * t
    ck_refs = (ck1_ref, ck2_ref)
    assert (past_len - 1) // CHUNK <= past_len // CHUNK

    @pl.when(kb == 0)
    def _():
        m_sc[...] = jnp.full(m_sc.shape, NEG, F32)
        l_sc[...] = jnp.zeros(l_sc.shape, F32)
        a_sc[...] = jnp.zeros(a_sc.shape, F32)

    col_blk = lambda c, h: slice((c * N_HEADS + h) * HEAD_DIM, (c * N_HEADS + h + 1) * HEAD_DIM)
    head_rows = lambda h: pl.ds(h, tkv, stride=N_HEADS)

    def update(with_new):
        ncols = tkv + (LANES if with_new else 0)
        if with_new:
            row = lax.broadcasted_iota(jnp.int32, (t, LANES), 0) + past_len
            col = lax.broadcasted_iota(jnp.int32, (t, LANES), 1)
            new_mask = (col < t) & (((col + past_len) // CHUNK) <= (row // CHUNK))
            pad_k = jnp.zeros((LANES - t, HEAD_DIM), BF16)
        for h in range(N_HEADS):
            for n in range(2):
                qn = q_ref[:, (2 * h + n) * HEAD_DIM:(2 * h + n + 1) * HEAD_DIM]
                rr = slice((2 * h + n) * t, (2 * h + n + 1) * t)
                k = ck_refs[n][head_rows(h), :].astype(BF16)
                s_sc[rr, 0:tkv] = lax.dot_general(qn, k, _NT, preferred_element_type=F32)
                if with_new:
                    kn = jnp.concatenate([kvn_ref[:, col_blk(n, h)], pad_k], axis=0)
                    sn = lax.dot_general(qn, kn, _NT, preferred_element_type=F32)
                    s_sc[rr, tkv:ncols] = jnp.where(new_mask, sn, NEG)
        for r in range(rows_all // strip):
            rr = slice(r * strip, (r + 1) * strip)
            cols = [s_sc[rr, c * LANES:(c + 1) * LANES] for c in range(ncols // LANES)]
            m_cur = jnp.max(functools.reduce(jnp.maximum, cols), axis=-1, keepdims=True)
            m_old = m_sc[rr, :]
            m_new = jnp.maximum(m_old, m_cur)
            ps = [jnp.exp2(c - m_new) for c in cols]
            l_cur = jnp.sum(functools.reduce(jnp.add, ps), axis=-1, keepdims=True)
            alpha = jnp.exp2(m_old - m_new)
            l_sc[rr, :] = alpha * l_sc[rr, :] + l_cur
            m_sc[rr, :] = m_new
            a_sc[rr, :] = a_sc[rr, :] * jnp.concatenate([alpha, alpha], axis=1)
            for c, pc in enumerate(ps):
                p_sc[rr, c * LANES:(c + 1) * LANES] = pc.astype(BF16)
        for h in range(N_HEADS):
            rr = slice(2 * h * t, (2 * h + 2) * t)
            v = jnp.concatenate([cv1_ref[head_rows(h), :], cv2_ref[head_rows(h), :]],
                                axis=1).astype(BF16)
            pv = jnp.dot(p_sc[rr, 0:tkv], v, preferred_element_type=F32)
            if with_new:
                vn = jnp.concatenate([kvn_ref[:, col_blk(2, h)], kvn_ref[:, col_blk(3, h)]], axis=1)
                vn = jnp.concatenate([vn, jnp.zeros((LANES - t, hw), BF16)], axis=0)
                pv = pv + jnp.dot(p_sc[rr, tkv:ncols], vn, preferred_element_type=F32)
            a_sc[rr, :] = a_sc[rr, :] + pv

    @pl.when(kb < nkb - 1)
    def _():
        update(False)

    @pl.when(kb == nkb - 1)
    def _():
        update(True)
        lam = _diff_lambda(lq1_ref, lk1_ref, lq2_ref, lk2_ref, lambda_init)
        sub = sub_ref[...]
        for h in range(N_HEADS):
            r1 = slice(2 * h * t, (2 * h + 1) * t)
            r2 = slice((2 * h + 1) * t, (2 * h + 2) * t)
            l1 = jnp.concatenate([l_sc[r1, :], l_sc[r1, :]], axis=1)
            l2 = jnp.concatenate([l_sc[r2, :], l_sc[r2, :]], axis=1)
            o_ref[:, h * hw:(h + 1) * hw] = _attn_finish(
                a_sc[r1, :], l1, a_sc[r2, :], l2, lam, sub, z_ref[:, h * hw:(h + 1) * hw], lambda_init)


def _attn_sample(qb, z32, kvb, cache_k2, cache_v2, lq1, lk1, lq2, lk2, subln, lambda_init, t, tkv):
    bsz, rows8, hw = cache_k2.shape
    past_len = rows8 // N_HEADS
    nkb = past_len // tkv
    rows_all = 2 * N_HEADS * t
    assert t <= LANES
    vec = pl.BlockSpec((1, HEAD_DIM), lambda b, k: (0, 0))
    cache = lambda c: pl.BlockSpec((None, tkv * N_HEADS, HEAD_DIM), lambda b, k: (b, k, c))
    full = lambda a: pl.BlockSpec((t, a.shape[1]), lambda b, k: (b, 0))
    return pl.pallas_call(
        functools.partial(_attn_sample_kernel, tkv=tkv, nkb=nkb, past_len=past_len,
                          lambda_init=lambda_init),
        out_shape=jax.ShapeDtypeStruct((bsz * t, N_HEADS * hw), BF16),
        grid=(bsz, nkb),
        in_specs=[
            full(qb), cache(0), cache(1), cache(0), cache(1), full(kvb), full(z32),
            vec, vec, vec, vec,
            pl.BlockSpec((1, hw), lambda b, k: (0, 0)),
        ],
        out_specs=pl.BlockSpec((t, N_HEADS * hw), lambda b, k: (b, 0)),
        scratch_shapes=[
            pltpu.VMEM((rows_all, LANES), F32),
            pltpu.VMEM((rows_all, LANES), F32),
            pltpu.VMEM((rows_all, hw), F32),
            pltpu.VMEM((rows_all, tkv + LANES), F32),
            pltpu.VMEM((rows_all, tkv + LANES), BF16),
        ],
        compiler_params=_params("arbitrary", "arbitrary"),
        name="attn_sample",
    )(qb, cache_k2, cache_k2, cache_v2, cache_v2, kvb, z32, lq1, lk1, lq2, lk2, subln)


def _b_out_kernel(op_ref, os_ref, w_ref, x_ref, g_ref, yp_ref, ys_ref, wb_ref, *, mp):
    m = pl.program_id(0) - 1

    @pl.when(m < 0)
    def _():
        wb_ref[...] = w_ref[...].astype(BF16)

    def finish(o_ref, y_ref):
        acc = jnp.dot(o_ref[...], wb_ref[...], preferred_element_type=F32)
        y_ref[...] = _rms_scale(x_ref[...] + acc, g_ref[...])

    @pl.when((m >= 0) & (m < mp))
    def _():
        finish(op_ref, yp_ref)

    @pl.when(m == mp)
    def _():
        finish(os_ref, ys_ref)


def _b_out_proj(og_p, og_s, w_out, x1, g):
    tm = ROW_TILE
    rows_p, e = og_p.shape
    rows, d = x1.shape
    mp = rows_p // tm
    assert og_s.shape[0] == tm and rows == rows_p + tm and rows_p % tm == 0
    prow = lambda r: jnp.minimum(_row_block(r), mp - 1)
    return pl.pallas_call(
        functools.partial(_b_out_kernel, mp=mp),
        out_shape=[jax.ShapeDtypeStruct((rows_p, d), F32), jax.ShapeDtypeStruct((tm, d), F32)],
        grid=(1 + rows // tm,),
        in_specs=[
            pl.BlockSpec((tm, e), lambda r: (prow(r), 0)),
            _resident((tm, e), lambda r: (0, 0)),
            _resident((e, d), lambda r: (0, 0)),
            pl.BlockSpec((tm, d), lambda r: (_row_block(r), 0)),
            _resident((1, d), lambda r: (0, 0)),
        ],
        out_specs=[pl.BlockSpec((tm, d), lambda r: (prow(r), 0)),
                   pl.BlockSpec((tm, d), lambda r: (0, 0))],
        scratch_shapes=[pltpu.VMEM((e, d), BF16)],
        compiler_params=_params("arbitrary"),
        name="b_out_proj",
    )(og_p, og_s, w_out, x1, g)


def kernel(x_prompt, x_sample, state_ssm_re, state_ssm_im, cache_k, cache_v, a_norm, a_w_in, a_lambda_re, a_lambda_im, a_log_dt, a_b_re, a_b_im, a_c_re, a_c_im, a_d, a_w_glu, a_w_out, kv_norm, w_kv, b_norm, b_w_in, b_lambda_q1, b_lambda_k1, b_lambda_q2, b_lambda_k2, b_subln, b_w_out, final_norm):
    bp, seq, d = x_prompt.shape
    bs, t_s, _ = x_sample.shape
    past_len = cache_k.shape[1]
    n_a, n_b = a_norm.shape[0], b_norm.shape[0]
    g_cnt, p = a_lambda_re.shape[1:]
    assert bp == 1 and n_a == 1 and n_b == 1
    assert t_s == SSM_T and seq % (SSM_T * bs) == 0 and seq % CHUNK == 0
    assert cache_k.shape[2:] == (N_HEADS, 2 * HEAD_DIM)

    jp, js = seq // SSM_T, bs
    jpad = -(-(jp + js) // CHUNK_TILE) * CHUNK_TILE
    rows_s = bs * t_s
    rows_a = seq + rows_s

    xp = x_prompt.reshape(seq, d)
    xs = x_sample.reshape(rows_s, d)
    uz = _a_in_proj(xp, xs, a_norm[0].reshape(1, d), a_w_in[0], jpad * SSM_T, tn=1024)
    mt, bend, cin, a16r, a16i = _ssm_prep(a_lambda_re[0], a_lambda_im[0], a_log_dt[0],
                                          a_b_re[0], a_b_im[0], a_c_re[0], a_c_im[0])
    s_re, s_im = _ssm_state_contrib(uz, bend, jpad)
    h0r = state_ssm_re[:, 0].reshape(bs, g_cnt * p)
    h0i = state_ssm_im[:, 0].reshape(bs, g_cnt * p)
    hp_re, hp_im, pre, pim, sre, sim = _ssm_scan(s_re, s_im, a16r, a16i, h0r, h0i, jp)
    d_col = jnp.tile(a_d[0].reshape(g_cnt, 1, SSM_GROUP), (1, SSM_T, 1)).reshape(g_cnt, -1, 1)
    yg = _ssm_output(uz, mt, cin, hp_re, hp_im, d_col, jpad)
    o_a = _a_glu(yg, uz, a_w_glu[0], rows_a)
    x1 = _a_out_proj(o_a, a_w_out[0], xp, xs)

    lambda_init = 0.8 - 0.6 * math.exp(-0.3 * n_a)
    w_q = b_w_in[0]
    kv_g = kv_norm.reshape(1, d)
    b_g = b_norm[0].reshape(1, d)
    rope = _rope_tables(seq, past_len, t_s, ROW_TILE)
    k_p, v_p, kvb_p, k_s, v_s, kvb_s = _kv_proj(x1, seq, kv_g, w_kv, rope)
    qb_p, z_p, qb_s, z_s = _qz_proj(x1, seq, b_g, w_q, rope, tn=1024)

    vecs = [a[0].reshape(1, HEAD_DIM) for a in (b_lambda_q1, b_lambda_k1, b_lambda_q2, b_lambda_k2)]
    subln = b_subln[0].reshape(1, 2 * HEAD_DIM)
    hw = 2 * HEAD_DIM
    og_p = _attn_prompt(qb_p, z_p, kvb_p, *vecs, subln, lambda_init, tq=1024, ts=512, tk=1024)
    og_s = _attn_sample(qb_s, z_s, kvb_s, cache_k.reshape(bs, past_len * N_HEADS, hw),
                        cache_v.reshape(bs, past_len * N_HEADS, hw), *vecs, subln, lambda_init,
                        t_s, tkv=1024)
    w_o = b_w_out[0]
    fg = final_norm.reshape(1, d)
    y_p, y_s = _b_out_proj(og_p, og_s, w_o, x1, fg)

    return (y_p.reshape(bp, seq, d), y_s.reshape(bs, t_s, d),
            pre.reshape(bp, n_a, g_cnt, p), pim.reshape(bp, n_a, g_cnt, p),
            k_p.reshape(bp, seq, N_HEADS, hw), v_p.reshape(bp, seq, N_HEADS, hw),
            sre.reshape(bs, n_a, g_cnt, p), sim.reshape(bs, n_a, g_cnt, p),
            k_s.reshape(bs, t_s, N_HEADS, hw), v_s.reshape(bs, t_s, N_HEADS, hw))
```

```python
import functools
import math

import jax
import jax.numpy as jnp
from jax import lax
from jax.experimental import pallas as pl
from jax.experimental.pallas import tpu as pltpu

F32 = jnp.float32
BF16 = jnp.bfloat16

CHUNK = 64
HEAD_DIM = 128
N_HEADS = 8
SSM_GROUP = 16
SSM_STATE = 64
SSM_T = 16
ROPE_THETA = 10000.0
Q_SCALE = HEAD_DIM ** -0.5 * math.log2(math.e)
EPS = 1e-6
NEG = -1e30

LANES = 128
SUBLANES = 8
GROUPS_PER_TILE = LANES // SSM_GROUP
CHUNK_TILE = 128
ROW_TILE = 512
STRIP = 32
VMEM_LIMIT = 56 * 1024 * 1024
KV_PROJ_VMEM_LIMIT = 58 * 1024 * 1024

_NT = (((1,), (1,)), ((), ()))


def _params(*sem, vmem_limit=VMEM_LIMIT):
    return pltpu.CompilerParams(dimension_semantics=sem, vmem_limit_bytes=vmem_limit)


def _resident(shape, index_map):
    return pl.BlockSpec(shape, index_map, pipeline_mode=pl.Buffered(1))


def _rms_scale(x, g):
    ms = jnp.mean(x * x, axis=-1, keepdims=True)
    return x * lax.rsqrt(ms + EPS) * g


def _staged_weight_spec(d, tn, nb):
    return pl.BlockSpec((d, tn), lambda r, n: (0, jnp.where(r == 0, n, nb - 1)),
                        pipeline_mode=pl.Buffered(1))


def _row_block(r):
    return jnp.maximum(r - 1, 0)


def _col_block(r, n):
    return jnp.where(r == 0, 0, n)


def _a_in_kernel(xp_ref, xs_ref, g_ref, w_ref, o_ref, xn_ref, wb_ref, *, mp, nb):
    m, n = pl.program_id(0) - 1, pl.program_id(1)

    @pl.when(m < 0)
    def _():
        wb_ref[n] = w_ref[...].astype(BF16)

    @pl.when((n == 0) & (m >= 0) & (m < mp))
    def _():
        xn_ref[...] = _rms_scale(xp_ref[...], g_ref[...]).astype(BF16)

    @pl.when((n == 0) & (m == mp))
    def _():
        xn_ref[...] = _rms_scale(xs_ref[...], g_ref[...]).astype(BF16)

    for c in range(nb):
        @pl.when((m >= 0) & (m <= mp) & (n == c))
        def _():
            o_ref[...] = jnp.dot(xn_ref[...], wb_ref[c], preferred_element_type=F32)

    @pl.when(m > mp)
    def _():
        o_ref[...] = jnp.zeros(o_ref.shape, F32)


def _a_in_proj(xp, xs, g, w, rows_pad, tn):
    tm = ROW_TILE
    d = g.shape[-1]
    n_out = w.shape[1]
    mp = xp.shape[0] // tm
    nb = n_out // tn
    assert xs.shape[0] == tm and xp.shape[0] % tm == 0 and rows_pad % tm == 0
    return pl.pallas_call(
        functools.partial(_a_in_kernel, mp=mp, nb=nb),
        out_shape=jax.ShapeDtypeStruct((rows_pad, n_out), F32),
        grid=(1 + rows_pad // tm, nb),
        in_specs=[
            pl.BlockSpec((tm, d), lambda r, n: (jnp.minimum(_row_block(r), mp - 1), 0)),
            _resident((tm, d), lambda r, n: (0, 0)),
            _resident((1, d), lambda r, n: (0, 0)),
            _staged_weight_spec(d, tn, nb),
        ],
        out_specs=pl.BlockSpec((tm, tn), lambda r, n: (_row_block(r), _col_block(r, n))),
        scratch_shapes=[pltpu.VMEM((tm, d), BF16), pltpu.VMEM((nb, d, tn), BF16)],
        compiler_params=_params("arbitrary", "arbitrary"),
        name="a_in_proj",
    )(xp, xs, g, w)


PREP_PAIRS = LANES // SSM_T


def _ssm_prep_kernel(lrx_ref, lix_ref, dtx_ref, bre_ref, bim_ref, lrr_ref, lir_ref, dtr_ref,
                     cre_ref, cim_ref, mt_ref, bend_ref, cin_ref, a16r_ref, a16i_ref):
    p = SSM_STATE
    p2 = 2 * p
    w = SSM_T * SSM_GROUP
    hi = lax.Precision.HIGHEST

    lr, li = lrx_ref[...], lix_ref[...]
    dt = jnp.exp(dtx_ref[...])
    zr, zi = lr * dt, li * dt
    mag = jnp.exp(zr)
    n_re, n_im = mag * jnp.cos(zi) - 1.0, mag * jnp.sin(zi)
    den = lr * lr + li * li
    cf_re = (n_re * lr + n_im * li) / den
    cf_im = (n_im * lr - n_re * li) / den
    lane = lax.broadcasted_iota(jnp.int32, (p2, LANES), 1)
    e_end = (SSM_T - 1 - lane % SSM_T).astype(F32)
    pm = jnp.exp(zr * e_end)
    pw_re, pw_im = pm * jnp.cos(zi * e_end), pm * jnp.sin(zi * e_end)
    cp_re = cf_re * pw_re - cf_im * pw_im
    cp_im = cf_re * pw_im + cf_im * pw_re
    coef = jnp.concatenate([cp_re, cp_im, cf_re, cf_im], axis=0)
    sel_row = lax.broadcasted_iota(jnp.int32, (LANES, w), 0)
    sel_col = lax.broadcasted_iota(jnp.int32, (LANES, w), 1)

    tau = lax.broadcasted_iota(jnp.int32, (SSM_T + SUBLANES, p2), 0).astype(F32)
    lane2 = lax.broadcasted_iota(jnp.int32, (w, p2), 1)
    sblk = lax.broadcasted_iota(jnp.int32, (w, w), 1) // SSM_GROUP

    for j in range(PREP_PAIRS):
        own_pair = (sel_row // SSM_T) == j
        spread = jnp.where(own_pair & (sel_row % SSM_T == sel_col // SSM_GROUP), 1.0, 0.0)
        cpx = jnp.dot(coef[:2 * p2], spread, precision=hi, preferred_element_type=F32)
        cfx = jnp.broadcast_to(coef[2 * p2:, j * SSM_T:j * SSM_T + 1], (2 * p2, w))
        b_re, b_im = jnp.tile(bre_ref[j], (1, SSM_T)), jnp.tile(bim_ref[j], (1, SSM_T))
        end_re = cpx[:p2] * b_re - cpx[p2:] * b_im
        end_im = cpx[:p2] * b_im + cpx[p2:] * b_re
        bb_re = cfx[:p2] * b_re - cfx[p2:] * b_im
        bb_im = cfx[:p2] * b_im + cfx[p2:] * b_re
        for r in range(2):
            sl = slice(r * p, (r + 1) * p)
            bend_ref[2 * j + r] = jnp.concatenate([end_re[sl], end_im[sl]], axis=0).astype(BF16)
        bst = jnp.concatenate([bb_re, bb_im], axis=0)

        lr2, li2 = lrr_ref[j], lir_ref[j]
        dt2 = jnp.exp(dtr_ref[j])
        zr2, zi2 = lr2 * dt2, li2 * dt2
        qm = jnp.exp(zr2 * tau)
        q_re, q_im = qm * jnp.cos(zi2 * tau), qm * jnp.sin(zi2 * tau)
        c_re, c_im = cre_ref[j], cim_ref[j]

        def c_times_powers(first):
            re = [c_re * q_re[t:t + 1] - c_im * q_im[t:t + 1] for t in range(first, first + SSM_T)]
            im = [c_re * q_im[t:t + 1] + c_im * q_re[t:t + 1] for t in range(first, first + SSM_T)]
            return jnp.concatenate(re, axis=0), jnp.concatenate(im, axis=0)

        g_re, g_im = c_times_powers(0)
        ci_re, ci_im = c_times_powers(1)
        for r in range(2):
            own = (lane2 // p) == r
            lhs = jnp.concatenate([jnp.where(own, g_re, 0.0), jnp.where(own, -g_im, 0.0)], axis=1)
            gen = jnp.dot(lhs, bst, preferred_element_type=F32)
            k = 0
            while (SSM_GROUP << k) < w:
                sh = SSM_GROUP << k
                shifted = jnp.concatenate([jnp.zeros((sh, w), F32), gen[:w - sh]], axis=0)
                gen = jnp.where(((sblk >> k) & 1) == 1, shifted, gen)
                k += 1
            mt_ref[2 * j + r] = gen.astype(BF16)
            cin_ref[2 * j + r] = jnp.concatenate(
                [jnp.where(own, ci_re, 0.0), jnp.where(own, -ci_im, 0.0)], axis=1).astype(BF16)
        a16r_ref[j] = q_re[SSM_T:SSM_T + 1]
        a16i_ref[j] = q_im[SSM_T:SSM_T + 1]


def _ssm_prep(lam_re, lam_im, log_dt, b_re, b_im, c_re, c_im):
    g, p = lam_re.shape
    gp = g // 2
    p2 = 2 * p
    w = SSM_T * SSM_GROUP
    pp = PREP_PAIRS
    assert gp % pp == 0
    colx = lambda a: jnp.repeat(a.reshape(gp, p2).T, SSM_T, axis=1)
    row = lambda a: a.reshape(gp, 1, p2)
    dt_full = jnp.broadcast_to(log_dt[:, None], (g, p))
    b_t = lambda a: a.reshape(gp, p2, SSM_GROUP)
    c_pair = lambda a: a.reshape(gp, 2, SSM_GROUP, p).transpose(0, 2, 1, 3).reshape(gp, SSM_GROUP, p2)
    xspec = pl.BlockSpec((p2, LANES), lambda i: (0, i))
    rowspec = pl.BlockSpec((pp, 1, p2), lambda i: (i, 0, 0))
    bspec = pl.BlockSpec((pp, p2, SSM_GROUP), lambda i: (i, 0, 0))
    cspec = pl.BlockSpec((pp, SSM_GROUP, p2), lambda i: (i, 0, 0))
    mt, bend, cin, a16r, a16i = pl.pallas_call(
        _ssm_prep_kernel,
        out_shape=[
            jax.ShapeDtypeStruct((g, w, w), BF16),
            jax.ShapeDtypeStruct((g, p2, w), BF16),
            jax.ShapeDtypeStruct((g, w, 2 * p2), BF16),
            jax.ShapeDtypeStruct((gp, 1, p2), F32),
            jax.ShapeDtypeStruct((gp, 1, p2), F32),
        ],
        grid=(gp // pp,),
        in_specs=[xspec, xspec, xspec, bspec, bspec, rowspec, rowspec, rowspec, cspec, cspec],
        out_specs=[
            pl.BlockSpec((2 * pp, w, w), lambda i: (i, 0, 0)),
            pl.BlockSpec((2 * pp, p2, w), lambda i: (i, 0, 0)),
            pl.BlockSpec((2 * pp, w, 2 * p2), lambda i: (i, 0, 0)),
            rowspec, rowspec,
        ],
        compiler_params=_params("arbitrary"),
        name="ssm_prep",
    )(colx(lam_re), colx(lam_im), colx(dt_full), b_t(b_re), b_t(b_im),
      row(lam_re), row(lam_im), row(dt_full), c_pair(c_re), c_pair(c_im))
    return mt, bend, cin, a16r.reshape(1, g * p), a16i.reshape(1, g * p)


def _step_rows(s):
    return pl.ds(s, CHUNK_TILE, stride=SSM_T)


def _build_ut(u_ref, ut_ref):
    for s in range(SSM_T):
        xt = u_ref[_step_rows(s), :].T.astype(BF16)
        for gl in range(GROUPS_PER_TILE):
            ut_ref[gl, s * SSM_GROUP:(s + 1) * SSM_GROUP, :] = xt[gl * SSM_GROUP:(gl + 1) * SSM_GROUP, :]


STATE_TILES = 2


def _ssm_state_kernel(*refs):
    u_refs = refs[:STATE_TILES]
    bend_ref, sre_ref, sim_ref = refs[STATE_TILES:STATE_TILES + 3]
    ut_refs = refs[STATE_TILES + 3:]
    p = SSM_STATE
    for tile in range(STATE_TILES):
        _build_ut(u_refs[tile], ut_refs[tile])
        for pr in range(GROUPS_PER_TILE // 2):
            st = [jnp.dot(bend_ref[tile * GROUPS_PER_TILE + 2 * pr + r], ut_refs[tile][2 * pr + r],
                          preferred_element_type=F32)
                  for r in range(2)]
            cols = slice((tile * GROUPS_PER_TILE // 2 + pr) * 2 * p,
                         (tile * GROUPS_PER_TILE // 2 + pr + 1) * 2 * p)
            sre_ref[:, cols] = jnp.concatenate([st[0][:p], st[1][:p]], axis=0).T
            sim_ref[:, cols] = jnp.concatenate([st[0][p:], st[1][p:]], axis=0).T


def _ssm_state_contrib(uz, bend, jpad):
    g, p2, w = bend.shape
    gstep = STATE_TILES * GROUPS_PER_TILE
    assert g % gstep == 0
    sw = gstep * SSM_STATE
    out = jax.ShapeDtypeStruct((jpad, g * SSM_STATE), F32)
    ospec = pl.BlockSpec((CHUNK_TILE, sw), lambda q, j: (j, q))
    utile = lambda t: pl.BlockSpec((CHUNK_TILE * SSM_T, LANES), lambda q, j: (j, STATE_TILES * q + t))
    return pl.pallas_call(
        _ssm_state_kernel,
        out_shape=[out, out],
        grid=(g // gstep, jpad // CHUNK_TILE),
        in_specs=[*[utile(t) for t in range(STATE_TILES)],
                  pl.BlockSpec((gstep, p2, w), lambda q, j: (q, 0, 0))],
        out_specs=[ospec, ospec],
        scratch_shapes=[pltpu.VMEM((GROUPS_PER_TILE, w, CHUNK_TILE), BF16) for _ in range(STATE_TILES)],
        compiler_params=_params("arbitrary", "arbitrary"),
        name="ssm_state_contrib",
    )(*[uz] * STATE_TILES, bend)


def _ssm_scan_kernel(sre_ref, sim_ref, ar_ref, ai_ref, h0r_ref, h0i_ref,
                     hpr_ref, hpi_ref, pr_ref, pi_ref, sr_ref, si_ref, hr_sc, hi_sc, *, n_prompt):
    i = pl.program_id(0)
    rb = sre_ref.shape[0]
    ar, ai = ar_ref[...], ai_ref[...]

    @pl.when(i == 0)
    def _():
        hr_sc[...] = jnp.zeros_like(hr_sc)
        hi_sc[...] = jnp.zeros_like(hi_sc)

    @pl.when(i < n_prompt)
    def _():
        def body(j, c):
            row = pl.ds(j, 1)
            hr, hi = hr_sc[...], hi_sc[...]
            hpr_ref[row, :] = hr
            hpi_ref[row, :] = hi
            hr_sc[...] = ar * hr - ai * hi + sre_ref[row, :]
            hi_sc[...] = ar * hi + ai * hr + sim_ref[row, :]
            return c

        lax.fori_loop(0, rb, body, 0)

    @pl.when(i == n_prompt - 1)
    def _():
        pr_ref[...] = hr_sc[...]
        pi_ref[...] = hi_sc[...]

    @pl.when(i == n_prompt)
    def _():
        h0r, h0i = h0r_ref[...], h0i_ref[...]
        hpr_ref[...] = h0r
        hpi_ref[...] = h0i
        sr_ref[...] = ar * h0r - ai * h0i + sre_ref[...]
        si_ref[...] = ar * h0i + ai * h0r + sim_ref[...]

    @pl.when(i > n_prompt)
    def _():
        hpr_ref[...] = jnp.zeros_like(hpr_ref)
        hpi_ref[...] = jnp.zeros_like(hpi_ref)


def _ssm_scan(s_re, s_im, a16r, a16i, h0r, h0i, jp):
    jpad, n = s_re.shape
    rb = h0r.shape[0]
    assert jp % rb == 0 and jpad % rb == 0
    n_prompt = jp // rb
    rows = pl.BlockSpec((rb, n), lambda i: (i, 0))
    const1 = pl.BlockSpec((1, n), lambda i: (0, 0))
    constb = pl.BlockSpec((rb, n), lambda i: (0, 0))
    big = jax.ShapeDtypeStruct((jpad, n), F32)
    one = jax.ShapeDtypeStruct((1, n), F32)
    bat = jax.ShapeDtypeStruct((rb, n), F32)
    return pl.pallas_call(
        functools.partial(_ssm_scan_kernel, n_prompt=n_prompt),
        out_shape=[big, big, one, one, bat, bat],
        grid=(jpad // rb,),
        in_specs=[rows, rows, const1, const1, constb, constb],
        out_specs=[rows, rows, const1, const1, constb, constb],
        scratch_shapes=[pltpu.VMEM((1, n), F32), pltpu.VMEM((1, n), F32)],
        compiler_params=_params("arbitrary"),
        name="ssm_scan",
    )(s_re, s_im, a16r, a16i, h0r, h0i)


def _ssm_out_kernel(u_ref, mt_ref, cin_ref, hpr_ref, hpi_ref, d_ref, o_ref, ut_ref, uf_ref, yt_ref):
    for s in range(SSM_T):
        xt = u_ref[_step_rows(s), :].T
        xb = xt.astype(BF16)
        for gl in range(GROUPS_PER_TILE):
            rows = slice(gl * SSM_GROUP, (gl + 1) * SSM_GROUP)
            ut_ref[gl, s * SSM_GROUP:(s + 1) * SSM_GROUP, :] = xb[rows, :]
            uf_ref[gl, s * SSM_GROUP:(s + 1) * SSM_GROUP, :] = xt[rows, :]
    p2 = 2 * SSM_STATE
    for gl in range(GROUPS_PER_TILE):
        pr = gl // 2
        hp = jnp.concatenate([hpr_ref[:, pr * p2:(pr + 1) * p2], hpi_ref[:, pr * p2:(pr + 1) * p2]],
                             axis=1).astype(BF16)
        yt = jnp.dot(mt_ref[gl], ut_ref[gl], preferred_element_type=F32)
        yt = yt + lax.dot_general(cin_ref[gl], hp, _NT, preferred_element_type=F32)
        yt = jax.nn.gelu(yt + jnp.tile(d_ref[gl], (SSM_T, 1)) * uf_ref[gl])
        for t in range(SSM_T):
            yt_ref[t, gl * SSM_GROUP:(gl + 1) * SSM_GROUP, :] = yt[t * SSM_GROUP:(t + 1) * SSM_GROUP, :]
    for t in range(SSM_T):
        o_ref[_step_rows(t), :] = yt_ref[t].T


def _ssm_output(uz, mt, cin, hp_re, hp_im, d_skip, jpad):
    g, w, _ = mt.shape
    w2 = cin.shape[-1]
    ntile = g // GROUPS_PER_TILE
    sw = GROUPS_PER_TILE * SSM_STATE
    tok = pl.BlockSpec((CHUNK_TILE * SSM_T, LANES), lambda q, j: (j, q))
    hspec = pl.BlockSpec((CHUNK_TILE, sw), lambda q, j: (j, q))
    return pl.pallas_call(
        _ssm_out_kernel,
        out_shape=jax.ShapeDtypeStruct((jpad * SSM_T, g * SSM_GROUP), F32),
        grid=(ntile, jpad // CHUNK_TILE),
        in_specs=[
            tok,
            pl.BlockSpec((GROUPS_PER_TILE, w, w), lambda q, j: (q, 0, 0)),
            pl.BlockSpec((GROUPS_PER_TILE, w, w2), lambda q, j: (q, 0, 0)),
            hspec, hspec,
            pl.BlockSpec((GROUPS_PER_TILE, SSM_GROUP, CHUNK_TILE), lambda q, j: (q, 0, 0)),
        ],
        out_specs=tok,
        scratch_shapes=[
            pltpu.VMEM((GROUPS_PER_TILE, w, CHUNK_TILE), BF16),
            pltpu.VMEM((GROUPS_PER_TILE, w, CHUNK_TILE), F32),
            pltpu.VMEM((SSM_T, LANES, CHUNK_TILE), F32),
        ],
        compiler_params=_params("arbitrary", "arbitrary"),
        name="ssm_output",
    )(uz, mt, cin, hp_re, hp_im, d_skip)


def _a_glu_kernel(y_ref, z_ref, w_ref, o_ref, wb_ref):
    r = pl.program_id(0)

    @pl.when(r == 0)
    def _():
        wb_ref[...] = w_ref[...].astype(BF16)

    @pl.when(r > 0)
    def _():
        y = y_ref[...]
        gate = jnp.dot(y.astype(BF16), wb_ref[...], preferred_element_type=F32)
        y2 = y * jax.nn.sigmoid(gate)
        o_ref[...] = (y2 * jax.nn.silu(z_ref[...])).astype(BF16)


def _a_glu(yg, uz, w_glu, rows):
    tm = ROW_TILE
    e = yg.shape[1]
    return pl.pallas_call(
        _a_glu_kernel,
        out_shape=jax.ShapeDtypeStruct((rows, e), BF16),
        grid=(1 + rows // tm,),
        in_specs=[
            pl.BlockSpec((tm, e), lambda r: (_row_block(r), 0)),
            pl.BlockSpec((tm, e), lambda r: (_row_block(r), 1)),
            _resident((e, e), lambda r: (0, 0)),
        ],
        out_specs=pl.BlockSpec((tm, e), lambda r: (_row_block(r), 0)),
        scratch_shapes=[pltpu.VMEM((e, e), BF16)],
        compiler_params=_params("arbitrary"),
        name="a_glu",
    )(yg, uz, w_glu)


def _a_out_kernel(o_ref, w_ref, xp_ref, xs_ref, x1_ref, wb_ref, *, mp):
    m = pl.program_id(0) - 1

    @pl.when(m < 0)
    def _():
        wb_ref[...] = w_ref[...].astype(BF16)

    @pl.when((m >= 0) & (m < mp))
    def _():
        x1_ref[...] = xp_ref[...] + jnp.dot(o_ref[...], wb_ref[...], preferred_element_type=F32)

    @pl.when(m == mp)
    def _():
        x1_ref[...] = xs_ref[...] + jnp.dot(o_ref[...], wb_ref[...], preferred_element_type=F32)


def _a_out_proj(o, w_out, xp, xs):
    tm = ROW_TILE
    rows, e = o.shape
    d = w_out.shape[1]
    mp = xp.shape[0] // tm
    return pl.pallas_call(
        functools.partial(_a_out_kernel, mp=mp),
        out_shape=jax.ShapeDtypeStruct((rows, d), F32),
        grid=(1 + rows // tm,),
        in_specs=[
            pl.BlockSpec((tm, e), lambda r: (_row_block(r), 0)),
            _resident((e, d), lambda r: (0, 0)),
            pl.BlockSpec((tm, d), lambda r: (jnp.minimum(_row_block(r), mp - 1), 0)),
            _resident((tm, d), lambda r: (0, 0)),
        ],
        out_specs=pl.BlockSpec((tm, d), lambda r: (_row_block(r), 0)),
        scratch_shapes=[pltpu.VMEM((e, d), BF16)],
        compiler_params=_params("arbitrary"),
        name="a_out_proj",
    )(o, w_out, xp, xs)


def _rope(x, cos, sin):
    outs = []
    for c in range(x.shape[1] // HEAD_DIM):
        xc = x[:, c * HEAD_DIM:(c + 1) * HEAD_DIM]
        outs.append(xc * cos + pltpu.roll(xc, HEAD_DIM // 2, axis=1) * sin)
    return jnp.concatenate(outs, axis=1) if len(outs) > 1 else outs[0]


def _rope_block(co_ref, so_ref, cb_ref, sb_ref):
    co, so = co_ref[...], so_ref[...]
    cb, sb = cb_ref[...], sb_ref[...]
    lane = lax.broadcasted_iota(jnp.int32, (1, HEAD_DIM), 1)
    sign = jnp.where(lane < HEAD_DIM // 2, -1.0, 1.0)
    return cb * co - sb * so, (sb * co + cb * so) * sign


def _store_heads(o_ref, val, tm):
    for h in range(N_HEADS):
        o_ref[pl.ds(h, tm, stride=N_HEADS), :] = val[:, h * HEAD_DIM:(h + 1) * HEAD_DIM]


def _prompt_col(r, mp, col, last):
    return jnp.where(_row_block(r) >= mp, last, col)


def _sample_col(r, mp, col):
    return jnp.where(_row_block(r) == mp, col, 0)


def _kv_proj_kernel(x_ref, g_ref, w_ref, co_ref, so_ref, cb_ref, sb_ref, kp_ref, vp_ref, cbp_ref,
                    ks_ref, vs_ref, cbs_ref, xn_ref, wb_ref, *, tm, mp):
    m, n = pl.program_id(0) - 1, pl.program_id(1)
    hpb = w_ref.shape[1] // (2 * HEAD_DIM)

    for nn in range(4):
        @pl.when((m < 0) & (n == nn))
        def _():
            for hl in range(hpb):
                h = hpb * (nn % 2) + hl
                for half in range(2):
                    src = (2 * hl + half) * HEAD_DIM
                    wb_ref[2 * (nn // 2) + half, :, h * HEAD_DIM:(h + 1) * HEAD_DIM] = (
                        w_ref[:, src:src + HEAD_DIM].astype(BF16))

    @pl.when((m >= 0) & (n == 0))
    def _():
        xn_ref[...] = _rms_scale(x_ref[...], g_ref[...]).astype(BF16)

    def column(c, k_ref, v_ref, copy_ref):
        acc = jnp.dot(xn_ref[...], wb_ref[c], preferred_element_type=F32)
        if c < 2:
            acc = _rope(acc, *_rope_block(co_ref, so_ref, cb_ref, sb_ref))
        _store_heads(k_ref if c < 2 else v_ref, acc, tm)
        copy_ref[...] = acc.astype(BF16)

    for c in range(4):
        @pl.when((m >= 0) & (m < mp) & (n == c))
        def _():
            column(c, kp_ref, vp_ref, cbp_ref)

        @pl.when((m == mp) & (n == c))
        def _():
            column(c, ks_ref, vs_ref, cbs_ref)


def _kv_proj(x1, rows_p, g, w_kv, rope):
    tm = ROW_TILE
    rows, d = x1.shape
    tn = N_HEADS * HEAD_DIM
    mp = rows_p // tm
    assert w_kv.shape[1] == 4 * tn and rows == rows_p + tm and rows_p % tm == 0
    out4 = lambda nrows: jax.ShapeDtypeStruct((nrows * N_HEADS, 2 * HEAD_DIM), F32)
    outb = lambda nrows: jax.ShapeDtypeStruct((nrows, 4 * tn), BF16)
    prow = lambda r: jnp.minimum(_row_block(r), mp - 1)
    col = _col_block
    return pl.pallas_call(
        functools.partial(_kv_proj_kernel, tm=tm, mp=mp),
        out_shape=[out4(rows_p), out4(rows_p), outb(rows_p), out4(tm), out4(tm), outb(tm)],
        grid=(1 + rows // tm, 4),
        in_specs=[
            pl.BlockSpec((tm, d), lambda r, n: (_row_block(r), 0)),
            _resident((1, d), lambda r, n: (0, 0)),
            _staged_weight_spec(d, tn, 4),
            *_rope_specs(tm, mp),
        ],
        out_specs=[
            pl.BlockSpec((tm * N_HEADS, HEAD_DIM),
                         lambda r, n: (prow(r), _prompt_col(r, mp, jnp.minimum(col(r, n), 1), 1))),
            pl.BlockSpec((tm * N_HEADS, HEAD_DIM),
                         lambda r, n: (prow(r), _prompt_col(r, mp, jnp.maximum(col(r, n) - 2, 0), 1))),
            pl.BlockSpec((tm, tn), lambda r, n: (prow(r), _prompt_col(r, mp, col(r, n), 3))),
            pl.BlockSpec((tm * N_HEADS, HEAD_DIM),
                         lambda r, n: (0, _sample_col(r, mp, jnp.minimum(col(r, n), 1)))),
            pl.BlockSpec((tm * N_HEADS, HEAD_DIM),
                         lambda r, n: (0, _sample_col(r, mp, jnp.maximum(col(r, n) - 2, 0)))),
            pl.BlockSpec((tm, tn), lambda r, n: (0, _sample_col(r, mp, col(r, n)))),
        ],
        scratch_shapes=[pltpu.VMEM((tm, d), BF16), pltpu.VMEM((4, d, tn), BF16)],
        compiler_params=_params("arbitrary", "arbitrary", vmem_limit=KV_PROJ_VMEM_LIMIT),
        name="kv_proj",
    )(x1, g, w_kv, *rope)


def _qz_proj_kernel(x_ref, g_ref, w_ref, co_ref, so_ref, cb_ref, sb_ref, qp_ref, zp_ref, qs_ref, zs_ref,
                    xn_ref, wb_ref, *, nq, mp):
    m, n = pl.program_id(0) - 1, pl.program_id(1)

    @pl.when(m < 0)
    def _():
        wb_ref[n] = w_ref[...].astype(BF16)

    @pl.when((m >= 0) & (n == 0))
    def _():
        xn_ref[...] = _rms_scale(x_ref[...], g_ref[...]).astype(BF16)

    def column(c, q_ref, z_ref):
        acc = jnp.dot(xn_ref[...], wb_ref[c], preferred_element_type=F32)
        if c < nq:
            rot = _rope(acc, *_rope_block(co_ref, so_ref, cb_ref, sb_ref))
            q_ref[...] = (rot * Q_SCALE).astype(BF16)
        else:
            z_ref[...] = acc

    for c in range(2 * nq):
        @pl.when((m >= 0) & (m < mp) & (n == c))
        def _():
            column(c, qp_ref, zp_ref)

        @pl.when((m == mp) & (n == c))
        def _():
            column(c, qs_ref, zs_ref)


def _qz_proj(x1, rows_p, g, w, rope, tn):
    tm = ROW_TILE
    rows, d = x1.shape
    half = w.shape[1] // 2
    nq = half // tn
    mp = rows_p // tm
    assert rows == rows_p + tm and rows_p % tm == 0
    prow = lambda r: jnp.minimum(_row_block(r), mp - 1)
    qcol = lambda r, n: jnp.minimum(_col_block(r, n), nq - 1)
    zcol = lambda r, n: jnp.maximum(_col_block(r, n) - nq, 0)
    return pl.pallas_call(
        functools.partial(_qz_proj_kernel, nq=nq, mp=mp),
        out_shape=[jax.ShapeDtypeStruct((rows_p, half), BF16), jax.ShapeDtypeStruct((rows_p, half), F32),
                   jax.ShapeDtypeStruct((tm, half), BF16), jax.ShapeDtypeStruct((tm, half), F32)],
        grid=(1 + rows // tm, 2 * nq),
        in_specs=[
            pl.BlockSpec((tm, d), lambda r, n: (_row_block(r), 0)),
            _resident((1, d), lambda r, n: (0, 0)),
            _staged_weight_spec(d, tn, 2 * nq),
            *_rope_specs(tm, mp),
        ],
        out_specs=[
            pl.BlockSpec((tm, tn), lambda r, n: (prow(r), _prompt_col(r, mp, qcol(r, n), nq - 1))),
            pl.BlockSpec((tm, tn), lambda r, n: (prow(r), _prompt_col(r, mp, zcol(r, n), nq - 1))),
            pl.BlockSpec((tm, tn), lambda r, n: (0, _sample_col(r, mp, qcol(r, n)))),
            pl.BlockSpec((tm, tn), lambda r, n: (0, _sample_col(r, mp, zcol(r, n)))),
        ],
        scratch_shapes=[pltpu.VMEM((tm, d), BF16), pltpu.VMEM((2 * nq, d, tn), BF16)],
        compiler_params=_params("arbitrary", "arbitrary"),
        name="qz_proj",
    )(x1, g, w, *rope)


def _rope_specs(tm, mp):
    off = pl.BlockSpec((None, tm, HEAD_DIM), lambda r, n: (jnp.where(_row_block(r) == mp, 1, 0), 0, 0))
    base = pl.BlockSpec((None, 1, HEAD_DIM), lambda r, n: (_row_block(r), 0, 0))
    return [off, off, base, base]


def _rope_tables(rows_p, past_len, t_s, tm):
    inv_freq = ROPE_THETA ** (-jnp.arange(0, HEAD_DIM, 2, dtype=F32) / HEAD_DIM)
    full = lambda a: jnp.concatenate([a, a], axis=-1)
    rows = jnp.arange(tm, dtype=jnp.int32)
    off = jnp.stack([rows, rows % t_s]).astype(F32)[..., None] * inv_freq
    base = jnp.concatenate([jnp.arange(0, rows_p, tm, dtype=jnp.int32),
                            jnp.full((1,), past_len, jnp.int32)]).astype(F32)[:, None, None] * inv_freq
    return full(jnp.cos(off)), full(jnp.sin(off)), full(jnp.cos(base)), full(jnp.sin(base))


def _diff_lambda(lq1_ref, lk1_ref, lq2_ref, lk2_ref, lambda_init):
    s1 = jnp.sum(lq1_ref[...] * lk1_ref[...], axis=-1, keepdims=True)
    s2 = jnp.sum(lq2_ref[...] * lk2_ref[...], axis=-1, keepdims=True)
    return jnp.exp(s1) - jnp.exp(s2) + lambda_init


def _attn_finish(a1, l1, a2, l2, lam, subln, z, lambda_init):
    o = a1 / l1 - lam * (a2 / l2)
    o = _rms_scale(o, subln) * (1.0 - lambda_init)
    return (o * jax.nn.silu(z)).astype(BF16)


def _attn_prompt_kernel(q_ref, k1_ref, k2_ref, v1_ref, v2_ref, z_ref,
                        lq1_ref, lk1_ref, lq2_ref, lk2_ref, sub_ref,
                        o_ref, m_sc, l_sc, al_sc, a_sc, *sp_scs, tq, ts, tk, lambda_init):
    qi = pl.program_id(1)
    k_refs = (k1_ref, k2_ref)
    nsub = tq // ts
    s_scs, p_scs = sp_scs[:2 * nsub], sp_scs[2 * nsub:]
    m_sc[...] = jnp.full(m_sc.shape, NEG, F32)
    l_sc[...] = jnp.zeros(l_sc.shape, F32)
    a_sc[...] = jnp.zeros(a_sc.shape, F32)

    def softmax_strips(sub, n, col0):
        for r in range(ts // STRIP):
            row0 = sub * ts + r * STRIP
            lr = slice(r * STRIP, (r + 1) * STRIP)
            gr = slice(row0, row0 + STRIP)
            visible = tk if col0 is None else min(tk, (row0 // CHUNK + 1) * CHUNK - col0)
            assert visible > 0
            ncv = -(-visible // LANES)
            s = s_scs[2 * sub + n][lr, 0:ncv * LANES]
            if visible < ncv * LANES:
                col = lax.broadcasted_iota(jnp.int32, s.shape, 1)
                s = jnp.where(col < visible, s, NEG)
            cols = [s[:, c * LANES:(c + 1) * LANES] for c in range(ncv)]
            m_cur = jnp.max(functools.reduce(jnp.maximum, cols), axis=-1, keepdims=True)
            m_old = m_sc[n, gr, :]
            m_new = jnp.maximum(m_old, m_cur)
            ps = [jnp.exp2(c - m_new) for c in cols]
            l_cur = jnp.sum(functools.reduce(jnp.add, ps), axis=-1, keepdims=True)
            alpha = jnp.exp2(m_old - m_new)
            l_sc[n, gr, :] = alpha * l_sc[n, gr, :] + l_cur
            m_sc[n, gr, :] = m_new
            al_sc[n, gr, :] = alpha
            for c in range(keys_seen(sub, col0) // LANES):
                pc = ps[c].astype(BF16) if c < ncv else jnp.zeros((STRIP, LANES), BF16)
                p_scs[2 * sub + n][lr, c * LANES:(c + 1) * LANES] = pc

    def keys_seen(sub, col0):
        return tk if col0 is None else min(tk, (sub + 1) * ts - col0)

    def block(kb, col0, subs):
        start = pl.multiple_of(kb * tk, tk)
        for sub in subs:
            rows = pl.ds(start, keys_seen(sub, col0))
            for n in range(2):
                qn = q_ref[sub * ts:(sub + 1) * ts, n * HEAD_DIM:(n + 1) * HEAD_DIM]
                s_scs[2 * sub + n][:, 0:rows.size] = lax.dot_general(
                    qn, k_refs[n][rows, :], _NT, preferred_element_type=F32)
        for sub in subs:
            gr = slice(sub * ts, (sub + 1) * ts)
            rows = pl.ds(start, keys_seen(sub, col0))
            vblk = jnp.concatenate([v1_ref[rows, :], v2_ref[rows, :]], axis=1)
            for n in range(2):
                softmax_strips(sub, n, col0)
                pv = jnp.dot(p_scs[2 * sub + n][:, 0:rows.size], vblk, preferred_element_type=F32)
                alpha = al_sc[n, gr, :]
                a_sc[n, gr, :] = a_sc[n, gr, :] * jnp.concatenate([alpha, alpha], axis=1) + pv

    nd = tq // tk

    def body(kb, c):
        block(kb, None, range(nsub))
        return c

    lax.fori_loop(0, qi * nd, body, 0)
    for d in range(nd):
        block(qi * nd + d, d * tk, [sub for sub in range(nsub) if (sub + 1) * ts > d * tk])

    lam = _diff_lambda(lq1_ref, lk1_ref, lq2_ref, lk2_ref, lambda_init)
    l1 = jnp.concatenate([l_sc[0], l_sc[0]], axis=1)
    l2 = jnp.concatenate([l_sc[1], l_sc[1]], axis=1)
    o_ref[...] = _attn_finish(a_sc[0], l1, a_sc[1], l2, lam, sub_ref[...], z_ref[...], lambda_init)


def _attn_prompt(qb, z32, kvb, lq1, lk1, lq2, lk2, subln, lambda_init, tq, ts, tk):
    rows = qb.shape[0]
    hw = 2 * HEAD_DIM
    nsub = tq // ts
    assert tq % tk == 0 and tk % ts == 0 and ts % STRIP == 0 and CHUNK % STRIP == 0
    vec = pl.BlockSpec((1, HEAD_DIM), lambda h, i: (0, 0))
    kcol = lambda c: pl.BlockSpec((rows, HEAD_DIM), lambda h, i: (0, c * N_HEADS + h))
    stat = pltpu.VMEM((2, tq, LANES), F32)
    return pl.pallas_call(
        functools.partial(_attn_prompt_kernel, tq=tq, ts=ts, tk=tk, lambda_init=lambda_init),
        out_shape=jax.ShapeDtypeStruct((rows, N_HEADS * hw), BF16),
        grid=(N_HEADS, rows // tq),
        in_specs=[
            pl.BlockSpec((tq, hw), lambda h, i: (i, h)),
            kcol(0), kcol(1), kcol(2), kcol(3),
            pl.BlockSpec((tq, hw), lambda h, i: (i, h)),
            vec, vec, vec, vec,
            pl.BlockSpec((1, hw), lambda h, i: (0, 0)),
        ],
        out_specs=pl.BlockSpec((tq, hw), lambda h, i: (i, h)),
        scratch_shapes=[
            stat, stat, stat,
            pltpu.VMEM((2, tq, hw), F32),
            *[pltpu.VMEM((ts, tk), F32) for _ in range(2 * nsub)],
            *[pltpu.VMEM((ts, tk), BF16) for _ in range(2 * nsub)],
        ],
        compiler_params=_params("arbitrary", "arbitrary"),
        name="attn_prompt",
    )(qb, kvb, kvb, kvb, kvb, z32, lq1, lk1, lq2, lk2, subln)


def _attn_sample_kernel(q_ref, ck1_ref, ck2_ref, cv1_ref, cv2_ref, kvn_ref, z_ref,
                        lq1_ref, lk1_ref, lq2_ref, lk2_ref, sub_ref,
                        o_ref, m_sc, l_sc, a_sc, s_sc, p_sc, *, tkv, nkb, past_len, lambda_init):
    kb = pl.program_id(1)
    t = q_ref.shape[0]
    hw = 2 * HEAD_DIM
    rows_all = 2 * N_HEADS * t
    strip = 4 * t
    ck_refs = (ck1_ref, ck2_ref)
    assert (past_len - 1) // CHUNK <= past_len // CHUNK

    @pl.when(kb == 0)
    def _():
        m_sc[...] = jnp.full(m_sc.shape, NEG, F32)
        l_sc[...] = jnp.zeros(l_sc.shape, F32)
        a_sc[...] = jnp.zeros(a_sc.shape, F32)

    col_blk = lambda c, h: slice((c * N_HEADS + h) * HEAD_DIM, (c * N_HEADS + h + 1) * HEAD_DIM)
    head_rows = lambda h: pl.ds(h, tkv, stride=N_HEADS)

    def update(with_new):
        ncols = tkv + (LANES if with_new else 0)
        if with_new:
            row = lax.broadcasted_iota(jnp.int32, (t, LANES), 0) + past_len
            col = lax.broadcasted_iota(jnp.int32, (t, LANES), 1)
            new_mask = (col < t) & (((col + past_len) // CHUNK) <= (row // CHUNK))
            pad_k = jnp.zeros((LANES - t, HEAD_DIM), BF16)
        for h in range(N_HEADS):
            for n in range(2):
                qn = q_ref[:, (2 * h + n) * HEAD_DIM:(2 * h + n + 1) * HEAD_DIM]
                rr = slice((2 * h + n) * t, (2 * h + n + 1) * t)
                k = ck_refs[n][head_rows(h), :].astype(BF16)
                s_sc[rr, 0:tkv] = lax.dot_general(qn, k, _NT, preferred_element_type=F32)
                if with_new:
                    kn = jnp.concatenate([kvn_ref[:, col_blk(n, h)], pad_k], axis=0)
                    sn = lax.dot_general(qn, kn, _NT, preferred_element_type=F32)
                    s_sc[rr, tkv:ncols] = jnp.where(new_mask, sn, NEG)
        for r in range(rows_all // strip):
            rr = slice(r * strip, (r + 1) * strip)
            cols = [s_sc[rr, c * LANES:(c + 1) * LANES] for c in range(ncols // LANES)]
            m_cur = jnp.max(functools.reduce(jnp.maximum, cols), axis=-1, keepdims=True)
            m_old = m_sc[rr, :]
            m_new = jnp.maximum(m_old, m_cur)
            ps = [jnp.exp2(c - m_new) for c in cols]
            l_cur = jnp.sum(functools.reduce(jnp.add, ps), axis=-1, keepdims=True)
            alpha = jnp.exp2(m_old - m_new)
            l_sc[rr, :] = alpha * l_sc[rr, :] + l_cur
            m_sc[rr, :] = m_new
            a_sc[rr, :] = a_sc[rr, :] * jnp.concatenate([alpha, alpha], axis=1)
            for c, pc in enumerate(ps):
                p_sc[rr, c * LANES:(c + 1) * LANES] = pc.astype(BF16)
        for h in range(N_HEADS):
            rr = slice(2 * h * t, (2 * h + 2) * t)
            v = jnp.concatenate([cv1_ref[head_rows(h), :], cv2_ref[head_rows(h), :]],
                                axis=1).astype(BF16)
            pv = jnp.dot(p_sc[rr, 0:tkv], v, preferred_element_type=F32)
            if with_new:
                vn = jnp.concatenate([kvn_ref[:, col_blk(2, h)], kvn_ref[:, col_blk(3, h)]], axis=1)
                vn = jnp.concatenate([vn, jnp.zeros((LANES - t, hw), BF16)], axis=0)
                pv = pv + jnp.dot(p_sc[rr, tkv:ncols], vn, preferred_element_type=F32)
            a_sc[rr, :] = a_sc[rr, :] + pv

    @pl.when(kb < nkb - 1)
    def _():
        update(False)

    @pl.when(kb == nkb - 1)
    def _():
        update(True)
        lam = _diff_lambda(lq1_ref, lk1_ref, lq2_ref, lk2_ref, lambda_init)
        sub = sub_ref[...]
        for h in range(N_HEADS):
            r1 = slice(2 * h * t, (2 * h + 1) * t)
            r2 = slice((2 * h + 1) * t, (2 * h + 2) * t)
            l1 = jnp.concatenate([l_sc[r1, :], l_sc[r1, :]], axis=1)
            l2 = jnp.concatenate([l_sc[r2, :], l_sc[r2, :]], axis=1)
            o_ref[:, h * hw:(h + 1) * hw] = _attn_finish(
                a_sc[r1, :], l1, a_sc[r2, :], l2, lam, sub, z_ref[:, h * hw:(h + 1) * hw], lambda_init)


def _attn_sample(qb, z32, kvb, cache_k2, cache_v2, lq1, lk1, lq2, lk2, subln, lambda_init, t, tkv):
    bsz, rows8, hw = cache_k2.shape
    past_len = rows8 // N_HEADS
    nkb = past_len // tkv
    rows_all = 2 * N_HEADS * t
    assert t <= LANES
    vec = pl.BlockSpec((1, HEAD_DIM), lambda b, k: (0, 0))
    cache = lambda c: pl.BlockSpec((None, tkv * N_HEADS, HEAD_DIM), lambda b, k: (b, k, c))
    full = lambda a: pl.BlockSpec((t, a.shape[1]), lambda b, k: (b, 0))
    return pl.pallas_call(
        functools.partial(_attn_sample_kernel, tkv=tkv, nkb=nkb, past_len=past_len,
                          lambda_init=lambda_init),
        out_shape=jax.ShapeDtypeStruct((bsz * t, N_HEADS * hw), BF16),
        grid=(bsz, nkb),
        in_specs=[
            full(qb), cache(0), cache(1), cache(0), cache(1), full(kvb), full(z32),
            vec, vec, vec, vec,
            pl.BlockSpec((1, hw), lambda b, k: (0, 0)),
        ],
        out_specs=pl.BlockSpec((t, N_HEADS * hw), lambda b, k: (b, 0)),
        scratch_shapes=[
            pltpu.VMEM((rows_all, LANES), F32),
            pltpu.VMEM((rows_all, LANES), F32),
            pltpu.VMEM((rows_all, hw), F32),
            pltpu.VMEM((rows_all, tkv + LANES), F32),
            pltpu.VMEM((rows_all, tkv + LANES), BF16),
        ],
        compiler_params=_params("arbitrary", "arbitrary"),
        name="attn_sample",
    )(qb, cache_k2, cache_k2, cache_v2, cache_v2, kvb, z32, lq1, lk1, lq2, lk2, subln)


def _b_out_kernel(op_ref, os_ref, w_ref, x_ref, g_ref, yp_ref, ys_ref, wb_ref, *, mp):
    m = pl.program_id(0) - 1

    @pl.when(m < 0)
    def _():
        wb_ref[...] = w_ref[...].astype(BF16)

    def finish(o_ref, y_ref):
        acc = jnp.dot(o_ref[...], wb_ref[...], preferred_element_type=F32)
        y_ref[...] = _rms_scale(x_ref[...] + acc, g_ref[...])

    @pl.when((m >= 0) & (m < mp))
    def _():
        finish(op_ref, yp_ref)

    @pl.when(m == mp)
    def _():
        finish(os_ref, ys_ref)


def _b_out_proj(og_p, og_s, w_out, x1, g):
    tm = ROW_TILE
    rows_p, e = og_p.shape
    rows, d = x1.shape
    mp = rows_p // tm
    assert og_s.shape[0] == tm and rows == rows_p + tm and rows_p % tm == 0
    prow = lambda r: jnp.minimum(_row_block(r), mp - 1)
    return pl.pallas_call(
        functools.partial(_b_out_kernel, mp=mp),
        out_shape=[jax.ShapeDtypeStruct((rows_p, d), F32), jax.ShapeDtypeStruct((tm, d), F32)],
        grid=(1 + rows // tm,),
        in_specs=[
            pl.BlockSpec((tm, e), lambda r: (prow(r), 0)),
            _resident((tm, e), lambda r: (0, 0)),
            _resident((e, d), lambda r: (0, 0)),
            pl.BlockSpec((tm, d), lambda r: (_row_block(r), 0)),
            _resident((1, d), lambda r: (0, 0)),
        ],
        out_specs=[pl.BlockSpec((tm, d), lambda r: (prow(r), 0)),
                   pl.BlockSpec((tm, d), lambda r: (0, 0))],
        scratch_shapes=[pltpu.VMEM((e, d), BF16)],
        compiler_params=_params("arbitrary"),
        name="b_out_proj",
    )(og_p, og_s, w_out, x1, g)


def kernel(x_prompt, x_sample, state_ssm_re, state_ssm_im, cache_k, cache_v, a_norm, a_w_in, a_lambda_re, a_lambda_im, a_log_dt, a_b_re, a_b_im, a_c_re, a_c_im, a_d, a_w_glu, a_w_out, kv_norm, w_kv, b_norm, b_w_in, b_lambda_q1, b_lambda_k1, b_lambda_q2, b_lambda_k2, b_subln, b_w_out, final_norm):
    bp, seq, d = x_prompt.shape
    bs, t_s, _ = x_sample.shape
    past_len = cache_k.shape[1]
    n_a, n_b = a_norm.shape[0], b_norm.shape[0]
    g_cnt, p = a_lambda_re.shape[1:]
    assert bp == 1 and n_a == 1 and n_b == 1
    assert t_s == SSM_T and seq % (SSM_T * bs) == 0 and seq % CHUNK == 0
    assert cache_k.shape[2:] == (N_HEADS, 2 * HEAD_DIM)

    jp, js = seq // SSM_T, bs
    jpad = -(-(jp + js) // CHUNK_TILE) * CHUNK_TILE
    rows_s = bs * t_s
    rows_a = seq + rows_s

    xp = x_prompt.reshape(seq, d)
    xs = x_sample.reshape(rows_s, d)
    uz = _a_in_proj(xp, xs, a_norm[0].reshape(1, d), a_w_in[0], jpad * SSM_T, tn=1024)
    mt, bend, cin, a16r, a16i = _ssm_prep(a_lambda_re[0], a_lambda_im[0], a_log_dt[0],
                                          a_b_re[0], a_b_im[0], a_c_re[0], a_c_im[0])
    s_re, s_im = _ssm_state_contrib(uz, bend, jpad)
    h0r = state_ssm_re[:, 0].reshape(bs, g_cnt * p)
    h0i = state_ssm_im[:, 0].reshape(bs, g_cnt * p)
    hp_re, hp_im, pre, pim, sre, sim = _ssm_scan(s_re, s_im, a16r, a16i, h0r, h0i, jp)
    d_rep = jnp.broadcast_to(a_d[0].reshape(g_cnt, SSM_GROUP, 1), (g_cnt, SSM_GROUP, CHUNK_TILE))
    yg = _ssm_output(uz, mt, cin, hp_re, hp_im, d_rep, jpad)
    o_a = _a_glu(yg, uz, a_w_glu[0], rows_a)
    x1 = _a_out_proj(o_a, a_w_out[0], xp, xs)

    lambda_init = 0.8 - 0.6 * math.exp(-0.3 * n_a)
    w_q = b_w_in[0]
    kv_g = kv_norm.reshape(1, d)
    b_g = b_norm[0].reshape(1, d)
    rope = _rope_tables(seq, past_len, t_s, ROW_TILE)
    k_p, v_p, kvb_p, k_s, v_s, kvb_s = _kv_proj(x1, seq, kv_g, w_kv, rope)
    qb_p, z_p, qb_s, z_s = _qz_proj(x1, seq, b_g, w_q, rope, tn=1024)

    vecs = [a[0].reshape(1, HEAD_DIM) for a in (b_lambda_q1, b_lambda_k1, b_lambda_q2, b_lambda_k2)]
    subln = b_subln[0].reshape(1, 2 * HEAD_DIM)
    hw = 2 * HEAD_DIM
    og_p = _attn_prompt(qb_p, z_p, kvb_p, *vecs, subln, lambda_init, tq=1024, ts=512, tk=1024)
    og_s = _attn_sample(qb_s, z_s, kvb_s, cache_k.reshape(bs, past_len * N_HEADS, hw),
                        cache_v.reshape(bs, past_len * N_HEADS, hw), *vecs, subln, lambda_init,
                        t_s, tkv=1024)
    w_o = b_w_out[0]
    fg = final_norm.reshape(1, d)
    y_p, y_s = _b_out_proj(og_p, og_s, w_o, x1, fg)

    return (y_p.reshape(bp, seq, d), y_s.reshape(bs, t_s, d),
            pre.reshape(bp, n_a, g_cnt, p), pim.reshape(bp, n_a, g_cnt, p),
            k_p.reshape(bp, seq, N_HEADS, hw), v_p.reshape(bp, seq, N_HEADS, hw),
            sre.reshape(bs, n_a, g_cnt, p), sim.reshape(bs, n_a, g_cnt, p),
            k_s.reshape(bs, t_s, N_HEADS, hw), v_s.reshape(bs, t_s, N_HEADS, hw))
```

```python
import functools
import math

import jax
import jax.numpy as jnp
from jax import lax
from jax.experimental import pallas as pl
from jax.experimental.pallas import tpu as pltpu

F32 = jnp.float32
BF16 = jnp.bfloat16

CHUNK = 64
HEAD_DIM = 128
N_HEADS = 8
SSM_GROUP = 16
SSM_STATE = 64
SSM_T = 16
ROPE_THETA = 10000.0
Q_SCALE = HEAD_DIM ** -0.5 * math.log2(math.e)
EPS = 1e-6
NEG = -1e30

LANES = 128
SUBLANES = 8
GROUPS_PER_TILE = LANES // SSM_GROUP
CHUNK_TILE = 128
ROW_TILE = 512
WIDE_PROJ_COLS = 1024
ATTN_Q_TILE = 1024
ATTN_SUB_TILE = 512
ATTN_KEY_TILE = 1024
CACHE_KEY_TILE = 1024
STRIP = 32
VMEM_LIMIT = 56 * 1024 * 1024
KV_PROJ_VMEM_LIMIT = 58 * 1024 * 1024

_NT = (((1,), (1,)), ((), ()))


def _params(*sem, vmem_limit=VMEM_LIMIT):
    return pltpu.CompilerParams(dimension_semantics=sem, vmem_limit_bytes=vmem_limit)


def _resident(shape, index_map):
    return pl.BlockSpec(shape, index_map, pipeline_mode=pl.Buffered(1))


def _rms_scale(x, g):
    ms = jnp.mean(x * x, axis=-1, keepdims=True)
    return x * lax.rsqrt(ms + EPS) * g


def _staged_weight_spec(d, tn, nb):
    return pl.BlockSpec((d, tn), lambda r, n: (0, jnp.where(r == 0, n, nb - 1)),
                        pipeline_mode=pl.Buffered(1))


def _row_block(r):
    return jnp.maximum(r - 1, 0)


def _col_block(r, n):
    return jnp.where(r == 0, 0, n)


def _a_in_kernel(xp_ref, xs_ref, g_ref, w_ref, o_ref, xn_ref, wb_ref, *, mp, nb):
    m, n = pl.program_id(0) - 1, pl.program_id(1)

    @pl.when(m < 0)
    def _():
        wb_ref[n] = w_ref[...].astype(BF16)

    @pl.when((n == 0) & (m >= 0) & (m < mp))
    def _():
        xn_ref[...] = _rms_scale(xp_ref[...], g_ref[...]).astype(BF16)

    @pl.when((n == 0) & (m == mp))
    def _():
        xn_ref[...] = _rms_scale(xs_ref[...], g_ref[...]).astype(BF16)

    for c in range(nb):
        @pl.when((m >= 0) & (m <= mp) & (n == c))
        def _():
            o_ref[...] = jnp.dot(xn_ref[...], wb_ref[c], preferred_element_type=F32)

    @pl.when(m > mp)
    def _():
        o_ref[...] = jnp.zeros(o_ref.shape, F32)


def _a_in_proj(xp, xs, g, w, rows_pad, tn):
    tm = ROW_TILE
    d = g.shape[-1]
    n_out = w.shape[1]
    mp = xp.shape[0] // tm
    nb = n_out // tn
    assert xs.shape[0] == tm and xp.shape[0] % tm == 0 and rows_pad % tm == 0
    return pl.pallas_call(
        functools.partial(_a_in_kernel, mp=mp, nb=nb),
        out_shape=jax.ShapeDtypeStruct((rows_pad, n_out), F32),
        grid=(1 + rows_pad // tm, nb),
        in_specs=[
            pl.BlockSpec((tm, d), lambda r, n: (jnp.minimum(_row_block(r), mp - 1), 0)),
            _resident((tm, d), lambda r, n: (0, 0)),
            _resident((1, d), lambda r, n: (0, 0)),
            _staged_weight_spec(d, tn, nb),
        ],
        out_specs=pl.BlockSpec((tm, tn), lambda r, n: (_row_block(r), _col_block(r, n))),
        scratch_shapes=[pltpu.VMEM((tm, d), BF16), pltpu.VMEM((nb, d, tn), BF16)],
        compiler_params=_params("arbitrary", "arbitrary"),
        name="a_in_proj",
    )(xp, xs, g, w)


PREP_PAIRS = LANES // SSM_T


def _ssm_prep_kernel(lrx_ref, lix_ref, dtx_ref, bre_ref, bim_ref, lrr_ref, lir_ref, dtr_ref,
                     cre_ref, cim_ref, mt_ref, bend_ref, cin_ref, a16r_ref, a16i_ref):
    p = SSM_STATE
    p2 = 2 * p
    w = SSM_T * SSM_GROUP
    hi = lax.Precision.HIGHEST

    lr, li = lrx_ref[...], lix_ref[...]
    dt = jnp.exp(dtx_ref[...])
    zr, zi = lr * dt, li * dt
    mag = jnp.exp(zr)
    n_re, n_im = mag * jnp.cos(zi) - 1.0, mag * jnp.sin(zi)
    den = lr * lr + li * li
    cf_re = (n_re * lr + n_im * li) / den
    cf_im = (n_im * lr - n_re * li) / den
    lane = lax.broadcasted_iota(jnp.int32, (p2, LANES), 1)
    e_end = (SSM_T - 1 - lane % SSM_T).astype(F32)
    pm = jnp.exp(zr * e_end)
    pw_re, pw_im = pm * jnp.cos(zi * e_end), pm * jnp.sin(zi * e_end)
    cp_re = cf_re * pw_re - cf_im * pw_im
    cp_im = cf_re * pw_im + cf_im * pw_re
    coef = jnp.concatenate([cp_re, cp_im, cf_re, cf_im], axis=0)
    sel_row = lax.broadcasted_iota(jnp.int32, (LANES, w), 0)
    sel_col = lax.broadcasted_iota(jnp.int32, (LANES, w), 1)

    tau = lax.broadcasted_iota(jnp.int32, (SSM_T + SUBLANES, p2), 0).astype(F32)
    lane2 = lax.broadcasted_iota(jnp.int32, (w, p2), 1)
    sblk = lax.broadcasted_iota(jnp.int32, (w, w), 1) // SSM_GROUP

    for j in range(PREP_PAIRS):
        own_pair = (sel_row // SSM_T) == j
        spread = jnp.where(own_pair & (sel_row % SSM_T == sel_col // SSM_GROUP), 1.0, 0.0)
        cpx = jnp.dot(coef[:2 * p2], spread, precision=hi, preferred_element_type=F32)
        cfx = jnp.broadcast_to(coef[2 * p2:, j * SSM_T:j * SSM_T + 1], (2 * p2, w))
        b_re, b_im = jnp.tile(bre_ref[j], (1, SSM_T)), jnp.tile(bim_ref[j], (1, SSM_T))
        end_re = cpx[:p2] * b_re - cpx[p2:] * b_im
        end_im = cpx[:p2] * b_im + cpx[p2:] * b_re
        bb_re = cfx[:p2] * b_re - cfx[p2:] * b_im
        bb_im = cfx[:p2] * b_im + cfx[p2:] * b_re
        for r in range(2):
            sl = slice(r * p, (r + 1) * p)
            bend_ref[2 * j + r] = jnp.concatenate([end_re[sl], end_im[sl]], axis=0).astype(BF16)
        bst = jnp.concatenate([bb_re, bb_im], axis=0)

        lr2, li2 = lrr_ref[j], lir_ref[j]
        dt2 = jnp.exp(dtr_ref[j])
        zr2, zi2 = lr2 * dt2, li2 * dt2
        qm = jnp.exp(zr2 * tau)
        q_re, q_im = qm * jnp.cos(zi2 * tau), qm * jnp.sin(zi2 * tau)
        c_re, c_im = cre_ref[j], cim_ref[j]

        def c_times_powers(first):
            re = [c_re * q_re[t:t + 1] - c_im * q_im[t:t + 1] for t in range(first, first + SSM_T)]
            im = [c_re * q_im[t:t + 1] + c_im * q_re[t:t + 1] for t in range(first, first + SSM_T)]
            return jnp.concatenate(re, axis=0), jnp.concatenate(im, axis=0)

        g_re, g_im = c_times_powers(0)
        ci_re, ci_im = c_times_powers(1)
        for r in range(2):
            own = (lane2 // p) == r
            lhs = jnp.concatenate([jnp.where(own, g_re, 0.0), jnp.where(own, -g_im, 0.0)], axis=1)
            gen = jnp.dot(lhs, bst, preferred_element_type=F32)
            k = 0
            while (SSM_GROUP << k) < w:
                sh = SSM_GROUP << k
                shifted = jnp.concatenate([jnp.zeros((sh, w), F32), gen[:w - sh]], axis=0)
                gen = jnp.where(((sblk >> k) & 1) == 1, shifted, gen)
                k += 1
            mt_ref[2 * j + r] = gen.astype(BF16)
            cin_ref[2 * j + r] = jnp.concatenate(
                [jnp.where(own, ci_re, 0.0), jnp.where(own, -ci_im, 0.0)], axis=1).astype(BF16)
        a16r_ref[j] = q_re[SSM_T:SSM_T + 1]
        a16i_ref[j] = q_im[SSM_T:SSM_T + 1]


def _ssm_prep(lam_re, lam_im, log_dt, b_re, b_im, c_re, c_im):
    g, p = lam_re.shape
    gp = g // 2
    p2 = 2 * p
    w = SSM_T * SSM_GROUP
    pp = PREP_PAIRS
    assert gp % pp == 0
    colx = lambda a: jnp.repeat(a.reshape(gp, p2).T, SSM_T, axis=1)
    row = lambda a: a.reshape(gp, 1, p2)
    dt_full = jnp.broadcast_to(log_dt[:, None], (g, p))
    b_t = lambda a: a.reshape(gp, p2, SSM_GROUP)
    c_pair = lambda a: a.reshape(gp, 2, SSM_GROUP, p).transpose(0, 2, 1, 3).reshape(gp, SSM_GROUP, p2)
    xspec = pl.BlockSpec((p2, LANES), lambda i: (0, i))
    rowspec = pl.BlockSpec((pp, 1, p2), lambda i: (i, 0, 0))
    bspec = pl.BlockSpec((pp, p2, SSM_GROUP), lambda i: (i, 0, 0))
    cspec = pl.BlockSpec((pp, SSM_GROUP, p2), lambda i: (i, 0, 0))
    mt, bend, cin, a16r, a16i = pl.pallas_call(
        _ssm_prep_kernel,
        out_shape=[
            jax.ShapeDtypeStruct((g, w, w), BF16),
            jax.ShapeDtypeStruct((g, p2, w), BF16),
            jax.ShapeDtypeStruct((g, w, 2 * p2), BF16),
            jax.ShapeDtypeStruct((gp, 1, p2), F32),
            jax.ShapeDtypeStruct((gp, 1, p2), F32),
        ],
        grid=(gp // pp,),
        in_specs=[xspec, xspec, xspec, bspec, bspec, rowspec, rowspec, rowspec, cspec, cspec],
        out_specs=[
            pl.BlockSpec((2 * pp, w, w), lambda i: (i, 0, 0)),
            pl.BlockSpec((2 * pp, p2, w), lambda i: (i, 0, 0)),
            pl.BlockSpec((2 * pp, w, 2 * p2), lambda i: (i, 0, 0)),
            rowspec, rowspec,
        ],
        compiler_params=_params("arbitrary"),
        name="ssm_prep",
    )(colx(lam_re), colx(lam_im), colx(dt_full), b_t(b_re), b_t(b_im),
      row(lam_re), row(lam_im), row(dt_full), c_pair(c_re), c_pair(c_im))
    return mt, bend, cin, a16r.reshape(1, g * p), a16i.reshape(1, g * p)


def _step_rows(s):
    return pl.ds(s, CHUNK_TILE, stride=SSM_T)


def _build_ut(u_ref, ut_ref):
    for s in range(SSM_T):
        xt = u_ref[_step_rows(s), :].T.astype(BF16)
        for gl in range(GROUPS_PER_TILE):
            ut_ref[gl, s * SSM_GROUP:(s + 1) * SSM_GROUP, :] = xt[gl * SSM_GROUP:(gl + 1) * SSM_GROUP, :]


STATE_TILES = 2


def _ssm_state_kernel(*refs):
    u_refs = refs[:STATE_TILES]
    bend_ref, sre_ref, sim_ref = refs[STATE_TILES:STATE_TILES + 3]
    ut_refs = refs[STATE_TILES + 3:]
    p = SSM_STATE
    for tile in range(STATE_TILES):
        _build_ut(u_refs[tile], ut_refs[tile])
        for pr in range(GROUPS_PER_TILE // 2):
            st = [jnp.dot(bend_ref[tile * GROUPS_PER_TILE + 2 * pr + r], ut_refs[tile][2 * pr + r],
                          preferred_element_type=F32)
                  for r in range(2)]
            cols = slice((tile * GROUPS_PER_TILE // 2 + pr) * 2 * p,
                         (tile * GROUPS_PER_TILE // 2 + pr + 1) * 2 * p)
            sre_ref[:, cols] = jnp.concatenate([st[0][:p], st[1][:p]], axis=0).T
            sim_ref[:, cols] = jnp.concatenate([st[0][p:], st[1][p:]], axis=0).T


def _ssm_state_contrib(uz, bend, jpad):
    g, p2, w = bend.shape
    gstep = STATE_TILES * GROUPS_PER_TILE
    assert g % gstep == 0
    sw = gstep * SSM_STATE
    out = jax.ShapeDtypeStruct((jpad, g * SSM_STATE), F32)
    ospec = pl.BlockSpec((CHUNK_TILE, sw), lambda q, j: (j, q))
    utile = lambda t: pl.BlockSpec((CHUNK_TILE * SSM_T, LANES), lambda q, j: (j, STATE_TILES * q + t))
    return pl.pallas_call(
        _ssm_state_kernel,
        out_shape=[out, out],
        grid=(g // gstep, jpad // CHUNK_TILE),
        in_specs=[*[utile(t) for t in range(STATE_TILES)],
                  pl.BlockSpec((gstep, p2, w), lambda q, j: (q, 0, 0))],
        out_specs=[ospec, ospec],
        scratch_shapes=[pltpu.VMEM((GROUPS_PER_TILE, w, CHUNK_TILE), BF16) for _ in range(STATE_TILES)],
        compiler_params=_params("arbitrary", "arbitrary"),
        name="ssm_state_contrib",
    )(*[uz] * STATE_TILES, bend)


def _ssm_scan_kernel(sre_ref, sim_ref, ar_ref, ai_ref, h0r_ref, h0i_ref,
                     hpr_ref, hpi_ref, pr_ref, pi_ref, sr_ref, si_ref, hr_sc, hi_sc, *, n_prompt):
    i = pl.program_id(0)
    rb = sre_ref.shape[0]
    ar, ai = ar_ref[...], ai_ref[...]

    @pl.when(i == 0)
    def _():
        hr_sc[...] = jnp.zeros_like(hr_sc)
        hi_sc[...] = jnp.zeros_like(hi_sc)

    @pl.when(i < n_prompt)
    def _():
        def body(j, c):
            row = pl.ds(j, 1)
            hr, hi = hr_sc[...], hi_sc[...]
            hpr_ref[row, :] = hr
            hpi_ref[row, :] = hi
            hr_sc[...] = ar * hr - ai * hi + sre_ref[row, :]
            hi_sc[...] = ar * hi + ai * hr + sim_ref[row, :]
            return c

        lax.fori_loop(0, rb, body, 0)

    @pl.when(i == n_prompt - 1)
    def _():
        pr_ref[...] = hr_sc[...]
        pi_ref[...] = hi_sc[...]

    @pl.when(i == n_prompt)
    def _():
        h0r, h0i = h0r_ref[...], h0i_ref[...]
        hpr_ref[...] = h0r
        hpi_ref[...] = h0i
        sr_ref[...] = ar * h0r - ai * h0i + sre_ref[...]
        si_ref[...] = ar * h0i + ai * h0r + sim_ref[...]

    @pl.when(i > n_prompt)
    def _():
        hpr_ref[...] = jnp.zeros_like(hpr_ref)
        hpi_ref[...] = jnp.zeros_like(hpi_ref)


def _ssm_scan(s_re, s_im, a16r, a16i, h0r, h0i, jp):
    jpad, n = s_re.shape
    rb = h0r.shape[0]
    assert jp % rb == 0 and jpad % rb == 0
    n_prompt = jp // rb
    rows = pl.BlockSpec((rb, n), lambda i: (i, 0))
    const1 = pl.BlockSpec((1, n), lambda i: (0, 0))
    constb = pl.BlockSpec((rb, n), lambda i: (0, 0))
    big = jax.ShapeDtypeStruct((jpad, n), F32)
    one = jax.ShapeDtypeStruct((1, n), F32)
    bat = jax.ShapeDtypeStruct((rb, n), F32)
    return pl.pallas_call(
        functools.partial(_ssm_scan_kernel, n_prompt=n_prompt),
        out_shape=[big, big, one, one, bat, bat],
        grid=(jpad // rb,),
        in_specs=[rows, rows, const1, const1, constb, constb],
        out_specs=[rows, rows, const1, const1, constb, constb],
        scratch_shapes=[pltpu.VMEM((1, n), F32), pltpu.VMEM((1, n), F32)],
        compiler_params=_params("arbitrary"),
        name="ssm_scan",
    )(s_re, s_im, a16r, a16i, h0r, h0i)


def _ssm_out_kernel(u_ref, mt_ref, cin_ref, hpr_ref, hpi_ref, d_ref, o_ref, ut_ref, uf_ref, yt_ref):
    for s in range(SSM_T):
        xt = u_ref[_step_rows(s), :].T
        xb = xt.astype(BF16)
        for gl in range(GROUPS_PER_TILE):
            rows = slice(gl * SSM_GROUP, (gl + 1) * SSM_GROUP)
            ut_ref[gl, s * SSM_GROUP:(s + 1) * SSM_GROUP, :] = xb[rows, :]
            uf_ref[gl, s * SSM_GROUP:(s + 1) * SSM_GROUP, :] = xt[rows, :]
    p2 = 2 * SSM_STATE
    for gl in range(GROUPS_PER_TILE):
        pr = gl // 2
        hp = jnp.concatenate([hpr_ref[:, pr * p2:(pr + 1) * p2], hpi_ref[:, pr * p2:(pr + 1) * p2]],
                             axis=1).astype(BF16)
        yt = jnp.dot(mt_ref[gl], ut_ref[gl], preferred_element_type=F32)
        yt = yt + lax.dot_general(cin_ref[gl], hp, _NT, preferred_element_type=F32)
        yt = jax.nn.gelu(yt + jnp.tile(d_ref[gl], (SSM_T, 1)) * uf_ref[gl])
        for t in range(SSM_T):
            yt_ref[t, gl * SSM_GROUP:(gl + 1) * SSM_GROUP, :] = yt[t * SSM_GROUP:(t + 1) * SSM_GROUP, :]
    for t in range(SSM_T):
        o_ref[_step_rows(t), :] = yt_ref[t].T


def _ssm_output(uz, mt, cin, hp_re, hp_im, d_skip, jpad):
    g, w, _ = mt.shape
    w2 = cin.shape[-1]
    ntile = g // GROUPS_PER_TILE
    sw = GROUPS_PER_TILE * SSM_STATE
    tok = pl.BlockSpec((CHUNK_TILE * SSM_T, LANES), lambda q, j: (j, q))
    hspec = pl.BlockSpec((CHUNK_TILE, sw), lambda q, j: (j, q))
    return pl.pallas_call(
        _ssm_out_kernel,
        out_shape=jax.ShapeDtypeStruct((jpad * SSM_T, g * SSM_GROUP), F32),
        grid=(ntile, jpad // CHUNK_TILE),
        in_specs=[
            tok,
            pl.BlockSpec((GROUPS_PER_TILE, w, w), lambda q, j: (q, 0, 0)),
            pl.BlockSpec((GROUPS_PER_TILE, w, w2), lambda q, j: (q, 0, 0)),
            hspec, hspec,
            pl.BlockSpec((GROUPS_PER_TILE, SSM_GROUP, CHUNK_TILE), lambda q, j: (q, 0, 0)),
        ],
        out_specs=tok,
        scratch_shapes=[
            pltpu.VMEM((GROUPS_PER_TILE, w, CHUNK_TILE), BF16),
            pltpu.VMEM((GROUPS_PER_TILE, w, CHUNK_TILE), F32),
            pltpu.VMEM((SSM_T, LANES, CHUNK_TILE), F32),
        ],
        compiler_params=_params("arbitrary", "arbitrary"),
        name="ssm_output",
    )(uz, mt, cin, hp_re, hp_im, d_skip)


def _a_glu_kernel(y_ref, z_ref, w_ref, o_ref, wb_ref):
    r = pl.program_id(0)

    @pl.when(r == 0)
    def _():
        wb_ref[...] = w_ref[...].astype(BF16)

    @pl.when(r > 0)
    def _():
        y = y_ref[...]
        gate = jnp.dot(y.astype(BF16), wb_ref[...], preferred_element_type=F32)
        y2 = y * jax.nn.sigmoid(gate)
        o_ref[...] = (y2 * jax.nn.silu(z_ref[...])).astype(BF16)


def _a_glu(yg, uz, w_glu, rows):
    tm = ROW_TILE
    e = yg.shape[1]
    return pl.pallas_call(
        _a_glu_kernel,
        out_shape=jax.ShapeDtypeStruct((rows, e), BF16),
        grid=(1 + rows // tm,),
        in_specs=[
            pl.BlockSpec((tm, e), lambda r: (_row_block(r), 0)),
            pl.BlockSpec((tm, e), lambda r: (_row_block(r), 1)),
            _resident((e, e), lambda r: (0, 0)),
        ],
        out_specs=pl.BlockSpec((tm, e), lambda r: (_row_block(r), 0)),
        scratch_shapes=[pltpu.VMEM((e, e), BF16)],
        compiler_params=_params("arbitrary"),
        name="a_glu",
    )(yg, uz, w_glu)


def _a_out_kernel(o_ref, w_ref, xp_ref, xs_ref, x1_ref, wb_ref, *, mp):
    m = pl.program_id(0) - 1

    @pl.when(m < 0)
    def _():
        wb_ref[...] = w_ref[...].astype(BF16)

    @pl.when((m >= 0) & (m < mp))
    def _():
        x1_ref[...] = xp_ref[...] + jnp.dot(o_ref[...], wb_ref[...], preferred_element_type=F32)

    @pl.when(m == mp)
    def _():
        x1_ref[...] = xs_ref[...] + jnp.dot(o_ref[...], wb_ref[...], preferred_element_type=F32)


def _a_out_proj(o, w_out, xp, xs):
    tm = ROW_TILE
    rows, e = o.shape
    d = w_out.shape[1]
    mp = xp.shape[0] // tm
    return pl.pallas_call(
        functools.partial(_a_out_kernel, mp=mp),
        out_shape=jax.ShapeDtypeStruct((rows, d), F32),
        grid=(1 + rows // tm,),
        in_specs=[
            pl.BlockSpec((tm, e), lambda r: (_row_block(r), 0)),
            _resident((e, d), lambda r: (0, 0)),
            pl.BlockSpec((tm, d), lambda r: (jnp.minimum(_row_block(r), mp - 1), 0)),
            _resident((tm, d), lambda r: (0, 0)),
        ],
        out_specs=pl.BlockSpec((tm, d), lambda r: (_row_block(r), 0)),
        scratch_shapes=[pltpu.VMEM((e, d), BF16)],
        compiler_params=_params("arbitrary"),
        name="a_out_proj",
    )(o, w_out, xp, xs)


def _rope(x, cos, sin):
    outs = []
    for c in range(x.shape[1] // HEAD_DIM):
        xc = x[:, c * HEAD_DIM:(c + 1) * HEAD_DIM]
        outs.append(xc * cos + pltpu.roll(xc, HEAD_DIM // 2, axis=1) * sin)
    return jnp.concatenate(outs, axis=1) if len(outs) > 1 else outs[0]


def _rope_block(co_ref, so_ref, cb_ref, sb_ref):
    co, so = co_ref[...], so_ref[...]
    cb, sb = cb_ref[...], sb_ref[...]
    lane = lax.broadcasted_iota(jnp.int32, (1, HEAD_DIM), 1)
    sign = jnp.where(lane < HEAD_DIM // 2, -1.0, 1.0)
    return cb * co - sb * so, (sb * co + cb * so) * sign


def _store_heads(o_ref, val, tm):
    for h in range(N_HEADS):
        o_ref[pl.ds(h, tm, stride=N_HEADS), :] = val[:, h * HEAD_DIM:(h + 1) * HEAD_DIM]


def _prompt_col(r, mp, col, last):
    return jnp.where(_row_block(r) >= mp, last, col)


def _sample_col(r, mp, col):
    return jnp.where(_row_block(r) == mp, col, 0)


def _kv_proj_kernel(x_ref, g_ref, w_ref, co_ref, so_ref, cb_ref, sb_ref, kp_ref, vp_ref, cbp_ref,
                    ks_ref, vs_ref, cbs_ref, xn_ref, wb_ref, *, tm, mp):
    m, n = pl.program_id(0) - 1, pl.program_id(1)
    hpb = w_ref.shape[1] // (2 * HEAD_DIM)

    for nn in range(4):
        @pl.when((m < 0) & (n == nn))
        def _():
            for hl in range(hpb):
                h = hpb * (nn % 2) + hl
                for half in range(2):
                    src = (2 * hl + half) * HEAD_DIM
                    wb_ref[2 * (nn // 2) + half, :, h * HEAD_DIM:(h + 1) * HEAD_DIM] = (
                        w_ref[:, src:src + HEAD_DIM].astype(BF16))

    @pl.when((m >= 0) & (n == 0))
    def _():
        xn_ref[...] = _rms_scale(x_ref[...], g_ref[...]).astype(BF16)

    def column(c, k_ref, v_ref, copy_ref):
        acc = jnp.dot(xn_ref[...], wb_ref[c], preferred_element_type=F32)
        if c < 2:
            acc = _rope(acc, *_rope_block(co_ref, so_ref, cb_ref, sb_ref))
        _store_heads(k_ref if c < 2 else v_ref, acc, tm)
        copy_ref[...] = acc.astype(BF16)

    for c in range(4):
        @pl.when((m >= 0) & (m < mp) & (n == c))
        def _():
            column(c, kp_ref, vp_ref, cbp_ref)

        @pl.when((m == mp) & (n == c))
        def _():
            column(c, ks_ref, vs_ref, cbs_ref)


def _kv_proj(x1, rows_p, g, w_kv, rope):
    tm = ROW_TILE
    rows, d = x1.shape
    tn = N_HEADS * HEAD_DIM
    mp = rows_p // tm
    assert w_kv.shape[1] == 4 * tn and rows == rows_p + tm and rows_p % tm == 0
    out4 = lambda nrows: jax.ShapeDtypeStruct((nrows * N_HEADS, 2 * HEAD_DIM), F32)
    outb = lambda nrows: jax.ShapeDtypeStruct((nrows, 4 * tn), BF16)
    prow = lambda r: jnp.minimum(_row_block(r), mp - 1)
    col = _col_block
    return pl.pallas_call(
        functools.partial(_kv_proj_kernel, tm=tm, mp=mp),
        out_shape=[out4(rows_p), out4(rows_p), outb(rows_p), out4(tm), out4(tm), outb(tm)],
        grid=(1 + rows // tm, 4),
        in_specs=[
            pl.BlockSpec((tm, d), lambda r, n: (_row_block(r), 0)),
            _resident((1, d), lambda r, n: (0, 0)),
            _staged_weight_spec(d, tn, 4),
            *_rope_specs(tm, mp),
        ],
        out_specs=[
            pl.BlockSpec((tm * N_HEADS, HEAD_DIM),
                         lambda r, n: (prow(r), _prompt_col(r, mp, jnp.minimum(col(r, n), 1), 1))),
            pl.BlockSpec((tm * N_HEADS, HEAD_DIM),
                         lambda r, n: (prow(r), _prompt_col(r, mp, jnp.maximum(col(r, n) - 2, 0), 1))),
            pl.BlockSpec((tm, tn), lambda r, n: (prow(r), _prompt_col(r, mp, col(r, n), 3))),
            pl.BlockSpec((tm * N_HEADS, HEAD_DIM),
                         lambda r, n: (0, _sample_col(r, mp, jnp.minimum(col(r, n), 1)))),
            pl.BlockSpec((tm * N_HEADS, HEAD_DIM),
                         lambda r, n: (0, _sample_col(r, mp, jnp.maximum(col(r, n) - 2, 0)))),
            pl.BlockSpec((tm, tn), lambda r, n: (0, _sample_col(r, mp, col(r, n)))),
        ],
        scratch_shapes=[pltpu.VMEM((tm, d), BF16), pltpu.VMEM((4, d, tn), BF16)],
        compiler_params=_params("arbitrary", "arbitrary", vmem_limit=KV_PROJ_VMEM_LIMIT),
        name="kv_proj",
    )(x1, g, w_kv, *rope)


def _qz_proj_kernel(x_ref, g_ref, w_ref, co_ref, so_ref, cb_ref, sb_ref, qp_ref, zp_ref, qs_ref, zs_ref,
                    xn_ref, wb_ref, *, nq, mp):
    m, n = pl.program_id(0) - 1, pl.program_id(1)

    @pl.when(m < 0)
    def _():
        wb_ref[n] = w_ref[...].astype(BF16)

    @pl.when((m >= 0) & (n == 0))
    def _():
        xn_ref[...] = _rms_scale(x_ref[...], g_ref[...]).astype(BF16)

    def column(c, q_ref, z_ref):
        acc = jnp.dot(xn_ref[...], wb_ref[c], preferred_element_type=F32)
        if c < nq:
            rot = _rope(acc, *_rope_block(co_ref, so_ref, cb_ref, sb_ref))
            q_ref[...] = (rot * Q_SCALE).astype(BF16)
        else:
            z_ref[...] = acc

    for c in range(2 * nq):
        @pl.when((m >= 0) & (m < mp) & (n == c))
        def _():
            column(c, qp_ref, zp_ref)

        @pl.when((m == mp) & (n == c))
        def _():
            column(c, qs_ref, zs_ref)


def _qz_proj(x1, rows_p, g, w, rope, tn):
    tm = ROW_TILE
    rows, d = x1.shape
    half = w.shape[1] // 2
    nq = half // tn
    mp = rows_p // tm
    assert rows == rows_p + tm and rows_p % tm == 0
    prow = lambda r: jnp.minimum(_row_block(r), mp - 1)
    qcol = lambda r, n: jnp.minimum(_col_block(r, n), nq - 1)
    zcol = lambda r, n: jnp.maximum(_col_block(r, n) - nq, 0)
    return pl.pallas_call(
        functools.partial(_qz_proj_kernel, nq=nq, mp=mp),
        out_shape=[jax.ShapeDtypeStruct((rows_p, half), BF16), jax.ShapeDtypeStruct((rows_p, half), F32),
                   jax.ShapeDtypeStruct((tm, half), BF16), jax.ShapeDtypeStruct((tm, half), F32)],
        grid=(1 + rows // tm, 2 * nq),
        in_specs=[
            pl.BlockSpec((tm, d), lambda r, n: (_row_block(r), 0)),
            _resident((1, d), lambda r, n: (0, 0)),
            _staged_weight_spec(d, tn, 2 * nq),
            *_rope_specs(tm, mp),
        ],
        out_specs=[
            pl.BlockSpec((tm, tn), lambda r, n: (prow(r), _prompt_col(r, mp, qcol(r, n), nq - 1))),
            pl.BlockSpec((tm, tn), lambda r, n: (prow(r), _prompt_col(r, mp, zcol(r, n), nq - 1))),
            pl.BlockSpec((tm, tn), lambda r, n: (0, _sample_col(r, mp, qcol(r, n)))),
            pl.BlockSpec((tm, tn), lambda r, n: (0, _sample_col(r, mp, zcol(r, n)))),
        ],
        scratch_shapes=[pltpu.VMEM((tm, d), BF16), pltpu.VMEM((2 * nq, d, tn), BF16)],
        compiler_params=_params("arbitrary", "arbitrary"),
        name="qz_proj",
    )(x1, g, w, *rope)


def _rope_specs(tm, mp):
    off = pl.BlockSpec((None, tm, HEAD_DIM), lambda r, n: (jnp.where(_row_block(r) == mp, 1, 0), 0, 0))
    base = pl.BlockSpec((None, 1, HEAD_DIM), lambda r, n: (_row_block(r), 0, 0))
    return [off, off, base, base]


def _rope_tables(rows_p, past_len, t_s, tm):
    inv_freq = ROPE_THETA ** (-jnp.arange(0, HEAD_DIM, 2, dtype=F32) / HEAD_DIM)
    full = lambda a: jnp.concatenate([a, a], axis=-1)
    rows = jnp.arange(tm, dtype=jnp.int32)
    off = jnp.stack([rows, rows % t_s]).astype(F32)[..., None] * inv_freq
    base = jnp.concatenate([jnp.arange(0, rows_p, tm, dtype=jnp.int32),
                            jnp.full((1,), past_len, jnp.int32)]).astype(F32)[:, None, None] * inv_freq
    return full(jnp.cos(off)), full(jnp.sin(off)), full(jnp.cos(base)), full(jnp.sin(base))


def _diff_lambda(lq1_ref, lk1_ref, lq2_ref, lk2_ref, lambda_init):
    s1 = jnp.sum(lq1_ref[...] * lk1_ref[...], axis=-1, keepdims=True)
    s2 = jnp.sum(lq2_ref[...] * lk2_ref[...], axis=-1, keepdims=True)
    return jnp.exp(s1) - jnp.exp(s2) + lambda_init


def _attn_finish(a1, l1, a2, l2, lam, subln, z, lambda_init):
    o = a1 / l1 - lam * (a2 / l2)
    o = _rms_scale(o, subln) * (1.0 - lambda_init)
    return (o * jax.nn.silu(z)).astype(BF16)


def _attn_prompt_kernel(q_ref, k1_ref, k2_ref, v1_ref, v2_ref, z_ref,
                        lq1_ref, lk1_ref, lq2_ref, lk2_ref, sub_ref,
                        o_ref, m_sc, l_sc, al_sc, a_sc, *sp_scs, tq, ts, tk, lambda_init):
    qi = pl.program_id(1)
    k_refs = (k1_ref, k2_ref)
    nsub = tq // ts
    s_scs, p_scs = sp_scs[:2 * nsub], sp_scs[2 * nsub:]

    def softmax_strips(sub, n, col0, first):
        for r in range(ts // STRIP):
            row0 = sub * ts + r * STRIP
            lr = slice(r * STRIP, (r + 1) * STRIP)
            gr = slice(row0, row0 + STRIP)
            visible = tk if col0 is None else min(tk, (row0 // CHUNK + 1) * CHUNK - col0)
            assert visible > 0
            ncv = -(-visible // LANES)
            s = s_scs[2 * sub + n][lr, 0:ncv * LANES]
            if visible < ncv * LANES:
                col = lax.broadcasted_iota(jnp.int32, s.shape, 1)
                s = jnp.where(col < visible, s, NEG)
            cols = [s[:, c * LANES:(c + 1) * LANES] for c in range(ncv)]
            m_cur = jnp.max(functools.reduce(jnp.maximum, cols), axis=-1, keepdims=True)
            if first:
                m_new = jnp.broadcast_to(m_cur, (STRIP, LANES))
            else:
                m_old = m_sc[n, gr, :]
                m_new = jnp.maximum(m_old, m_cur)
            ps = [jnp.exp2(c - m_new) for c in cols]
            l_cur = jnp.sum(functools.reduce(jnp.add, ps), axis=-1, keepdims=True)
            if first:
                l_sc[n, gr, :] = jnp.broadcast_to(l_cur, (STRIP, LANES))
            else:
                alpha = jnp.exp2(m_old - m_new)
                l_sc[n, gr, :] = alpha * l_sc[n, gr, :] + l_cur
                al_sc[n, gr, :] = alpha
            m_sc[n, gr, :] = m_new
            for c in range(keys_seen(sub, col0) // LANES):
                pc = ps[c].astype(BF16) if c < ncv else jnp.zeros((STRIP, LANES), BF16)
                p_scs[2 * sub + n][lr, c * LANES:(c + 1) * LANES] = pc

    def keys_seen(sub, col0):
        return tk if col0 is None else min(tk, (sub + 1) * ts - col0)

    def block(kb, col0, subs, first=False):
        start = pl.multiple_of(kb * tk, tk)
        for sub in subs:
            rows = pl.ds(start, keys_seen(sub, col0))
            for n in range(2):
                qn = q_ref[sub * ts:(sub + 1) * ts, n * HEAD_DIM:(n + 1) * HEAD_DIM]
                s_scs[2 * sub + n][:, 0:rows.size] = lax.dot_general(
                    qn, k_refs[n][rows, :], _NT, preferred_element_type=F32)
        for sub in subs:
            gr = slice(sub * ts, (sub + 1) * ts)
            rows = pl.ds(start, keys_seen(sub, col0))
            vblk = jnp.concatenate([v1_ref[rows, :], v2_ref[rows, :]], axis=1)
            for n in range(2):
                softmax_strips(sub, n, col0, first)
                pv = jnp.dot(p_scs[2 * sub + n][:, 0:rows.size], vblk, preferred_element_type=F32)
                if first:
                    a_sc[n, gr, :] = pv
                else:
                    alpha = al_sc[n, gr, :]
                    a_sc[n, gr, :] = a_sc[n, gr, :] * jnp.concatenate([alpha, alpha], axis=1) + pv

    nd = tq // tk
    for d in range(nd):
        block(qi * nd + d, d * tk, [sub for sub in range(nsub) if (sub + 1) * ts > d * tk],
              first=(d == 0))

    def body(kb, c):
        block(kb, None, range(nsub))
        return c

    lax.fori_loop(0, qi * nd, body, 0)

    lam = _diff_lambda(lq1_ref, lk1_ref, lq2_ref, lk2_ref, lambda_init)
    l1 = jnp.concatenate([l_sc[0], l_sc[0]], axis=1)
    l2 = jnp.concatenate([l_sc[1], l_sc[1]], axis=1)
    o_ref[...] = _attn_finish(a_sc[0], l1, a_sc[1], l2, lam, sub_ref[...], z_ref[...], lambda_init)


def _attn_prompt(qb, z32, kvb, lq1, lk1, lq2, lk2, subln, lambda_init, tq, ts, tk):
    rows = qb.shape[0]
    hw = 2 * HEAD_DIM
    nsub = tq // ts
    assert tq % tk == 0 and tk % ts == 0 and ts % STRIP == 0 and CHUNK % STRIP == 0
    vec = pl.BlockSpec((1, HEAD_DIM), lambda h, i: (0, 0))
    kcol = lambda c: pl.BlockSpec((rows, HEAD_DIM), lambda h, i: (0, c * N_HEADS + h))
    stat = pltpu.VMEM((2, tq, LANES), F32)
    return pl.pallas_call(
        functools.partial(_attn_prompt_kernel, tq=tq, ts=ts, tk=tk, lambda_init=lambda_init),
        out_shape=jax.ShapeDtypeStruct((rows, N_HEADS * hw), BF16),
        grid=(N_HEADS, rows // tq),
        in_specs=[
            pl.BlockSpec((tq, hw), lambda h, i: (i, h)),
            kcol(0), kcol(1), kcol(2), kcol(3),
            pl.BlockSpec((tq, hw), lambda h, i: (i, h)),
            vec, vec, vec, vec,
            pl.BlockSpec((1, hw), lambda h, i: (0, 0)),
        ],
        out_specs=pl.BlockSpec((tq, hw), lambda h, i: (i, h)),
        scratch_shapes=[
            stat, stat, stat,
            pltpu.VMEM((2, tq, hw), F32),
            *[pltpu.VMEM((ts, tk), F32) for _ in range(2 * nsub)],
            *[pltpu.VMEM((ts, tk), BF16) for _ in range(2 * nsub)],
        ],
        compiler_params=_params("arbitrary", "arbitrary"),
        name="attn_prompt",
    )(qb, kvb, kvb, kvb, kvb, z32, lq1, lk1, lq2, lk2, subln)


def _attn_sample_kernel(q_ref, ck1_ref, ck2_ref, cv1_ref, cv2_ref, kvn_ref, z_ref,
                        lq1_ref, lk1_ref, lq2_ref, lk2_ref, sub_ref,
                        o_ref, m_sc, l_sc, a_sc, s_sc, p_sc, *, tkv, nkb, past_len, lambda_init):
    kb = pl.program_id(1)
    t = q_ref.shape[0]
    hw = 2 * HEAD_DIM
    rows_all = 2 * N_HEADS * t
    strip = 4 * t
    ck_refs = (ck1_ref, ck2_ref)
    assert (past_len - 1) // CHUNK <= past_len // CHUNK

    @pl.when(kb == 0)
    def _():
        m_sc[...] = jnp.full(m_sc.shape, NEG, F32)
        l_sc[...] = jnp.zeros(l_sc.shape, F32)
        a_sc[...] = jnp.zeros(a_sc.shape, F32)

    col_blk = lambda c, h: slice((c * N_HEADS + h) * HEAD_DIM, (c * N_HEADS + h + 1) * HEAD_DIM)
    head_rows = lambda h: pl.ds(h, tkv, stride=N_HEADS)

    def update(with_new):
        ncols = tkv + (LANES if with_new else 0)
        if with_new:
            row = lax.broadcasted_iota(jnp.int32, (t, LANES), 0) + past_len
            col = lax.broadcasted_iota(jnp.int32, (t, LANES), 1)
            new_mask = (col < t) & (((col + past_len) // CHUNK) <= (row // CHUNK))
            pad_k = jnp.zeros((LANES - t, HEAD_DIM), BF16)
        for h in range(N_HEADS):
            for n in range(2):
                qn = q_ref[:, (2 * h + n) * HEAD_DIM:(2 * h + n + 1) * HEAD_DIM]
                rr = slice((2 * h + n) * t, (2 * h + n + 1) * t)
                k = ck_refs[n][head_rows(h), :].astype(BF16)
                s_sc[rr, 0:tkv] = lax.dot_general(qn, k, _NT, preferred_element_type=F32)
                if with_new:
                    kn = jnp.concatenate([kvn_ref[:, col_blk(n, h)], pad_k], axis=0)
                    sn = lax.dot_general(qn, kn, _NT, preferred_element_type=F32)
                    s_sc[rr, tkv:ncols] = jnp.where(new_mask, sn, NEG)
        for r in range(rows_all // strip):
            rr = slice(r * strip, (r + 1) * strip)
            cols = [s_sc[rr, c * LANES:(c + 1) * LANES] for c in range(ncols // LANES)]
            m_cur = jnp.max(functools.reduce(jnp.maximum, cols), axis=-1, keepdims=True)
            m_old = m_sc[rr, :]
            m_new = jnp.maximum(m_old, m_cur)
            ps = [jnp.exp2(c - m_new) for c in cols]
            l_cur = jnp.sum(functools.reduce(jnp.add, ps), axis=-1, keepdims=True)
            alpha = jnp.exp2(m_old - m_new)
            l_sc[rr, :] = alpha * l_sc[rr, :] + l_cur
            m_sc[rr, :] = m_new
            a_sc[rr, :] = a_sc[rr, :] * jnp.concatenate([alpha, alpha], axis=1)
            for c, pc in enumerate(ps):
                p_sc[rr, c * LANES:(c + 1) * LANES] = pc.astype(BF16)
        for h in range(N_HEADS):
            rr = slice(2 * h * t, (2 * h + 2) * t)
            v = jnp.concatenate([cv1_ref[head_rows(h), :], cv2_ref[head_rows(h), :]],
                                axis=1).astype(BF16)
            pv = jnp.dot(p_sc[rr, 0:tkv], v, preferred_element_type=F32)
            if with_new:
                vn = jnp.concatenate([kvn_ref[:, col_blk(2, h)], kvn_ref[:, col_blk(3, h)]], axis=1)
                vn = jnp.concatenate([vn, jnp.zeros((LANES - t, hw), BF16)], axis=0)
                pv = pv + jnp.dot(p_sc[rr, tkv:ncols], vn, preferred_element_type=F32)
            a_sc[rr, :] = a_sc[rr, :] + pv

    @pl.when(kb < nkb - 1)
    def _():
        update(False)

    @pl.when(kb == nkb - 1)
    def _():
        update(True)
        lam = _diff_lambda(lq1_ref, lk1_ref, lq2_ref, lk2_ref, lambda_init)
        sub = sub_ref[...]
        for h in range(N_HEADS):
            r1 = slice(2 * h * t, (2 * h + 1) * t)
            r2 = slice((2 * h + 1) * t, (2 * h + 2) * t)
            l1 = jnp.concatenate([l_sc[r1, :], l_sc[r1, :]], axis=1)
            l2 = jnp.concatenate([l_sc[r2, :], l_sc[r2, :]], axis=1)
            o_ref[:, h * hw:(h + 1) * hw] = _attn_finish(
                a_sc[r1, :], l1, a_sc[r2, :], l2, lam, sub, z_ref[:, h * hw:(h + 1) * hw], lambda_init)


def _attn_sample(qb, z32, kvb, cache_k2, cache_v2, lq1, lk1, lq2, lk2, subln, lambda_init, t, tkv):
    bsz, rows8, hw = cache_k2.shape
    past_len = rows8 // N_HEADS
    nkb = past_len // tkv
    rows_all = 2 * N_HEADS * t
    assert t <= LANES
    vec = pl.BlockSpec((1, HEAD_DIM), lambda b, k: (0, 0))
    cache = lambda c: pl.BlockSpec((None, tkv * N_HEADS, HEAD_DIM), lambda b, k: (b, k, c))
    full = lambda a: pl.BlockSpec((t, a.shape[1]), lambda b, k: (b, 0))
    return pl.pallas_call(
        functools.partial(_attn_sample_kernel, tkv=tkv, nkb=nkb, past_len=past_len,
                          lambda_init=lambda_init),
        out_shape=jax.ShapeDtypeStruct((bsz * t, N_HEADS * hw), BF16),
        grid=(bsz, nkb),
        in_specs=[
            full(qb), cache(0), cache(1), cache(0), cache(1), full(kvb), full(z32),
            vec, vec, vec, vec,
            pl.BlockSpec((1, hw), lambda b, k: (0, 0)),
        ],
        out_specs=pl.BlockSpec((t, N_HEADS * hw), lambda b, k: (b, 0)),
        scratch_shapes=[
            pltpu.VMEM((rows_all, LANES), F32),
            pltpu.VMEM((rows_all, LANES), F32),
            pltpu.VMEM((rows_all, hw), F32),
            pltpu.VMEM((rows_all, tkv + LANES), F32),
            pltpu.VMEM((rows_all, tkv + LANES), BF16),
        ],
        compiler_params=_params("arbitrary", "arbitrary"),
        name="attn_sample",
    )(qb, cache_k2, cache_k2, cache_v2, cache_v2, kvb, z32, lq1, lk1, lq2, lk2, subln)


def _b_out_kernel(op_ref, os_ref, w_ref, x_ref, g_ref, yp_ref, ys_ref, wb_ref, *, mp):
    m = pl.program_id(0) - 1

    @pl.when(m < 0)
    def _():
        wb_ref[...] = w_ref[...].astype(BF16)

    def finish(o_ref, y_ref):
        acc = jnp.dot(o_ref[...], wb_ref[...], preferred_element_type=F32)
        y_ref[...] = _rms_scale(x_ref[...] + acc, g_ref[...])

    @pl.when((m >= 0) & (m < mp))
    def _():
        finish(op_ref, yp_ref)

    @pl.when(m == mp)
    def _():
        finish(os_ref, ys_ref)


def _b_out_proj(og_p, og_s, w_out, x1, g):
    tm = ROW_TILE
    rows_p, e = og_p.shape
    rows, d = x1.shape
    mp = rows_p // tm
    assert og_s.shape[0] == tm and rows == rows_p + tm and rows_p % tm == 0
    prow = lambda r: jnp.minimum(_row_block(r), mp - 1)
    return pl.pallas_call(
        functools.partial(_b_out_kernel, mp=mp),
        out_shape=[jax.ShapeDtypeStruct((rows_p, d), F32), jax.ShapeDtypeStruct((tm, d), F32)],
        grid=(1 + rows // tm,),
        in_specs=[
            pl.BlockSpec((tm, e), lambda r: (prow(r), 0)),
            _resident((tm, e), lambda r: (0, 0)),
            _resident((e, d), lambda r: (0, 0)),
            pl.BlockSpec((tm, d), lambda r: (_row_block(r), 0)),
            _resident((1, d), lambda r: (0, 0)),
        ],
        out_specs=[pl.BlockSpec((tm, d), lambda r: (prow(r), 0)),
                   pl.BlockSpec((tm, d), lambda r: (0, 0))],
        scratch_shapes=[pltpu.VMEM((e, d), BF16)],
        compiler_params=_params("arbitrary"),
        name="b_out_proj",
    )(og_p, og_s, w_out, x1, g)


def kernel(x_prompt, x_sample, state_ssm_re, state_ssm_im, cache_k, cache_v, a_norm, a_w_in, a_lambda_re, a_lambda_im, a_log_dt, a_b_re, a_b_im, a_c_re, a_c_im, a_d, a_w_glu, a_w_out, kv_norm, w_kv, b_norm, b_w_in, b_lambda_q1, b_lambda_k1, b_lambda_q2, b_lambda_k2, b_subln, b_w_out, final_norm):
    bp, seq, d = x_prompt.shape
    bs, t_s, _ = x_sample.shape
    past_len = cache_k.shape[1]
    n_a, n_b = a_norm.shape[0], b_norm.shape[0]
    g_cnt, p = a_lambda_re.shape[1:]
    assert bp == 1 and n_a == 1 and n_b == 1
    assert t_s == SSM_T and seq % (SSM_T * bs) == 0 and seq % CHUNK == 0
    assert cache_k.shape[2:] == (N_HEADS, 2 * HEAD_DIM)

    jp, js = seq // SSM_T, bs
    jpad = -(-(jp + js) // CHUNK_TILE) * CHUNK_TILE
    rows_s = bs * t_s
    rows_a = seq + rows_s

    xp = x_prompt.reshape(seq, d)
    xs = x_sample.reshape(rows_s, d)
    uz = _a_in_proj(xp, xs, a_norm[0].reshape(1, d), a_w_in[0], jpad * SSM_T, tn=WIDE_PROJ_COLS)
    mt, bend, cin, a16r, a16i = _ssm_prep(a_lambda_re[0], a_lambda_im[0], a_log_dt[0],
                                          a_b_re[0], a_b_im[0], a_c_re[0], a_c_im[0])
    s_re, s_im = _ssm_state_contrib(uz, bend, jpad)
    h0r = state_ssm_re[:, 0].reshape(bs, g_cnt * p)
    h0i = state_ssm_im[:, 0].reshape(bs, g_cnt * p)
    hp_re, hp_im, pre, pim, sre, sim = _ssm_scan(s_re, s_im, a16r, a16i, h0r, h0i, jp)
    d_rep = jnp.broadcast_to(a_d[0].reshape(g_cnt, SSM_GROUP, 1), (g_cnt, SSM_GROUP, CHUNK_TILE))
    yg = _ssm_output(uz, mt, cin, hp_re, hp_im, d_rep, jpad)
    o_a = _a_glu(yg, uz, a_w_glu[0], rows_a)
    x1 = _a_out_proj(o_a, a_w_out[0], xp, xs)

    lambda_init = 0.8 - 0.6 * math.exp(-0.3 * n_a)
    w_q = b_w_in[0]
    kv_g = kv_norm.reshape(1, d)
    b_g = b_norm[0].reshape(1, d)
    rope = _rope_tables(seq, past_len, t_s, ROW_TILE)
    k_p, v_p, kvb_p, k_s, v_s, kvb_s = _kv_proj(x1, seq, kv_g, w_kv, rope)
    qb_p, z_p, qb_s, z_s = _qz_proj(x1, seq, b_g, w_q, rope, tn=WIDE_PROJ_COLS)

    vecs = [a[0].reshape(1, HEAD_DIM) for a in (b_lambda_q1, b_lambda_k1, b_lambda_q2, b_lambda_k2)]
    subln = b_subln[0].reshape(1, 2 * HEAD_DIM)
    hw = 2 * HEAD_DIM
    og_p = _attn_prompt(qb_p, z_p, kvb_p, *vecs, subln, lambda_init,
                        tq=ATTN_Q_TILE, ts=ATTN_SUB_TILE, tk=ATTN_KEY_TILE)
    og_s = _attn_sample(qb_s, z_s, kvb_s, cache_k.reshape(bs, past_len * N_HEADS, hw),
                        cache_v.reshape(bs, past_len * N_HEADS, hw), *vecs, subln, lambda_init,
                        t_s, tkv=CACHE_KEY_TILE)
    w_o = b_w_out[0]
    fg = final_norm.reshape(1, d)
    y_p, y_s = _b_out_proj(og_p, og_s, w_o, x1, fg)

    return (y_p.reshape(bp, seq, d), y_s.reshape(bs, t_s, d),
            pre.reshape(bp, n_a, g_cnt, p), pim.reshape(bp, n_a, g_cnt, p),
            k_p.reshape(bp, seq, N_HEADS, hw), v_p.reshape(bp, seq, N_HEADS, hw),
            sre.reshape(bs, n_a, g_cnt, p), sim.reshape(bs, n_a, g_cnt, p),
            k_s.reshape(bs, t_s, N_HEADS, hw), v_s.reshape(bs, t_s, N_HEADS, hw))
```

```python
import functools
import math

import jax
import jax.numpy as jnp
from jax import lax
from jax.experimental import pallas as pl
from jax.experimental.pallas import tpu as pltpu

F32 = jnp.float32
BF16 = jnp.bfloat16

CHUNK = 64
HEAD_DIM = 128
N_HEADS = 8
SSM_GROUP = 16
SSM_STATE = 64
SSM_T = 16
ROPE_THETA = 10000.0
Q_SCALE = HEAD_DIM ** -0.5 * math.log2(math.e)
EPS = 1e-6
NEG = -1e30

LANES = 128
SUBLANES = 8
GROUPS_PER_TILE = LANES // SSM_GROUP
CHUNK_TILE = 128
ROW_TILE = 512
WIDE_PROJ_COLS = 1024
ATTN_Q_TILE = 1024
ATTN_SUB_TILE = 512
ATTN_KEY_TILE = 1024
CACHE_KEY_TILE = 1024
STRIP = 32
VMEM_LIMIT = 56 * 1024 * 1024
KV_PROJ_VMEM_LIMIT = 58 * 1024 * 1024

_NT = (((1,), (1,)), ((), ()))


def _params(*sem, vmem_limit=VMEM_LIMIT):
    return pltpu.CompilerParams(dimension_semantics=sem, vmem_limit_bytes=vmem_limit)


def _resident(shape, index_map):
    return pl.BlockSpec(shape, index_map, pipeline_mode=pl.Buffered(1))


def _rms_scale(x, g):
    ms = jnp.mean(x * x, axis=-1, keepdims=True)
    return x * lax.rsqrt(ms + EPS) * g


def _staged_weight_spec(d, tn, nb):
    return pl.BlockSpec((d, tn), lambda r, n: (0, jnp.where(r == 0, n, nb - 1)),
                        pipeline_mode=pl.Buffered(1))


def _row_block(r):
    return jnp.maximum(r - 1, 0)


def _col_block(r, n):
    return jnp.where(r == 0, 0, n)


def _a_in_kernel(xp_ref, xs_ref, g_ref, w_ref, o_ref, xn_ref, wb_ref, *, mp, nb):
    m, n = pl.program_id(0) - 1, pl.program_id(1)

    @pl.when(m < 0)
    def _():
        wb_ref[n] = w_ref[...].astype(BF16)

    def column(c, x_ref):
        if c == 0:
            xn_ref[...] = _rms_scale(x_ref[...], g_ref[...]).astype(BF16)
        o_ref[...] = jnp.dot(xn_ref[...], wb_ref[c], preferred_element_type=F32)

    for c in range(nb):
        @pl.when((m >= 0) & (m < mp) & (n == c))
        def _():
            column(c, xp_ref)

        @pl.when((m == mp) & (n == c))
        def _():
            column(c, xs_ref)

    @pl.when(m > mp)
    def _():
        o_ref[...] = jnp.zeros(o_ref.shape, F32)


def _a_in_proj(xp, xs, g, w, rows_pad, tn):
    tm = ROW_TILE
    d = g.shape[-1]
    n_out = w.shape[1]
    mp = xp.shape[0] // tm
    nb = n_out // tn
    assert xs.shape[0] == tm and xp.shape[0] % tm == 0 and rows_pad % tm == 0
    return pl.pallas_call(
        functools.partial(_a_in_kernel, mp=mp, nb=nb),
        out_shape=jax.ShapeDtypeStruct((rows_pad, n_out), F32),
        grid=(1 + rows_pad // tm, nb),
        in_specs=[
            pl.BlockSpec((tm, d), lambda r, n: (jnp.minimum(_row_block(r), mp - 1), 0)),
            _resident((tm, d), lambda r, n: (0, 0)),
            _resident((1, d), lambda r, n: (0, 0)),
            _staged_weight_spec(d, tn, nb),
        ],
        out_specs=pl.BlockSpec((tm, tn), lambda r, n: (_row_block(r), _col_block(r, n))),
        scratch_shapes=[pltpu.VMEM((tm, d), BF16), pltpu.VMEM((nb, d, tn), BF16)],
        compiler_params=_params("arbitrary", "arbitrary"),
        name="a_in_proj",
    )(xp, xs, g, w)


PREP_PAIRS = LANES // SSM_T


def _ssm_prep_kernel(lrx_ref, lix_ref, dtx_ref, bre_ref, bim_ref, lrr_ref, lir_ref, dtr_ref,
                     cre_ref, cim_ref, mt_ref, bend_ref, cin_ref, a16r_ref, a16i_ref):
    p = SSM_STATE
    p2 = 2 * p
    w = SSM_T * SSM_GROUP
    hi = lax.Precision.HIGHEST

    lr, li = lrx_ref[...], lix_ref[...]
    dt = jnp.exp(dtx_ref[...])
    zr, zi = lr * dt, li * dt
    mag = jnp.exp(zr)
    n_re, n_im = mag * jnp.cos(zi) - 1.0, mag * jnp.sin(zi)
    den = lr * lr + li * li
    cf_re = (n_re * lr + n_im * li) / den
    cf_im = (n_im * lr - n_re * li) / den
    lane = lax.broadcasted_iota(jnp.int32, (p2, LANES), 1)
    e_end = (SSM_T - 1 - lane % SSM_T).astype(F32)
    pm = jnp.exp(zr * e_end)
    pw_re, pw_im = pm * jnp.cos(zi * e_end), pm * jnp.sin(zi * e_end)
    cp_re = cf_re * pw_re - cf_im * pw_im
    cp_im = cf_re * pw_im + cf_im * pw_re
    coef = jnp.concatenate([cp_re, cp_im, cf_re, cf_im], axis=0)
    sel_row = lax.broadcasted_iota(jnp.int32, (LANES, w), 0)
    sel_col = lax.broadcasted_iota(jnp.int32, (LANES, w), 1)

    tau = lax.broadcasted_iota(jnp.int32, (SSM_T + SUBLANES, p2), 0).astype(F32)
    lane2 = lax.broadcasted_iota(jnp.int32, (w, p2), 1)
    sblk = lax.broadcasted_iota(jnp.int32, (w, w), 1) // SSM_GROUP

    for j in range(PREP_PAIRS):
        own_pair = (sel_row // SSM_T) == j
        spread = jnp.where(own_pair & (sel_row % SSM_T == sel_col // SSM_GROUP), 1.0, 0.0)
        cpx = jnp.dot(coef[:2 * p2], spread, precision=hi, preferred_element_type=F32)
        cfx = jnp.broadcast_to(coef[2 * p2:, j * SSM_T:j * SSM_T + 1], (2 * p2, w))
        b_re, b_im = jnp.tile(bre_ref[j], (1, SSM_T)), jnp.tile(bim_ref[j], (1, SSM_T))
        end_re = cpx[:p2] * b_re - cpx[p2:] * b_im
        end_im = cpx[:p2] * b_im + cpx[p2:] * b_re
        bb_re = cfx[:p2] * b_re - cfx[p2:] * b_im
        bb_im = cfx[:p2] * b_im + cfx[p2:] * b_re
        for r in range(2):
            sl = slice(r * p, (r + 1) * p)
            bend_ref[2 * j + r] = jnp.concatenate([end_re[sl], end_im[sl]], axis=0).astype(BF16)
        bst = jnp.concatenate([bb_re, bb_im], axis=0)

        lr2, li2 = lrr_ref[j], lir_ref[j]
        dt2 = jnp.exp(dtr_ref[j])
        zr2, zi2 = lr2 * dt2, li2 * dt2
        qm = jnp.exp(zr2 * tau)
        q_re, q_im = qm * jnp.cos(zi2 * tau), qm * jnp.sin(zi2 * tau)
        c_re, c_im = cre_ref[j], cim_ref[j]

        def c_times_powers(first):
            re = [c_re * q_re[t:t + 1] - c_im * q_im[t:t + 1] for t in range(first, first + SSM_T)]
            im = [c_re * q_im[t:t + 1] + c_im * q_re[t:t + 1] for t in range(first, first + SSM_T)]
            return jnp.concatenate(re, axis=0), jnp.concatenate(im, axis=0)

        g_re, g_im = c_times_powers(0)
        ci_re, ci_im = c_times_powers(1)
        for r in range(2):
            own = (lane2 // p) == r
            lhs = jnp.concatenate([jnp.where(own, g_re, 0.0), jnp.where(own, -g_im, 0.0)], axis=1)
            gen = jnp.dot(lhs, bst, preferred_element_type=F32)
            k = 0
            while (SSM_GROUP << k) < w:
                sh = SSM_GROUP << k
                shifted = jnp.concatenate([jnp.zeros((sh, w), F32), gen[:w - sh]], axis=0)
                gen = jnp.where(((sblk >> k) & 1) == 1, shifted, gen)
                k += 1
            mt_ref[2 * j + r] = gen.astype(BF16)
            cin_ref[2 * j + r] = jnp.concatenate(
                [jnp.where(own, ci_re, 0.0), jnp.where(own, -ci_im, 0.0)], axis=1).astype(BF16)
        a16r_ref[j] = q_re[SSM_T:SSM_T + 1]
        a16i_ref[j] = q_im[SSM_T:SSM_T + 1]


def _ssm_prep(lam_re, lam_im, log_dt, b_re, b_im, c_re, c_im):
    g, p = lam_re.shape
    gp = g // 2
    p2 = 2 * p
    w = SSM_T * SSM_GROUP
    pp = PREP_PAIRS
    assert gp % pp == 0
    colx = lambda a: jnp.repeat(a.reshape(gp, p2).T, SSM_T, axis=1)
    row = lambda a: a.reshape(gp, 1, p2)
    dt_full = jnp.broadcast_to(log_dt[:, None], (g, p))
    b_t = lambda a: a.reshape(gp, p2, SSM_GROUP)
    c_pair = lambda a: a.reshape(gp, 2, SSM_GROUP, p).transpose(0, 2, 1, 3).reshape(gp, SSM_GROUP, p2)
    xspec = pl.BlockSpec((p2, LANES), lambda i: (0, i))
    rowspec = pl.BlockSpec((pp, 1, p2), lambda i: (i, 0, 0))
    bspec = pl.BlockSpec((pp, p2, SSM_GROUP), lambda i: (i, 0, 0))
    cspec = pl.BlockSpec((pp, SSM_GROUP, p2), lambda i: (i, 0, 0))
    mt, bend, cin, a16r, a16i = pl.pallas_call(
        _ssm_prep_kernel,
        out_shape=[
            jax.ShapeDtypeStruct((g, w, w), BF16),
            jax.ShapeDtypeStruct((g, p2, w), BF16),
            jax.ShapeDtypeStruct((g, w, 2 * p2), BF16),
            jax.ShapeDtypeStruct((gp, 1, p2), F32),
            jax.ShapeDtypeStruct((gp, 1, p2), F32),
        ],
        grid=(gp // pp,),
        in_specs=[xspec, xspec, xspec, bspec, bspec, rowspec, rowspec, rowspec, cspec, cspec],
        out_specs=[
            pl.BlockSpec((2 * pp, w, w), lambda i: (i, 0, 0)),
            pl.BlockSpec((2 * pp, p2, w), lambda i: (i, 0, 0)),
            pl.BlockSpec((2 * pp, w, 2 * p2), lambda i: (i, 0, 0)),
            rowspec, rowspec,
        ],
        compiler_params=_params("arbitrary"),
        name="ssm_prep",
    )(colx(lam_re), colx(lam_im), colx(dt_full), b_t(b_re), b_t(b_im),
      row(lam_re), row(lam_im), row(dt_full), c_pair(c_re), c_pair(c_im))
    return mt, bend, cin, a16r.reshape(1, g * p), a16i.reshape(1, g * p)


def _step_rows(s):
    return pl.ds(s, CHUNK_TILE, stride=SSM_T)


def _build_ut(u_ref, ut_ref):
    for s in range(SSM_T):
        xt = u_ref[_step_rows(s), :].T.astype(BF16)
        for gl in range(GROUPS_PER_TILE):
            ut_ref[gl, s * SSM_GROUP:(s + 1) * SSM_GROUP, :] = xt[gl * SSM_GROUP:(gl + 1) * SSM_GROUP, :]


STATE_TILES = 2


def _ssm_state_kernel(*refs):
    u_refs = refs[:STATE_TILES]
    bend_ref, sre_ref, sim_ref = refs[STATE_TILES:STATE_TILES + 3]
    ut_refs = refs[STATE_TILES + 3:]
    p = SSM_STATE
    for tile in range(STATE_TILES):
        _build_ut(u_refs[tile], ut_refs[tile])
        for pr in range(GROUPS_PER_TILE // 2):
            st = [jnp.dot(bend_ref[tile * GROUPS_PER_TILE + 2 * pr + r], ut_refs[tile][2 * pr + r],
                          preferred_element_type=F32)
                  for r in range(2)]
            cols = slice((tile * GROUPS_PER_TILE // 2 + pr) * 2 * p,
                         (tile * GROUPS_PER_TILE // 2 + pr + 1) * 2 * p)
            sre_ref[:, cols] = jnp.concatenate([st[0][:p], st[1][:p]], axis=0).T
            sim_ref[:, cols] = jnp.concatenate([st[0][p:], st[1][p:]], axis=0).T


def _ssm_state_contrib(uz, bend, jpad):
    g, p2, w = bend.shape
    gstep = STATE_TILES * GROUPS_PER_TILE
    assert g % gstep == 0
    sw = gstep * SSM_STATE
    out = jax.ShapeDtypeStruct((jpad, g * SSM_STATE), F32)
    ospec = pl.BlockSpec((CHUNK_TILE, sw), lambda q, j: (j, q))
    utile = lambda t: pl.BlockSpec((CHUNK_TILE * SSM_T, LANES), lambda q, j: (j, STATE_TILES * q + t))
    return pl.pallas_call(
        _ssm_state_kernel,
        out_shape=[out, out],
        grid=(g // gstep, jpad // CHUNK_TILE),
        in_specs=[*[utile(t) for t in range(STATE_TILES)],
                  pl.BlockSpec((gstep, p2, w), lambda q, j: (q, 0, 0))],
        out_specs=[ospec, ospec],
        scratch_shapes=[pltpu.VMEM((GROUPS_PER_TILE, w, CHUNK_TILE), BF16) for _ in range(STATE_TILES)],
        compiler_params=_params("arbitrary", "arbitrary"),
        name="ssm_state_contrib",
    )(*[uz] * STATE_TILES, bend)


def _ssm_scan_kernel(sre_ref, sim_ref, ar_ref, ai_ref, h0r_ref, h0i_ref,
                     hpr_ref, hpi_ref, pr_ref, pi_ref, sr_ref, si_ref, hr_sc, hi_sc, *, n_prompt):
    i = pl.program_id(0)
    rb = sre_ref.shape[0]
    ar, ai = ar_ref[...], ai_ref[...]

    @pl.when(i == 0)
    def _():
        hr_sc[...] = jnp.zeros_like(hr_sc)
        hi_sc[...] = jnp.zeros_like(hi_sc)

    @pl.when(i < n_prompt)
    def _():
        def body(j, c):
            row = pl.ds(j, 1)
            hr, hi = hr_sc[...], hi_sc[...]
            hpr_ref[row, :] = hr
            hpi_ref[row, :] = hi
            hr_sc[...] = ar * hr - ai * hi + sre_ref[row, :]
            hi_sc[...] = ar * hi + ai * hr + sim_ref[row, :]
            return c

        lax.fori_loop(0, rb, body, 0)

    @pl.when(i == n_prompt - 1)
    def _():
        pr_ref[...] = hr_sc[...]
        pi_ref[...] = hi_sc[...]

    @pl.when(i == n_prompt)
    def _():
        h0r, h0i = h0r_ref[...], h0i_ref[...]
        hpr_ref[...] = h0r
        hpi_ref[...] = h0i
        sr_ref[...] = ar * h0r - ai * h0i + sre_ref[...]
        si_ref[...] = ar * h0i + ai * h0r + sim_ref[...]

    @pl.when(i > n_prompt)
    def _():
        hpr_ref[...] = jnp.zeros_like(hpr_ref)
        hpi_ref[...] = jnp.zeros_like(hpi_ref)


def _ssm_scan(s_re, s_im, a16r, a16i, h0r, h0i, jp):
    jpad, n = s_re.shape
    rb = h0r.shape[0]
    assert jp % rb == 0 and jpad % rb == 0
    n_prompt = jp // rb
    rows = pl.BlockSpec((rb, n), lambda i: (i, 0))
    const1 = pl.BlockSpec((1, n), lambda i: (0, 0))
    constb = pl.BlockSpec((rb, n), lambda i: (0, 0))
    big = jax.ShapeDtypeStruct((jpad, n), F32)
    one = jax.ShapeDtypeStruct((1, n), F32)
    bat = jax.ShapeDtypeStruct((rb, n), F32)
    return pl.pallas_call(
        functools.partial(_ssm_scan_kernel, n_prompt=n_prompt),
        out_shape=[big, big, one, one, bat, bat],
        grid=(jpad // rb,),
        in_specs=[rows, rows, const1, const1, constb, constb],
        out_specs=[rows, rows, const1, const1, constb, constb],
        scratch_shapes=[pltpu.VMEM((1, n), F32), pltpu.VMEM((1, n), F32)],
        compiler_params=_params("arbitrary"),
        name="ssm_scan",
    )(s_re, s_im, a16r, a16i, h0r, h0i)


def _ssm_out_kernel(u_ref, mt_ref, cin_ref, hpr_ref, hpi_ref, d_ref, o_ref, ut_ref, uf_ref, yt_ref):
    for s in range(SSM_T):
        xt = u_ref[_step_rows(s), :].T
        xb = xt.astype(BF16)
        for gl in range(GROUPS_PER_TILE):
            rows = slice(gl * SSM_GROUP, (gl + 1) * SSM_GROUP)
            ut_ref[gl, s * SSM_GROUP:(s + 1) * SSM_GROUP, :] = xb[rows, :]
            uf_ref[gl, s * SSM_GROUP:(s + 1) * SSM_GROUP, :] = xt[rows, :]
    p2 = 2 * SSM_STATE
    for gl in range(GROUPS_PER_TILE):
        pr = gl // 2
        hp = jnp.concatenate([hpr_ref[:, pr * p2:(pr + 1) * p2], hpi_ref[:, pr * p2:(pr + 1) * p2]],
                             axis=1).astype(BF16)
        yt = jnp.dot(mt_ref[gl], ut_ref[gl], preferred_element_type=F32)
        yt = yt + lax.dot_general(cin_ref[gl], hp, _NT, preferred_element_type=F32)
        yt = jax.nn.gelu(yt + jnp.tile(d_ref[gl], (SSM_T, 1)) * uf_ref[gl])
        for t in range(SSM_T):
            yt_ref[t, gl * SSM_GROUP:(gl + 1) * SSM_GROUP, :] = yt[t * SSM_GROUP:(t + 1) * SSM_GROUP, :]
    for t in range(SSM_T):
        o_ref[_step_rows(t), :] = yt_ref[t].T


def _ssm_output(uz, mt, cin, hp_re, hp_im, d_skip, jpad):
    g, w, _ = mt.shape
    w2 = cin.shape[-1]
    ntile = g // GROUPS_PER_TILE
    sw = GROUPS_PER_TILE * SSM_STATE
    tok = pl.BlockSpec((CHUNK_TILE * SSM_T, LANES), lambda q, j: (j, q))
    hspec = pl.BlockSpec((CHUNK_TILE, sw), lambda q, j: (j, q))
    return pl.pallas_call(
        _ssm_out_kernel,
        out_shape=jax.ShapeDtypeStruct((jpad * SSM_T, g * SSM_GROUP), F32),
        grid=(ntile, jpad // CHUNK_TILE),
        in_specs=[
            tok,
            pl.BlockSpec((GROUPS_PER_TILE, w, w), lambda q, j: (q, 0, 0)),
            pl.BlockSpec((GROUPS_PER_TILE, w, w2), lambda q, j: (q, 0, 0)),
            hspec, hspec,
            pl.BlockSpec((GROUPS_PER_TILE, SSM_GROUP, CHUNK_TILE), lambda q, j: (q, 0, 0)),
        ],
        out_specs=tok,
        scratch_shapes=[
            pltpu.VMEM((GROUPS_PER_TILE, w, CHUNK_TILE), BF16),
            pltpu.VMEM((GROUPS_PER_TILE, w, CHUNK_TILE), F32),
            pltpu.VMEM((SSM_T, LANES, CHUNK_TILE), F32),
        ],
        compiler_params=_params("arbitrary", "arbitrary"),
        name="ssm_output",
    )(uz, mt, cin, hp_re, hp_im, d_skip)


def _a_glu_kernel(y_ref, z_ref, w_ref, o_ref, wb_ref):
    r = pl.program_id(0)

    @pl.when(r == 0)
    def _():
        wb_ref[...] = w_ref[...].astype(BF16)

    @pl.when(r > 0)
    def _():
        y = y_ref[...]
        gate = jnp.dot(y.astype(BF16), wb_ref[...], preferred_element_type=F32)
        y2 = y * jax.nn.sigmoid(gate)
        o_ref[...] = (y2 * jax.nn.silu(z_ref[...])).astype(BF16)


def _a_glu(yg, uz, w_glu, rows):
    tm = ROW_TILE
    e = yg.shape[1]
    return pl.pallas_call(
        _a_glu_kernel,
        out_shape=jax.ShapeDtypeStruct((rows, e), BF16),
        grid=(1 + rows // tm,),
        in_specs=[
            pl.BlockSpec((tm, e), lambda r: (_row_block(r), 0)),
            pl.BlockSpec((tm, e), lambda r: (_row_block(r), 1)),
            _resident((e, e), lambda r: (0, 0)),
        ],
        out_specs=pl.BlockSpec((tm, e), lambda r: (_row_block(r), 0)),
        scratch_shapes=[pltpu.VMEM((e, e), BF16)],
        compiler_params=_params("arbitrary"),
        name="a_glu",
    )(yg, uz, w_glu)


def _a_out_kernel(o_ref, w_ref, xp_ref, xs_ref, x1_ref, wb_ref, *, mp):
    m = pl.program_id(0) - 1

    @pl.when(m < 0)
    def _():
        wb_ref[...] = w_ref[...].astype(BF16)

    @pl.when((m >= 0) & (m < mp))
    def _():
        x1_ref[...] = xp_ref[...] + jnp.dot(o_ref[...], wb_ref[...], preferred_element_type=F32)

    @pl.when(m == mp)
    def _():
        x1_ref[...] = xs_ref[...] + jnp.dot(o_ref[...], wb_ref[...], preferred_element_type=F32)


def _a_out_proj(o, w_out, xp, xs):
    tm = ROW_TILE
    rows, e = o.shape
    d = w_out.shape[1]
    mp = xp.shape[0] // tm
    return pl.pallas_call(
        functools.partial(_a_out_kernel, mp=mp),
        out_shape=jax.ShapeDtypeStruct((rows, d), F32),
        grid=(1 + rows // tm,),
        in_specs=[
            pl.BlockSpec((tm, e), lambda r: (_row_block(r), 0)),
            _resident((e, d), lambda r: (0, 0)),
            pl.BlockSpec((tm, d), lambda r: (jnp.minimum(_row_block(r), mp - 1), 0)),
            _resident((tm, d), lambda r: (0, 0)),
        ],
        out_specs=pl.BlockSpec((tm, d), lambda r: (_row_block(r), 0)),
        scratch_shapes=[pltpu.VMEM((e, d), BF16)],
        compiler_params=_params("arbitrary"),
        name="a_out_proj",
    )(o, w_out, xp, xs)


def _rope(x, cos, sin):
    outs = []
    for c in range(x.shape[1] // HEAD_DIM):
        xc = x[:, c * HEAD_DIM:(c + 1) * HEAD_DIM]
        outs.append(xc * cos + pltpu.roll(xc, HEAD_DIM // 2, axis=1) * sin)
    return jnp.concatenate(outs, axis=1) if len(outs) > 1 else outs[0]


def _rope_block(co_ref, so_ref, cb_ref, sb_ref):
    co, so = co_ref[...], so_ref[...]
    cb, sb = cb_ref[...], sb_ref[...]
    lane = lax.broadcasted_iota(jnp.int32, (1, HEAD_DIM), 1)
    sign = jnp.where(lane < HEAD_DIM // 2, -1.0, 1.0)
    return cb * co - sb * so, (sb * co + cb * so) * sign


def _store_heads(o_ref, val, tm):
    for h in range(N_HEADS):
        o_ref[pl.ds(h, tm, stride=N_HEADS), :] = val[:, h * HEAD_DIM:(h + 1) * HEAD_DIM]


def _prompt_col(r, mp, col, last):
    return jnp.where(_row_block(r) >= mp, last, col)


def _sample_col(r, mp, col):
    return jnp.where(_row_block(r) == mp, col, 0)


def _kv_proj_kernel(x_ref, g_ref, w_ref, co_ref, so_ref, cb_ref, sb_ref, kp_ref, vp_ref, cbp_ref,
                    ks_ref, vs_ref, cbs_ref, xn_ref, wb_ref, *, tm, mp):
    m, n = pl.program_id(0) - 1, pl.program_id(1)
    hpb = w_ref.shape[1] // (2 * HEAD_DIM)

    for nn in range(4):
        @pl.when((m < 0) & (n == nn))
        def _():
            for hl in range(hpb):
                h = hpb * (nn % 2) + hl
                for half in range(2):
                    src = (2 * hl + half) * HEAD_DIM
                    wb_ref[2 * (nn // 2) + half, :, h * HEAD_DIM:(h + 1) * HEAD_DIM] = (
                        w_ref[:, src:src + HEAD_DIM].astype(BF16))

    def column(c, k_ref, v_ref, copy_ref):
        if c == 0:
            xn_ref[...] = _rms_scale(x_ref[...], g_ref[...]).astype(BF16)
        acc = jnp.dot(xn_ref[...], wb_ref[c], preferred_element_type=F32)
        if c < 2:
            acc = _rope(acc, *_rope_block(co_ref, so_ref, cb_ref, sb_ref))
        _store_heads(k_ref if c < 2 else v_ref, acc, tm)
        copy_ref[...] = acc.astype(BF16)

    for c in range(4):
        @pl.when((m >= 0) & (m < mp) & (n == c))
        def _():
            column(c, kp_ref, vp_ref, cbp_ref)

        @pl.when((m == mp) & (n == c))
        def _():
            column(c, ks_ref, vs_ref, cbs_ref)


def _kv_proj(x1, rows_p, g, w_kv, rope):
    tm = ROW_TILE
    rows, d = x1.shape
    tn = N_HEADS * HEAD_DIM
    mp = rows_p // tm
    assert w_kv.shape[1] == 4 * tn and rows == rows_p + tm and rows_p % tm == 0
    out4 = lambda nrows: jax.ShapeDtypeStruct((nrows * N_HEADS, 2 * HEAD_DIM), F32)
    outb = lambda nrows: jax.ShapeDtypeStruct((nrows, 4 * tn), BF16)
    prow = lambda r: jnp.minimum(_row_block(r), mp - 1)
    col = _col_block
    return pl.pallas_call(
        functools.partial(_kv_proj_kernel, tm=tm, mp=mp),
        out_shape=[out4(rows_p), out4(rows_p), outb(rows_p), out4(tm), out4(tm), outb(tm)],
        grid=(1 + rows // tm, 4),
        in_specs=[
            pl.BlockSpec((tm, d), lambda r, n: (_row_block(r), 0)),
            _resident((1, d), lambda r, n: (0, 0)),
            _staged_weight_spec(d, tn, 4),
            *_rope_specs(tm, mp),
        ],
        out_specs=[
            pl.BlockSpec((tm * N_HEADS, HEAD_DIM),
                         lambda r, n: (prow(r), _prompt_col(r, mp, jnp.minimum(col(r, n), 1), 1))),
            pl.BlockSpec((tm * N_HEADS, HEAD_DIM),
                         lambda r, n: (prow(r), _prompt_col(r, mp, jnp.maximum(col(r, n) - 2, 0), 1))),
            pl.BlockSpec((tm, tn), lambda r, n: (prow(r), _prompt_col(r, mp, col(r, n), 3))),
            pl.BlockSpec((tm * N_HEADS, HEAD_DIM),
                         lambda r, n: (0, _sample_col(r, mp, jnp.minimum(col(r, n), 1)))),
            pl.BlockSpec((tm * N_HEADS, HEAD_DIM),
                         lambda r, n: (0, _sample_col(r, mp, jnp.maximum(col(r, n) - 2, 0)))),
            pl.BlockSpec((tm, tn), lambda r, n: (0, _sample_col(r, mp, col(r, n)))),
        ],
        scratch_shapes=[pltpu.VMEM((tm, d), BF16), pltpu.VMEM((4, d, tn), BF16)],
        compiler_params=_params("arbitrary", "arbitrary", vmem_limit=KV_PROJ_VMEM_LIMIT),
        name="kv_proj",
    )(x1, g, w_kv, *rope)


def _qz_proj_kernel(x_ref, g_ref, w_ref, co_ref, so_ref, cb_ref, sb_ref, qp_ref, zp_ref, qs_ref, zs_ref,
                    xn_ref, wb_ref, *, nq, mp):
    m, n = pl.program_id(0) - 1, pl.program_id(1)

    @pl.when(m < 0)
    def _():
        wb_ref[n] = w_ref[...].astype(BF16)

    def column(c, q_ref, z_ref):
        if c == 0:
            xn_ref[...] = _rms_scale(x_ref[...], g_ref[...]).astype(BF16)
        acc = jnp.dot(xn_ref[...], wb_ref[c], preferred_element_type=F32)
        if c < nq:
            rot = _rope(acc, *_rope_block(co_ref, so_ref, cb_ref, sb_ref))
            q_ref[...] = (rot * Q_SCALE).astype(BF16)
        else:
            z_ref[...] = acc

    for c in range(2 * nq):
        @pl.when((m >= 0) & (m < mp) & (n == c))
        def _():
            column(c, qp_ref, zp_ref)

        @pl.when((m == mp) & (n == c))
        def _():
            column(c, qs_ref, zs_ref)


def _qz_proj(x1, rows_p, g, w, rope, tn):
    tm = ROW_TILE
    rows, d = x1.shape
    half = w.shape[1] // 2
    nq = half // tn
    mp = rows_p // tm
    assert rows == rows_p + tm and rows_p % tm == 0
    prow = lambda r: jnp.minimum(_row_block(r), mp - 1)
    qcol = lambda r, n: jnp.minimum(_col_block(r, n), nq - 1)
    zcol = lambda r, n: jnp.maximum(_col_block(r, n) - nq, 0)
    return pl.pallas_call(
        functools.partial(_qz_proj_kernel, nq=nq, mp=mp),
        out_shape=[jax.ShapeDtypeStruct((rows_p, half), BF16), jax.ShapeDtypeStruct((rows_p, half), F32),
                   jax.ShapeDtypeStruct((tm, half), BF16), jax.ShapeDtypeStruct((tm, half), F32)],
        grid=(1 + rows // tm, 2 * nq),
        in_specs=[
            pl.BlockSpec((tm, d), lambda r, n: (_row_block(r), 0)),
            _resident((1, d), lambda r, n: (0, 0)),
            _staged_weight_spec(d, tn, 2 * nq),
            *_rope_specs(tm, mp),
        ],
        out_specs=[
            pl.BlockSpec((tm, tn), lambda r, n: (prow(r), _prompt_col(r, mp, qcol(r, n), nq - 1))),
            pl.BlockSpec((tm, tn), lambda r, n: (prow(r), _prompt_col(r, mp, zcol(r, n), nq - 1))),
            pl.BlockSpec((tm, tn), lambda r, n: (0, _sample_col(r, mp, qcol(r, n)))),
            pl.BlockSpec((tm, tn), lambda r, n: (0, _sample_col(r, mp, zcol(r, n)))),
        ],
        scratch_shapes=[pltpu.VMEM((tm, d), BF16), pltpu.VMEM((2 * nq, d, tn), BF16)],
        compiler_params=_params("arbitrary", "arbitrary"),
        name="qz_proj",
    )(x1, g, w, *rope)


def _rope_specs(tm, mp):
    off = pl.BlockSpec((None, tm, HEAD_DIM), lambda r, n: (jnp.where(_row_block(r) == mp, 1, 0), 0, 0))
    base = pl.BlockSpec((None, 1, HEAD_DIM), lambda r, n: (_row_block(r), 0, 0))
    return [off, off, base, base]


def _rope_tables(rows_p, past_len, t_s, tm):
    inv_freq = ROPE_THETA ** (-jnp.arange(0, HEAD_DIM, 2, dtype=F32) / HEAD_DIM)
    full = lambda a: jnp.concatenate([a, a], axis=-1)
    rows = jnp.arange(tm, dtype=jnp.int32)
    off = jnp.stack([rows, rows % t_s]).astype(F32)[..., None] * inv_freq
    base = jnp.concatenate([jnp.arange(0, rows_p, tm, dtype=jnp.int32),
                            jnp.full((1,), past_len, jnp.int32)]).astype(F32)[:, None, None] * inv_freq
    return full(jnp.cos(off)), full(jnp.sin(off)), full(jnp.cos(base)), full(jnp.sin(base))


def _diff_lambda(lq1_ref, lk1_ref, lq2_ref, lk2_ref, lambda_init):
    s1 = jnp.sum(lq1_ref[...] * lk1_ref[...], axis=-1, keepdims=True)
    s2 = jnp.sum(lq2_ref[...] * lk2_ref[...], axis=-1, keepdims=True)
    return jnp.exp(s1) - jnp.exp(s2) + lambda_init


def _attn_finish(a1, l1, a2, l2, lam, subln, z, lambda_init):
    o = a1 / l1 - lam * (a2 / l2)
    o = _rms_scale(o, subln) * (1.0 - lambda_init)
    return (o * jax.nn.silu(z)).astype(BF16)


def _attn_prompt_kernel(q_ref, k1_ref, k2_ref, v1_ref, v2_ref, z_ref,
                        lq1_ref, lk1_ref, lq2_ref, lk2_ref, sub_ref,
                        o_ref, m_sc, l_sc, al_sc, a_sc, *sp_scs, tq, ts, tk, lambda_init):
    qi = pl.program_id(1)
    k_refs = (k1_ref, k2_ref)
    nsub = tq // ts
    s_scs, p_scs = sp_scs[:2 * nsub], sp_scs[2 * nsub:]

    def softmax_strips(sub, n, col0, first):
        for r in range(ts // STRIP):
            row0 = sub * ts + r * STRIP
            lr = slice(r * STRIP, (r + 1) * STRIP)
            gr = slice(row0, row0 + STRIP)
            visible = tk if col0 is None else min(tk, (row0 // CHUNK + 1) * CHUNK - col0)
            assert visible > 0
            ncv = -(-visible // LANES)
            s = s_scs[2 * sub + n][lr, 0:ncv * LANES]
            if visible < ncv * LANES:
                col = lax.broadcasted_iota(jnp.int32, s.shape, 1)
                s = jnp.where(col < visible, s, NEG)
            cols = [s[:, c * LANES:(c + 1) * LANES] for c in range(ncv)]
            m_cur = jnp.max(functools.reduce(jnp.maximum, cols), axis=-1, keepdims=True)
            if first:
                m_new = jnp.broadcast_to(m_cur, (STRIP, LANES))
            else:
                m_old = m_sc[n, gr, :]
                m_new = jnp.maximum(m_old, m_cur)
            ps = [jnp.exp2(c - m_new) for c in cols]
            l_cur = jnp.sum(functools.reduce(jnp.add, ps), axis=-1, keepdims=True)
            if first:
                l_sc[n, gr, :] = jnp.broadcast_to(l_cur, (STRIP, LANES))
            else:
                alpha = jnp.exp2(m_old - m_new)
                l_sc[n, gr, :] = alpha * l_sc[n, gr, :] + l_cur
                al_sc[n, gr, :] = alpha
            m_sc[n, gr, :] = m_new
            for c in range(keys_seen(sub, col0) // LANES):
                pc = ps[c].astype(BF16) if c < ncv else jnp.zeros((STRIP, LANES), BF16)
                p_scs[2 * sub + n][lr, c * LANES:(c + 1) * LANES] = pc

    def keys_seen(sub, col0):
        return tk if col0 is None else min(tk, (sub + 1) * ts - col0)

    def block(kb, col0, subs, first=False):
        start = pl.multiple_of(kb * tk, tk)
        for sub in subs:
            rows = pl.ds(start, keys_seen(sub, col0))
            for n in range(2):
                qn = q_ref[sub * ts:(sub + 1) * ts, n * HEAD_DIM:(n + 1) * HEAD_DIM]
                s_scs[2 * sub + n][:, 0:rows.size] = lax.dot_general(
                    qn, k_refs[n][rows, :], _NT, preferred_element_type=F32)
        for sub in subs:
            gr = slice(sub * ts, (sub + 1) * ts)
            rows = pl.ds(start, keys_seen(sub, col0))
            vblk = jnp.concatenate([v1_ref[rows, :], v2_ref[rows, :]], axis=1)
            for n in range(2):
                softmax_strips(sub, n, col0, first)
                pv = jnp.dot(p_scs[2 * sub + n][:, 0:rows.size], vblk, preferred_element_type=F32)
                if first:
                    a_sc[n, gr, :] = pv
                else:
                    alpha = al_sc[n, gr, :]
                    a_sc[n, gr, :] = a_sc[n, gr, :] * jnp.concatenate([alpha, alpha], axis=1) + pv

    nd = tq // tk
    for d in range(nd):
        block(qi * nd + d, d * tk, [sub for sub in range(nsub) if (sub + 1) * ts > d * tk],
              first=(d == 0))

    def body(kb, c):
        block(kb, None, range(nsub))
        return c

    lax.fori_loop(0, qi * nd, body, 0)

    lam = _diff_lambda(lq1_ref, lk1_ref, lq2_ref, lk2_ref, lambda_init)
    l1 = jnp.concatenate([l_sc[0], l_sc[0]], axis=1)
    l2 = jnp.concatenate([l_sc[1], l_sc[1]], axis=1)
    o_ref[...] = _attn_finish(a_sc[0], l1, a_sc[1], l2, lam, sub_ref[...], z_ref[...], lambda_init)


def _attn_prompt(qb, z32, kvb, lq1, lk1, lq2, lk2, subln, lambda_init, tq, ts, tk):
    rows = qb.shape[0]
    hw = 2 * HEAD_DIM
    nsub = tq // ts
    assert tq % tk == 0 and tk % ts == 0 and ts % STRIP == 0 and CHUNK % STRIP == 0
    vec = pl.BlockSpec((1, HEAD_DIM), lambda h, i: (0, 0))
    kcol = lambda c: pl.BlockSpec((rows, HEAD_DIM), lambda h, i: (0, c * N_HEADS + h))
    stat = pltpu.VMEM((2, tq, LANES), F32)
    return pl.pallas_call(
        functools.partial(_attn_prompt_kernel, tq=tq, ts=ts, tk=tk, lambda_init=lambda_init),
        out_shape=jax.ShapeDtypeStruct((rows, N_HEADS * hw), BF16),
        grid=(N_HEADS, rows // tq),
        in_specs=[
            pl.BlockSpec((tq, hw), lambda h, i: (i, h)),
            kcol(0), kcol(1), kcol(2), kcol(3),
            pl.BlockSpec((tq, hw), lambda h, i: (i, h)),
            vec, vec, vec, vec,
            pl.BlockSpec((1, hw), lambda h, i: (0, 0)),
        ],
        out_specs=pl.BlockSpec((tq, hw), lambda h, i: (i, h)),
        scratch_shapes=[
            stat, stat, stat,
            pltpu.VMEM((2, tq, hw), F32),
            *[pltpu.VMEM((ts, tk), F32) for _ in range(2 * nsub)],
            *[pltpu.VMEM((ts, tk), BF16) for _ in range(2 * nsub)],
        ],
        compiler_params=_params("arbitrary", "arbitrary"),
        name="attn_prompt",
    )(qb, kvb, kvb, kvb, kvb, z32, lq1, lk1, lq2, lk2, subln)


def _attn_sample_kernel(q_ref, ck1_ref, ck2_ref, cv1_ref, cv2_ref, kvn_ref, z_ref,
                        lq1_ref, lk1_ref, lq2_ref, lk2_ref, sub_ref,
                        o_ref, m_sc, l_sc, a_sc, s_sc, p_sc, *, tkv, nkb, past_len, lambda_init):
    kb = pl.program_id(1)
    t = q_ref.shape[0]
    hw = 2 * HEAD_DIM
    rows_all = 2 * N_HEADS * t
    strip = 4 * t
    ck_refs = (ck1_ref, ck2_ref)
    assert (past_len - 1) // CHUNK <= past_len // CHUNK

    @pl.when(kb == 0)
    def _():
        m_sc[...] = jnp.full(m_sc.shape, NEG, F32)
        l_sc[...] = jnp.zeros(l_sc.shape, F32)
        a_sc[...] = jnp.zeros(a_sc.shape, F32)

    col_blk = lambda c, h: slice((c * N_HEADS + h) * HEAD_DIM, (c * N_HEADS + h + 1) * HEAD_DIM)
    head_rows = lambda h: pl.ds(h, tkv, stride=N_HEADS)

    def update(with_new):
        ncols = tkv + (LANES if with_new else 0)
        if with_new:
            row = lax.broadcasted_iota(jnp.int32, (t, LANES), 0) + past_len
            col = lax.broadcasted_iota(jnp.int32, (t, LANES), 1)
            new_mask = (col < t) & (((col + past_len) // CHUNK) <= (row // CHUNK))
            pad_k = jnp.zeros((LANES - t, HEAD_DIM), BF16)
        for h in range(N_HEADS):
            for n in range(2):
                qn = q_ref[:, (2 * h + n) * HEAD_DIM:(2 * h + n + 1) * HEAD_DIM]
                rr = slice((2 * h + n) * t, (2 * h + n + 1) * t)
                k = ck_refs[n][head_rows(h), :].astype(BF16)
                s_sc[rr, 0:tkv] = lax.dot_general(qn, k, _NT, preferred_element_type=F32)
                if with_new:
                    kn = jnp.concatenate([kvn_ref[:, col_blk(n, h)], pad_k], axis=0)
                    sn = lax.dot_general(qn, kn, _NT, preferred_element_type=F32)
                    s_sc[rr, tkv:ncols] = jnp.where(new_mask, sn, NEG)
        for r in range(rows_all // strip):
            rr = slice(r * strip, (r + 1) * strip)
            cols = [s_sc[rr, c * LANES:(c + 1) * LANES] for c in range(ncols // LANES)]
            m_cur = jnp.max(functools.reduce(jnp.maximum, cols), axis=-1, keepdims=True)
            m_old = m_sc[rr, :]
            m_new = jnp.maximum(m_old, m_cur)
            ps = [jnp.exp2(c - m_new) for c in cols]
            l_cur = jnp.sum(functools.reduce(jnp.add, ps), axis=-1, keepdims=True)
            alpha = jnp.exp2(m_old - m_new)
            l_sc[rr, :] = alpha * l_sc[rr, :] + l_cur
            m_sc[rr, :] = m_new
            a_sc[rr, :] = a_sc[rr, :] * jnp.concatenate([alpha, alpha], axis=1)
            for c, pc in enumerate(ps):
                p_sc[rr, c * LANES:(c + 1) * LANES] = pc.astype(BF16)
        for h in range(N_HEADS):
            rr = slice(2 * h * t, (2 * h + 2) * t)
            v = jnp.concatenate([cv1_ref[head_rows(h), :], cv2_ref[head_rows(h), :]],
                                axis=1).astype(BF16)
            pv = jnp.dot(p_sc[rr, 0:tkv], v, preferred_element_type=F32)
            if with_new:
                vn = jnp.concatenate([kvn_ref[:, col_blk(2, h)], kvn_ref[:, col_blk(3, h)]], axis=1)
                vn = jnp.concatenate([vn, jnp.zeros((LANES - t, hw), BF16)], axis=0)
                pv = pv + jnp.dot(p_sc[rr, tkv:ncols], vn, preferred_element_type=F32)
            a_sc[rr, :] = a_sc[rr, :] + pv

    @pl.when(kb < nkb - 1)
    def _():
        update(False)

    @pl.when(kb == nkb - 1)
    def _():
        update(True)
        lam = _diff_lambda(lq1_ref, lk1_ref, lq2_ref, lk2_ref, lambda_init)
        sub = sub_ref[...]
        for h in range(N_HEADS):
            r1 = slice(2 * h * t, (2 * h + 1) * t)
            r2 = slice((2 * h + 1) * t, (2 * h + 2) * t)
            l1 = jnp.concatenate([l_sc[r1, :], l_sc[r1, :]], axis=1)
            l2 = jnp.concatenate([l_sc[r2, :], l_sc[r2, :]], axis=1)
            o_ref[:, h * hw:(h + 1) * hw] = _attn_finish(
                a_sc[r1, :], l1, a_sc[r2, :], l2, lam, sub, z_ref[:, h * hw:(h + 1) * hw], lambda_init)


def _attn_sample(qb, z32, kvb, cache_k2, cache_v2, lq1, lk1, lq2, lk2, subln, lambda_init, t, tkv):
    bsz, rows8, hw = cache_k2.shape
    past_len = rows8 // N_HEADS
    nkb = past_len // tkv
    rows_all = 2 * N_HEADS * t
    assert t <= LANES
    vec = pl.BlockSpec((1, HEAD_DIM), lambda b, k: (0, 0))
    cache = lambda c: pl.BlockSpec((None, tkv * N_HEADS, HEAD_DIM), lambda b, k: (b, k, c))
    full = lambda a: pl.BlockSpec((t, a.shape[1]), lambda b, k: (b, 0))
    return pl.pallas_call(
        functools.partial(_attn_sample_kernel, tkv=tkv, nkb=nkb, past_len=past_len,
                          lambda_init=lambda_init),
        out_shape=jax.ShapeDtypeStruct((bsz * t, N_HEADS * hw), BF16),
        grid=(bsz, nkb),
        in_specs=[
            full(qb), cache(0), cache(1), cache(0), cache(1), full(kvb), full(z32),
            vec, vec, vec, vec,
            pl.BlockSpec((1, hw), lambda b, k: (0, 0)),
        ],
        out_specs=pl.BlockSpec((t, N_HEADS * hw), lambda b, k: (b, 0)),
        scratch_shapes=[
            pltpu.VMEM((rows_all, LANES), F32),
            pltpu.VMEM((rows_all, LANES), F32),
            pltpu.VMEM((rows_all, hw), F32),
            pltpu.VMEM((rows_all, tkv + LANES), F32),
            pltpu.VMEM((rows_all, tkv + LANES), BF16),
        ],
        compiler_params=_params("arbitrary", "arbitrary"),
        name="attn_sample",
    )(qb, cache_k2, cache_k2, cache_v2, cache_v2, kvb, z32, lq1, lk1, lq2, lk2, subln)


def _b_out_kernel(op_ref, os_ref, w_ref, x_ref, g_ref, yp_ref, ys_ref, wb_ref, *, mp):
    m = pl.program_id(0) - 1

    @pl.when(m < 0)
    def _():
        wb_ref[...] = w_ref[...].astype(BF16)

    def finish(o_ref, y_ref):
        acc = jnp.dot(o_ref[...], wb_ref[...], preferred_element_type=F32)
        y_ref[...] = _rms_scale(x_ref[...] + acc, g_ref[...])

    @pl.when((m >= 0) & (m < mp))
    def _():
        finish(op_ref, yp_ref)

    @pl.when(m == mp)
    def _():
        finish(os_ref, ys_ref)


def _b_out_proj(og_p, og_s, w_out, x1, g):
    tm = ROW_TILE
    rows_p, e = og_p.shape
    rows, d = x1.shape
    mp = rows_p // tm
    assert og_s.shape[0] == tm and rows == rows_p + tm and rows_p % tm == 0
    prow = lambda r: jnp.minimum(_row_block(r), mp - 1)
    return pl.pallas_call(
        functools.partial(_b_out_kernel, mp=mp),
        out_shape=[jax.ShapeDtypeStruct((rows_p, d), F32), jax.ShapeDtypeStruct((tm, d), F32)],
        grid=(1 + rows // tm,),
        in_specs=[
            pl.BlockSpec((tm, e), lambda r: (prow(r), 0)),
            _resident((tm, e), lambda r: (0, 0)),
            _resident((e, d), lambda r: (0, 0)),
            pl.BlockSpec((tm, d), lambda r: (_row_block(r), 0)),
            _resident((1, d), lambda r: (0, 0)),
        ],
        out_specs=[pl.BlockSpec((tm, d), lambda r: (prow(r), 0)),
                   pl.BlockSpec((tm, d), lambda r: (0, 0))],
        scratch_shapes=[pltpu.VMEM((e, d), BF16)],
        compiler_params=_params("arbitrary"),
        name="b_out_proj",
    )(og_p, og_s, w_out, x1, g)


def kernel(x_prompt, x_sample, state_ssm_re, state_ssm_im, cache_k, cache_v, a_norm, a_w_in, a_lambda_re, a_lambda_im, a_log_dt, a_b_re, a_b_im, a_c_re, a_c_im, a_d, a_w_glu, a_w_out, kv_norm, w_kv, b_norm, b_w_in, b_lambda_q1, b_lambda_k1, b_lambda_q2, b_lambda_k2, b_subln, b_w_out, final_norm):
    bp, seq, d = x_prompt.shape
    bs, t_s, _ = x_sample.shape
    past_len = cache_k.shape[1]
    n_a, n_b = a_norm.shape[0], b_norm.shape[0]
    g_cnt, p = a_lambda_re.shape[1:]
    assert bp == 1 and n_a == 1 and n_b == 1
    assert t_s == SSM_T and seq % (SSM_T * bs) == 0 and seq % CHUNK == 0
    assert cache_k.shape[2:] == (N_HEADS, 2 * HEAD_DIM)

    jp, js = seq // SSM_T, bs
    jpad = -(-(jp + js) // CHUNK_TILE) * CHUNK_TILE
    rows_s = bs * t_s
    rows_a = seq + rows_s

    xp = x_prompt.reshape(seq, d)
    xs = x_sample.reshape(rows_s, d)
    uz = _a_in_proj(xp, xs, a_norm[0].reshape(1, d), a_w_in[0], jpad * SSM_T, tn=WIDE_PROJ_COLS)
    mt, bend, cin, a16r, a16i = _ssm_prep(a_lambda_re[0], a_lambda_im[0], a_log_dt[0],
                                          a_b_re[0], a_b_im[0], a_c_re[0], a_c_im[0])
    s_re, s_im = _ssm_state_contrib(uz, bend, jpad)
    h0r = state_ssm_re[:, 0].reshape(bs, g_cnt * p)
    h0i = state_ssm_im[:, 0].reshape(bs, g_cnt * p)
    hp_re, hp_im, pre, pim, sre, sim = _ssm_scan(s_re, s_im, a16r, a16i, h0r, h0i, jp)
    d_rep = jnp.broadcast_to(a_d[0].reshape(g_cnt, SSM_GROUP, 1), (g_cnt, SSM_GROUP, CHUNK_TILE))
    yg = _ssm_output(uz, mt, cin, hp_re, hp_im, d_rep, jpad)
    o_a = _a_glu(yg, uz, a_w_glu[0], rows_a)
    x1 = _a_out_proj(o_a, a_w_out[0], xp, xs)

    lambda_init = 0.8 - 0.6 * math.exp(-0.3 * n_a)
    w_q = b_w_in[0]
    kv_g = kv_norm.reshape(1, d)
    b_g = b_norm[0].reshape(1, d)
    rope = _rope_tables(seq, past_len, t_s, ROW_TILE)
    k_p, v_p, kvb_p, k_s, v_s, kvb_s = _kv_proj(x1, seq, kv_g, w_kv, rope)
    qb_p, z_p, qb_s, z_s = _qz_proj(x1, seq, b_g, w_q, rope, tn=WIDE_PROJ_COLS)

    vecs = [a[0].reshape(1, HEAD_DIM) for a in (b_lambda_q1, b_lambda_k1, b_lambda_q2, b_lambda_k2)]
    subln = b_subln[0].reshape(1, 2 * HEAD_DIM)
    hw = 2 * HEAD_DIM
    og_p = _attn_prompt(qb_p, z_p, kvb_p, *vecs, subln, lambda_init,
                        tq=ATTN_Q_TILE, ts=ATTN_SUB_TILE, tk=ATTN_KEY_TILE)
    og_s = _attn_sample(qb_s, z_s, kvb_s, cache_k.reshape(bs, past_len * N_HEADS, hw),
                        cache_v.reshape(bs, past_len * N_HEADS, hw), *vecs, subln, lambda_init,
                        t_s, tkv=CACHE_KEY_TILE)
    w_o = b_w_out[0]
    fg = final_norm.reshape(1, d)
    y_p, y_s = _b_out_proj(og_p, og_s, w_o, x1, fg)

    return (y_p.reshape(bp, seq, d), y_s.reshape(bs, t_s, d),
            pre.reshape(bp, n_a, g_cnt, p), pim.reshape(bp, n_a, g_cnt, p),
            k_p.reshape(bp, seq, N_HEADS, hw), v_p.reshape(bp, seq, N_HEADS, hw),
            sre.reshape(bs, n_a, g_cnt, p), sim.reshape(bs, n_a, g_cnt, p),
            k_s.reshape(bs, t_s, N_HEADS, hw), v_s.reshape(bs, t_s, N_HEADS, hw))
```

```python
import functools
import math

import jax
import jax.numpy as jnp
from jax import lax
from jax.experimental import pallas as pl
from jax.experimental.pallas import tpu as pltpu

F32 = jnp.float32
BF16 = jnp.bfloat16

CHUNK = 64
HEAD_DIM = 128
N_HEADS = 8
SSM_GROUP = 16
SSM_STATE = 64
SSM_T = 16
ROPE_THETA = 10000.0
Q_SCALE = HEAD_DIM ** -0.5 * math.log2(math.e)
EPS = 1e-6
NEG = -1e30

LANES = 128
SUBLANES = 8
GROUPS_PER_TILE = LANES // SSM_GROUP
CHUNK_TILE = 128
ROW_TILE = 512
WIDE_PROJ_COLS = 1024
ATTN_Q_TILE = 1024
ATTN_SUB_TILE = 512
ATTN_KEY_TILE = 1024
CACHE_KEY_TILE = 1024
STRIP = 32
VMEM_LIMIT = 56 * 1024 * 1024
KV_PROJ_VMEM_LIMIT = 58 * 1024 * 1024

_NT = (((1,), (1,)), ((), ()))


def _params(*sem, vmem_limit=VMEM_LIMIT):
    return pltpu.CompilerParams(dimension_semantics=sem, vmem_limit_bytes=vmem_limit)


def _resident(shape, index_map):
    return pl.BlockSpec(shape, index_map, pipeline_mode=pl.Buffered(1))


def _rms_scale(x, g):
    ms = jnp.mean(x * x, axis=-1, keepdims=True)
    return x * lax.rsqrt(ms + EPS) * g


def _staged_weight_spec(d, tn, nb):
    return pl.BlockSpec((d, tn), lambda r, n: (0, jnp.where(r == 0, n, nb - 1)),
                        pipeline_mode=pl.Buffered(1))


def _row_block(r):
    return jnp.maximum(r - 1, 0)


def _col_block(r, n):
    return jnp.where(r == 0, 0, n)


def _a_in_kernel(xp_ref, xs_ref, g_ref, w_ref, o_ref, xn_ref, wb_ref, *, mp, nb):
    m, n = pl.program_id(0) - 1, pl.program_id(1)

    @pl.when(m < 0)
    def _():
        wb_ref[n] = w_ref[...].astype(BF16)

    def column(c, x_ref):
        if c == 0:
            xn_ref[...] = _rms_scale(x_ref[...], g_ref[...]).astype(BF16)
        o_ref[...] = jnp.dot(xn_ref[...], wb_ref[c], preferred_element_type=F32)

    for c in range(nb):
        @pl.when((m >= 0) & (m < mp) & (n == c))
        def _():
            column(c, xp_ref)

        @pl.when((m == mp) & (n == c))
        def _():
            column(c, xs_ref)

    @pl.when(m > mp)
    def _():
        o_ref[...] = jnp.zeros(o_ref.shape, F32)


def _a_in_proj(xp, xs, g, w, rows_pad, tn):
    tm = ROW_TILE
    d = g.shape[-1]
    n_out = w.shape[1]
    mp = xp.shape[0] // tm
    nb = n_out // tn
    assert xs.shape[0] == tm and xp.shape[0] % tm == 0 and rows_pad % tm == 0
    return pl.pallas_call(
        functools.partial(_a_in_kernel, mp=mp, nb=nb),
        out_shape=jax.ShapeDtypeStruct((rows_pad, n_out), F32),
        grid=(1 + rows_pad // tm, nb),
        in_specs=[
            pl.BlockSpec((tm, d), lambda r, n: (jnp.minimum(_row_block(r), mp - 1), 0)),
            _resident((tm, d), lambda r, n: (0, 0)),
            _resident((1, d), lambda r, n: (0, 0)),
            _staged_weight_spec(d, tn, nb),
        ],
        out_specs=pl.BlockSpec((tm, tn), lambda r, n: (_row_block(r), _col_block(r, n))),
        scratch_shapes=[pltpu.VMEM((tm, d), BF16), pltpu.VMEM((nb, d, tn), BF16)],
        compiler_params=_params("arbitrary", "arbitrary"),
        name="a_in_proj",
    )(xp, xs, g, w)


PREP_PAIRS = LANES // SSM_T


def _ssm_prep_kernel(lrx_ref, lix_ref, dtx_ref, bre_ref, bim_ref, lrr_ref, lir_ref, dtr_ref,
                     cre_ref, cim_ref, mt_ref, bend_ref, cin_ref, a16r_ref, a16i_ref):
    p = SSM_STATE
    p2 = 2 * p
    w = SSM_T * SSM_GROUP
    hi = lax.Precision.HIGHEST

    lr, li = lrx_ref[...], lix_ref[...]
    dt = jnp.exp(dtx_ref[...])
    zr, zi = lr * dt, li * dt
    mag = jnp.exp(zr)
    n_re, n_im = mag * jnp.cos(zi) - 1.0, mag * jnp.sin(zi)
    den = lr * lr + li * li
    cf_re = (n_re * lr + n_im * li) / den
    cf_im = (n_im * lr - n_re * li) / den
    lane = lax.broadcasted_iota(jnp.int32, (p2, LANES), 1)
    e_end = (SSM_T - 1 - lane % SSM_T).astype(F32)
    pm = jnp.exp(zr * e_end)
    pw_re, pw_im = pm * jnp.cos(zi * e_end), pm * jnp.sin(zi * e_end)
    cp_re = cf_re * pw_re - cf_im * pw_im
    cp_im = cf_re * pw_im + cf_im * pw_re
    coef = jnp.concatenate([cp_re, cp_im, cf_re, cf_im], axis=0)
    sel_row = lax.broadcasted_iota(jnp.int32, (LANES, w), 0)
    sel_col = lax.broadcasted_iota(jnp.int32, (LANES, w), 1)

    tau = lax.broadcasted_iota(jnp.int32, (SSM_T + SUBLANES, p2), 0).astype(F32)
    lane2 = lax.broadcasted_iota(jnp.int32, (w, p2), 1)
    sblk = lax.broadcasted_iota(jnp.int32, (w, w), 1) // SSM_GROUP

    for j in range(PREP_PAIRS):
        own_pair = (sel_row // SSM_T) == j
        spread = jnp.where(own_pair & (sel_row % SSM_T == sel_col // SSM_GROUP), 1.0, 0.0)
        cpx = jnp.dot(coef[:2 * p2], spread, precision=hi, preferred_element_type=F32)
        cfx = jnp.broadcast_to(coef[2 * p2:, j * SSM_T:j * SSM_T + 1], (2 * p2, w))
        b_re, b_im = jnp.tile(bre_ref[j], (1, SSM_T)), jnp.tile(bim_ref[j], (1, SSM_T))
        end_re = cpx[:p2] * b_re - cpx[p2:] * b_im
        end_im = cpx[:p2] * b_im + cpx[p2:] * b_re
        bb_re = cfx[:p2] * b_re - cfx[p2:] * b_im
        bb_im = cfx[:p2] * b_im + cfx[p2:] * b_re
        for r in range(2):
            sl = slice(r * p, (r + 1) * p)
            bend_ref[2 * j + r] = jnp.concatenate([end_re[sl], end_im[sl]], axis=0).astype(BF16)
        bst = jnp.concatenate([bb_re, bb_im], axis=0)

        lr2, li2 = lrr_ref[j], lir_ref[j]
        dt2 = jnp.exp(dtr_ref[j])
        zr2, zi2 = lr2 * dt2, li2 * dt2
        qm = jnp.exp(zr2 * tau)
        q_re, q_im = qm * jnp.cos(zi2 * tau), qm * jnp.sin(zi2 * tau)
        c_re, c_im = cre_ref[j], cim_ref[j]

        def c_times_powers(first):
            re = [c_re * q_re[t:t + 1] - c_im * q_im[t:t + 1] for t in range(first, first + SSM_T)]
            im = [c_re * q_im[t:t + 1] + c_im * q_re[t:t + 1] for t in range(first, first + SSM_T)]
            return jnp.concatenate(re, axis=0), jnp.concatenate(im, axis=0)

        g_re, g_im = c_times_powers(0)
        ci_re, ci_im = c_times_powers(1)
        for r in range(2):
            own = (lane2 // p) == r
            lhs = jnp.concatenate([jnp.where(own, g_re, 0.0), jnp.where(own, -g_im, 0.0)], axis=1)
            gen = jnp.dot(lhs, bst, preferred_element_type=F32)
            k = 0
            while (SSM_GROUP << k) < w:
                sh = SSM_GROUP << k
                shifted = jnp.concatenate([jnp.zeros((sh, w), F32), gen[:w - sh]], axis=0)
                gen = jnp.where(((sblk >> k) & 1) == 1, shifted, gen)
                k += 1
            mt_ref[2 * j + r] = gen.astype(BF16)
            cin_ref[2 * j + r] = jnp.concatenate(
                [jnp.where(own, ci_re, 0.0), jnp.where(own, -ci_im, 0.0)], axis=1).astype(BF16)
        a16r_ref[j] = q_re[SSM_T:SSM_T + 1]
        a16i_ref[j] = q_im[SSM_T:SSM_T + 1]


def _ssm_prep(lam_re, lam_im, log_dt, b_re, b_im, c_re, c_im):
    g, p = lam_re.shape
    gp = g // 2
    p2 = 2 * p
    w = SSM_T * SSM_GROUP
    pp = PREP_PAIRS
    assert gp % pp == 0
    colx = lambda a: jnp.repeat(a.reshape(gp, p2).T, SSM_T, axis=1)
    row = lambda a: a.reshape(gp, 1, p2)
    dt_full = jnp.broadcast_to(log_dt[:, None], (g, p))
    b_t = lambda a: a.reshape(gp, p2, SSM_GROUP)
    c_pair = lambda a: a.reshape(gp, 2, SSM_GROUP, p).transpose(0, 2, 1, 3).reshape(gp, SSM_GROUP, p2)
    xspec = pl.BlockSpec((p2, LANES), lambda i: (0, i))
    rowspec = pl.BlockSpec((pp, 1, p2), lambda i: (i, 0, 0))
    bspec = pl.BlockSpec((pp, p2, SSM_GROUP), lambda i: (i, 0, 0))
    cspec = pl.BlockSpec((pp, SSM_GROUP, p2), lambda i: (i, 0, 0))
    mt, bend, cin, a16r, a16i = pl.pallas_call(
        _ssm_prep_kernel,
        out_shape=[
            jax.ShapeDtypeStruct((g, w, w), BF16),
            jax.ShapeDtypeStruct((g, p2, w), BF16),
            jax.ShapeDtypeStruct((g, w, 2 * p2), BF16),
            jax.ShapeDtypeStruct((gp, 1, p2), F32),
            jax.ShapeDtypeStruct((gp, 1, p2), F32),
        ],
        grid=(gp // pp,),
        in_specs=[xspec, xspec, xspec, bspec, bspec, rowspec, rowspec, rowspec, cspec, cspec],
        out_specs=[
            pl.BlockSpec((2 * pp, w, w), lambda i: (i, 0, 0)),
            pl.BlockSpec((2 * pp, p2, w), lambda i: (i, 0, 0)),
            pl.BlockSpec((2 * pp, w, 2 * p2), lambda i: (i, 0, 0)),
            rowspec, rowspec,
        ],
        compiler_params=_params("arbitrary"),
        name="ssm_prep",
    )(colx(lam_re), colx(lam_im), colx(dt_full), b_t(b_re), b_t(b_im),
      row(lam_re), row(lam_im), row(dt_full), c_pair(c_re), c_pair(c_im))
    return mt, bend, cin, a16r.reshape(1, g * p), a16i.reshape(1, g * p)


def _step_rows(s):
    return pl.ds(s, CHUNK_TILE, stride=SSM_T)


def _build_ut(u_ref, ut_ref):
    for s in range(SSM_T):
        xt = u_ref[_step_rows(s), :].T.astype(BF16)
        for gl in range(GROUPS_PER_TILE):
            ut_ref[gl, s * SSM_GROUP:(s + 1) * SSM_GROUP, :] = xt[gl * SSM_GROUP:(gl + 1) * SSM_GROUP, :]


STATE_TILES = 2


def _ssm_state_kernel(*refs):
    u_refs = refs[:STATE_TILES]
    bend_ref, sre_ref, sim_ref = refs[STATE_TILES:STATE_TILES + 3]
    ut_refs = refs[STATE_TILES + 3:]
    p = SSM_STATE
    for tile in range(STATE_TILES):
        _build_ut(u_refs[tile], ut_refs[tile])
        for pr in range(GROUPS_PER_TILE // 2):
            st = [jnp.dot(bend_ref[tile * GROUPS_PER_TILE + 2 * pr + r], ut_refs[tile][2 * pr + r],
                          preferred_element_type=F32)
                  for r in range(2)]
            cols = slice((tile * GROUPS_PER_TILE // 2 + pr) * 2 * p,
                         (tile * GROUPS_PER_TILE // 2 + pr + 1) * 2 * p)
            sre_ref[:, cols] = jnp.concatenate([st[0][:p], st[1][:p]], axis=0).T
            sim_ref[:, cols] = jnp.concatenate([st[0][p:], st[1][p:]], axis=0).T


def _ssm_state_contrib(uz, bend, jpad):
    g, p2, w = bend.shape
    gstep = STATE_TILES * GROUPS_PER_TILE
    assert g % gstep == 0
    sw = gstep * SSM_STATE
    out = jax.ShapeDtypeStruct((jpad, g * SSM_STATE), F32)
    ospec = pl.BlockSpec((CHUNK_TILE, sw), lambda q, j: (j, q))
    utile = lambda t: pl.BlockSpec((CHUNK_TILE * SSM_T, LANES), lambda q, j: (j, STATE_TILES * q + t))
    return pl.pallas_call(
        _ssm_state_kernel,
        out_shape=[out, out],
        grid=(g // gstep, jpad // CHUNK_TILE),
        in_specs=[*[utile(t) for t in range(STATE_TILES)],
                  pl.BlockSpec((gstep, p2, w), lambda q, j: (q, 0, 0))],
        out_specs=[ospec, ospec],
        scratch_shapes=[pltpu.VMEM((GROUPS_PER_TILE, w, CHUNK_TILE), BF16) for _ in range(STATE_TILES)],
        compiler_params=_params("arbitrary", "arbitrary"),
        name="ssm_state_contrib",
    )(*[uz] * STATE_TILES, bend)


def _ssm_scan_kernel(sre_ref, sim_ref, ar_ref, ai_ref, h0r_ref, h0i_ref,
                     hpr_ref, hpi_ref, pr_ref, pi_ref, sr_ref, si_ref, hr_sc, hi_sc, *, n_prompt):
    i = pl.program_id(0)
    rb = sre_ref.shape[0]
    ar, ai = ar_ref[...], ai_ref[...]

    @pl.when(i == 0)
    def _():
        hr_sc[...] = jnp.zeros_like(hr_sc)
        hi_sc[...] = jnp.zeros_like(hi_sc)

    @pl.when(i < n_prompt)
    def _():
        def body(jb, c):
            hr, hi = hr_sc[...], hi_sc[...]
            for r in range(SUBLANES):
                row = pl.ds(jb * SUBLANES + r, 1)
                hpr_ref[row, :] = hr
                hpi_ref[row, :] = hi
                hr, hi = (ar * hr - ai * hi + sre_ref[row, :],
                          ar * hi + ai * hr + sim_ref[row, :])
            hr_sc[...] = hr
            hi_sc[...] = hi
            return c

        lax.fori_loop(0, rb // SUBLANES, body, 0)

    @pl.when(i == n_prompt - 1)
    def _():
        pr_ref[...] = hr_sc[...]
        pi_ref[...] = hi_sc[...]

    @pl.when(i == n_prompt)
    def _():
        h0r, h0i = h0r_ref[...], h0i_ref[...]
        hpr_ref[...] = h0r
        hpi_ref[...] = h0i
        sr_ref[...] = ar * h0r - ai * h0i + sre_ref[...]
        si_ref[...] = ar * h0i + ai * h0r + sim_ref[...]

    @pl.when(i > n_prompt)
    def _():
        hpr_ref[...] = jnp.zeros_like(hpr_ref)
        hpi_ref[...] = jnp.zeros_like(hpi_ref)


def _ssm_scan(s_re, s_im, a16r, a16i, h0r, h0i, jp):
    jpad, n = s_re.shape
    rb = h0r.shape[0]
    assert jp % rb == 0 and jpad % rb == 0
    n_prompt = jp // rb
    rows = pl.BlockSpec((rb, n), lambda i: (i, 0))
    const1 = pl.BlockSpec((1, n), lambda i: (0, 0))
    constb = pl.BlockSpec((rb, n), lambda i: (0, 0))
    big = jax.ShapeDtypeStruct((jpad, n), F32)
    one = jax.ShapeDtypeStruct((1, n), F32)
    bat = jax.ShapeDtypeStruct((rb, n), F32)
    return pl.pallas_call(
        functools.partial(_ssm_scan_kernel, n_prompt=n_prompt),
        out_shape=[big, big, one, one, bat, bat],
        grid=(jpad // rb,),
        in_specs=[rows, rows, const1, const1, constb, constb],
        out_specs=[rows, rows, const1, const1, constb, constb],
        scratch_shapes=[pltpu.VMEM((1, n), F32), pltpu.VMEM((1, n), F32)],
        compiler_params=_params("arbitrary"),
        name="ssm_scan",
    )(s_re, s_im, a16r, a16i, h0r, h0i)


def _ssm_out_kernel(u_ref, mt_ref, cin_ref, hpr_ref, hpi_ref, d_ref, o_ref, ut_ref, uf_ref, yt_ref):
    for s in range(SSM_T):
        xt = u_ref[_step_rows(s), :].T
        xb = xt.astype(BF16)
        for gl in range(GROUPS_PER_TILE):
            rows = slice(gl * SSM_GROUP, (gl + 1) * SSM_GROUP)
            ut_ref[gl, s * SSM_GROUP:(s + 1) * SSM_GROUP, :] = xb[rows, :]
            uf_ref[gl, s * SSM_GROUP:(s + 1) * SSM_GROUP, :] = xt[rows, :]
    p2 = 2 * SSM_STATE
    for gl in range(GROUPS_PER_TILE):
        pr = gl // 2
        hp = jnp.concatenate([hpr_ref[:, pr * p2:(pr + 1) * p2], hpi_ref[:, pr * p2:(pr + 1) * p2]],
                             axis=1).astype(BF16)
        yt = jnp.dot(mt_ref[gl], ut_ref[gl], preferred_element_type=F32)
        yt = yt + lax.dot_general(cin_ref[gl], hp, _NT, preferred_element_type=F32)
        yt = jax.nn.gelu(yt + jnp.tile(d_ref[gl], (SSM_T, 1)) * uf_ref[gl])
        for t in range(SSM_T):
            yt_ref[t, gl * SSM_GROUP:(gl + 1) * SSM_GROUP, :] = yt[t * SSM_GROUP:(t + 1) * SSM_GROUP, :]
    for t in range(SSM_T):
        o_ref[_step_rows(t), :] = yt_ref[t].T


def _ssm_output(uz, mt, cin, hp_re, hp_im, d_skip, jpad):
    g, w, _ = mt.shape
    w2 = cin.shape[-1]
    ntile = g // GROUPS_PER_TILE
    sw = GROUPS_PER_TILE * SSM_STATE
    tok = pl.BlockSpec((CHUNK_TILE * SSM_T, LANES), lambda q, j: (j, q))
    hspec = pl.BlockSpec((CHUNK_TILE, sw), lambda q, j: (j, q))
    return pl.pallas_call(
        _ssm_out_kernel,
        out_shape=jax.ShapeDtypeStruct((jpad * SSM_T, g * SSM_GROUP), F32),
        grid=(ntile, jpad // CHUNK_TILE),
        in_specs=[
            tok,
            pl.BlockSpec((GROUPS_PER_TILE, w, w), lambda q, j: (q, 0, 0)),
            pl.BlockSpec((GROUPS_PER_TILE, w, w2), lambda q, j: (q, 0, 0)),
            hspec, hspec,
            pl.BlockSpec((GROUPS_PER_TILE, SSM_GROUP, CHUNK_TILE), lambda q, j: (q, 0, 0)),
        ],
        out_specs=tok,
        scratch_shapes=[
            pltpu.VMEM((GROUPS_PER_TILE, w, CHUNK_TILE), BF16),
            pltpu.VMEM((GROUPS_PER_TILE, w, CHUNK_TILE), F32),
            pltpu.VMEM((SSM_T, LANES, CHUNK_TILE), F32),
        ],
        compiler_params=_params("arbitrary", "arbitrary"),
        name="ssm_output",
    )(uz, mt, cin, hp_re, hp_im, d_skip)


def _a_glu_kernel(y_ref, z_ref, w_ref, o_ref, wb_ref):
    r = pl.program_id(0)

    @pl.when(r == 0)
    def _():
        wb_ref[...] = w_ref[...].astype(BF16)

    @pl.when(r > 0)
    def _():
        y = y_ref[...]
        gate = jnp.dot(y.astype(BF16), wb_ref[...], preferred_element_type=F32)
        y2 = y * jax.nn.sigmoid(gate)
        o_ref[...] = (y2 * jax.nn.silu(z_ref[...])).astype(BF16)


def _a_glu(yg, uz, w_glu, rows):
    tm = ROW_TILE
    e = yg.shape[1]
    return pl.pallas_call(
        _a_glu_kernel,
        out_shape=jax.ShapeDtypeStruct((rows, e), BF16),
        grid=(1 + rows // tm,),
        in_specs=[
            pl.BlockSpec((tm, e), lambda r: (_row_block(r), 0)),
            pl.BlockSpec((tm, e), lambda r: (_row_block(r), 1)),
            _resident((e, e), lambda r: (0, 0)),
        ],
        out_specs=pl.BlockSpec((tm, e), lambda r: (_row_block(r), 0)),
        scratch_shapes=[pltpu.VMEM((e, e), BF16)],
        compiler_params=_params("arbitrary"),
        name="a_glu",
    )(yg, uz, w_glu)


def _a_out_kernel(o_ref, w_ref, xp_ref, xs_ref, x1_ref, wb_ref, *, mp):
    m = pl.program_id(0) - 1

    @pl.when(m < 0)
    def _():
        wb_ref[...] = w_ref[...].astype(BF16)

    @pl.when((m >= 0) & (m < mp))
    def _():
        x1_ref[...] = xp_ref[...] + jnp.dot(o_ref[...], wb_ref[...], preferred_element_type=F32)

    @pl.when(m == mp)
    def _():
        x1_ref[...] = xs_ref[...] + jnp.dot(o_ref[...], wb_ref[...], preferred_element_type=F32)


def _a_out_proj(o, w_out, xp, xs):
    tm = ROW_TILE
    rows, e = o.shape
    d = w_out.shape[1]
    mp = xp.shape[0] // tm
    return pl.pallas_call(
        functools.partial(_a_out_kernel, mp=mp),
        out_shape=jax.ShapeDtypeStruct((rows, d), F32),
        grid=(1 + rows // tm,),
        in_specs=[
            pl.BlockSpec((tm, e), lambda r: (_row_block(r), 0)),
            _resident((e, d), lambda r: (0, 0)),
            pl.BlockSpec((tm, d), lambda r: (jnp.minimum(_row_block(r), mp - 1), 0)),
            _resident((tm, d), lambda r: (0, 0)),
        ],
        out_specs=pl.BlockSpec((tm, d), lambda r: (_row_block(r), 0)),
        scratch_shapes=[pltpu.VMEM((e, d), BF16)],
        compiler_params=_params("arbitrary"),
        name="a_out_proj",
    )(o, w_out, xp, xs)


def _rope(x, cos, sin):
    outs = []
    for c in range(x.shape[1] // HEAD_DIM):
        xc = x[:, c * HEAD_DIM:(c + 1) * HEAD_DIM]
        outs.append(xc * cos + pltpu.roll(xc, HEAD_DIM // 2, axis=1) * sin)
    return jnp.concatenate(outs, axis=1) if len(outs) > 1 else outs[0]


def _rope_block(co_ref, so_ref, cb_ref, sb_ref):
    co, so = co_ref[...], so_ref[...]
    cb, sb = cb_ref[...], sb_ref[...]
    lane = lax.broadcasted_iota(jnp.int32, (1, HEAD_DIM), 1)
    sign = jnp.where(lane < HEAD_DIM // 2, -1.0, 1.0)
    return cb * co - sb * so, (sb * co + cb * so) * sign


def _store_heads(o_ref, val, tm):
    for h in range(N_HEADS):
        o_ref[pl.ds(h, tm, stride=N_HEADS), :] = val[:, h * HEAD_DIM:(h + 1) * HEAD_DIM]


def _prompt_col(r, mp, col, last):
    return jnp.where(_row_block(r) >= mp, last, col)


def _sample_col(r, mp, col):
    return jnp.where(_row_block(r) == mp, col, 0)


def _kv_proj_kernel(x_ref, g_ref, w_ref, co_ref, so_ref, cb_ref, sb_ref, kp_ref, vp_ref, cbp_ref,
                    ks_ref, vs_ref, cbs_ref, xn_ref, wb_ref, *, tm, mp):
    m, n = pl.program_id(0) - 1, pl.program_id(1)
    hpb = w_ref.shape[1] // (2 * HEAD_DIM)

    for nn in range(4):
        @pl.when((m < 0) & (n == nn))
        def _():
            for hl in range(hpb):
                h = hpb * (nn % 2) + hl
                for half in range(2):
                    src = (2 * hl + half) * HEAD_DIM
                    wb_ref[2 * (nn // 2) + half, :, h * HEAD_DIM:(h + 1) * HEAD_DIM] = (
                        w_ref[:, src:src + HEAD_DIM].astype(BF16))

    def column(c, k_ref, v_ref, copy_ref):
        if c == 0:
            xn_ref[...] = _rms_scale(x_ref[...], g_ref[...]).astype(BF16)
        acc = jnp.dot(xn_ref[...], wb_ref[c], preferred_element_type=F32)
        if c < 2:
            acc = _rope(acc, *_rope_block(co_ref, so_ref, cb_ref, sb_ref))
        _store_heads(k_ref if c < 2 else v_ref, acc, tm)
        copy_ref[...] = acc.astype(BF16)

    for c in range(4):
        @pl.when((m >= 0) & (m < mp) & (n == c))
        def _():
            column(c, kp_ref, vp_ref, cbp_ref)

        @pl.when((m == mp) & (n == c))
        def _():
            column(c, ks_ref, vs_ref, cbs_ref)


def _kv_proj(x1, rows_p, g, w_kv, rope):
    tm = ROW_TILE
    rows, d = x1.shape
    tn = N_HEADS * HEAD_DIM
    mp = rows_p // tm
    assert w_kv.shape[1] == 4 * tn and rows == rows_p + tm and rows_p % tm == 0
    out4 = lambda nrows: jax.ShapeDtypeStruct((nrows * N_HEADS, 2 * HEAD_DIM), F32)
    outb = lambda nrows: jax.ShapeDtypeStruct((nrows, 4 * tn), BF16)
    prow = lambda r: jnp.minimum(_row_block(r), mp - 1)
    col = _col_block
    return pl.pallas_call(
        functools.partial(_kv_proj_kernel, tm=tm, mp=mp),
        out_shape=[out4(rows_p), out4(rows_p), outb(rows_p), out4(tm), out4(tm), outb(tm)],
        grid=(1 + rows // tm, 4),
        in_specs=[
            pl.BlockSpec((tm, d), lambda r, n: (_row_block(r), 0)),
            _resident((1, d), lambda r, n: (0, 0)),
            _staged_weight_spec(d, tn, 4),
            *_rope_specs(tm, mp),
        ],
        out_specs=[
            pl.BlockSpec((tm * N_HEADS, HEAD_DIM),
                         lambda r, n: (prow(r), _prompt_col(r, mp, jnp.minimum(col(r, n), 1), 1))),
            pl.BlockSpec((tm * N_HEADS, HEAD_DIM),
                         lambda r, n: (prow(r), _prompt_col(r, mp, jnp.maximum(col(r, n) - 2, 0), 1))),
            pl.BlockSpec((tm, tn), lambda r, n: (prow(r), _prompt_col(r, mp, col(r, n), 3))),
            pl.BlockSpec((tm * N_HEADS, HEAD_DIM),
                         lambda r, n: (0, _sample_col(r, mp, jnp.minimum(col(r, n), 1)))),
            pl.BlockSpec((tm * N_HEADS, HEAD_DIM),
                         lambda r, n: (0, _sample_col(r, mp, jnp.maximum(col(r, n) - 2, 0)))),
            pl.BlockSpec((tm, tn), lambda r, n: (0, _sample_col(r, mp, col(r, n)))),
        ],
        scratch_shapes=[pltpu.VMEM((tm, d), BF16), pltpu.VMEM((4, d, tn), BF16)],
        compiler_params=_params("arbitrary", "arbitrary", vmem_limit=KV_PROJ_VMEM_LIMIT),
        name="kv_proj",
    )(x1, g, w_kv, *rope)


def _qz_proj_kernel(x_ref, g_ref, w_ref, co_ref, so_ref, cb_ref, sb_ref, qp_ref, zp_ref, qs_ref, zs_ref,
                    xn_ref, wb_ref, *, nq, mp):
    m, n = pl.program_id(0) - 1, pl.program_id(1)

    @pl.when(m < 0)
    def _():
        wb_ref[n] = w_ref[...].astype(BF16)

    def column(c, q_ref, z_ref):
        if c == 0:
            xn_ref[...] = _rms_scale(x_ref[...], g_ref[...]).astype(BF16)
        acc = jnp.dot(xn_ref[...], wb_ref[c], preferred_element_type=F32)
        if c < nq:
            rot = _rope(acc, *_rope_block(co_ref, so_ref, cb_ref, sb_ref))
            q_ref[...] = (rot * Q_SCALE).astype(BF16)
        else:
            z_ref[...] = acc

    for c in range(2 * nq):
        @pl.when((m >= 0) & (m < mp) & (n == c))
        def _():
            column(c, qp_ref, zp_ref)

        @pl.when((m == mp) & (n == c))
        def _():
            column(c, qs_ref, zs_ref)


def _qz_proj(x1, rows_p, g, w, rope, tn):
    tm = ROW_TILE
    rows, d = x1.shape
    half = w.shape[1] // 2
    nq = half // tn
    mp = rows_p // tm
    assert rows == rows_p + tm and rows_p % tm == 0
    prow = lambda r: jnp.minimum(_row_block(r), mp - 1)
    qcol = lambda r, n: jnp.minimum(_col_block(r, n), nq - 1)
    zcol = lambda r, n: jnp.maximum(_col_block(r, n) - nq, 0)
    return pl.pallas_call(
        functools.partial(_qz_proj_kernel, nq=nq, mp=mp),
        out_shape=[jax.ShapeDtypeStruct((rows_p, half), BF16), jax.ShapeDtypeStruct((rows_p, half), F32),
                   jax.ShapeDtypeStruct((tm, half), BF16), jax.ShapeDtypeStruct((tm, half), F32)],
        grid=(1 + rows // tm, 2 * nq),
        in_specs=[
            pl.BlockSpec((tm, d), lambda r, n: (_row_block(r), 0)),
            _resident((1, d), lambda r, n: (0, 0)),
            _staged_weight_spec(d, tn, 2 * nq),
            *_rope_specs(tm, mp),
        ],
        out_specs=[
            pl.BlockSpec((tm, tn), lambda r, n: (prow(r), _prompt_col(r, mp, qcol(r, n), nq - 1))),
            pl.BlockSpec((tm, tn), lambda r, n: (prow(r), _prompt_col(r, mp, zcol(r, n), nq - 1))),
            pl.BlockSpec((tm, tn), lambda r, n: (0, _sample_col(r, mp, qcol(r, n)))),
            pl.BlockSpec((tm, tn), lambda r, n: (0, _sample_col(r, mp, zcol(r, n)))),
        ],
        scratch_shapes=[pltpu.VMEM((tm, d), BF16), pltpu.VMEM((2 * nq, d, tn), BF16)],
        compiler_params=_params("arbitrary", "arbitrary"),
        name="qz_proj",
    )(x1, g, w, *rope)


def _rope_specs(tm, mp):
    off = pl.BlockSpec((None, tm, HEAD_DIM), lambda r, n: (jnp.where(_row_block(r) == mp, 1, 0), 0, 0))
    base = pl.BlockSpec((None, 1, HEAD_DIM), lambda r, n: (_row_block(r), 0, 0))
    return [off, off, base, base]


def _rope_tables(rows_p, past_len, t_s, tm):
    inv_freq = ROPE_THETA ** (-jnp.arange(0, HEAD_DIM, 2, dtype=F32) / HEAD_DIM)
    full = lambda a: jnp.concatenate([a, a], axis=-1)
    rows = jnp.arange(tm, dtype=jnp.int32)
    off = jnp.stack([rows, rows % t_s]).astype(F32)[..., None] * inv_freq
    base = jnp.concatenate([jnp.arange(0, rows_p, tm, dtype=jnp.int32),
                            jnp.full((1,), past_len, jnp.int32)]).astype(F32)[:, None, None] * inv_freq
    return full(jnp.cos(off)), full(jnp.sin(off)), full(jnp.cos(base)), full(jnp.sin(base))


def _diff_lambda(lq1_ref, lk1_ref, lq2_ref, lk2_ref, lambda_init):
    s1 = jnp.sum(lq1_ref[...] * lk1_ref[...], axis=-1, keepdims=True)
    s2 = jnp.sum(lq2_ref[...] * lk2_ref[...], axis=-1, keepdims=True)
    return jnp.exp(s1) - jnp.exp(s2) + lambda_init


def _attn_finish(a1, l1, a2, l2, lam, subln, z, lambda_init):
    o = a1 / l1 - lam * (a2 / l2)
    o = _rms_scale(o, subln) * (1.0 - lambda_init)
    return (o * jax.nn.silu(z)).astype(BF16)


def _attn_prompt_kernel(q_ref, k1_ref, k2_ref, v1_ref, v2_ref, z_ref,
                        lq1_ref, lk1_ref, lq2_ref, lk2_ref, sub_ref,
                        o_ref, m_sc, l_sc, al_sc, a_sc, *sp_scs, tq, ts, tk, lambda_init):
    qi = pl.program_id(1)
    k_refs = (k1_ref, k2_ref)
    nsub = tq // ts
    s_scs, p_scs = sp_scs[:2 * nsub], sp_scs[2 * nsub:]

    def softmax_strips(sub, n, col0, first):
        for r in range(ts // STRIP):
            row0 = sub * ts + r * STRIP
            lr = slice(r * STRIP, (r + 1) * STRIP)
            gr = slice(row0, row0 + STRIP)
            visible = tk if col0 is None else min(tk, (row0 // CHUNK + 1) * CHUNK - col0)
            assert visible > 0
            ncv = -(-visible // LANES)
            s = s_scs[2 * sub + n][lr, 0:ncv * LANES]
            if visible < ncv * LANES:
                col = lax.broadcasted_iota(jnp.int32, s.shape, 1)
                s = jnp.where(col < visible, s, NEG)
            cols = [s[:, c * LANES:(c + 1) * LANES] for c in range(ncv)]
            m_cur = jnp.max(functools.reduce(jnp.maximum, cols), axis=-1, keepdims=True)
            if first:
                m_new = jnp.broadcast_to(m_cur, (STRIP, LANES))
            else:
                m_old = m_sc[n, gr, :]
                m_new = jnp.maximum(m_old, m_cur)
            ps = [jnp.exp2(c - m_new) for c in cols]
            l_cur = jnp.sum(functools.reduce(jnp.add, ps), axis=-1, keepdims=True)
            if first:
                l_sc[n, gr, :] = jnp.broadcast_to(l_cur, (STRIP, LANES))
            else:
                alpha = jnp.exp2(m_old - m_new)
                l_sc[n, gr, :] = alpha * l_sc[n, gr, :] + l_cur
                al_sc[n, gr, :] = alpha
            m_sc[n, gr, :] = m_new
            for c in range(keys_seen(sub, col0) // LANES):
                pc = ps[c].astype(BF16) if c < ncv else jnp.zeros((STRIP, LANES), BF16)
                p_scs[2 * sub + n][lr, c * LANES:(c + 1) * LANES] = pc

    def keys_seen(sub, col0):
        return tk if col0 is None else min(tk, (sub + 1) * ts - col0)

    def block(kb, col0, subs, first=False):
        start = pl.multiple_of(kb * tk, tk)
        for sub in subs:
            rows = pl.ds(start, keys_seen(sub, col0))
            for n in range(2):
                qn = q_ref[sub * ts:(sub + 1) * ts, n * HEAD_DIM:(n + 1) * HEAD_DIM]
                s_scs[2 * sub + n][:, 0:rows.size] = lax.dot_general(
                    qn, k_refs[n][rows, :], _NT, preferred_element_type=F32)
        for sub in subs:
            gr = slice(sub * ts, (sub + 1) * ts)
            rows = pl.ds(start, keys_seen(sub, col0))
            vblk = jnp.concatenate([v1_ref[rows, :], v2_ref[rows, :]], axis=1)
            for n in range(2):
                softmax_strips(sub, n, col0, first)
                pv = jnp.dot(p_scs[2 * sub + n][:, 0:rows.size], vblk, preferred_element_type=F32)
                if first:
                    a_sc[n, gr, :] = pv
                else:
                    alpha = al_sc[n, gr, :]
                    a_sc[n, gr, :] = a_sc[n, gr, :] * jnp.concatenate([alpha, alpha], axis=1) + pv

    nd = tq // tk
    for d in range(nd):
        block(qi * nd + d, d * tk, [sub for sub in range(nsub) if (sub + 1) * ts > d * tk],
              first=(d == 0))

    def body(kb, c):
        block(kb, None, range(nsub))
        return c

    lax.fori_loop(0, qi * nd, body, 0)

    lam = _diff_lambda(lq1_ref, lk1_ref, lq2_ref, lk2_ref, lambda_init)
    l1 = jnp.concatenate([l_sc[0], l_sc[0]], axis=1)
    l2 = jnp.concatenate([l_sc[1], l_sc[1]], axis=1)
    o_ref[...] = _attn_finish(a_sc[0], l1, a_sc[1], l2, lam, sub_ref[...], z_ref[...], lambda_init)


def _attn_prompt(qb, z32, kvb, lq1, lk1, lq2, lk2, subln, lambda_init, tq, ts, tk):
    rows = qb.shape[0]
    hw = 2 * HEAD_DIM
    nsub = tq // ts
    assert tq % tk == 0 and tk % ts == 0 and ts % STRIP == 0 and CHUNK % STRIP == 0
    vec = pl.BlockSpec((1, HEAD_DIM), lambda h, i: (0, 0))
    kcol = lambda c: pl.BlockSpec((rows, HEAD_DIM), lambda h, i: (0, c * N_HEADS + h))
    stat = pltpu.VMEM((2, tq, LANES), F32)
    return pl.pallas_call(
        functools.partial(_attn_prompt_kernel, tq=tq, ts=ts, tk=tk, lambda_init=lambda_init),
        out_shape=jax.ShapeDtypeStruct((rows, N_HEADS * hw), BF16),
        grid=(N_HEADS, rows // tq),
        in_specs=[
            pl.BlockSpec((tq, hw), lambda h, i: (i, h)),
            kcol(0), kcol(1), kcol(2), kcol(3),
            pl.BlockSpec((tq, hw), lambda h, i: (i, h)),
            vec, vec, vec, vec,
            pl.BlockSpec((1, hw), lambda h, i: (0, 0)),
        ],
        out_specs=pl.BlockSpec((tq, hw), lambda h, i: (i, h)),
        scratch_shapes=[
            stat, stat, stat,
            pltpu.VMEM((2, tq, hw), F32),
            *[pltpu.VMEM((ts, tk), F32) for _ in range(2 * nsub)],
            *[pltpu.VMEM((ts, tk), BF16) for _ in range(2 * nsub)],
        ],
        compiler_params=_params("arbitrary", "arbitrary"),
        name="attn_prompt",
    )(qb, kvb, kvb, kvb, kvb, z32, lq1, lk1, lq2, lk2, subln)


def _attn_sample_kernel(q_ref, ck1_ref, ck2_ref, cv1_ref, cv2_ref, kvn_ref, z_ref,
                        lq1_ref, lk1_ref, lq2_ref, lk2_ref, sub_ref,
                        o_ref, m_sc, l_sc, a_sc, s_sc, p_sc, *, tkv, nkb, past_len, lambda_init):
    kb = pl.program_id(1)
    t = q_ref.shape[0]
    hw = 2 * HEAD_DIM
    rows_all = 2 * N_HEADS * t
    strip = 4 * t
    ck_refs = (ck1_ref, ck2_ref)
    assert (past_len - 1) // CHUNK <= past_len // CHUNK

    @pl.when(kb == 0)
    def _():
        m_sc[...] = jnp.full(m_sc.shape, NEG, F32)
        l_sc[...] = jnp.zeros(l_sc.shape, F32)
        a_sc[...] = jnp.zeros(a_sc.shape, F32)

    col_blk = lambda c, h: slice((c * N_HEADS + h) * HEAD_DIM, (c * N_HEADS + h + 1) * HEAD_DIM)
    head_rows = lambda h: pl.ds(h, tkv, stride=N_HEADS)

    def update(with_new):
        ncols = tkv + (LANES if with_new else 0)
        if with_new:
            row = lax.broadcasted_iota(jnp.int32, (t, LANES), 0) + past_len
            col = lax.broadcasted_iota(jnp.int32, (t, LANES), 1)
            new_mask = (col < t) & (((col + past_len) // CHUNK) <= (row // CHUNK))
            pad_k = jnp.zeros((LANES - t, HEAD_DIM), BF16)
        for h in range(N_HEADS):
            for n in range(2):
                qn = q_ref[:, (2 * h + n) * HEAD_DIM:(2 * h + n + 1) * HEAD_DIM]
                rr = slice((2 * h + n) * t, (2 * h + n + 1) * t)
                k = ck_refs[n][head_rows(h), :].astype(BF16)
                s_sc[rr, 0:tkv] = lax.dot_general(qn, k, _NT, preferred_element_type=F32)
                if with_new:
                    kn = jnp.concatenate([kvn_ref[:, col_blk(n, h)], pad_k], axis=0)
                    sn = lax.dot_general(qn, kn, _NT, preferred_element_type=F32)
                    s_sc[rr, tkv:ncols] = jnp.where(new_mask, sn, NEG)
        for r in range(rows_all // strip):
            rr = slice(r * strip, (r + 1) * strip)
            cols = [s_sc[rr, c * LANES:(c + 1) * LANES] for c in range(ncols // LANES)]
            m_cur = jnp.max(functools.reduce(jnp.maximum, cols), axis=-1, keepdims=True)
            m_old = m_sc[rr, :]
            m_new = jnp.maximum(m_old, m_cur)
            ps = [jnp.exp2(c - m_new) for c in cols]
            l_cur = jnp.sum(functools.reduce(jnp.add, ps), axis=-1, keepdims=True)
            alpha = jnp.exp2(m_old - m_new)
            l_sc[rr, :] = alpha * l_sc[rr, :] + l_cur
            m_sc[rr, :] = m_new
            a_sc[rr, :] = a_sc[rr, :] * jnp.concatenate([alpha, alpha], axis=1)
            for c, pc in enumerate(ps):
                p_sc[rr, c * LANES:(c + 1) * LANES] = pc.astype(BF16)
        for h in range(N_HEADS):
            rr = slice(2 * h * t, (2 * h + 2) * t)
            v = jnp.concatenate([cv1_ref[head_rows(h), :], cv2_ref[head_rows(h), :]],
                                axis=1).astype(BF16)
            pv = jnp.dot(p_sc[rr, 0:tkv], v, preferred_element_type=F32)
            if with_new:
                vn = jnp.concatenate([kvn_ref[:, col_blk(2, h)], kvn_ref[:, col_blk(3, h)]], axis=1)
                vn = jnp.concatenate([vn, jnp.zeros((LANES - t, hw), BF16)], axis=0)
                pv = pv + jnp.dot(p_sc[rr, tkv:ncols], vn, preferred_element_type=F32)
            a_sc[rr, :] = a_sc[rr, :] + pv

    @pl.when(kb < nkb - 1)
    def _():
        update(False)

    @pl.when(kb == nkb - 1)
    def _():
        update(True)
        lam = _diff_lambda(lq1_ref, lk1_ref, lq2_ref, lk2_ref, lambda_init)
        sub = sub_ref[...]
        for h in range(N_HEADS):
            r1 = slice(2 * h * t, (2 * h + 1) * t)
            r2 = slice((2 * h + 1) * t, (2 * h + 2) * t)
            l1 = jnp.concatenate([l_sc[r1, :], l_sc[r1, :]], axis=1)
            l2 = jnp.concatenate([l_sc[r2, :], l_sc[r2, :]], axis=1)
            o_ref[:, h * hw:(h + 1) * hw] = _attn_finish(
                a_sc[r1, :], l1, a_sc[r2, :], l2, lam, sub, z_ref[:, h * hw:(h + 1) * hw], lambda_init)


def _attn_sample(qb, z32, kvb, cache_k2, cache_v2, lq1, lk1, lq2, lk2, subln, lambda_init, t, tkv):
    bsz, rows8, hw = cache_k2.shape
    past_len = rows8 // N_HEADS
    nkb = past_len // tkv
    rows_all = 2 * N_HEADS * t
    assert t <= LANES
    vec = pl.BlockSpec((1, HEAD_DIM), lambda b, k: (0, 0))
    cache = lambda c: pl.BlockSpec((None, tkv * N_HEADS, HEAD_DIM), lambda b, k: (b, k, c))
    full = lambda a: pl.BlockSpec((t, a.shape[1]), lambda b, k: (b, 0))
    return pl.pallas_call(
        functools.partial(_attn_sample_kernel, tkv=tkv, nkb=nkb, past_len=past_len,
                          lambda_init=lambda_init),
        out_shape=jax.ShapeDtypeStruct((bsz * t, N_HEADS * hw), BF16),
        grid=(bsz, nkb),
        in_specs=[
            full(qb), cache(0), cache(1), cache(0), cache(1), full(kvb), full(z32),
            vec, vec, vec, vec,
            pl.BlockSpec((1, hw), lambda b, k: (0, 0)),
        ],
        out_specs=pl.BlockSpec((t, N_HEADS * hw), lambda b, k: (b, 0)),
        scratch_shapes=[
            pltpu.VMEM((rows_all, LANES), F32),
            pltpu.VMEM((rows_all, LANES), F32),
            pltpu.VMEM((rows_all, hw), F32),
            pltpu.VMEM((rows_all, tkv + LANES), F32),
            pltpu.VMEM((rows_all, tkv + LANES), BF16),
        ],
        compiler_params=_params("arbitrary", "arbitrary"),
        name="attn_sample",
    )(qb, cache_k2, cache_k2, cache_v2, cache_v2, kvb, z32, lq1, lk1, lq2, lk2, subln)


def _b_out_kernel(op_ref, os_ref, w_ref, x_ref, g_ref, yp_ref, ys_ref, wb_ref, *, mp):
    m = pl.program_id(0) - 1

    @pl.when(m < 0)
    def _():
        wb_ref[...] = w_ref[...].astype(BF16)

    def finish(o_ref, y_ref):
        acc = jnp.dot(o_ref[...], wb_ref[...], preferred_element_type=F32)
        y_ref[...] = _rms_scale(x_ref[...] + acc, g_ref[...])

    @pl.when((m >= 0) & (m < mp))
    def _():
        finish(op_ref, yp_ref)

    @pl.when(m == mp)
    def _():
        finish(os_ref, ys_ref)


def _b_out_proj(og_p, og_s, w_out, x1, g):
    tm = ROW_TILE
    rows_p, e = og_p.shape
    rows, d = x1.shape
    mp = rows_p // tm
    assert og_s.shape[0] == tm and rows == rows_p + tm and rows_p % tm == 0
    prow = lambda r: jnp.minimum(_row_block(r), mp - 1)
    return pl.pallas_call(
        functools.partial(_b_out_kernel, mp=mp),
        out_shape=[jax.ShapeDtypeStruct((rows_p, d), F32), jax.ShapeDtypeStruct((tm, d), F32)],
        grid=(1 + rows // tm,),
        in_specs=[
            pl.BlockSpec((tm, e), lambda r: (prow(r), 0)),
            _resident((tm, e), lambda r: (0, 0)),
            _resident((e, d), lambda r: (0, 0)),
            pl.BlockSpec((tm, d), lambda r: (_row_block(r), 0)),
            _resident((1, d), lambda r: (0, 0)),
        ],
        out_specs=[pl.BlockSpec((tm, d), lambda r: (prow(r), 0)),
                   pl.BlockSpec((tm, d), lambda r: (0, 0))],
        scratch_shapes=[pltpu.VMEM((e, d), BF16)],
        compiler_params=_params("arbitrary"),
        name="b_out_proj",
    )(og_p, og_s, w_out, x1, g)


def kernel(x_prompt, x_sample, state_ssm_re, state_ssm_im, cache_k, cache_v, a_norm, a_w_in, a_lambda_re, a_lambda_im, a_log_dt, a_b_re, a_b_im, a_c_re, a_c_im, a_d, a_w_glu, a_w_out, kv_norm, w_kv, b_norm, b_w_in, b_lambda_q1, b_lambda_k1, b_lambda_q2, b_lambda_k2, b_subln, b_w_out, final_norm):
    bp, seq, d = x_prompt.shape
    bs, t_s, _ = x_sample.shape
    past_len = cache_k.shape[1]
    n_a, n_b = a_norm.shape[0], b_norm.shape[0]
    g_cnt, p = a_lambda_re.shape[1:]
    assert bp == 1 and n_a == 1 and n_b == 1
    assert t_s == SSM_T and seq % (SSM_T * bs) == 0 and seq % CHUNK == 0
    assert cache_k.shape[2:] == (N_HEADS, 2 * HEAD_DIM)

    jp, js = seq // SSM_T, bs
    jpad = -(-(jp + js) // CHUNK_TILE) * CHUNK_TILE
    rows_s = bs * t_s
    rows_a = seq + rows_s

    xp = x_prompt.reshape(seq, d)
    xs = x_sample.reshape(rows_s, d)
    uz = _a_in_proj(xp, xs, a_norm[0].reshape(1, d), a_w_in[0], jpad * SSM_T, tn=WIDE_PROJ_COLS)
    mt, bend, cin, a16r, a16i = _ssm_prep(a_lambda_re[0], a_lambda_im[0], a_log_dt[0],
                                          a_b_re[0], a_b_im[0], a_c_re[0], a_c_im[0])
    s_re, s_im = _ssm_state_contrib(uz, bend, jpad)
    h0r = state_ssm_re[:, 0].reshape(bs, g_cnt * p)
    h0i = state_ssm_im[:, 0].reshape(bs, g_cnt * p)
    hp_re, hp_im, pre, pim, sre, sim = _ssm_scan(s_re, s_im, a16r, a16i, h0r, h0i, jp)
    d_rep = jnp.broadcast_to(a_d[0].reshape(g_cnt, SSM_GROUP, 1), (g_cnt, SSM_GROUP, CHUNK_TILE))
    yg = _ssm_output(uz, mt, cin, hp_re, hp_im, d_rep, jpad)
    o_a = _a_glu(yg, uz, a_w_glu[0], rows_a)
    x1 = _a_out_proj(o_a, a_w_out[0], xp, xs)

    lambda_init = 0.8 - 0.6 * math.exp(-0.3 * n_a)
    w_q = b_w_in[0]
    kv_g = kv_norm.reshape(1, d)
    b_g = b_norm[0].reshape(1, d)
    rope = _rope_tables(seq, past_len, t_s, ROW_TILE)
    k_p, v_p, kvb_p, k_s, v_s, kvb_s = _kv_proj(x1, seq, kv_g, w_kv, rope)
    qb_p, z_p, qb_s, z_s = _qz_proj(x1, seq, b_g, w_q, rope, tn=WIDE_PROJ_COLS)

    vecs = [a[0].reshape(1, HEAD_DIM) for a in (b_lambda_q1, b_lambda_k1, b_lambda_q2, b_lambda_k2)]
    subln = b_subln[0].reshape(1, 2 * HEAD_DIM)
    hw = 2 * HEAD_DIM
    og_p = _attn_prompt(qb_p, z_p, kvb_p, *vecs, subln, lambda_init,
                        tq=ATTN_Q_TILE, ts=ATTN_SUB_TILE, tk=ATTN_KEY_TILE)
    og_s = _attn_sample(qb_s, z_s, kvb_s, cache_k.reshape(bs, past_len * N_HEADS, hw),
                        cache_v.reshape(bs, past_len * N_HEADS, hw), *vecs, subln, lambda_init,
                        t_s, tkv=CACHE_KEY_TILE)
    w_o = b_w_out[0]
    fg = final_norm.reshape(1, d)
    y_p, y_s = _b_out_proj(og_p, og_s, w_o, x1, fg)

    return (y_p.reshape(bp, seq, d), y_s.reshape(bs, t_s, d),
            pre.reshape(bp, n_a, g_cnt, p), pim.reshape(bp, n_a, g_cnt, p),
            k_p.reshape(bp, seq, N_HEADS, hw), v_p.reshape(bp, seq, N_HEADS, hw),
            sre.reshape(bs, n_a, g_cnt, p), sim.reshape(bs, n_a, g_cnt, p),
            k_s.reshape(bs, t_s, N_HEADS, hw), v_s.reshape(bs, t_s, N_HEADS, hw))
```
